```python
import math
import jax
import jax.numpy as jnp
from jax import lax
import numpy as np

D_MODEL = 1024
BATCH = 16
SEQ = 256
DEPTH = 2
DEC_BATCH = 4
DEC_SEQ = 1024
PAST_LEN = 256

GRID_W = 64
D_MIX = 1024
Q_BLK = 128
ROPE_BASE = 10000.0
NORM_EPS = 1e-6
NEG_INF = -1e30
F32 = jnp.float32
W_A = 256
H_A = 4
N_A = 64
LORA_W = 64
LORA_A = 64
GN_EPS = 64e-5
W_B = 256
H_B = 4
KV_B = 2
G_B = 2
HD_B = 64
WINDOW = 128
W_C = 256
NB_C = 4
BS_C = 64
CONV_W = 4
C_RG = 8.0
W_D = 256
H_D = 4
DQ_D = 32
HD_D = 64

PROJ_SIZES = (W_A, W_A, W_A, LORA_W, LORA_A, W_A,
              H_B * HD_B, KV_B * HD_B, KV_B * HD_B, W_B,
              W_C, W_C,
              H_D * 2 * DQ_D, H_D * 2 * DQ_D, H_D * HD_D, W_D)
P_TOTAL = 3456

kernel_name = 'hybrid_diffusion_prefix_step'


def split_cols(p):
    idx = np.cumsum(PROJ_SIZES)[:-1].tolist()
    return jnp.split(p, idx, axis=-1)


def rms_norm(x, g, eps=NORM_EPS):
    xf = x.astype(F32)
    y = xf * lax.rsqrt(jnp.mean(jnp.square(xf), -1, keepdims=True) + eps)
    return (y * g.astype(F32)).astype(x.dtype)


def grid_positions(n_tok):
    rows = n_tok // GRID_W
    row = jnp.repeat(jnp.arange(rows), GRID_W).astype(F32)
    col = jnp.tile(jnp.arange(GRID_W), rows).astype(F32)
    return row, col


def rope_1d(x, pos):
    d = x.shape[-1]
    inv = ROPE_BASE ** (-jnp.arange(0, d, 2, dtype=F32) / d)
    ang = pos[:, None] * inv[None]
    shape = (1, x.shape[1]) + (1,) * (x.ndim - 3) + (d // 2,)
    cos = jnp.cos(ang).reshape(shape)
    sin = jnp.sin(ang).reshape(shape)
    xf = x.astype(F32)
    x1, x2 = xf[..., :d // 2], xf[..., d // 2:]
    return jnp.concatenate([x1 * cos - x2 * sin, x1 * sin + x2 * cos], -1).astype(x.dtype)


def rope_2d(x, row, col):
    h = x.shape[-1] // 2
    return jnp.concatenate([rope_1d(x[..., :h], row), rope_1d(x[..., h:], col)], -1)


def over_query_blocks(fn, q):
    b, t = q.shape[:2]
    nb = t // Q_BLK
    qb = jnp.moveaxis(q.reshape((b, nb, Q_BLK) + q.shape[2:]), 1, 0)
    out = lax.map(lambda a: fn(a[0], a[1]), (jnp.arange(nb), qb))
    return jnp.moveaxis(out, 0, 1).reshape((b, t) + out.shape[3:])


def wkv7_scan(r, w, k, v, kk, a, s0, reverse):
    def step(s, inp):
        r_t, w_t, k_t, v_t, kk_t, a_t = inp
        sa = jnp.einsum('bhij,bhj->bhi', s, -kk_t)
        s = (s * w_t[:, :, None, :] + sa[..., None] * (kk_t * a_t)[:, :, None, :]
             + v_t[..., None] * k_t[:, :, None, :])
        return s, jnp.einsum('bhij,bhj->bhi', s, r_t)
    xs = tuple(jnp.moveaxis(z.astype(F32), 1, 0) for z in (r, w, k, v, kk, a))
    s_fin, y = lax.scan(step, s0.astype(F32), xs, reverse=reverse)
    return s_fin, jnp.moveaxis(y, 0, 1)


def rwkv_branch(r, k, v, wd, ad, lp, s0):
    b, t = r.shape[:2]
    heads = lambda z: z.reshape(b, t, H_A, N_A)
    kk = heads(k * lp['rwkv_k_k']).astype(F32)
    kk = kk * lax.rsqrt(jnp.sum(kk * kk, -1, keepdims=True) + 1e-12)
    wd_t = jnp.tanh(wd)
    ys, finals = [], []
    for d, rev in enumerate((False, True)):
        w_log = -jax.nn.softplus(-(lp['rwkv_w0'][d] + wd_t @ lp['rwkv_w_up'][d]).astype(F32)) - 0.5
        decay = jnp.exp(-jnp.exp(w_log))
        a = jax.nn.sigmoid((lp['rwkv_a0'][d] + ad @ lp['rwkv_a_up'][d]).astype(F32))
        k_d = k.astype(F32) * (1.0 + (a - 1.0) * lp['rwkv_k_a'].astype(F32))
        s_fin, y = wkv7_scan(heads(r), heads(decay), heads(k_d), heads(v), kk, heads(a), s0[:, d], rev)
        ys.append(y)
        finals.append(s_fin)
    y = ys[0] + ys[1]
    mu = jnp.mean(y, -1, keepdims=True)
    var = jnp.mean(jnp.square(y - mu), -1, keepdims=True)
    y = ((y - mu) * lax.rsqrt(var + GN_EPS)).reshape(b, t, W_A) * lp['rwkv_gn_g'].astype(F32) + lp['rwkv_gn_b'].astype(F32)
    bonus = jnp.sum(heads(r).astype(F32) * heads(k).astype(F32) * lp['rwkv_r_k'].astype(F32), -1, keepdims=True) * heads(v).astype(F32)
    y = y + bonus.reshape(b, t, W_A)
    return y.astype(r.dtype), jnp.stack(finals, 1)


def sink_gqa_block(qj, key_sets, sink):
    scale = HD_B ** -0.5
    logits = []
    for k, v, m in key_sets:
        s = jnp.einsum('bqhgd,bkhd->bhgqk', qj, k).astype(F32) * scale
        if m is not None:
            s = jnp.where(m, s, NEG_INF)
        logits.append(s)
    b, q = qj.shape[:2]
    sink_l = jnp.broadcast_to(sink.astype(F32).reshape(1, KV_B, G_B, 1, 1), (b, KV_B, G_B, q, 1))
    p = jax.nn.softmax(jnp.concatenate(logits + [sink_l], -1), -1)
    out, off = None, 0
    for k, v, m in key_sets:
        n = k.shape[1]
        o = jnp.einsum('bhgqk,bkhd->bqhgd', p[..., off:off + n].astype(v.dtype), v)
        out = o if out is None else out + o
        off += n
    return out


def window_latent(q, k, v, ck, cv, sink):
    t = q.shape[1]
    pad = ((0, 0), (Q_BLK, Q_BLK), (0, 0), (0, 0))
    kp = jnp.pad(k, pad)
    vp = jnp.pad(v, pad)

    def blk(j, qj):
        kj = lax.dynamic_slice_in_dim(kp, j * Q_BLK, 3 * Q_BLK, axis=1)
        vj = lax.dynamic_slice_in_dim(vp, j * Q_BLK, 3 * Q_BLK, axis=1)
        qpos = j * Q_BLK + jnp.arange(Q_BLK)
        kpos = (j - 1) * Q_BLK + jnp.arange(3 * Q_BLK)
        valid = (jnp.abs(kpos[None] - qpos[:, None]) <= WINDOW) & (kpos[None] >= 0) & (kpos[None] < t)
        return sink_gqa_block(qj, [(kj, vj, valid), (ck, cv, None)], sink)
    return over_query_blocks(blk, q)


def conv_centred(x, w, bias):
    t = x.shape[1]
    xp = jnp.pad(x, ((0, 0), (CONV_W // 2, CONV_W - 1 - CONV_W // 2), (0, 0)))
    y = bias + xp[:, 0:t] * w[0]
    for i in range(1, CONV_W):
        y = y + xp[:, i:i + t] * w[i]
    return y


def lin_combine(e1, e2):
    a1, b1 = e1
    a2, b2 = e2
    return a1 * a2, a2 * b1 + b2


def rglru_branch(x, lp, h0):
    b, t = x.shape[:2]
    x = conv_centred(x, lp['lru_conv_w'], lp['lru_conv_b'])
    xb = x.reshape(b, t, NB_C, BS_C)
    ys, finals = [], []
    for d, rev in enumerate((False, True)):
        gate_a = jax.nn.sigmoid((jnp.einsum('btnd,nde->btne', xb, lp['lru_wa'][d]).reshape(b, t, W_C) + lp['lru_ba'][d]).astype(F32))
        gate_x = jax.nn.sigmoid((jnp.einsum('btnd,nde->btne', xb, lp['lru_wx'][d]).reshape(b, t, W_C) + lp['lru_bx'][d]).astype(F32))
        log_a = -C_RG * gate_a * jax.nn.softplus(-lp['lru_lambda'][d].astype(F32))
        a = jnp.exp(log_a)
        u = jnp.sqrt(-jnp.expm1(2.0 * log_a)) * (gate_x * x.astype(F32))
        a_cum, h = lax.associative_scan(lin_combine, (a, u), reverse=rev, axis=1)
        h = h + a_cum * h0[:, d][:, None].astype(F32)
        ys.append(h)
        finals.append(h[:, 0] if rev else h[:, -1])
    return (ys[0] + ys[1]).astype(x.dtype), jnp.stack(finals, 1)


def diff_block(qj, key_sets, lam):
    s = jnp.concatenate([jnp.einsum('bqhmd,bkhmd->bhmqk', qj, k) for k, _ in key_sets], -1).astype(F32) * (DQ_D ** -0.5)
    p = jax.nn.softmax(s, -1)
    p = p[:, :, 0] - lam * p[:, :, 1]
    out, off = None, 0
    for k, v in key_sets:
        n = k.shape[1]
        o = jnp.einsum('bhqk,bkhd->bqhd', p[..., off:off + n].astype(v.dtype), v)
        out = o if out is None else out + o
        off += n
    return out


def diff_attend(q, key_sets, lp, lam_init):
    b, t = q.shape[:2]
    lq1, lk1, lq2, lk2 = lp['diff_lambda'].astype(F32)
    lam = jnp.exp(jnp.sum(lq1 * lk1)) - jnp.exp(jnp.sum(lq2 * lk2)) + lam_init
    y = over_query_blocks(lambda j, qj: diff_block(qj, key_sets, lam), q)
    y = rms_norm(y, lp['diff_subln_g']) * (1.0 - lam_init)
    return y.reshape(b, t, W_D)


def mixer(h, lp, lam_init, cache):
    b, t = h.shape[:2]
    (ar, ak, av, awd, aad, ag, bq, bk, bv, bg, cx, cg, dq, dk, dv, dg) = split_cols(h @ lp['w_in'])
    latent = cache is not None
    s0_a = cache['rwkv'] if latent else jnp.zeros((b, 2, H_A, N_A, N_A), F32)
    ya, st_a = rwkv_branch(ar, ak, av, awd, aad, lp, s0_a)
    bq = bq.reshape(b, t, KV_B, G_B, HD_B)
    bk = bk.reshape(b, t, KV_B, HD_B)
    bv = bv.reshape(b, t, KV_B, HD_B)
    dq = dq.reshape(b, t, H_D, 2, DQ_D)
    dk = dk.reshape(b, t, H_D, 2, DQ_D)
    dv = dv.reshape(b, t, H_D, HD_D)
    if latent:
        row, col = grid_positions(t)
        yb = window_latent(rope_2d(bq, row, col), rope_2d(bk, row, col), bv,
                           cache['win_k'], cache['win_v'], lp['win_sink'])
        yd = diff_attend(rope_2d(dq, row, col),
                         [(rope_2d(dk, row, col), dv), (cache['diff_k'], cache['diff_v'])], lp, lam_init)
    else:
        yb = over_query_blocks(lambda j, qj: sink_gqa_block(qj, [(bk, bv, None)], lp['win_sink']), bq)
        yd = diff_attend(dq, [(dk, dv)], lp, lam_init)
    h0_c = cache['lru'] if latent else jnp.zeros((b, 2, W_C), F32)
    yc, st_c = rglru_branch(cx, lp, h0_c)
    y = jnp.concatenate([ya * jax.nn.silu(ag), yb.reshape(b, t, W_B) * jax.nn.silu(bg),
                         yc * jax.nn.silu(cg), yd * jax.nn.silu(dg)], -1) @ lp['w_out']
    new_cache = None if latent else (bk, bv, dk, dv, st_a, st_c)
    return y, new_cache


def layer(x, cvec, lp, lam_init, cache):
    mod = jax.nn.silu(cvec) @ lp['w_mod'] + lp['b_mod']
    shift, scale, gate = jnp.split(mod[:, None, :], 3, -1)
    h = rms_norm(x, lp['g_pre']) * (1.0 + scale) + shift
    y, new_cache = mixer(h, lp, lam_init, cache)
    return x + gate * rms_norm(y, lp['g_post']), new_cache


def setup_inputs(seed: int = 0) -> dict:
    key = jax.random.key(seed)
    ks = iter(jax.random.split(key, 48))
    nrm = lambda shape, s=1.0: jax.random.normal(next(ks), shape, F32) * s
    L = DEPTH
    u = jax.random.uniform(next(ks), (L, 2, W_C), F32, 0.9, 0.999)
    sl = u ** (1.0 / C_RG)
    lru_lambda = jnp.log(sl) - jnp.log1p(-sl)
    rwkv_w0 = jax.random.uniform(next(ks), (L, 2, W_A), F32, -6.0, 1.0)
    return {
        'x_prompt': nrm((BATCH, SEQ, D_MODEL)),
        'x_sample': nrm((DEC_BATCH, DEC_SEQ, D_MODEL)),
        'c': nrm((DEC_BATCH, D_MODEL)),
        'cache_win_k': nrm((DEC_BATCH, L, PAST_LEN, KV_B, HD_B)),
        'cache_win_v': nrm((DEC_BATCH, L, PAST_LEN, KV_B, HD_B)),
        'cache_diff_k': nrm((DEC_BATCH, L, PAST_LEN, H_D, 2, DQ_D)),
        'cache_diff_v': nrm((DEC_BATCH, L, PAST_LEN, H_D, HD_D)),
        'state_rwkv': nrm((DEC_BATCH, L, 2, H_A, N_A, N_A), 0.3),
        'state_lru': nrm((DEC_BATCH, L, 2, W_C), 0.5),
        'c_ctx': nrm((D_MODEL,)),
        'w_mod': nrm((L, D_MODEL, 3 * D_MODEL), 0.5 * D_MODEL ** -0.5),
        'b_mod': nrm((L, 3 * D_MODEL), 0.02),
        'g_pre': 1.0 + nrm((L, D_MODEL), 0.02),
        'g_post': 1.0 + nrm((L, D_MODEL), 0.02),
        'w_in': nrm((L, D_MODEL, P_TOTAL), D_MODEL ** -0.5),
        'w_out': nrm((L, D_MIX, D_MODEL), D_MIX ** -0.5),
        'rwkv_w0': rwkv_w0,
        'rwkv_w_up': nrm((L, 2, LORA_W, W_A), 0.1 * LORA_W ** -0.5),
        'rwkv_a0': nrm((L, 2, W_A), 0.1),
        'rwkv_a_up': nrm((L, 2, LORA_A, W_A), 0.1 * LORA_A ** -0.5),
        'rwkv_k_k': 0.85 + nrm((L, W_A), 0.02),
        'rwkv_k_a': 1.0 + nrm((L, W_A), 0.02),
        'rwkv_r_k': nrm((L, H_A, N_A), 0.1),
        'rwkv_gn_g': 1.0 + nrm((L, W_A), 0.02),
        'rwkv_gn_b': nrm((L, W_A), 0.02),
        'win_sink': nrm((L, H_B), 0.5),
        'lru_conv_w': nrm((L, CONV_W, W_C), CONV_W ** -0.5),
        'lru_conv_b': nrm((L, W_C), 0.02),
        'lru_wa': nrm((L, 2, NB_C, BS_C, BS_C), BS_C ** -0.5),
        'lru_ba': nrm((L, 2, W_C), 0.02),
        'lru_wx': nrm((L, 2, NB_C, BS_C, BS_C), BS_C ** -0.5),
        'lru_bx': nrm((L, 2, W_C), 0.02),
        'lru_lambda': lru_lambda,
        'diff_lambda': nrm((L, 4, DQ_D), 0.1),
        'diff_subln_g': 1.0 + nrm((L, HD_D), 0.02),
    }


def reference(x_prompt, x_sample, c, cache_win_k, cache_win_v, cache_diff_k, cache_diff_v, state_rwkv, state_lru,
              c_ctx, w_mod, b_mod, g_pre, g_post, w_in, w_out,
              rwkv_w0, rwkv_w_up, rwkv_a0, rwkv_a_up, rwkv_k_k, rwkv_k_a, rwkv_r_k, rwkv_gn_g, rwkv_gn_b,
              win_sink, lru_conv_w, lru_conv_b, lru_wa, lru_ba, lru_wx, lru_bx, lru_lambda,
              diff_lambda, diff_subln_g):
    y_p = x_prompt
    y_s = x_sample
    ctx_tensors = []
    for l in range(DEPTH):
        lp = dict(w_mod=w_mod[l], b_mod=b_mod[l], g_pre=g_pre[l], g_post=g_post[l], w_in=w_in[l], w_out=w_out[l],
                  rwkv_w0=rwkv_w0[l], rwkv_w_up=rwkv_w_up[l], rwkv_a0=rwkv_a0[l], rwkv_a_up=rwkv_a_up[l],
                  rwkv_k_k=rwkv_k_k[l], rwkv_k_a=rwkv_k_a[l], rwkv_r_k=rwkv_r_k[l],
                  rwkv_gn_g=rwkv_gn_g[l], rwkv_gn_b=rwkv_gn_b[l], win_sink=win_sink[l],
                  lru_conv_w=lru_conv_w[l], lru_conv_b=lru_conv_b[l], lru_wa=lru_wa[l], lru_ba=lru_ba[l],
                  lru_wx=lru_wx[l], lru_bx=lru_bx[l], lru_lambda=lru_lambda[l],
                  diff_lambda=diff_lambda[l], diff_subln_g=diff_subln_g[l])
        lam_init = 0.8 - 0.6 * math.exp(-0.3 * l)
        y_p, nc = layer(y_p, c_ctx[None], lp, lam_init, None)
        ctx_tensors.append(nc)
        layer_cache = dict(win_k=cache_win_k[:, l], win_v=cache_win_v[:, l], diff_k=cache_diff_k[:, l],
                           diff_v=cache_diff_v[:, l], rwkv=state_rwkv[:, l], lru=state_lru[:, l])
        y_s, _ = layer(y_s, c, lp, lam_init, layer_cache)
    new_win_k = jnp.stack([ct[0] for ct in ctx_tensors], 1)
    new_win_v = jnp.stack([ct[1] for ct in ctx_tensors], 1)
    new_diff_k = jnp.stack([ct[2] for ct in ctx_tensors], 1)
    new_diff_v = jnp.stack([ct[3] for ct in ctx_tensors], 1)
    new_state_rwkv = jnp.stack([ct[4] for ct in ctx_tensors], 1)
    new_state_lru = jnp.stack([ct[5] for ct in ctx_tensors], 1)
    return (y_p, y_s, new_win_k, new_win_v, new_diff_k, new_diff_v, new_state_rwkv, new_state_lru)
```

```python
import functools
import math

import numpy as np
import jax
import jax.numpy as jnp
from jax import lax
from jax.experimental import pallas as pl
from jax.experimental.pallas import tpu as pltpu

F32 = jnp.float32
BF16 = jnp.bfloat16
HI = lax.Precision.HIGHEST

D_MODEL = 1024
DEPTH = 2
GRID_W = 64
ROPE_BASE = 10000.0
NORM_EPS = 1e-6
NEG_INF = -1e30
GN_EPS = 64e-5
C_RG = 8.0
W_BR = 256
HEAD = 64
LORA = 64
DQ_D = 32
WINDOW = 128
Q_BLK = 128
P_TOTAL = 3456
WA, WB, WC, WD = 1152, 768, 512, 1024
CHUNK = 64
ROW_TILE = 256
VMEM_LIMIT = 48 * 1024 * 1024


def _dot_hi(a, b):
    return jnp.dot(a, b, precision=HI, preferred_element_type=F32)


def _dot_nt_hi(a, b):
    return lax.dot_general(a, b, (((1,), (1,)), ((), ())), precision=HI, preferred_element_type=F32)


def _dot_tn_hi(a, b):
    return lax.dot_general(a, b, (((0,), (0,)), ((), ())), precision=HI, preferred_element_type=F32)


def _dot_bf(a, b):
    return jnp.dot(a.astype(BF16), b.astype(BF16), preferred_element_type=F32)


def _dot_nt_bf(a, b):
    return lax.dot_general(a.astype(BF16), b.astype(BF16), (((1,), (1,)), ((), ())),
                           preferred_element_type=F32)


def _sigmoid(x):
    return 1.0 / (1.0 + jnp.exp(-x))


def _silu(x):
    return x * _sigmoid(x)


def _softplus(x):
    return jnp.maximum(x, 0.0) + jnp.log1p(jnp.exp(-jnp.abs(x)))


def _head_ones(n):
    r = lax.broadcasted_iota(jnp.int32, (n, n), 0) // HEAD
    c = lax.broadcasted_iota(jnp.int32, (n, n), 1) // HEAD
    return jnp.where(r == c, 1.0, 0.0).astype(F32)


def _params(sem):
    return pltpu.CompilerParams(dimension_semantics=sem, vmem_limit_bytes=VMEM_LIMIT)


def _mod_kernel(c_ref, w_ref, b_ref, o_ref):
    o_ref[0] = _dot_hi(_silu(c_ref[...]), w_ref[0]) + b_ref[0]


def _modulation(cvec, w_mod, b_mod):
    n_l = w_mod.shape[0]
    tn = 512
    return pl.pallas_call(
        _mod_kernel,
        grid=(n_l, 3 * D_MODEL // tn),
        in_specs=[pl.BlockSpec((8, D_MODEL), lambda l, j: (0, 0)),
                  pl.BlockSpec((1, D_MODEL, tn), lambda l, j: (l, 0, j)),
                  pl.BlockSpec((1, 1, tn), lambda l, j: (l, 0, j))],
        out_specs=pl.BlockSpec((1, 8, tn), lambda l, j: (l, 0, j)),
        out_shape=jax.ShapeDtypeStruct((n_l, 8, 3 * D_MODEL), F32),
        compiler_params=_params(("parallel", "parallel")),
        name="mod",
    )(cvec, w_mod, b_mod.reshape(n_l, 1, 3 * D_MODEL))


def _rope(x, cos, sin_signed, off):
    w = x.shape[-1]
    lane = lax.broadcasted_iota(jnp.int32, x.shape, 1)
    first = (lane % (2 * off)) < off
    partner = jnp.where(first, pltpu.roll(x, w - off, 1), pltpu.roll(x, off, 1))
    return x * cos + partner * sin_signed


def _proj_kernel(*refs, latent):
    if latent:
        x_ref, mod_ref, g_ref, w_ref, cb_ref, sb_ref, cd_ref, sd_ref, oa, ob, oc, od = refs
    else:
        x_ref, mod_ref, g_ref, w_ref, oa, ob, oc, od = refs
    x = x_ref[0]
    y = x * lax.rsqrt(jnp.mean(x * x, -1, keepdims=True) + NORM_EPS) * g_ref[...]
    m = mod_ref[0]
    h = y * (1.0 + m[:, D_MODEL:2 * D_MODEL]) + m[:, :D_MODEL]
    p = jnp.dot(h.astype(BF16), w_ref[0], preferred_element_type=F32)
    oa[0] = p[:, :WA]
    pb = p[:, WA:WA + WB]
    pd = p[:, WA + WB + WC:]
    if latent:
        ob[0, :, :384] = _rope(pb[:, :384], cb_ref[...], sb_ref[...], 16)
        ob[0, :, 384:] = pb[:, 384:]
        od[0, :, :512] = _rope(pd[:, :512], cd_ref[...], sd_ref[...], 8)
        od[0, :, 512:] = pd[:, 512:]
    else:
        ob[0] = pb
        od[0] = pd
    oc[0] = p[:, WA + WB:WA + WB + WC]


def _project(x, mod_l, g_pre, w_in_bf, layer, tables):
    b, t, _ = x.shape
    latent = tables is not None
    tm = ROW_TILE
    mod_idx = (lambda i, j: (1 + i, 0, 0)) if latent else (lambda i, j: (0, 0, 0))
    in_specs = [pl.BlockSpec((1, tm, D_MODEL), lambda i, j: (i, j, 0)),
                pl.BlockSpec((1, 1, 3 * D_MODEL), mod_idx),
                pl.BlockSpec((1, D_MODEL), lambda i, j: (0, 0)),
                pl.BlockSpec((1, D_MODEL, P_TOTAL), lambda i, j: (layer, 0, 0))]
    args = [x, mod_l, g_pre, w_in_bf]
    if latent:
        for tab in tables:
            in_specs.append(pl.BlockSpec((tm, tab.shape[1]), lambda i, j: (j, 0)))
            args.append(tab)
    widths = (WA, WB, WC, WD)
    return pl.pallas_call(
        functools.partial(_proj_kernel, latent=latent),
        grid=(b, t // tm),
        in_specs=in_specs,
        out_specs=[pl.BlockSpec((1, tm, w), lambda i, j: (i, j, 0)) for w in widths],
        out_shape=[jax.ShapeDtypeStruct((b, t, w), F32) for w in widths],
        compiler_params=_params(("parallel", "parallel")),
        name="proj",
    )(*args)


def _rope_tables(t, head_dim, n_lanes):
    half = head_dim // 2
    quarter = half // 2
    pos = jnp.arange(t)
    row = (pos // GRID_W).astype(F32)
    col = (pos % GRID_W).astype(F32)
    inv = ROPE_BASE ** (-jnp.arange(0, half, 2, dtype=F32) / half)
    lane = np.arange(n_lanes) % head_dim
    in_half = lane % half
    p = jnp.where(jnp.asarray(lane < half)[None, :], row[:, None], col[:, None])
    ang = p * inv[in_half % quarter][None, :]
    sign = jnp.asarray(np.where(in_half < quarter, -1.0, 1.0), F32)[None, :]
    return jnp.cos(ang), jnp.sin(ang) * sign


def _rwkv_chunk(xc, w0, wup, a0, aup, k_k, k_a, s_ref, d, rev):
    c = xc.shape[0]
    r = xc[:, 0:256]
    k = xc[:, 256:512]
    v = xc[:, 512:768]
    wd = xc[:, 768:832]
    ad = xc[:, 832:896]
    ones = _head_ones(W_BR)
    kk = k * k_k
    kk = kk * lax.rsqrt(_dot_hi(kk * kk, ones) + 1e-12)
    z = w0 + _dot_hi(jnp.tanh(wd), wup)
    e = jnp.exp(-_softplus(-z) - 0.5)
    a = _sigmoid(a0 + _dot_hi(ad, aup))
    kd = k * (1.0 + (a - 1.0) * k_a)
    alpha = -kk
    beta = kk * a

    ti = lax.broadcasted_iota(jnp.int32, (c, c), 0)
    si = lax.broadcasted_iota(jnp.int32, (c, c), 1)
    strict = (si > ti) if rev else (si < ti)
    incl = (si >= ti) if rev else (si <= ti)
    eye = si == ti
    l_incl = _dot_hi(jnp.where(incl, 1.0, 0.0).astype(F32), e)
    l_excl = l_incl - e
    l_tot = jnp.sum(e, axis=0, keepdims=True)
    a_t = alpha * jnp.exp(-l_excl)
    r_t = r * jnp.exp(-l_incl)
    grow = jnp.exp(l_incl)
    b_t = beta * grow
    k_t = kd * grow
    tail = jnp.exp(l_incl - l_tot)
    b_h = beta * tail
    k_h = kd * tail
    g_c = jnp.exp(-l_tot)

    ys = []
    for h in range(W_BR // HEAD):
        sl = slice(h * HEAD, (h + 1) * HEAD)
        s0 = s_ref[d, h]
        ah, rh, bh, kh, vh = a_t[:, sl], r_t[:, sl], b_t[:, sl], k_t[:, sl], v[:, sl]
        n = jnp.where(strict, _dot_nt_hi(ah, bh), 0.0)
        m = jnp.where(strict, _dot_nt_hi(ah, kh), 0.0)
        p = jnp.where(incl, _dot_nt_hi(rh, bh), 0.0)
        q = jnp.where(incl, _dot_nt_hi(rh, kh), 0.0)
        inv = jnp.where(eye, 1.0, n)
        nk = n
        for _ in range(int(math.log2(c)) - 1):
            nk = _dot_hi(nk, nk)
            inv = inv + _dot_hi(inv, nk)
        u = _dot_hi(inv, _dot_nt_hi(ah, s0) + _dot_hi(m, vh))
        ys.append(_dot_nt_hi(rh, s0) + _dot_hi(p, u) + _dot_hi(q, vh))
        s_ref[d, h] = s0 * g_c[:, sl] + _dot_tn_hi(u, b_h[:, sl]) + _dot_tn_hi(vh, k_h[:, sl])
    return jnp.concatenate(ys, axis=-1)


def _rwkv_kernel(*refs, latent):
    if latent:
        (xf_ref, xb_ref, w0_ref, wup_ref, a0_ref, aup_ref, kk_ref, ka_ref, s0_ref,
         yf_ref, yb_ref, s_scr) = refs
    else:
        (xf_ref, xb_ref, w0_ref, wup_ref, a0_ref, aup_ref, kk_ref, ka_ref,
         yf_ref, yb_ref, st_ref, s_scr) = refs
    ci = pl.program_id(1)

    @pl.when(ci == 0)
    def _():
        if latent:
            s_scr[...] = s0_ref[0, 0]
        else:
            s_scr[...] = jnp.zeros(s_scr.shape, F32)

    for d, (x_ref, y_ref) in enumerate(((xf_ref, yf_ref), (xb_ref, yb_ref))):
        y_ref[0] = _rwkv_chunk(x_ref[0], w0_ref[d:d + 1], wup_ref[d], a0_ref[d:d + 1], aup_ref[d],
                               kk_ref[...], ka_ref[...], s_scr, d, rev=(d == 1))

    if not latent:
        @pl.when(ci == pl.num_programs(1) - 1)
        def _():
            st_ref[0] = s_scr[...]


def _rwkv(pa, w0, wup, a0, aup, k_k, k_a, state, layer):
    b, t, _ = pa.shape
    latent = state is not None
    nc = t // CHUNK
    n_h = W_BR // HEAD
    full = lambda shape: pl.BlockSpec(shape, lambda i, j: (0,) * len(shape))
    in_specs = [pl.BlockSpec((1, CHUNK, WA), lambda i, j: (i, j, 0)),
                pl.BlockSpec((1, CHUNK, WA), lambda i, j: (i, nc - 1 - j, 0)),
                full((2, W_BR)), full((2, LORA, W_BR)), full((2, W_BR)), full((2, LORA, W_BR)),
                full((1, W_BR)), full((1, W_BR))]
    args = [pa, pa, w0, wup, a0, aup, k_k, k_a]
    out_specs = [pl.BlockSpec((1, CHUNK, W_BR), lambda i, j: (i, j, 0)),
                 pl.BlockSpec((1, CHUNK, W_BR), lambda i, j: (i, nc - 1 - j, 0))]
    out_shape = [jax.ShapeDtypeStruct((b, t, W_BR), F32)] * 2
    if latent:
        in_specs.append(pl.BlockSpec((1, 1, 2, n_h, HEAD, HEAD), lambda i, j: (i, layer, 0, 0, 0, 0)))
        args.append(state)
    else:
        out_specs.append(pl.BlockSpec((1, 2, n_h, HEAD, HEAD), lambda i, j: (i, 0, 0, 0, 0)))
        out_shape.append(jax.ShapeDtypeStruct((b, 2, n_h, HEAD, HEAD), F32))
    return pl.pallas_call(
        functools.partial(_rwkv_kernel, latent=latent),
        grid=(b, nc),
        in_specs=in_specs,
        out_specs=out_specs,
        out_shape=out_shape,
        scratch_shapes=[pltpu.VMEM((2, n_h, HEAD, HEAD), F32)],
        compiler_params=_params(("parallel", "arbitrary")),
        name="rwkv",
    )(*args)


def _lru_kernel(*refs, latent):
    if latent:
        x_ref, cw_ref, cb_ref, wa_ref, ba_ref, wx_ref, bx_ref, lam_ref, h0_ref, y_ref = refs
    else:
        x_ref, cw_ref, cb_ref, wa_ref, ba_ref, wx_ref, bx_ref, lam_ref, y_ref, st_ref = refs
    x = x_ref[0, :, :W_BR]
    t = x.shape[0]
    row = lax.broadcasted_iota(jnp.int32, x.shape, 0)

    def shift_dn(z, k, fill):
        return jnp.where(row >= k, pltpu.roll(z, k, 0), fill)

    def shift_up(z, k, fill):
        return jnp.where(row < t - k, pltpu.roll(z, t - k, 0), fill)

    cw = cw_ref[...]
    xc = (cb_ref[...] + shift_dn(x, 2, 0.0) * cw[0:1] + shift_dn(x, 1, 0.0) * cw[1:2]
          + x * cw[2:3] + shift_up(x, 1, 0.0) * cw[3:4])
    n_blk = W_BR // HEAD

    def block_proj(w_ref, d):
        return jnp.concatenate(
            [_dot_hi(xc[:, n * HEAD:(n + 1) * HEAD], w_ref[d, n]) for n in range(n_blk)], axis=-1)

    ys, finals = [], []
    for d in range(2):
        shift = shift_up if d == 1 else shift_dn
        gate_a = _sigmoid(block_proj(wa_ref, d) + ba_ref[d:d + 1])
        gate_x = _sigmoid(block_proj(wx_ref, d) + bx_ref[d:d + 1])
        log_a = -C_RG * gate_a * _softplus(-lam_ref[d:d + 1])
        a = jnp.exp(log_a)
        u = jnp.sqrt(-jnp.tanh(log_a) * (a * a + 1.0)) * (gate_x * xc)
        k = 1
        while k < t:
            u = a * shift(u, k, 0.0) + u
            a = a * shift(a, k, 1.0)
            k *= 2
        if latent:
            u = u + a * h0_ref[0, 0, d:d + 1]
        ys.append(u)
        finals.append(u[0:1] if d == 1 else u[t - 1:t])
    y_ref[0] = ys[0] + ys[1]
    if not latent:
        st_ref[0] = jnp.concatenate(finals, axis=0)


def _lru(pc, cw, cb, wa, ba, wx, bx, lam, state, layer):
    b, t, _ = pc.shape
    latent = state is not None
    n_blk = W_BR // HEAD
    full = lambda shape: pl.BlockSpec(shape, lambda i: (0,) * len(shape))
    in_specs = [pl.BlockSpec((1, t, WC), lambda i: (i, 0, 0)),
                full((4, W_BR)), full((1, W_BR)), full((2, n_blk, HEAD, HEAD)), full((2, W_BR)),
                full((2, n_blk, HEAD, HEAD)), full((2, W_BR)), full((2, W_BR))]
    args = [pc, cw, cb, wa, ba, wx, bx, lam]
    out_specs = [pl.BlockSpec((1, t, W_BR), lambda i: (i, 0, 0))]
    out_shape = [jax.ShapeDtypeStruct((b, t, W_BR), F32)]
    if latent:
        in_specs.append(pl.BlockSpec((1, 1, 2, W_BR), lambda i: (i, layer, 0, 0)))
        args.append(state)
    else:
        out_specs.append(pl.BlockSpec((1, 2, W_BR), lambda i: (i, 0, 0)))
        out_shape.append(jax.ShapeDtypeStruct((b, 2, W_BR), F32))
    return pl.pallas_call(
        functools.partial(_lru_kernel, latent=latent),
        grid=(b,),
        in_specs=in_specs,
        out_specs=out_specs,
        out_shape=out_shape,
        compiler_params=_params(("parallel",)),
        name="lru",
    )(*args)


def _softmax_pv(blocks, sink):
    m = None
    for s, _ in blocks:
        mi = jnp.max(s, axis=-1, keepdims=True)
        m = mi if m is None else jnp.maximum(m, mi)
    if sink is not None:
        m = jnp.maximum(m, sink)
    den = None if sink is None else jnp.exp(sink - m)
    acc = None
    for s, v in blocks:
        e = jnp.exp(s - m)
        di = jnp.sum(e, axis=-1, keepdims=True)
        den = di if den is None else den + di
        o = _dot_bf(e, v)
        acc = o if acc is None else acc + o
    return acc, den


def _win_kernel(*refs, latent):
    scale = HEAD ** -0.5
    if latent:
        sink_ref, q_ref, kp_ref, kc_ref, kn_ref, ck_ref, cv_ref, o_ref = refs
        j = pl.program_id(1)
        nq = pl.num_programs(1)
        qi = lax.broadcasted_iota(jnp.int32, (Q_BLK, Q_BLK), 0)
        ki = lax.broadcasted_iota(jnp.int32, (Q_BLK, Q_BLK), 1)
        kv_blocks = [(kp_ref[0], (ki >= qi) & (j > 0)), (kc_ref[0], None),
                     (kn_ref[0], (ki <= qi) & (j < nq - 1))]
        ctx_k, ctx_v = ck_ref[0, 0], cv_ref[0, 0]
    else:
        sink_ref, x_ref, o_ref = refs
    q = q_ref[0] if latent else x_ref[0, :, :256]
    outs = []
    for h in range(2):
        hs = slice(h * HEAD, (h + 1) * HEAD)
        for g in range(2):
            qg = q[:, (2 * h + g) * HEAD:(2 * h + g + 1) * HEAD]
            blocks = []
            if latent:
                for kv, mask in kv_blocks:
                    s = _dot_nt_bf(qg, kv[:, hs]) * scale
                    if mask is not None:
                        s = jnp.where(mask, s, NEG_INF)
                    blocks.append((s, kv[:, 128 + h * HEAD:128 + (h + 1) * HEAD]))
                blocks.append((_dot_nt_bf(qg, ctx_k[:, hs]) * scale, ctx_v[:, hs]))
            else:
                k = x_ref[0, :, 256 + h * HEAD:256 + (h + 1) * HEAD]
                v = x_ref[0, :, 384 + h * HEAD:384 + (h + 1) * HEAD]
                blocks.append((_dot_nt_bf(qg, k) * scale, v))
            acc, den = _softmax_pv(blocks, sink_ref[2 * h + g])
            outs.append(acc / den)
    o_ref[0] = jnp.concatenate(outs, axis=-1)


def _win(pb, sink, cache_k, cache_v, layer):
    b, t, _ = pb.shape
    latent = cache_k is not None
    smem = pl.BlockSpec(memory_space=pltpu.SMEM)
    if latent:
        nq = t // Q_BLK
        grid = (b, nq)
        in_specs = [smem,
                    pl.BlockSpec((1, Q_BLK, 256), lambda i, j: (i, j, 0)),
                    pl.BlockSpec((1, Q_BLK, 256), lambda i, j: (i, jnp.maximum(j - 1, 0), 1)),
                    pl.BlockSpec((1, Q_BLK, 256), lambda i, j: (i, j, 1)),
                    pl.BlockSpec((1, Q_BLK, 256), lambda i, j: (i, jnp.minimum(j + 1, nq - 1), 1)),
                    pl.BlockSpec((1, 1) + cache_k.shape[2:], lambda i, j: (i, layer, 0, 0)),
                    pl.BlockSpec((1, 1) + cache_v.shape[2:], lambda i, j: (i, layer, 0, 0))]
        args = [sink, pb, pb, pb, pb, cache_k, cache_v]
        out_spec = pl.BlockSpec((1, Q_BLK, W_BR), lambda i, j: (i, j, 0))
        sem = ("parallel", "parallel")
    else:
        grid = (b,)
        in_specs = [smem, pl.BlockSpec((1, t, WB), lambda i: (i, 0, 0))]
        args = [sink, pb]
        out_spec = pl.BlockSpec((1, t, W_BR), lambda i: (i, 0, 0))
        sem = ("parallel",)
    return pl.pallas_call(
        functools.partial(_win_kernel, latent=latent),
        grid=grid,
        in_specs=in_specs,
        out_specs=out_spec,
        out_shape=jax.ShapeDtypeStruct((b, t, W_BR), F32),
        compiler_params=_params(sem),
        name="win",
    )(*args)


def _diff_kernel(*refs, latent, lam_init):
    if latent:
        lam_ref, g_ref, q_ref, k_ref, v_ref, ck_ref, cv_ref, o_ref = refs
    else:
        lam_ref, g_ref, q_ref, k_ref, v_ref, o_ref = refs
    scale = DQ_D ** -0.5
    lp = lam_ref[...]
    lam = (jnp.exp(jnp.sum(lp[0:1] * lp[1:2], axis=-1, keepdims=True))
           - jnp.exp(jnp.sum(lp[2:3] * lp[3:4], axis=-1, keepdims=True)) + lam_init)
    q, k, v = q_ref[0], k_ref[0], v_ref[0]
    key_sets = [(k, v)]
    if latent:
        key_sets.append((ck_ref[0, 0], cv_ref[0, 0]))
    outs = []
    for h in range(W_BR // HEAD):
        hs = slice(h * HEAD, (h + 1) * HEAD)
        probs = []
        for m in range(2):
            ms = slice(h * HEAD + m * DQ_D, h * HEAD + (m + 1) * DQ_D)
            logits = [_dot_nt_bf(q[:, ms], kk[:, ms]) * scale for kk, _ in key_sets]
            mx = functools.reduce(jnp.maximum, [jnp.max(s, axis=-1, keepdims=True) for s in logits])
            es = [jnp.exp(s - mx) for s in logits]
            den = functools.reduce(lambda a, b: a + b, [jnp.sum(e, axis=-1, keepdims=True) for e in es])
            inv = 1.0 / den
            probs.append([e * inv for e in es])
        o = None
        for i, (_, vv) in enumerate(key_sets):
            oi = _dot_bf(probs[0][i] - lam * probs[1][i], vv[:, hs])
            o = oi if o is None else o + oi
        o = o * lax.rsqrt(jnp.mean(o * o, -1, keepdims=True) + NORM_EPS) * g_ref[...]
        outs.append(o * (1.0 - lam_init))
    o_ref[0] = jnp.concatenate(outs, axis=-1)


def _diff(pd, lam_p, subln_g, cache_k, cache_v, layer, lam_init):
    b, t, _ = pd.shape
    latent = cache_k is not None
    tq = 256
    in_specs = [pl.BlockSpec((4, DQ_D), lambda i, j: (0, 0)),
                pl.BlockSpec((1, HEAD), lambda i, j: (0, 0)),
                pl.BlockSpec((1, tq, 256), lambda i, j: (i, j, 0)),
                pl.BlockSpec((1, t, 256), lambda i, j: (i, 0, 1)),
                pl.BlockSpec((1, t, 256), lambda i, j: (i, 0, 2))]
    args = [lam_p, subln_g, pd, pd, pd]
    if latent:
        in_specs += [pl.BlockSpec((1, 1) + cache_k.shape[2:], lambda i, j: (i, layer, 0, 0)),
                     pl.BlockSpec((1, 1) + cache_v.shape[2:], lambda i, j: (i, layer, 0, 0))]
        args += [cache_k, cache_v]
    return pl.pallas_call(
        functools.partial(_diff_kernel, latent=latent, lam_init=lam_init),
        grid=(b, t // tq),
        in_specs=in_specs,
        out_specs=pl.BlockSpec((1, tq, W_BR), lambda i, j: (i, j, 0)),
        out_shape=jax.ShapeDtypeStruct((b, t, W_BR), F32),
        compiler_params=_params(("parallel", "parallel")),
        name="diff",
    )(*args)


def _out_kernel(x_ref, mod_ref, gpost_ref, w_ref, pa_ref, yf_ref, yb_ref, gng_ref, gnb_ref, rk_ref,
                ywin_ref, bg_ref, ylru_ref, cg_ref, ydiff_ref, dg_ref, o_ref):
    ones = _head_ones(W_BR)
    pa = pa_ref[0]
    r, k, v, ag = pa[:, 0:256], pa[:, 256:512], pa[:, 512:768], pa[:, 896:1152]
    y = yf_ref[0] + yb_ref[0]
    mu = _dot_hi(y, ones) * (1.0 / HEAD)
    yc = y - mu
    var = _dot_hi(yc * yc, ones) * (1.0 / HEAD)
    ya = yc * lax.rsqrt(var + GN_EPS) * gng_ref[...] + gnb_ref[...]
    ya = ya + _dot_hi(r * k * rk_ref[...], ones) * v
    mix = jnp.concatenate([ya * _silu(ag), ywin_ref[0] * _silu(bg_ref[0]),
                           ylru_ref[0] * _silu(cg_ref[0]), ydiff_ref[0] * _silu(dg_ref[0])], axis=-1)
    o = jnp.dot(mix.astype(BF16), w_ref[0], preferred_element_type=F32)
    o = o * lax.rsqrt(jnp.mean(o * o, -1, keepdims=True) + NORM_EPS) * gpost_ref[...]
    o_ref[0] = x_ref[0] + mod_ref[0][:, 2 * D_MODEL:] * o


def _out(x, mod_l, g_post, w_out_bf, layer, latent, pa, yf, yb, gn_g, gn_b, r_k, ywin, pb, ylru, pc, ydiff, pd):
    b, t, _ = x.shape
    tm = ROW_TILE
    mod_idx = (lambda i, j: (1 + i, 0, 0)) if latent else (lambda i, j: (0, 0, 0))
    rows = lambda w, c=0: pl.BlockSpec((1, tm, w), lambda i, j: (i, j, c))
    vec = pl.BlockSpec((1, W_BR), lambda i, j: (0, 0))
    in_specs = [rows(D_MODEL),
                pl.BlockSpec((1, 1, 3 * D_MODEL), mod_idx),
                pl.BlockSpec((1, D_MODEL), lambda i, j: (0, 0)),
                pl.BlockSpec((1, D_MODEL, D_MODEL), lambda i, j: (layer, 0, 0)),
                rows(WA), rows(W_BR), rows(W_BR), vec, vec, vec,
                rows(W_BR), rows(W_BR, 2), rows(W_BR), rows(W_BR, 1), rows(W_BR), rows(W_BR, 3)]
    return pl.pallas_call(
        _out_kernel,
        grid=(b, t // tm),
        in_specs=in_specs,
        out_specs=rows(D_MODEL),
        out_shape=jax.ShapeDtypeStruct((b, t, D_MODEL), F32),
        compiler_params=_params(("parallel", "parallel")),
        name="out",
    )(x, mod_l, g_post, w_out_bf, pa, yf, yb, gn_g, gn_b, r_k, ywin, pb, ylru, pc, ydiff, pd)


def _layer(x, mod_l, layer, lam_init, wts, cache, tables):
    latent = cache is not None
    row = lambda a: a[layer].reshape(1, -1)
    pa, pb, pc, pd = _project(x, mod_l, row(wts['g_pre']), wts['w_in_bf'], layer, tables)
    rw = _rwkv(pa, wts['rwkv_w0'][layer], wts['rwkv_w_up'][layer], wts['rwkv_a0'][layer],
               wts['rwkv_a_up'][layer], row(wts['rwkv_k_k']), row(wts['rwkv_k_a']),
               cache['rwkv'] if latent else None, layer)
    lr = _lru(pc, wts['lru_conv_w'][layer], row(wts['lru_conv_b']), wts['lru_wa'][layer],
              wts['lru_ba'][layer], wts['lru_wx'][layer], wts['lru_bx'][layer], wts['lru_lambda'][layer],
              cache['lru'] if latent else None, layer)
    ywin = _win(pb, wts['win_sink'][layer], cache['win_k'] if latent else None,
                cache['win_v'] if latent else None, layer)
    ydiff = _diff(pd, wts['diff_lambda'][layer], row(wts['diff_subln_g']),
                  cache['diff_k'] if latent else None, cache['diff_v'] if latent else None, layer, lam_init)
    y = _out(x, mod_l, row(wts['g_post']), wts['w_out_bf'], layer, latent, pa, rw[0], rw[1],
             row(wts['rwkv_gn_g']), row(wts['rwkv_gn_b']), row(wts['rwkv_r_k']),
             ywin, pb, lr[0], pc, ydiff, pd)
    new_cache = None if latent else (pb, pd, rw[2], lr[1])
    return y, new_cache


def kernel(x_prompt, x_sample, c, cache_win_k, cache_win_v, cache_diff_k, cache_diff_v, state_rwkv, state_lru,
           c_ctx, w_mod, b_mod, g_pre, g_post, w_in, w_out,
           rwkv_w0, rwkv_w_up, rwkv_a0, rwkv_a_up, rwkv_k_k, rwkv_k_a, rwkv_r_k, rwkv_gn_g, rwkv_gn_b,
           win_sink, lru_conv_w, lru_conv_b, lru_wa, lru_ba, lru_wx, lru_bx, lru_lambda,
           diff_lambda, diff_subln_g):
    n_b, seq = x_prompt.shape[:2]
    n_dec, dec_seq = x_sample.shape[:2]
    past = cache_win_k.shape[2]
    wts = dict(g_pre=g_pre, g_post=g_post, w_in_bf=w_in.astype(BF16), w_out_bf=w_out.astype(BF16),
               rwkv_w0=rwkv_w0, rwkv_w_up=rwkv_w_up, rwkv_a0=rwkv_a0, rwkv_a_up=rwkv_a_up,
               rwkv_k_k=rwkv_k_k, rwkv_k_a=rwkv_k_a, rwkv_r_k=rwkv_r_k, rwkv_gn_g=rwkv_gn_g,
               rwkv_gn_b=rwkv_gn_b, win_sink=win_sink, lru_conv_w=lru_conv_w, lru_conv_b=lru_conv_b,
               lru_wa=lru_wa, lru_ba=lru_ba, lru_wx=lru_wx, lru_bx=lru_bx, lru_lambda=lru_lambda,
               diff_lambda=diff_lambda, diff_subln_g=diff_subln_g)
    cache = dict(win_k=cache_win_k.reshape(n_dec, DEPTH, past, 128),
                 win_v=cache_win_v.reshape(n_dec, DEPTH, past, 128),
                 diff_k=cache_diff_k.reshape(n_dec, DEPTH, past, 256),
                 diff_v=cache_diff_v.reshape(n_dec, DEPTH, past, 256),
                 rwkv=state_rwkv, lru=state_lru)
    cvec = jnp.concatenate([c_ctx[None], c, jnp.zeros((8 - 1 - n_dec, D_MODEL), F32)], axis=0)
    mod = _modulation(cvec, w_mod, b_mod)
    cos_b, sin_b = _rope_tables(dec_seq, HEAD, 384)
    cos_d, sin_d = _rope_tables(dec_seq, DQ_D, 512)
    tables = (cos_b, sin_b, cos_d, sin_d)

    y_p, y_s = x_prompt, x_sample
    ctx = []
    for l in range(DEPTH):
        lam_init = 0.8 - 0.6 * math.exp(-0.3 * l)
        mod_l = mod[l].reshape(8, 1, 3 * D_MODEL)
        y_p, nc = _layer(y_p, mod_l, l, lam_init, wts, None, None)
        ctx.append(nc)
        y_s, _ = _layer(y_s, mod_l, l, lam_init, wts, cache, tables)
    stack = lambda f: jnp.stack([f(ct) for ct in ctx], axis=1)
    new_win_k = stack(lambda ct: ct[0][:, :, 256:384].reshape(n_b, seq, 2, HEAD))
    new_win_v = stack(lambda ct: ct[0][:, :, 384:512].reshape(n_b, seq, 2, HEAD))
    new_diff_k = stack(lambda ct: ct[1][:, :, 256:512].reshape(n_b, seq, 4, 2, DQ_D))
    new_diff_v = stack(lambda ct: ct[1][:, :, 512:768].reshape(n_b, seq, 4, HEAD))
    new_state_rwkv = stack(lambda ct: ct[2])
    new_state_lru = stack(lambda ct: ct[3])
    return (y_p, y_s, new_win_k, new_win_v, new_diff_k, new_diff_v, new_state_rwkv, new_state_lru)
```

```python
import functools
import math

import numpy as np
import jax
import jax.numpy as jnp
from jax import lax
from jax.experimental import pallas as pl
from jax.experimental.pallas import tpu as pltpu

F32 = jnp.float32
BF16 = jnp.bfloat16
HI = lax.Precision.HIGHEST

D_MODEL = 1024
DEPTH = 2
GRID_W = 64
ROPE_BASE = 10000.0
NORM_EPS = 1e-6
NEG_INF = -1e30
GN_EPS = 64e-5
C_RG = 8.0
W_BR = 256
HEAD = 64
LORA = 64
DQ_D = 32
WINDOW = 128
Q_BLK = 128
P_TOTAL = 3456
WA, WB, WC, WD = 1152, 768, 512, 1024
CHUNK = 64
ROW_TILE = 256
VMEM_LIMIT = 48 * 1024 * 1024


def _dot_hi(a, b):
    return jnp.dot(a, b, precision=HI, preferred_element_type=F32)


def _dot_nt_hi(a, b):
    return lax.dot_general(a, b, (((1,), (1,)), ((), ())), precision=HI, preferred_element_type=F32)


def _dot_tn_hi(a, b):
    return lax.dot_general(a, b, (((0,), (0,)), ((), ())), precision=HI, preferred_element_type=F32)


def _dot_bf(a, b):
    return jnp.dot(a.astype(BF16), b.astype(BF16), preferred_element_type=F32)


def _dot_nt_bf(a, b):
    return lax.dot_general(a.astype(BF16), b.astype(BF16), (((1,), (1,)), ((), ())),
                           preferred_element_type=F32)


def _sigmoid(x):
    return 1.0 / (1.0 + jnp.exp(-x))


def _silu(x):
    return x * _sigmoid(x)


def _softplus(x):
    return jnp.maximum(x, 0.0) + jnp.log1p(jnp.exp(-jnp.abs(x)))


def _head_ones(n):
    r = lax.broadcasted_iota(jnp.int32, (n, n), 0) // HEAD
    c = lax.broadcasted_iota(jnp.int32, (n, n), 1) // HEAD
    return jnp.where(r == c, 1.0, 0.0).astype(F32)


def _params(sem):
    return pltpu.CompilerParams(dimension_semantics=sem, vmem_limit_bytes=VMEM_LIMIT)


def _mod_kernel(c_ref, w_ref, b_ref, o_ref):
    o_ref[0] = _dot_hi(_silu(c_ref[...]), w_ref[0]) + b_ref[0]


def _modulation(cvec, w_mod, b_mod):
    n_l = w_mod.shape[0]
    tn = 512
    return pl.pallas_call(
        _mod_kernel,
        grid=(n_l, 3 * D_MODEL // tn),
        in_specs=[pl.BlockSpec((8, D_MODEL), lambda l, j: (0, 0)),
                  pl.BlockSpec((1, D_MODEL, tn), lambda l, j: (l, 0, j)),
                  pl.BlockSpec((1, 1, tn), lambda l, j: (l, 0, j))],
        out_specs=pl.BlockSpec((1, 8, tn), lambda l, j: (l, 0, j)),
        out_shape=jax.ShapeDtypeStruct((n_l, 8, 3 * D_MODEL), F32),
        compiler_params=_params(("parallel", "parallel")),
        name="mod",
    )(cvec, w_mod, b_mod.reshape(n_l, 1, 3 * D_MODEL))


def _rope(x, cos, sin_signed, off):
    w = x.shape[-1]
    lane = lax.broadcasted_iota(jnp.int32, x.shape, 1)
    first = (lane % (2 * off)) < off
    partner = jnp.where(first, pltpu.roll(x, w - off, 1), pltpu.roll(x, off, 1))
    return x * cos + partner * sin_signed


def _proj_kernel(*refs, latent):
    if latent:
        x_ref, mod_ref, g_ref, w_ref, cb_ref, sb_ref, cd_ref, sd_ref, oa, ob, oc, od = refs
    else:
        x_ref, mod_ref, g_ref, w_ref, oa, ob, oc, od = refs
    x = x_ref[0]
    y = x * lax.rsqrt(jnp.mean(x * x, -1, keepdims=True) + NORM_EPS) * g_ref[...]
    m = mod_ref[0]
    h = y * (1.0 + m[:, D_MODEL:2 * D_MODEL]) + m[:, :D_MODEL]
    p = jnp.dot(h.astype(BF16), w_ref[0], preferred_element_type=F32)
    oa[0] = p[:, :WA]
    pb = p[:, WA:WA + WB]
    pd = p[:, WA + WB + WC:]
    if latent:
        ob[0, :, :384] = _rope(pb[:, :384], cb_ref[...], sb_ref[...], 16)
        ob[0, :, 384:] = pb[:, 384:]
        od[0, :, :512] = _rope(pd[:, :512], cd_ref[...], sd_ref[...], 8)
        od[0, :, 512:] = pd[:, 512:]
    else:
        ob[0] = pb
        od[0] = pd
    oc[0] = p[:, WA + WB:WA + WB + WC]


def _project(x, mod_l, g_pre, w_in_bf, layer, tables):
    b, t, _ = x.shape
    latent = tables is not None
    tm = ROW_TILE
    mod_idx = (lambda i, j: (1 + i, 0, 0)) if latent else (lambda i, j: (0, 0, 0))
    in_specs = [pl.BlockSpec((1, tm, D_MODEL), lambda i, j: (i, j, 0)),
                pl.BlockSpec((1, 1, 3 * D_MODEL), mod_idx),
                pl.BlockSpec((1, D_MODEL), lambda i, j: (0, 0)),
                pl.BlockSpec((1, D_MODEL, P_TOTAL), lambda i, j: (layer, 0, 0))]
    args = [x, mod_l, g_pre, w_in_bf]
    if latent:
        for tab in tables:
            in_specs.append(pl.BlockSpec((tm, tab.shape[1]), lambda i, j: (j, 0)))
            args.append(tab)
    widths = (WA, WB, WC, WD)
    return pl.pallas_call(
        functools.partial(_proj_kernel, latent=latent),
        grid=(b, t // tm),
        in_specs=in_specs,
        out_specs=[pl.BlockSpec((1, tm, w), lambda i, j: (i, j, 0)) for w in widths],
        out_shape=[jax.ShapeDtypeStruct((b, t, w), F32) for w in widths],
        compiler_params=_params(("parallel", "parallel")),
        name="proj",
    )(*args)


def _rope_tables(t, head_dim, n_lanes):
    half = head_dim // 2
    quarter = half // 2
    pos = jnp.arange(t)
    row = (pos // GRID_W).astype(F32)
    col = (pos % GRID_W).astype(F32)
    inv = ROPE_BASE ** (-jnp.arange(0, half, 2, dtype=F32) / half)
    lane = np.arange(n_lanes) % head_dim
    in_half = lane % half
    p = jnp.where(jnp.asarray(lane < half)[None, :], row[:, None], col[:, None])
    ang = p * inv[in_half % quarter][None, :]
    sign = jnp.asarray(np.where(in_half < quarter, -1.0, 1.0), F32)[None, :]
    return jnp.cos(ang), jnp.sin(ang) * sign


_NN = (((1,), (0,)), ((), ()))
_NT = (((1,), (1,)), ((), ()))
_TN = (((0,), (0,)), ((), ()))


def _split(x):
    hi = x.astype(BF16)
    return hi, (x - hi.astype(F32)).astype(BF16)


def _mm3(a, b, dims=_NN):
    d = lambda x, y: lax.dot_general(x, y, dims, preferred_element_type=F32)
    return d(a[0], b[0]) + (d(a[0], b[1]) + d(a[1], b[0]))


def _cat_rows(parts):
    return tuple(jnp.concatenate(p, axis=0) for p in zip(*parts))


def _rwkv_chunk(xc, w0, wup, a0, aup, k_k, k_a, s_ref, d, rev):
    c = xc.shape[0]
    n_h = W_BR // HEAD
    r = xc[:, 0:256]
    k = xc[:, 256:512]
    v = xc[:, 512:768]
    wd = xc[:, 768:832]
    ad = xc[:, 832:896]
    ones = _head_ones(W_BR)
    kk = k * k_k
    kk = kk * lax.rsqrt(_dot_hi(kk * kk, ones) + 1e-12)
    z = w0 + _dot_hi(jnp.tanh(wd), wup)
    e = jnp.exp(-_softplus(-z) - 0.5)
    a = _sigmoid(a0 + _dot_hi(ad, aup))
    kd = k * (1.0 + (a - 1.0) * k_a)
    alpha = -kk
    beta = kk * a

    ti = lax.broadcasted_iota(jnp.int32, (c, c), 0)
    si = lax.broadcasted_iota(jnp.int32, (c, c), 1)
    cum = (si >= ti) if rev else (si <= ti)
    l_incl = _dot_hi(jnp.where(cum, 1.0, 0.0).astype(F32), e)
    l_excl = l_incl - e
    l_tot = jnp.sum(e, axis=0, keepdims=True)
    a_t = alpha * jnp.exp(-l_excl)
    r_t = r * jnp.exp(-l_incl)
    grow = jnp.exp(l_incl)
    b_t = beta * grow
    k_t = kd * grow
    tail = jnp.exp(l_incl - l_tot)
    b_h = beta * tail
    k_h = kd * tail
    g_c = jnp.exp(-l_tot)

    n = n_h * c
    ri = lax.broadcasted_iota(jnp.int32, (n, n), 0)
    li = lax.broadcasted_iota(jnp.int32, (n, n), 1)
    same = (ri // c) == (li // HEAD)
    tt, ss = ri % c, li % c
    strict = same & ((ss > tt) if rev else (ss < tt))
    incl = same & ((ss >= tt) if rev else (ss <= tt))
    eye = ri == li

    tile = lambda x: jnp.concatenate([x] * n_h, axis=0)
    tile2 = lambda xs: tuple(tile(x) for x in xs)
    expand = lambda xs: tuple(jnp.where(same, tile(x), jnp.zeros((), x.dtype)) for x in xs)

    lhs = _cat_rows([expand(_split(a_t)), expand(_split(r_t))])
    rhs_t = _cat_rows([tile2(_split(b_t)), tile2(_split(k_t))])
    g4 = _mm3(lhs, rhs_t, _NT)
    nm = jnp.where(strict, g4[:n, :n], 0.0)
    mm = jnp.where(strict, g4[:n, n:], 0.0)
    pm = jnp.where(incl, g4[n:, :n], 0.0)
    qm = jnp.where(incl, g4[n:, n:], 0.0)

    inv = jnp.where(eye, 1.0, nm)
    nk = nm
    for _ in range(int(math.log2(c)) - 1):
        nks = _split(nk)
        nk = _mm3(nks, nks)
        inv = inv + _mm3(_split(inv), _split(nk))

    s0 = s_ref[d]
    as0 = _mm3(lhs, _split(s0), _NT)
    v_exp = expand(_split(v))
    u = _mm3(_split(inv), _split(as0[:n] + _mm3(_split(mm), v_exp)))
    uv = _cat_rows([_split(u), v_exp])
    pq = tuple(jnp.concatenate([x, y], axis=1) for x, y in zip(_split(pm), _split(qm)))
    y_exp = as0[n:] + _mm3(pq, uv)
    bk = _cat_rows([expand(_split(b_h)), expand(_split(k_h))])
    s_ref[d] = s0 * g_c + _mm3(uv, bk, _TN)
    y = y_exp[0:c]
    for h in range(1, n_h):
        y = y + y_exp[h * c:(h + 1) * c]
    return y


def _rwkv_kernel(*refs, latent):
    if latent:
        (xf_ref, xb_ref, w0_ref, wup_ref, a0_ref, aup_ref, kk_ref, ka_ref, s0_ref,
         yf_ref, yb_ref, s_scr) = refs
    else:
        (xf_ref, xb_ref, w0_ref, wup_ref, a0_ref, aup_ref, kk_ref, ka_ref,
         yf_ref, yb_ref, st_ref, s_scr) = refs
    ci = pl.program_id(1)
    n_h = W_BR // HEAD
    diag = lambda h: (slice(h * HEAD, (h + 1) * HEAD),) * 2

    @pl.when(ci == 0)
    def _():
        s_scr[...] = jnp.zeros(s_scr.shape, F32)
        if latent:
            for d in range(2):
                for h in range(n_h):
                    s_scr[(d,) + diag(h)] = s0_ref[0, 0, d, h]

    for d, (x_ref, y_ref) in enumerate(((xf_ref, yf_ref), (xb_ref, yb_ref))):
        y_ref[0] = _rwkv_chunk(x_ref[0], w0_ref[d:d + 1], wup_ref[d], a0_ref[d:d + 1], aup_ref[d],
                               kk_ref[...], ka_ref[...], s_scr, d, rev=(d == 1))

    if not latent:
        @pl.when(ci == pl.num_programs(1) - 1)
        def _():
            for d in range(2):
                for h in range(n_h):
                    st_ref[0, d, h] = s_scr[(d,) + diag(h)]


def _rwkv(pa, w0, wup, a0, aup, k_k, k_a, state, layer):
    b, t, _ = pa.shape
    latent = state is not None
    nc = t // CHUNK
    n_h = W_BR // HEAD
    full = lambda shape: pl.BlockSpec(shape, lambda i, j: (0,) * len(shape))
    in_specs = [pl.BlockSpec((1, CHUNK, WA), lambda i, j: (i, j, 0)),
                pl.BlockSpec((1, CHUNK, WA), lambda i, j: (i, nc - 1 - j, 0)),
                full((2, W_BR)), full((2, LORA, W_BR)), full((2, W_BR)), full((2, LORA, W_BR)),
                full((1, W_BR)), full((1, W_BR))]
    args = [pa, pa, w0, wup, a0, aup, k_k, k_a]
    out_specs = [pl.BlockSpec((1, CHUNK, W_BR), lambda i, j: (i, j, 0)),
                 pl.BlockSpec((1, CHUNK, W_BR), lambda i, j: (i, nc - 1 - j, 0))]
    out_shape = [jax.ShapeDtypeStruct((b, t, W_BR), F32)] * 2
    if latent:
        in_specs.append(pl.BlockSpec((1, 1, 2, n_h, HEAD, HEAD), lambda i, j: (i, layer, 0, 0, 0, 0)))
        args.append(state)
    else:
        out_specs.append(pl.BlockSpec((1, 2, n_h, HEAD, HEAD), lambda i, j: (i, 0, 0, 0, 0)))
        out_shape.append(jax.ShapeDtypeStruct((b, 2, n_h, HEAD, HEAD), F32))
    return pl.pallas_call(
        functools.partial(_rwkv_kernel, latent=latent),
        grid=(b, nc),
        in_specs=in_specs,
        out_specs=out_specs,
        out_shape=out_shape,
        scratch_shapes=[pltpu.VMEM((2, W_BR, W_BR), F32)],
        compiler_params=_params(("parallel", "arbitrary")),
        name="rwkv",
    )(*args)


def _lru_kernel(*refs, latent):
    if latent:
        x_ref, cw_ref, cb_ref, wa_ref, ba_ref, wx_ref, bx_ref, lam_ref, h0_ref, y_ref = refs
    else:
        x_ref, cw_ref, cb_ref, wa_ref, ba_ref, wx_ref, bx_ref, lam_ref, y_ref, st_ref = refs
    x = x_ref[0, :, :W_BR]
    t = x.shape[0]
    row = lax.broadcasted_iota(jnp.int32, x.shape, 0)

    def shift_dn(z, k, fill):
        return jnp.where(row >= k, pltpu.roll(z, k, 0), fill)

    def shift_up(z, k, fill):
        return jnp.where(row < t - k, pltpu.roll(z, t - k, 0), fill)

    cw = cw_ref[...]
    xc = (cb_ref[...] + shift_dn(x, 2, 0.0) * cw[0:1] + shift_dn(x, 1, 0.0) * cw[1:2]
          + x * cw[2:3] + shift_up(x, 1, 0.0) * cw[3:4])
    n_blk = W_BR // HEAD

    def block_proj(w_ref, d):
        return jnp.concatenate(
            [_dot_hi(xc[:, n * HEAD:(n + 1) * HEAD], w_ref[d, n]) for n in range(n_blk)], axis=-1)

    ys, finals = [], []
    for d in range(2):
        shift = shift_up if d == 1 else shift_dn
        gate_a = _sigmoid(block_proj(wa_ref, d) + ba_ref[d:d + 1])
        gate_x = _sigmoid(block_proj(wx_ref, d) + bx_ref[d:d + 1])
        log_a = -C_RG * gate_a * _softplus(-lam_ref[d:d + 1])
        a = jnp.exp(log_a)
        u = jnp.sqrt(-jnp.tanh(log_a) * (a * a + 1.0)) * (gate_x * xc)
        k = 1
        while k < t:
            u = a * shift(u, k, 0.0) + u
            a = a * shift(a, k, 1.0)
            k *= 2
        if latent:
            u = u + a * h0_ref[0, 0, d:d + 1]
        ys.append(u)
        finals.append(u[0:1] if d == 1 else u[t - 1:t])
    y_ref[0] = ys[0] + ys[1]
    if not latent:
        st_ref[0] = jnp.concatenate(finals, axis=0)


def _lru(pc, cw, cb, wa, ba, wx, bx, lam, state, layer):
    b, t, _ = pc.shape
    latent = state is not None
    n_blk = W_BR // HEAD
    full = lambda shape: pl.BlockSpec(shape, lambda i: (0,) * len(shape))
    in_specs = [pl.BlockSpec((1, t, WC), lambda i: (i, 0, 0)),
                full((4, W_BR)), full((1, W_BR)), full((2, n_blk, HEAD, HEAD)), full((2, W_BR)),
                full((2, n_blk, HEAD, HEAD)), full((2, W_BR)), full((2, W_BR))]
    args = [pc, cw, cb, wa, ba, wx, bx, lam]
    out_specs = [pl.BlockSpec((1, t, W_BR), lambda i: (i, 0, 0))]
    out_shape = [jax.ShapeDtypeStruct((b, t, W_BR), F32)]
    if latent:
        in_specs.append(pl.BlockSpec((1, 1, 2, W_BR), lambda i: (i, layer, 0, 0)))
        args.append(state)
    else:
        out_specs.append(pl.BlockSpec((1, 2, W_BR), lambda i: (i, 0, 0)))
        out_shape.append(jax.ShapeDtypeStruct((b, 2, W_BR), F32))
    return pl.pallas_call(
        functools.partial(_lru_kernel, latent=latent),
        grid=(b,),
        in_specs=in_specs,
        out_specs=out_specs,
        out_shape=out_shape,
        compiler_params=_params(("parallel",)),
        name="lru",
    )(*args)


def _softmax_pv(blocks, sink):
    m = None
    for s, _ in blocks:
        mi = jnp.max(s, axis=-1, keepdims=True)
        m = mi if m is None else jnp.maximum(m, mi)
    if sink is not None:
        m = jnp.maximum(m, sink)
    den = None if sink is None else jnp.exp(sink - m)
    acc = None
    for s, v in blocks:
        e = jnp.exp(s - m)
        di = jnp.sum(e, axis=-1, keepdims=True)
        den = di if den is None else den + di
        o = _dot_bf(e, v)
        acc = o if acc is None else acc + o
    return acc, den


def _win_kernel(*refs, latent):
    scale = HEAD ** -0.5
    if latent:
        sink_ref, q_ref, kp_ref, kc_ref, kn_ref, ck_ref, cv_ref, o_ref = refs
        j = pl.program_id(1)
        nq = pl.num_programs(1)
        qi = lax.broadcasted_iota(jnp.int32, (Q_BLK, Q_BLK), 0)
        ki = lax.broadcasted_iota(jnp.int32, (Q_BLK, Q_BLK), 1)
        kv_blocks = [(kp_ref[0], (ki >= qi) & (j > 0)), (kc_ref[0], None),
                     (kn_ref[0], (ki <= qi) & (j < nq - 1))]
        ctx_k, ctx_v = ck_ref[0, 0], cv_ref[0, 0]
    else:
        sink_ref, x_ref, o_ref = refs
    q = q_ref[0] if latent else x_ref[0, :, :256]
    outs = []
    for h in range(2):
        hs = slice(h * HEAD, (h + 1) * HEAD)
        for g in range(2):
            qg = q[:, (2 * h + g) * HEAD:(2 * h + g + 1) * HEAD]
            blocks = []
            if latent:
                for kv, mask in kv_blocks:
                    s = _dot_nt_bf(qg, kv[:, hs]) * scale
                    if mask is not None:
                        s = jnp.where(mask, s, NEG_INF)
                    blocks.append((s, kv[:, 128 + h * HEAD:128 + (h + 1) * HEAD]))
                blocks.append((_dot_nt_bf(qg, ctx_k[:, hs]) * scale, ctx_v[:, hs]))
            else:
                k = x_ref[0, :, 256 + h * HEAD:256 + (h + 1) * HEAD]
                v = x_ref[0, :, 384 + h * HEAD:384 + (h + 1) * HEAD]
                blocks.append((_dot_nt_bf(qg, k) * scale, v))
            acc, den = _softmax_pv(blocks, sink_ref[2 * h + g])
            outs.append(acc / den)
    o_ref[0] = jnp.concatenate(outs, axis=-1)


def _win(pb, sink, cache_k, cache_v, layer):
    b, t, _ = pb.shape
    latent = cache_k is not None
    smem = pl.BlockSpec(memory_space=pltpu.SMEM)
    if latent:
        nq = t // Q_BLK
        grid = (b, nq)
        in_specs = [smem,
                    pl.BlockSpec((1, Q_BLK, 256), lambda i, j: (i, j, 0)),
                    pl.BlockSpec((1, Q_BLK, 256), lambda i, j: (i, jnp.maximum(j - 1, 0), 1)),
                    pl.BlockSpec((1, Q_BLK, 256), lambda i, j: (i, j, 1)),
                    pl.BlockSpec((1, Q_BLK, 256), lambda i, j: (i, jnp.minimum(j + 1, nq - 1), 1)),
                    pl.BlockSpec((1, 1) + cache_k.shape[2:], lambda i, j: (i, layer, 0, 0)),
                    pl.BlockSpec((1, 1) + cache_v.shape[2:], lambda i, j: (i, layer, 0, 0))]
        args = [sink, pb, pb, pb, pb, cache_k, cache_v]
        out_spec = pl.BlockSpec((1, Q_BLK, W_BR), lambda i, j: (i, j, 0))
        sem = ("parallel", "parallel")
    else:
        grid = (b,)
        in_specs = [smem, pl.BlockSpec((1, t, WB), lambda i: (i, 0, 0))]
        args = [sink, pb]
        out_spec = pl.BlockSpec((1, t, W_BR), lambda i: (i, 0, 0))
        sem = ("parallel",)
    return pl.pallas_call(
        functools.partial(_win_kernel, latent=latent),
        grid=grid,
        in_specs=in_specs,
        out_specs=out_spec,
        out_shape=jax.ShapeDtypeStruct((b, t, W_BR), F32),
        compiler_params=_params(sem),
        name="win",
    )(*args)


def _diff_kernel(*refs, latent, lam_init):
    if latent:
        lam_ref, g_ref, q_ref, k_ref, v_ref, ck_ref, cv_ref, o_ref = refs
    else:
        lam_ref, g_ref, q_ref, k_ref, v_ref, o_ref = refs
    scale = DQ_D ** -0.5
    lp = lam_ref[...]
    lam = (jnp.exp(jnp.sum(lp[0:1] * lp[1:2], axis=-1, keepdims=True))
           - jnp.exp(jnp.sum(lp[2:3] * lp[3:4], axis=-1, keepdims=True)) + lam_init)
    q, k, v = q_ref[0], k_ref[0], v_ref[0]
    key_sets = [(k, v)]
    if latent:
        key_sets.append((ck_ref[0, 0], cv_ref[0, 0]))
    outs = []
    for h in range(W_BR // HEAD):
        hs = slice(h * HEAD, (h + 1) * HEAD)
        probs = []
        for m in range(2):
            ms = slice(h * HEAD + m * DQ_D, h * HEAD + (m + 1) * DQ_D)
            logits = [_dot_nt_bf(q[:, ms], kk[:, ms]) * scale for kk, _ in key_sets]
            mx = functools.reduce(jnp.maximum, [jnp.max(s, axis=-1, keepdims=True) for s in logits])
            es = [jnp.exp(s - mx) for s in logits]
            den = functools.reduce(lambda a, b: a + b, [jnp.sum(e, axis=-1, keepdims=True) for e in es])
            inv = 1.0 / den
            probs.append([e * inv for e in es])
        o = None
        for i, (_, vv) in enumerate(key_sets):
            oi = _dot_bf(probs[0][i] - lam * probs[1][i], vv[:, hs])
            o = oi if o is None else o + oi
        o = o * lax.rsqrt(jnp.mean(o * o, -1, keepdims=True) + NORM_EPS) * g_ref[...]
        outs.append(o * (1.0 - lam_init))
    o_ref[0] = jnp.concatenate(outs, axis=-1)


def _diff(pd, lam_p, subln_g, cache_k, cache_v, layer, lam_init):
    b, t, _ = pd.shape
    latent = cache_k is not None
    tq = 256
    in_specs = [pl.BlockSpec((4, DQ_D), lambda i, j: (0, 0)),
                pl.BlockSpec((1, HEAD), lambda i, j: (0, 0)),
                pl.BlockSpec((1, tq, 256), lambda i, j: (i, j, 0)),
                pl.BlockSpec((1, t, 256), lambda i, j: (i, 0, 1)),
                pl.BlockSpec((1, t, 256), lambda i, j: (i, 0, 2))]
    args = [lam_p, subln_g, pd, pd, pd]
    if latent:
        in_specs += [pl.BlockSpec((1, 1) + cache_k.shape[2:], lambda i, j: (i, layer, 0, 0)),
                     pl.BlockSpec((1, 1) + cache_v.shape[2:], lambda i, j: (i, layer, 0, 0))]
        args += [cache_k, cache_v]
    return pl.pallas_call(
        functools.partial(_diff_kernel, latent=latent, lam_init=lam_init),
        grid=(b, t // tq),
        in_specs=in_specs,
        out_specs=pl.BlockSpec((1, tq, W_BR), lambda i, j: (i, j, 0)),
        out_shape=jax.ShapeDtypeStruct((b, t, W_BR), F32),
        compiler_params=_params(("parallel", "parallel")),
        name="diff",
    )(*args)


def _out_kernel(x_ref, mod_ref, gpost_ref, w_ref, pa_ref, yf_ref, yb_ref, gng_ref, gnb_ref, rk_ref,
                ywin_ref, bg_ref, ylru_ref, cg_ref, ydiff_ref, dg_ref, o_ref):
    ones = _head_ones(W_BR)
    pa = pa_ref[0]
    r, k, v, ag = pa[:, 0:256], pa[:, 256:512], pa[:, 512:768], pa[:, 896:1152]
    y = yf_ref[0] + yb_ref[0]
    mu = _dot_hi(y, ones) * (1.0 / HEAD)
    yc = y - mu
    var = _dot_hi(yc * yc, ones) * (1.0 / HEAD)
    ya = yc * lax.rsqrt(var + GN_EPS) * gng_ref[...] + gnb_ref[...]
    ya = ya + _dot_hi(r * k * rk_ref[...], ones) * v
    mix = jnp.concatenate([ya * _silu(ag), ywin_ref[0] * _silu(bg_ref[0]),
                           ylru_ref[0] * _silu(cg_ref[0]), ydiff_ref[0] * _silu(dg_ref[0])], axis=-1)
    o = jnp.dot(mix.astype(BF16), w_ref[0], preferred_element_type=F32)
    o = o * lax.rsqrt(jnp.mean(o * o, -1, keepdims=True) + NORM_EPS) * gpost_ref[...]
    o_ref[0] = x_ref[0] + mod_ref[0][:, 2 * D_MODEL:] * o


def _out(x, mod_l, g_post, w_out_bf, layer, latent, pa, yf, yb, gn_g, gn_b, r_k, ywin, pb, ylru, pc, ydiff, pd):
    b, t, _ = x.shape
    tm = ROW_TILE
    mod_idx = (lambda i, j: (1 + i, 0, 0)) if latent else (lambda i, j: (0, 0, 0))
    rows = lambda w, c=0: pl.BlockSpec((1, tm, w), lambda i, j: (i, j, c))
    vec = pl.BlockSpec((1, W_BR), lambda i, j: (0, 0))
    in_specs = [rows(D_MODEL),
                pl.BlockSpec((1, 1, 3 * D_MODEL), mod_idx),
                pl.BlockSpec((1, D_MODEL), lambda i, j: (0, 0)),
                pl.BlockSpec((1, D_MODEL, D_MODEL), lambda i, j: (layer, 0, 0)),
                rows(WA), rows(W_BR), rows(W_BR), vec, vec, vec,
                rows(W_BR), rows(W_BR, 2), rows(W_BR), rows(W_BR, 1), rows(W_BR), rows(W_BR, 3)]
    return pl.pallas_call(
        _out_kernel,
        grid=(b, t // tm),
        in_specs=in_specs,
        out_specs=rows(D_MODEL),
        out_shape=jax.ShapeDtypeStruct((b, t, D_MODEL), F32),
        compiler_params=_params(("parallel", "parallel")),
        name="out",
    )(x, mod_l, g_post, w_out_bf, pa, yf, yb, gn_g, gn_b, r_k, ywin, pb, ylru, pc, ydiff, pd)


def _layer(x, mod_l, layer, lam_init, wts, cache, tables):
    latent = cache is not None
    row = lambda a: a[layer].reshape(1, -1)
    pa, pb, pc, pd = _project(x, mod_l, row(wts['g_pre']), wts['w_in_bf'], layer, tables)
    rw = _rwkv(pa, wts['rwkv_w0'][layer], wts['rwkv_w_up'][layer], wts['rwkv_a0'][layer],
               wts['rwkv_a_up'][layer], row(wts['rwkv_k_k']), row(wts['rwkv_k_a']),
               cache['rwkv'] if latent else None, layer)
    lr = _lru(pc, wts['lru_conv_w'][layer], row(wts['lru_conv_b']), wts['lru_wa'][layer],
              wts['lru_ba'][layer], wts['lru_wx'][layer], wts['lru_bx'][layer], wts['lru_lambda'][layer],
              cache['lru'] if latent else None, layer)
    ywin = _win(pb, wts['win_sink'][layer], cache['win_k'] if latent else None,
                cache['win_v'] if latent else None, layer)
    ydiff = _diff(pd, wts['diff_lambda'][layer], row(wts['diff_subln_g']),
                  cache['diff_k'] if latent else None, cache['diff_v'] if latent else None, layer, lam_init)
    y = _out(x, mod_l, row(wts['g_post']), wts['w_out_bf'], layer, latent, pa, rw[0], rw[1],
             row(wts['rwkv_gn_g']), row(wts['rwkv_gn_b']), row(wts['rwkv_r_k']),
             ywin, pb, lr[0], pc, ydiff, pd)
    new_cache = None if latent else (pb, pd, rw[2], lr[1])
    return y, new_cache


def kernel(x_prompt, x_sample, c, cache_win_k, cache_win_v, cache_diff_k, cache_diff_v, state_rwkv, state_lru,
           c_ctx, w_mod, b_mod, g_pre, g_post, w_in, w_out,
           rwkv_w0, rwkv_w_up, rwkv_a0, rwkv_a_up, rwkv_k_k, rwkv_k_a, rwkv_r_k, rwkv_gn_g, rwkv_gn_b,
           win_sink, lru_conv_w, lru_conv_b, lru_wa, lru_ba, lru_wx, lru_bx, lru_lambda,
           diff_lambda, diff_subln_g):
    n_b, seq = x_prompt.shape[:2]
    n_dec, dec_seq = x_sample.shape[:2]
    past = cache_win_k.shape[2]
    wts = dict(g_pre=g_pre, g_post=g_post, w_in_bf=w_in.astype(BF16), w_out_bf=w_out.astype(BF16),
               rwkv_w0=rwkv_w0, rwkv_w_up=rwkv_w_up, rwkv_a0=rwkv_a0, rwkv_a_up=rwkv_a_up,
               rwkv_k_k=rwkv_k_k, rwkv_k_a=rwkv_k_a, rwkv_r_k=rwkv_r_k, rwkv_gn_g=rwkv_gn_g,
               rwkv_gn_b=rwkv_gn_b, win_sink=win_sink, lru_conv_w=lru_conv_w, lru_conv_b=lru_conv_b,
               lru_wa=lru_wa, lru_ba=lru_ba, lru_wx=lru_wx, lru_bx=lru_bx, lru_lambda=lru_lambda,
               diff_lambda=diff_lambda, diff_subln_g=diff_subln_g)
    cache = dict(win_k=cache_win_k.reshape(n_dec, DEPTH, past, 128),
                 win_v=cache_win_v.reshape(n_dec, DEPTH, past, 128),
                 diff_k=cache_diff_k.reshape(n_dec, DEPTH, past, 256),
                 diff_v=cache_diff_v.reshape(n_dec, DEPTH, past, 256),
                 rwkv=state_rwkv, lru=state_lru)
    cvec = jnp.concatenate([c_ctx[None], c, jnp.zeros((8 - 1 - n_dec, D_MODEL), F32)], axis=0)
    mod = _modulation(cvec, w_mod, b_mod)
    cos_b, sin_b = _rope_tables(dec_seq, HEAD, 384)
    cos_d, sin_d = _rope_tables(dec_seq, DQ_D, 512)
    tables = (cos_b, sin_b, cos_d, sin_d)

    y_p, y_s = x_prompt, x_sample
    ctx = []
    for l in range(DEPTH):
        lam_init = 0.8 - 0.6 * math.exp(-0.3 * l)
        mod_l = mod[l].reshape(8, 1, 3 * D_MODEL)
        y_p, nc = _layer(y_p, mod_l, l, lam_init, wts, None, None)
        ctx.append(nc)
        y_s, _ = _layer(y_s, mod_l, l, lam_init, wts, cache, tables)
    stack = lambda f: jnp.stack([f(ct) for ct in ctx], axis=1)
    new_win_k = stack(lambda ct: ct[0][:, :, 256:384].reshape(n_b, seq, 2, HEAD))
    new_win_v = stack(lambda ct: ct[0][:, :, 384:512].reshape(n_b, seq, 2, HEAD))
    new_diff_k = stack(lambda ct: ct[1][:, :, 256:512].reshape(n_b, seq, 4, 2, DQ_D))
    new_diff_v = stack(lambda ct: ct[1][:, :, 512:768].reshape(n_b, seq, 4, HEAD))
    new_state_rwkv = stack(lambda ct: ct[2])
    new_state_lru = stack(lambda ct: ct[3])
    return (y_p, y_s, new_win_k, new_win_v, new_diff_k, new_diff_v, new_state_rwkv, new_state_lru)
```

```python
import functools
import math

import numpy as np
import jax
import jax.numpy as jnp
from jax import lax
from jax.experimental import pallas as pl
from jax.experimental.pallas import tpu as pltpu

F32 = jnp.float32
BF16 = jnp.bfloat16
HI = lax.Precision.HIGHEST

D_MODEL = 1024
DEPTH = 2
GRID_W = 64
ROPE_BASE = 10000.0
NORM_EPS = 1e-6
NEG_INF = -1e30
GN_EPS = 64e-5
C_RG = 8.0
W_BR = 256
HEAD = 64
LORA = 64
DQ_D = 32
WINDOW = 128
Q_BLK = 128
P_TOTAL = 3456
WA, WB, WC, WD = 1152, 768, 512, 1024
CHUNK = 64
RWKV_BATCH = 2
ROW_TILE = 256
VMEM_LIMIT = 48 * 1024 * 1024


def _dot_hi(a, b):
    return jnp.dot(a, b, precision=HI, preferred_element_type=F32)


def _dot_nt_hi(a, b):
    return lax.dot_general(a, b, (((1,), (1,)), ((), ())), precision=HI, preferred_element_type=F32)


def _dot_tn_hi(a, b):
    return lax.dot_general(a, b, (((0,), (0,)), ((), ())), precision=HI, preferred_element_type=F32)


def _dot_bf(a, b):
    return jnp.dot(a.astype(BF16), b.astype(BF16), preferred_element_type=F32)


def _dot_nt_bf(a, b):
    return lax.dot_general(a.astype(BF16), b.astype(BF16), (((1,), (1,)), ((), ())),
                           preferred_element_type=F32)


def _sigmoid(x):
    return 1.0 / (1.0 + jnp.exp(-x))


def _silu(x):
    return x * _sigmoid(x)


def _softplus(x):
    return jnp.maximum(x, 0.0) + jnp.log1p(jnp.exp(-jnp.abs(x)))


def _head_ones(n):
    r = lax.broadcasted_iota(jnp.int32, (n, n), 0) // HEAD
    c = lax.broadcasted_iota(jnp.int32, (n, n), 1) // HEAD
    return jnp.where(r == c, 1.0, 0.0).astype(F32)


def _params(sem):
    return pltpu.CompilerParams(dimension_semantics=sem, vmem_limit_bytes=VMEM_LIMIT)


def _mod_kernel(c_ref, w_ref, b_ref, o_ref):
    o_ref[0] = _dot_hi(_silu(c_ref[...]), w_ref[0]) + b_ref[0]


def _modulation(cvec, w_mod, b_mod):
    n_l = w_mod.shape[0]
    tn = 512
    return pl.pallas_call(
        _mod_kernel,
        grid=(n_l, 3 * D_MODEL // tn),
        in_specs=[pl.BlockSpec((8, D_MODEL), lambda l, j: (0, 0)),
                  pl.BlockSpec((1, D_MODEL, tn), lambda l, j: (l, 0, j)),
                  pl.BlockSpec((1, 1, tn), lambda l, j: (l, 0, j))],
        out_specs=pl.BlockSpec((1, 8, tn), lambda l, j: (l, 0, j)),
        out_shape=jax.ShapeDtypeStruct((n_l, 8, 3 * D_MODEL), F32),
        compiler_params=_params(("parallel", "parallel")),
        name="mod",
    )(cvec, w_mod, b_mod.reshape(n_l, 1, 3 * D_MODEL))


def _rope(x, cos, sin_signed, off):
    w = x.shape[-1]
    lane = lax.broadcasted_iota(jnp.int32, x.shape, 1)
    first = (lane % (2 * off)) < off
    partner = jnp.where(first, pltpu.roll(x, w - off, 1), pltpu.roll(x, off, 1))
    return x * cos + partner * sin_signed


def _proj_kernel(*refs, latent):
    if latent:
        x_ref, mod_ref, g_ref, w_ref, cb_ref, sb_ref, cd_ref, sd_ref, oa, ob, oc, od = refs
    else:
        x_ref, mod_ref, g_ref, w_ref, oa, ob, oc, od = refs
    x = x_ref[0]
    y = x * lax.rsqrt(jnp.mean(x * x, -1, keepdims=True) + NORM_EPS) * g_ref[...]
    m = mod_ref[0]
    h = y * (1.0 + m[:, D_MODEL:2 * D_MODEL]) + m[:, :D_MODEL]
    p = jnp.dot(h.astype(BF16), w_ref[0], preferred_element_type=F32)
    oa[0] = p[:, :WA]
    pb = p[:, WA:WA + WB]
    pd = p[:, WA + WB + WC:]
    if latent:
        ob[0, :, :384] = _rope(pb[:, :384], cb_ref[...], sb_ref[...], 16)
        ob[0, :, 384:] = pb[:, 384:]
        od[0, :, :512] = _rope(pd[:, :512], cd_ref[...], sd_ref[...], 8)
        od[0, :, 512:] = pd[:, 512:]
    else:
        ob[0] = pb
        od[0] = pd
    oc[0] = p[:, WA + WB:WA + WB + WC]


def _project(x, mod_l, g_pre, w_in_bf, layer, tables):
    b, t, _ = x.shape
    latent = tables is not None
    tm = ROW_TILE
    mod_idx = (lambda i, j: (1 + i, 0, 0)) if latent else (lambda i, j: (0, 0, 0))
    in_specs = [pl.BlockSpec((1, tm, D_MODEL), lambda i, j: (i, j, 0)),
                pl.BlockSpec((1, 1, 3 * D_MODEL), mod_idx),
                pl.BlockSpec((1, D_MODEL), lambda i, j: (0, 0)),
                pl.BlockSpec((1, D_MODEL, P_TOTAL), lambda i, j: (layer, 0, 0))]
    args = [x, mod_l, g_pre, w_in_bf]
    if latent:
        for tab in tables:
            in_specs.append(pl.BlockSpec((tm, tab.shape[1]), lambda i, j: (j, 0)))
            args.append(tab)
    widths = (WA, WB, WC, WD)
    return pl.pallas_call(
        functools.partial(_proj_kernel, latent=latent),
        grid=(b, t // tm),
        in_specs=in_specs,
        out_specs=[pl.BlockSpec((1, tm, w), lambda i, j: (i, j, 0)) for w in widths],
        out_shape=[jax.ShapeDtypeStruct((b, t, w), F32) for w in widths],
        compiler_params=_params(("parallel", "parallel")),
        name="proj",
    )(*args)


def _rope_tables(t, head_dim, n_lanes):
    half = head_dim // 2
    quarter = half // 2
    pos = jnp.arange(t)
    row = (pos // GRID_W).astype(F32)
    col = (pos % GRID_W).astype(F32)
    inv = ROPE_BASE ** (-jnp.arange(0, half, 2, dtype=F32) / half)
    lane = np.arange(n_lanes) % head_dim
    in_half = lane % half
    p = jnp.where(jnp.asarray(lane < half)[None, :], row[:, None], col[:, None])
    ang = p * inv[in_half % quarter][None, :]
    sign = jnp.asarray(np.where(in_half < quarter, -1.0, 1.0), F32)[None, :]
    return jnp.cos(ang), jnp.sin(ang) * sign


_NN = (((1,), (0,)), ((), ()))
_NT = (((1,), (1,)), ((), ()))
_TN = (((0,), (0,)), ((), ()))


def _split(x):
    hi = x.astype(BF16)
    return hi, (x - hi.astype(F32)).astype(BF16)


def _mm3(a, b, dims=_NN):
    d = lambda x, y: lax.dot_general(x, y, dims, preferred_element_type=F32)
    return d(a[0], b[0]) + (d(a[0], b[1]) + d(a[1], b[0]))


def _cat_rows(parts):
    return tuple(jnp.concatenate(p, axis=0) for p in zip(*parts))


def _bf(x):
    return x.astype(BF16)


def _dotf(x, y, dims=_NN):
    return lax.dot_general(x, y, dims, preferred_element_type=F32)


def _rwkv_prepare(xc, w0, wup, a0, aup, k_k, k_a, m_ref, rev):
    c = xc.shape[0]
    n_h = W_BR // HEAD
    r = xc[:, 0:256]
    k = xc[:, 256:512]
    v = xc[:, 512:768]
    wd = xc[:, 768:832]
    ad = xc[:, 832:896]
    ones = _head_ones(W_BR)
    kk = k * k_k
    kk = kk * lax.rsqrt(_dot_hi(kk * kk, ones) + 1e-12)
    z = w0 + _dot_hi(jnp.tanh(wd), wup)
    e = jnp.exp(-_softplus(-z) - 0.5)
    a = _sigmoid(a0 + _dot_hi(ad, aup))
    kd = k * (1.0 + (a - 1.0) * k_a)
    alpha = -kk
    beta = kk * a

    ti = lax.broadcasted_iota(jnp.int32, (c, c), 0)
    si = lax.broadcasted_iota(jnp.int32, (c, c), 1)
    cum = (si >= ti) if rev else (si <= ti)
    l_incl = _dot_hi(jnp.where(cum, 1.0, 0.0).astype(F32), e)
    l_excl = l_incl - e
    l_tot = jnp.sum(e, axis=0, keepdims=True)
    grow = jnp.exp(l_incl)
    tail = jnp.exp(l_incl - l_tot)

    n = n_h * c
    same = m_ref[_M_SAME]
    strict = m_ref[_M_STRICT + 2 * int(rev)]
    incl = m_ref[_M_INCL + 2 * int(rev)]
    tile = lambda x: jnp.concatenate([x] * n_h, axis=0)
    expand = lambda x: tile(x) * same
    lhs = jnp.concatenate([expand(_bf(alpha * jnp.exp(-l_excl))),
                           expand(_bf(r * jnp.exp(-l_incl)))], axis=0)
    rhs_t = jnp.concatenate([tile(_bf(beta * grow)), tile(_bf(kd * grow))], axis=0)
    g4 = _dotf(lhs, rhs_t, _NT)
    v_hi, v_lo = _split(v)
    return dict(
        lhs=lhs, nm=_bf(g4[:n, :n]) * strict, mm=_bf(g4[:n, n:]) * strict,
        pq=jnp.concatenate([_bf(g4[n:, :n]) * incl, _bf(g4[n:, n:]) * incl], axis=1),
        v_hi=expand(v_hi), v_lo=expand(v_lo),
        bk=_cat_rows([tuple(expand(x) for x in _split(beta * tail)),
                      tuple(expand(x) for x in _split(kd * tail))]),
        g_c=jnp.exp(-l_tot))


def _rwkv_finish(preps, invs, s_ref):
    n = invs[0].shape[0]
    c = n // (W_BR // HEAD)
    s0 = [s_ref[g] for g in range(len(preps))]
    as0 = [_dotf(p['lhs'], _bf(s), _NT) for p, s in zip(preps, s0)]
    rhs = [_bf(a[:n] + _dotf(p['mm'], p['v_hi'])) for p, a in zip(preps, as0)]
    u = [_split(_dotf(_bf(t), x)) for t, x in zip(invs, rhs)]
    uv = [(jnp.concatenate([uh, p['v_hi']], axis=0), jnp.concatenate([ul, p['v_lo']], axis=0))
          for p, (uh, ul) in zip(preps, u)]
    ys = []
    for g, p in enumerate(preps):
        s_ref[g] = s0[g] * p['g_c'] + _mm3(uv[g], p['bk'], _TN)
        y_exp = as0[g][n:] + _dotf(p['pq'], uv[g][0])
        y = y_exp[0:c]
        for h in range(1, n // c):
            y = y + y_exp[h * c:(h + 1) * c]
        ys.append(y)
    return ys


_M_SAME, _M_STRICT, _M_INCL, _M_EYE, _M_PAIR, _M_OFF = 0, 1, 2, 5, 6, 7


def _rwkv_masks(c, n_h):
    n = c * n_h
    r = np.arange(n)[:, None]
    l = np.arange(n)[None, :]
    same = (r // c) == (l // c)
    t, s = r % c, l % c
    masks = [same, same & (s < t), same & (s <= t), same & (s > t), same & (s >= t),
             r == l, ((r // 2) == (l // 2)) & (r != l)]
    b = 2
    while b < c:
        masks.append(((r // (2 * b)) == (l // (2 * b))) & ((r // b) != (l // b)))
        b *= 2
    return jnp.asarray(np.stack(masks).astype(np.float32), BF16)


def _rwkv_kernel(*refs, latent, n_b):
    if latent:
        (xf_ref, xb_ref, w0_ref, wup_ref, a0_ref, aup_ref, kk_ref, ka_ref, m_ref, s0_ref,
         yf_ref, yb_ref, s_scr) = refs
    else:
        (xf_ref, xb_ref, w0_ref, wup_ref, a0_ref, aup_ref, kk_ref, ka_ref, m_ref,
         yf_ref, yb_ref, st_ref, s_scr) = refs
    ci = pl.program_id(1)
    n_h = W_BR // HEAD
    diag = lambda h: (slice(h * HEAD, (h + 1) * HEAD),) * 2
    streams = [(bi, d) for bi in range(n_b) for d in range(2)]

    @pl.when(ci == 0)
    def _():
        s_scr[...] = jnp.zeros(s_scr.shape, F32)
        if latent:
            for g, (bi, d) in enumerate(streams):
                for h in range(n_h):
                    s_scr[(g,) + diag(h)] = s0_ref[bi, 0, d, h]

    x_refs = (xf_ref, xb_ref)
    preps = [_rwkv_prepare(x_refs[d][bi], w0_ref[d:d + 1], wup_ref[d], a0_ref[d:d + 1], aup_ref[d],
                           kk_ref[...], ka_ref[...], m_ref, rev=(d == 1)) for bi, d in streams]
    invs = [(p['nm'] * m_ref[_M_PAIR]).astype(F32) + m_ref[_M_EYE].astype(F32) for p in preps]
    for lvl in range(int(math.log2(CHUNK)) - 1):
        inv_b = [_bf(t) for t in invs]
        half = [_bf(_dotf(p['nm'] * m_ref[_M_OFF + lvl], tb)) for p, tb in zip(preps, inv_b)]
        invs = [t + _dotf(tb, hb) for t, tb, hb in zip(invs, inv_b, half)]
    ys = _rwkv_finish(preps, invs, s_scr)
    y_refs = (yf_ref, yb_ref)
    for y, (bi, d) in zip(ys, streams):
        y_refs[d][bi] = y

    if not latent:
        @pl.when(ci == pl.num_programs(1) - 1)
        def _():
            for g, (bi, d) in enumerate(streams):
                for h in range(n_h):
                    st_ref[bi, d, h] = s_scr[(g,) + diag(h)]


def _rwkv(pa, w0, wup, a0, aup, k_k, k_a, state, layer):
    b, t, _ = pa.shape
    latent = state is not None
    nc = t // CHUNK
    n_h = W_BR // HEAD
    n_b = RWKV_BATCH
    masks = _rwkv_masks(CHUNK, n_h)
    full = lambda shape: pl.BlockSpec(shape, lambda i, j: (0,) * len(shape))
    in_specs = [pl.BlockSpec((n_b, CHUNK, WA), lambda i, j: (i, j, 0)),
                pl.BlockSpec((n_b, CHUNK, WA), lambda i, j: (i, nc - 1 - j, 0)),
                full((2, W_BR)), full((2, LORA, W_BR)), full((2, W_BR)), full((2, LORA, W_BR)),
                full((1, W_BR)), full((1, W_BR)), full(masks.shape)]
    args = [pa, pa, w0, wup, a0, aup, k_k, k_a, masks]
    out_specs = [pl.BlockSpec((n_b, CHUNK, W_BR), lambda i, j: (i, j, 0)),
                 pl.BlockSpec((n_b, CHUNK, W_BR), lambda i, j: (i, nc - 1 - j, 0))]
    out_shape = [jax.ShapeDtypeStruct((b, t, W_BR), F32)] * 2
    if latent:
        in_specs.append(pl.BlockSpec((n_b, 1, 2, n_h, HEAD, HEAD), lambda i, j: (i, layer, 0, 0, 0, 0)))
        args.append(state)
    else:
        out_specs.append(pl.BlockSpec((n_b, 2, n_h, HEAD, HEAD), lambda i, j: (i, 0, 0, 0, 0)))
        out_shape.append(jax.ShapeDtypeStruct((b, 2, n_h, HEAD, HEAD), F32))
    return pl.pallas_call(
        functools.partial(_rwkv_kernel, latent=latent, n_b=n_b),
        grid=(b // n_b, nc),
        in_specs=in_specs,
        out_specs=out_specs,
        out_shape=out_shape,
        scratch_shapes=[pltpu.VMEM((2 * n_b, W_BR, W_BR), F32)],
        compiler_params=_params(("parallel", "arbitrary")),
        name="rwkv",
    )(*args)


def _lru_kernel(*refs, latent):
    if latent:
        x_ref, cw_ref, cb_ref, wa_ref, ba_ref, wx_ref, bx_ref, lam_ref, h0_ref, y_ref = refs
    else:
        x_ref, cw_ref, cb_ref, wa_ref, ba_ref, wx_ref, bx_ref, lam_ref, y_ref, st_ref = refs
    x = x_ref[0, :, :W_BR]
    t = x.shape[0]
    row = lax.broadcasted_iota(jnp.int32, x.shape, 0)

    def shift_dn(z, k, fill):
        return jnp.where(row >= k, pltpu.roll(z, k, 0), fill)

    def shift_up(z, k, fill):
        return jnp.where(row < t - k, pltpu.roll(z, t - k, 0), fill)

    cw = cw_ref[...]
    xc = (cb_ref[...] + shift_dn(x, 2, 0.0) * cw[0:1] + shift_dn(x, 1, 0.0) * cw[1:2]
          + x * cw[2:3] + shift_up(x, 1, 0.0) * cw[3:4])
    n_blk = W_BR // HEAD

    def block_proj(w_ref, d):
        return jnp.concatenate(
            [_dot_hi(xc[:, n * HEAD:(n + 1) * HEAD], w_ref[d, n]) for n in range(n_blk)], axis=-1)

    ys, finals = [], []
    for d in range(2):
        shift = shift_up if d == 1 else shift_dn
        gate_a = _sigmoid(block_proj(wa_ref, d) + ba_ref[d:d + 1])
        gate_x = _sigmoid(block_proj(wx_ref, d) + bx_ref[d:d + 1])
        log_a = -C_RG * gate_a * _softplus(-lam_ref[d:d + 1])
        a = jnp.exp(log_a)
        u = jnp.sqrt(-jnp.tanh(log_a) * (a * a + 1.0)) * (gate_x * xc)
        k = 1
        while k < t:
            u = a * shift(u, k, 0.0) + u
            a = a * shift(a, k, 1.0)
            k *= 2
        if latent:
            u = u + a * h0_ref[0, 0, d:d + 1]
        ys.append(u)
        finals.append(u[0:1] if d == 1 else u[t - 1:t])
    y_ref[0] = ys[0] + ys[1]
    if not latent:
        st_ref[0] = jnp.concatenate(finals, axis=0)


def _lru(pc, cw, cb, wa, ba, wx, bx, lam, state, layer):
    b, t, _ = pc.shape
    latent = state is not None
    n_blk = W_BR // HEAD
    full = lambda shape: pl.BlockSpec(shape, lambda i: (0,) * len(shape))
    in_specs = [pl.BlockSpec((1, t, WC), lambda i: (i, 0, 0)),
                full((4, W_BR)), full((1, W_BR)), full((2, n_blk, HEAD, HEAD)), full((2, W_BR)),
                full((2, n_blk, HEAD, HEAD)), full((2, W_BR)), full((2, W_BR))]
    args = [pc, cw, cb, wa, ba, wx, bx, lam]
    out_specs = [pl.BlockSpec((1, t, W_BR), lambda i: (i, 0, 0))]
    out_shape = [jax.ShapeDtypeStruct((b, t, W_BR), F32)]
    if latent:
        in_specs.append(pl.BlockSpec((1, 1, 2, W_BR), lambda i: (i, layer, 0, 0)))
        args.append(state)
    else:
        out_specs.append(pl.BlockSpec((1, 2, W_BR), lambda i: (i, 0, 0)))
        out_shape.append(jax.ShapeDtypeStruct((b, 2, W_BR), F32))
    return pl.pallas_call(
        functools.partial(_lru_kernel, latent=latent),
        grid=(b,),
        in_specs=in_specs,
        out_specs=out_specs,
        out_shape=out_shape,
        compiler_params=_params(("parallel",)),
        name="lru",
    )(*args)


def _softmax_pv(blocks, sink):
    m = None
    for s, _ in blocks:
        mi = jnp.max(s, axis=-1, keepdims=True)
        m = mi if m is None else jnp.maximum(m, mi)
    if sink is not None:
        m = jnp.maximum(m, sink)
    den = None if sink is None else jnp.exp(sink - m)
    acc = None
    for s, v in blocks:
        e = jnp.exp(s - m)
        di = jnp.sum(e, axis=-1, keepdims=True)
        den = di if den is None else den + di
        o = _dot_bf(e, v)
        acc = o if acc is None else acc + o
    return acc, den


def _win_kernel(*refs, latent):
    scale = HEAD ** -0.5
    if latent:
        sink_ref, q_ref, kp_ref, kc_ref, kn_ref, ck_ref, cv_ref, o_ref = refs
        j = pl.program_id(1)
        nq = pl.num_programs(1)
        qi = lax.broadcasted_iota(jnp.int32, (Q_BLK, Q_BLK), 0)
        ki = lax.broadcasted_iota(jnp.int32, (Q_BLK, Q_BLK), 1)
        kv_blocks = [(kp_ref[0], (ki >= qi) & (j > 0)), (kc_ref[0], None),
                     (kn_ref[0], (ki <= qi) & (j < nq - 1))]
        ctx_k, ctx_v = ck_ref[0, 0], cv_ref[0, 0]
    else:
        sink_ref, x_ref, o_ref = refs
    q = q_ref[0] if latent else x_ref[0, :, :256]
    outs = []
    for h in range(2):
        hs = slice(h * HEAD, (h + 1) * HEAD)
        for g in range(2):
            qg = q[:, (2 * h + g) * HEAD:(2 * h + g + 1) * HEAD]
            blocks = []
            if latent:
                for kv, mask in kv_blocks:
                    s = _dot_nt_bf(qg, kv[:, hs]) * scale
                    if mask is not None:
                        s = jnp.where(mask, s, NEG_INF)
                    blocks.append((s, kv[:, 128 + h * HEAD:128 + (h + 1) * HEAD]))
                blocks.append((_dot_nt_bf(qg, ctx_k[:, hs]) * scale, ctx_v[:, hs]))
            else:
                k = x_ref[0, :, 256 + h * HEAD:256 + (h + 1) * HEAD]
                v = x_ref[0, :, 384 + h * HEAD:384 + (h + 1) * HEAD]
                blocks.append((_dot_nt_bf(qg, k) * scale, v))
            acc, den = _softmax_pv(blocks, sink_ref[2 * h + g])
            outs.append(acc / den)
    o_ref[0] = jnp.concatenate(outs, axis=-1)


def _win(pb, sink, cache_k, cache_v, layer):
    b, t, _ = pb.shape
    latent = cache_k is not None
    smem = pl.BlockSpec(memory_space=pltpu.SMEM)
    if latent:
        nq = t // Q_BLK
        grid = (b, nq)
        in_specs = [smem,
                    pl.BlockSpec((1, Q_BLK, 256), lambda i, j: (i, j, 0)),
                    pl.BlockSpec((1, Q_BLK, 256), lambda i, j: (i, jnp.maximum(j - 1, 0), 1)),
                    pl.BlockSpec((1, Q_BLK, 256), lambda i, j: (i, j, 1)),
                    pl.BlockSpec((1, Q_BLK, 256), lambda i, j: (i, jnp.minimum(j + 1, nq - 1), 1)),
                    pl.BlockSpec((1, 1) + cache_k.shape[2:], lambda i, j: (i, layer, 0, 0)),
                    pl.BlockSpec((1, 1) + cache_v.shape[2:], lambda i, j: (i, layer, 0, 0))]
        args = [sink, pb, pb, pb, pb, cache_k, cache_v]
        out_spec = pl.BlockSpec((1, Q_BLK, W_BR), lambda i, j: (i, j, 0))
        sem = ("parallel", "parallel")
    else:
        grid = (b,)
        in_specs = [smem, pl.BlockSpec((1, t, WB), lambda i: (i, 0, 0))]
        args = [sink, pb]
        out_spec = pl.BlockSpec((1, t, W_BR), lambda i: (i, 0, 0))
        sem = ("parallel",)
    return pl.pallas_call(
        functools.partial(_win_kernel, latent=latent),
        grid=grid,
        in_specs=in_specs,
        out_specs=out_spec,
        out_shape=jax.ShapeDtypeStruct((b, t, W_BR), F32),
        compiler_params=_params(sem),
        name="win",
    )(*args)


def _diff_kernel(*refs, latent, lam_init):
    if latent:
        lam_ref, g_ref, q_ref, k_ref, v_ref, ck_ref, cv_ref, o_ref = refs
    else:
        lam_ref, g_ref, q_ref, k_ref, v_ref, o_ref = refs
    scale = DQ_D ** -0.5
    lp = lam_ref[...]
    lam = (jnp.exp(jnp.sum(lp[0:1] * lp[1:2], axis=-1, keepdims=True))
           - jnp.exp(jnp.sum(lp[2:3] * lp[3:4], axis=-1, keepdims=True)) + lam_init)
    q, k, v = q_ref[0], k_ref[0], v_ref[0]
    key_sets = [(k, v)]
    if latent:
        key_sets.append((ck_ref[0, 0], cv_ref[0, 0]))
    outs = []
    for h in range(W_BR // HEAD):
        hs = slice(h * HEAD, (h + 1) * HEAD)
        probs = []
        for m in range(2):
            ms = slice(h * HEAD + m * DQ_D, h * HEAD + (m + 1) * DQ_D)
            logits = [_dot_nt_bf(q[:, ms], kk[:, ms]) * scale for kk, _ in key_sets]
            mx = functools.reduce(jnp.maximum, [jnp.max(s, axis=-1, keepdims=True) for s in logits])
            es = [jnp.exp(s - mx) for s in logits]
            den = functools.reduce(lambda a, b: a + b, [jnp.sum(e, axis=-1, keepdims=True) for e in es])
            inv = 1.0 / den
            probs.append([e * inv for e in es])
        o = None
        for i, (_, vv) in enumerate(key_sets):
            oi = _dot_bf(probs[0][i] - lam * probs[1][i], vv[:, hs])
            o = oi if o is None else o + oi
        o = o * lax.rsqrt(jnp.mean(o * o, -1, keepdims=True) + NORM_EPS) * g_ref[...]
        outs.append(o * (1.0 - lam_init))
    o_ref[0] = jnp.concatenate(outs, axis=-1)


def _diff(pd, lam_p, subln_g, cache_k, cache_v, layer, lam_init):
    b, t, _ = pd.shape
    latent = cache_k is not None
    tq = 256
    in_specs = [pl.BlockSpec((4, DQ_D), lambda i, j: (0, 0)),
                pl.BlockSpec((1, HEAD), lambda i, j: (0, 0)),
                pl.BlockSpec((1, tq, 256), lambda i, j: (i, j, 0)),
                pl.BlockSpec((1, t, 256), lambda i, j: (i, 0, 1)),
                pl.BlockSpec((1, t, 256), lambda i, j: (i, 0, 2))]
    args = [lam_p, subln_g, pd, pd, pd]
    if latent:
        in_specs += [pl.BlockSpec((1, 1) + cache_k.shape[2:], lambda i, j: (i, layer, 0, 0)),
                     pl.BlockSpec((1, 1) + cache_v.shape[2:], lambda i, j: (i, layer, 0, 0))]
        args += [cache_k, cache_v]
    return pl.pallas_call(
        functools.partial(_diff_kernel, latent=latent, lam_init=lam_init),
        grid=(b, t // tq),
        in_specs=in_specs,
        out_specs=pl.BlockSpec((1, tq, W_BR), lambda i, j: (i, j, 0)),
        out_shape=jax.ShapeDtypeStruct((b, t, W_BR), F32),
        compiler_params=_params(("parallel", "parallel")),
        name="diff",
    )(*args)


def _out_kernel(x_ref, mod_ref, gpost_ref, w_ref, pa_ref, yf_ref, yb_ref, gng_ref, gnb_ref, rk_ref,
                ywin_ref, bg_ref, ylru_ref, cg_ref, ydiff_ref, dg_ref, o_ref):
    ones = _head_ones(W_BR)
    pa = pa_ref[0]
    r, k, v, ag = pa[:, 0:256], pa[:, 256:512], pa[:, 512:768], pa[:, 896:1152]
    y = yf_ref[0] + yb_ref[0]
    mu = _dot_hi(y, ones) * (1.0 / HEAD)
    yc = y - mu
    var = _dot_hi(yc * yc, ones) * (1.0 / HEAD)
    ya = yc * lax.rsqrt(var + GN_EPS) * gng_ref[...] + gnb_ref[...]
    ya = ya + _dot_hi(r * k * rk_ref[...], ones) * v
    mix = jnp.concatenate([ya * _silu(ag), ywin_ref[0] * _silu(bg_ref[0]),
                           ylru_ref[0] * _silu(cg_ref[0]), ydiff_ref[0] * _silu(dg_ref[0])], axis=-1)
    o = jnp.dot(mix.astype(BF16), w_ref[0], preferred_element_type=F32)
    o = o * lax.rsqrt(jnp.mean(o * o, -1, keepdims=True) + NORM_EPS) * gpost_ref[...]
    o_ref[0] = x_ref[0] + mod_ref[0][:, 2 * D_MODEL:] * o


def _out(x, mod_l, g_post, w_out_bf, layer, latent, pa, yf, yb, gn_g, gn_b, r_k, ywin, pb, ylru, pc, ydiff, pd):
    b, t, _ = x.shape
    tm = ROW_TILE
    mod_idx = (lambda i, j: (1 + i, 0, 0)) if latent else (lambda i, j: (0, 0, 0))
    rows = lambda w, c=0: pl.BlockSpec((1, tm, w), lambda i, j: (i, j, c))
    vec = pl.BlockSpec((1, W_BR), lambda i, j: (0, 0))
    in_specs = [rows(D_MODEL),
                pl.BlockSpec((1, 1, 3 * D_MODEL), mod_idx),
                pl.BlockSpec((1, D_MODEL), lambda i, j: (0, 0)),
                pl.BlockSpec((1, D_MODEL, D_MODEL), lambda i, j: (layer, 0, 0)),
                rows(WA), rows(W_BR), rows(W_BR), vec, vec, vec,
                rows(W_BR), rows(W_BR, 2), rows(W_BR), rows(W_BR, 1), rows(W_BR), rows(W_BR, 3)]
    return pl.pallas_call(
        _out_kernel,
        grid=(b, t // tm),
        in_specs=in_specs,
        out_specs=rows(D_MODEL),
        out_shape=jax.ShapeDtypeStruct((b, t, D_MODEL), F32),
        compiler_params=_params(("parallel", "parallel")),
        name="out",
    )(x, mod_l, g_post, w_out_bf, pa, yf, yb, gn_g, gn_b, r_k, ywin, pb, ylru, pc, ydiff, pd)


def _layer(x, mod_l, layer, lam_init, wts, cache, tables):
    latent = cache is not None
    row = lambda a: a[layer].reshape(1, -1)
    pa, pb, pc, pd = _project(x, mod_l, row(wts['g_pre']), wts['w_in_bf'], layer, tables)
    rw = _rwkv(pa, wts['rwkv_w0'][layer], wts['rwkv_w_up'][layer], wts['rwkv_a0'][layer],
               wts['rwkv_a_up'][layer], row(wts['rwkv_k_k']), row(wts['rwkv_k_a']),
               cache['rwkv'] if latent else None, layer)
    lr = _lru(pc, wts['lru_conv_w'][layer], row(wts['lru_conv_b']), wts['lru_wa'][layer],
              wts['lru_ba'][layer], wts['lru_wx'][layer], wts['lru_bx'][layer], wts['lru_lambda'][layer],
              cache['lru'] if latent else None, layer)
    ywin = _win(pb, wts['win_sink'][layer], cache['win_k'] if latent else None,
                cache['win_v'] if latent else None, layer)
    ydiff = _diff(pd, wts['diff_lambda'][layer], row(wts['diff_subln_g']),
                  cache['diff_k'] if latent else None, cache['diff_v'] if latent else None, layer, lam_init)
    y = _out(x, mod_l, row(wts['g_post']), wts['w_out_bf'], layer, latent, pa, rw[0], rw[1],
             row(wts['rwkv_gn_g']), row(wts['rwkv_gn_b']), row(wts['rwkv_r_k']),
             ywin, pb, lr[0], pc, ydiff, pd)
    new_cache = None if latent else (pb, pd, rw[2], lr[1])
    return y, new_cache


def kernel(x_prompt, x_sample, c, cache_win_k, cache_win_v, cache_diff_k, cache_diff_v, state_rwkv, state_lru,
           c_ctx, w_mod, b_mod, g_pre, g_post, w_in, w_out,
           rwkv_w0, rwkv_w_up, rwkv_a0, rwkv_a_up, rwkv_k_k, rwkv_k_a, rwkv_r_k, rwkv_gn_g, rwkv_gn_b,
           win_sink, lru_conv_w, lru_conv_b, lru_wa, lru_ba, lru_wx, lru_bx, lru_lambda,
           diff_lambda, diff_subln_g):
    n_b, seq = x_prompt.shape[:2]
    n_dec, dec_seq = x_sample.shape[:2]
    past = cache_win_k.shape[2]
    wts = dict(g_pre=g_pre, g_post=g_post, w_in_bf=w_in.astype(BF16), w_out_bf=w_out.astype(BF16),
               rwkv_w0=rwkv_w0, rwkv_w_up=rwkv_w_up, rwkv_a0=rwkv_a0, rwkv_a_up=rwkv_a_up,
               rwkv_k_k=rwkv_k_k, rwkv_k_a=rwkv_k_a, rwkv_r_k=rwkv_r_k, rwkv_gn_g=rwkv_gn_g,
               rwkv_gn_b=rwkv_gn_b, win_sink=win_sink, lru_conv_w=lru_conv_w, lru_conv_b=lru_conv_b,
               lru_wa=lru_wa, lru_ba=lru_ba, lru_wx=lru_wx, lru_bx=lru_bx, lru_lambda=lru_lambda,
               diff_lambda=diff_lambda, diff_subln_g=diff_subln_g)
    cache = dict(win_k=cache_win_k.reshape(n_dec, DEPTH, past, 128),
                 win_v=cache_win_v.reshape(n_dec, DEPTH, past, 128),
                 diff_k=cache_diff_k.reshape(n_dec, DEPTH, past, 256),
                 diff_v=cache_diff_v.reshape(n_dec, DEPTH, past, 256),
                 rwkv=state_rwkv, lru=state_lru)
    cvec = jnp.concatenate([c_ctx[None], c, jnp.zeros((8 - 1 - n_dec, D_MODEL), F32)], axis=0)
    mod = _modulation(cvec, w_mod, b_mod)
    cos_b, sin_b = _rope_tables(dec_seq, HEAD, 384)
    cos_d, sin_d = _rope_tables(dec_seq, DQ_D, 512)
    tables = (cos_b, sin_b, cos_d, sin_d)

    y_p, y_s = x_prompt, x_sample
    ctx = []
    for l in range(DEPTH):
        lam_init = 0.8 - 0.6 * math.exp(-0.3 * l)
        mod_l = mod[l].reshape(8, 1, 3 * D_MODEL)
        y_p, nc = _layer(y_p, mod_l, l, lam_init, wts, None, None)
        ctx.append(nc)
        y_s, _ = _layer(y_s, mod_l, l, lam_init, wts, cache, tables)
    stack = lambda f: jnp.stack([f(ct) for ct in ctx], axis=1)
    new_win_k = stack(lambda ct: ct[0][:, :, 256:384].reshape(n_b, seq, 2, HEAD))
    new_win_v = stack(lambda ct: ct[0][:, :, 384:512].reshape(n_b, seq, 2, HEAD))
    new_diff_k = stack(lambda ct: ct[1][:, :, 256:512].reshape(n_b, seq, 4, 2, DQ_D))
    new_diff_v = stack(lambda ct: ct[1][:, :, 512:768].reshape(n_b, seq, 4, HEAD))
    new_state_rwkv = stack(lambda ct: ct[2])
    new_state_lru = stack(lambda ct: ct[3])
    return (y_p, y_s, new_win_k, new_win_v, new_diff_k, new_diff_v, new_state_rwkv, new_state_lru)
```

```python
import functools
import math

import numpy as np
import jax
import jax.numpy as jnp
from jax import lax
from jax.experimental import pallas as pl
from jax.experimental.pallas import tpu as pltpu

F32 = jnp.float32
BF16 = jnp.bfloat16
HI = lax.Precision.HIGHEST

D_MODEL = 1024
DEPTH = 2
GRID_W = 64
ROPE_BASE = 10000.0
NORM_EPS = 1e-6
NEG_INF = -1e30
GN_EPS = 64e-5
C_RG = 8.0
W_BR = 256
HEAD = 64
SUBLANES = 8
LORA = 64
DQ_D = 32
WINDOW = 128
Q_BLK = 128
P_TOTAL = 3456
WA, WB, WC, WD = 1152, 768, 512, 1024
CHUNK = 64
RWKV_BATCH = 2
ROW_TILE = 256
VMEM_LIMIT = 48 * 1024 * 1024


_NN = (((1,), (0,)), ((), ()))
_NT = (((1,), (1,)), ((), ()))
_TN = (((0,), (0,)), ((), ()))


def _dot_hi(a, b):
    return jnp.dot(a, b, precision=HI, preferred_element_type=F32)


def _bf(x):
    return x.astype(BF16)


def _dotf(x, y, dims=_NN):
    return lax.dot_general(x, y, dims, preferred_element_type=F32)


def _split(x):
    hi = x.astype(BF16)
    return hi, (x - hi.astype(F32)).astype(BF16)


def _mm3(a, b, dims=_NN):
    return _dotf(a[0], b[0], dims) + (_dotf(a[0], b[1], dims) + _dotf(a[1], b[0], dims))


def _mm2(x, m01, left=False):
    hi, lo = _split(x)
    if left:
        return _dotf(m01, hi) + _dotf(m01, lo)
    return _dotf(hi, m01) + _dotf(lo, m01)


def _dot_bf(a, b):
    return jnp.dot(a.astype(BF16), b.astype(BF16), preferred_element_type=F32)


def _dot_nt_bf(a, b):
    return lax.dot_general(a.astype(BF16), b.astype(BF16), (((1,), (1,)), ((), ())),
                           preferred_element_type=F32)


def _sigmoid(x):
    return 1.0 / (1.0 + jnp.exp(-x))


def _silu(x):
    return x * _sigmoid(x)


def _softplus(x):
    return jnp.maximum(x, 0.0) + jnp.log1p(jnp.exp(-jnp.abs(x)))


def _head_ones(n):
    r = lax.broadcasted_iota(jnp.int32, (n, n), 0) // HEAD
    c = lax.broadcasted_iota(jnp.int32, (n, n), 1) // HEAD
    return jnp.where(r == c, 1.0, 0.0).astype(BF16)


def _params(sem):
    return pltpu.CompilerParams(dimension_semantics=sem, vmem_limit_bytes=VMEM_LIMIT)


def _mod_kernel(c_ref, w_ref, b_ref, o_ref):
    o_ref[0] = _dot_hi(_silu(c_ref[...]), w_ref[0]) + b_ref[0]


def _modulation(cvec, w_mod, b_mod):
    n_l = w_mod.shape[0]
    tn = 512
    return pl.pallas_call(
        _mod_kernel,
        grid=(n_l, 3 * D_MODEL // tn),
        in_specs=[pl.BlockSpec((8, D_MODEL), lambda l, j: (0, 0)),
                  pl.BlockSpec((1, D_MODEL, tn), lambda l, j: (l, 0, j)),
                  pl.BlockSpec((1, 1, tn), lambda l, j: (l, 0, j))],
        out_specs=pl.BlockSpec((1, 8, tn), lambda l, j: (l, 0, j)),
        out_shape=jax.ShapeDtypeStruct((n_l, 8, 3 * D_MODEL), F32),
        compiler_params=_params(("parallel", "parallel")),
        name="mod",
    )(cvec, w_mod, b_mod.reshape(n_l, 1, 3 * D_MODEL))


def _rope(x, cos, sin_signed, off):
    w = x.shape[-1]
    lane = lax.broadcasted_iota(jnp.int32, x.shape, 1)
    first = (lane % (2 * off)) < off
    partner = jnp.where(first, pltpu.roll(x, w - off, 1), pltpu.roll(x, off, 1))
    return x * cos + partner * sin_signed


def _proj_kernel(*refs, latent):
    if latent:
        x_ref, mod_ref, g_ref, w_ref, cb_ref, sb_ref, cd_ref, sd_ref, oa, ob, oc, od = refs
    else:
        x_ref, mod_ref, g_ref, w_ref, oa, ob, oc, od = refs
    x = x_ref[0]
    y = x * lax.rsqrt(jnp.mean(x * x, -1, keepdims=True) + NORM_EPS) * g_ref[...]
    m = mod_ref[0]
    h = y * (1.0 + m[:, D_MODEL:2 * D_MODEL]) + m[:, :D_MODEL]
    p = jnp.dot(h.astype(BF16), w_ref[0], preferred_element_type=F32)
    oa[0] = p[:, :WA]
    pb = p[:, WA:WA + WB]
    pd = p[:, WA + WB + WC:]
    if latent:
        ob[0, :, :384] = _rope(pb[:, :384], cb_ref[...], sb_ref[...], 16)
        ob[0, :, 384:] = pb[:, 384:]
        od[0, :, :512] = _rope(pd[:, :512], cd_ref[...], sd_ref[...], 8)
        od[0, :, 512:] = pd[:, 512:]
    else:
        ob[0] = pb
        od[0] = pd
    oc[0] = p[:, WA + WB:WA + WB + WC]


def _project(x, mod_l, g_pre, w_in_bf, layer, tables):
    b, t, _ = x.shape
    latent = tables is not None
    tm = ROW_TILE
    mod_idx = (lambda i, j: (1 + i, 0, 0)) if latent else (lambda i, j: (0, 0, 0))
    in_specs = [pl.BlockSpec((1, tm, D_MODEL), lambda i, j: (i, j, 0)),
                pl.BlockSpec((1, 1, 3 * D_MODEL), mod_idx),
                pl.BlockSpec((1, D_MODEL), lambda i, j: (0, 0)),
                pl.BlockSpec((1, D_MODEL, P_TOTAL), lambda i, j: (layer, 0, 0))]
    args = [x, mod_l, g_pre, w_in_bf]
    if latent:
        for tab in tables:
            in_specs.append(pl.BlockSpec((tm, tab.shape[1]), lambda i, j: (j, 0)))
            args.append(tab)
    widths = (WA, WB, WC, WD)
    return pl.pallas_call(
        functools.partial(_proj_kernel, latent=latent),
        grid=(b, t // tm),
        in_specs=in_specs,
        out_specs=[pl.BlockSpec((1, tm, w), lambda i, j: (i, j, 0)) for w in widths],
        out_shape=[jax.ShapeDtypeStruct((b, t, w), F32) for w in widths],
        compiler_params=_params(("parallel", "parallel")),
        name="proj",
    )(*args)


def _rope_tables(t, head_dim, n_lanes):
    half = head_dim // 2
    quarter = half // 2
    pos = jnp.arange(t)
    row = (pos // GRID_W).astype(F32)
    col = (pos % GRID_W).astype(F32)
    inv = ROPE_BASE ** (-jnp.arange(0, half, 2, dtype=F32) / half)
    lane = np.arange(n_lanes) % head_dim
    in_half = lane % half
    p = jnp.where(jnp.asarray(lane < half)[None, :], row[:, None], col[:, None])
    ang = p * inv[in_half % quarter][None, :]
    sign = jnp.asarray(np.where(in_half < quarter, -1.0, 1.0), F32)[None, :]
    return jnp.cos(ang), jnp.sin(ang) * sign


def _cat_rows(parts):
    return tuple(jnp.concatenate(p, axis=0) for p in zip(*parts))


def _rwkv_prepare(xc, w0, wup, a0, aup, k_k, k_a, m_ref, rev):
    c = xc.shape[0]
    n_h = W_BR // HEAD
    r = xc[:, 0:256]
    k = xc[:, 256:512]
    v = xc[:, 512:768]
    wd = xc[:, 768:832]
    ad = xc[:, 832:896]
    ones = _head_ones(W_BR)
    kk = k * k_k
    kk = kk * lax.rsqrt(_mm2(kk * kk, ones) + 1e-12)
    z = w0 + _mm3(_split(jnp.tanh(wd)), _split(wup))
    e = jnp.exp(-_softplus(-z) - 0.5)
    a = _sigmoid(a0 + _mm3(_split(ad), _split(aup)))
    kd = k * (1.0 + (a - 1.0) * k_a)
    alpha = -kk
    beta = kk * a

    ti = lax.broadcasted_iota(jnp.int32, (c, c), 0)
    si = lax.broadcasted_iota(jnp.int32, (c, c), 1)
    cum = (si >= ti) if rev else (si <= ti)
    l_incl = _mm2(e, jnp.where(cum, 1.0, 0.0).astype(BF16), left=True)
    l_excl = l_incl - e
    l_tot = jnp.sum(e, axis=0, keepdims=True)
    grow = jnp.exp(l_incl)
    tail = jnp.exp(l_incl - l_tot)

    n = n_h * c
    same = m_ref[_M_SAME]
    strict = m_ref[_M_STRICT + 2 * int(rev)]
    incl = m_ref[_M_INCL + 2 * int(rev)]
    tile = lambda x: jnp.concatenate([x] * n_h, axis=0)
    expand = lambda x: tile(x) * same
    lhs = jnp.concatenate([expand(_bf(alpha * jnp.exp(-l_excl))),
                           expand(_bf(r * jnp.exp(-l_incl)))], axis=0)
    rhs_t = jnp.concatenate([tile(_bf(beta * grow)), tile(_bf(kd * grow))], axis=0)
    g4 = _dotf(lhs, rhs_t, _NT)
    v_hi, v_lo = _split(v)
    return dict(
        lhs=lhs, nm=_bf(g4[:n, :n]) * strict, mm=_bf(g4[:n, n:]) * strict,
        pq=jnp.concatenate([_bf(g4[n:, :n]) * incl, _bf(g4[n:, n:]) * incl], axis=1),
        v_hi=expand(v_hi), v_lo=expand(v_lo),
        bk=_cat_rows([tuple(expand(x) for x in _split(beta * tail)),
                      tuple(expand(x) for x in _split(kd * tail))]),
        g_c=jnp.exp(-l_tot))


def _rwkv_finish(preps, invs, s_ref):
    n = invs[0].shape[0]
    c = n // (W_BR // HEAD)
    s0 = [s_ref[g] for g in range(len(preps))]
    as0 = [_dotf(p['lhs'], _bf(s), _NT) for p, s in zip(preps, s0)]
    rhs = [_bf(a[:n] + _dotf(p['mm'], p['v_hi'])) for p, a in zip(preps, as0)]
    u = [_split(_dotf(_bf(t), x)) for t, x in zip(invs, rhs)]
    uv = [(jnp.concatenate([uh, p['v_hi']], axis=0), jnp.concatenate([ul, p['v_lo']], axis=0))
          for p, (uh, ul) in zip(preps, u)]
    ys = []
    for g, p in enumerate(preps):
        s_ref[g] = s0[g] * p['g_c'] + _mm3(uv[g], p['bk'], _TN)
        y_exp = as0[g][n:] + _dotf(p['pq'], uv[g][0])
        y = y_exp[0:c]
        for h in range(1, n // c):
            y = y + y_exp[h * c:(h + 1) * c]
        ys.append(y)
    return ys


_M_SAME, _M_STRICT, _M_INCL, _M_EYE, _M_PAIR, _M_OFF = 0, 1, 2, 5, 6, 7


def _rwkv_masks(c, n_h):
    n = c * n_h
    r = np.arange(n)[:, None]
    l = np.arange(n)[None, :]
    same = (r // c) == (l // c)
    t, s = r % c, l % c
    masks = [same, same & (s < t), same & (s <= t), same & (s > t), same & (s >= t),
             r == l, ((r // 2) == (l // 2)) & (r != l)]
    b = 2
    while b < c:
        masks.append(((r // (2 * b)) == (l // (2 * b))) & ((r // b) != (l // b)))
        b *= 2
    return jnp.asarray(np.stack(masks).astype(np.float32), BF16)


def _rwkv_kernel(*refs, latent, n_b):
    if latent:
        (xf_ref, xb_ref, w0_ref, wup_ref, a0_ref, aup_ref, kk_ref, ka_ref, m_ref, s0_ref,
         yf_ref, yb_ref, s_scr) = refs
    else:
        (xf_ref, xb_ref, w0_ref, wup_ref, a0_ref, aup_ref, kk_ref, ka_ref, m_ref,
         yf_ref, yb_ref, st_ref, s_scr) = refs
    ci = pl.program_id(1)
    n_h = W_BR // HEAD
    diag = lambda h: (slice(h * HEAD, (h + 1) * HEAD),) * 2
    streams = [(bi, d) for bi in range(n_b) for d in range(2)]

    @pl.when(ci == 0)
    def _():
        s_scr[...] = jnp.zeros(s_scr.shape, F32)
        if latent:
            for g, (bi, d) in enumerate(streams):
                for h in range(n_h):
                    s_scr[(g,) + diag(h)] = s0_ref[bi, 0, d, h]

    x_refs = (xf_ref, xb_ref)
    preps = [_rwkv_prepare(x_refs[d][bi], w0_ref[d:d + 1], wup_ref[d], a0_ref[d:d + 1], aup_ref[d],
                           kk_ref[...], ka_ref[...], m_ref, rev=(d == 1)) for bi, d in streams]
    invs = [(p['nm'] * m_ref[_M_PAIR]).astype(F32) + m_ref[_M_EYE].astype(F32) for p in preps]
    for lvl in range(int(math.log2(CHUNK)) - 1):
        inv_b = [_bf(t) for t in invs]
        half = [_bf(_dotf(p['nm'] * m_ref[_M_OFF + lvl], tb)) for p, tb in zip(preps, inv_b)]
        invs = [t + _dotf(tb, hb) for t, tb, hb in zip(invs, inv_b, half)]
    ys = _rwkv_finish(preps, invs, s_scr)
    y_refs = (yf_ref, yb_ref)
    for y, (bi, d) in zip(ys, streams):
        y_refs[d][bi] = y

    if not latent:
        @pl.when(ci == pl.num_programs(1) - 1)
        def _():
            for g, (bi, d) in enumerate(streams):
                for h in range(n_h):
                    st_ref[bi, d, h] = s_scr[(g,) + diag(h)]


def _rwkv(pa, w0, wup, a0, aup, k_k, k_a, state, layer):
    b, t, _ = pa.shape
    latent = state is not None
    nc = t // CHUNK
    n_h = W_BR // HEAD
    n_b = RWKV_BATCH
    masks = _rwkv_masks(CHUNK, n_h)
    full = lambda shape: pl.BlockSpec(shape, lambda i, j: (0,) * len(shape))
    in_specs = [pl.BlockSpec((n_b, CHUNK, WA), lambda i, j: (i, j, 0)),
                pl.BlockSpec((n_b, CHUNK, WA), lambda i, j: (i, nc - 1 - j, 0)),
                full((2, W_BR)), full((2, LORA, W_BR)), full((2, W_BR)), full((2, LORA, W_BR)),
                full((1, W_BR)), full((1, W_BR)), full(masks.shape)]
    args = [pa, pa, w0, wup, a0, aup, k_k, k_a, masks]
    out_specs = [pl.BlockSpec((n_b, CHUNK, W_BR), lambda i, j: (i, j, 0)),
                 pl.BlockSpec((n_b, CHUNK, W_BR), lambda i, j: (i, nc - 1 - j, 0))]
    out_shape = [jax.ShapeDtypeStruct((b, t, W_BR), F32)] * 2
    if latent:
        in_specs.append(pl.BlockSpec((n_b, 1, 2, n_h, HEAD, HEAD), lambda i, j: (i, layer, 0, 0, 0, 0)))
        args.append(state)
    else:
        out_specs.append(pl.BlockSpec((n_b, 2, n_h, HEAD, HEAD), lambda i, j: (i, 0, 0, 0, 0)))
        out_shape.append(jax.ShapeDtypeStruct((b, 2, n_h, HEAD, HEAD), F32))
    return pl.pallas_call(
        functools.partial(_rwkv_kernel, latent=latent, n_b=n_b),
        grid=(b // n_b, nc),
        in_specs=in_specs,
        out_specs=out_specs,
        out_shape=out_shape,
        scratch_shapes=[pltpu.VMEM((2 * n_b, W_BR, W_BR), F32)],
        compiler_params=_params(("parallel", "arbitrary")),
        name="rwkv",
    )(*args)


def _lru_kernel(*refs, latent):
    if latent:
        (x_ref, cw_ref, cb_ref, wa_ref, ba_ref, wx_ref, bx_ref, lam_ref, h0_ref, y_ref,
         a_scr, h_scr) = refs
    else:
        (x_ref, cw_ref, cb_ref, wa_ref, ba_ref, wx_ref, bx_ref, lam_ref, y_ref, st_ref,
         a_scr, h_scr) = refs
    x = x_ref[0, :, :W_BR]
    t = x.shape[0]
    row = lax.broadcasted_iota(jnp.int32, x.shape, 0)

    def shift_dn(z, k, fill):
        return jnp.where(row >= k, pltpu.roll(z, k, 0), fill)

    def shift_up(z, k, fill):
        return jnp.where(row < t - k, pltpu.roll(z, t - k, 0), fill)

    cw = cw_ref[...]
    xc = (cb_ref[...] + shift_dn(x, 2, 0.0) * cw[0:1] + shift_dn(x, 1, 0.0) * cw[1:2]
          + x * cw[2:3] + shift_up(x, 1, 0.0) * cw[3:4])
    xs = _split(xc)
    sub = row % SUBLANES
    for d in range(2):
        gate_a = _sigmoid(_mm3(xs, _split(wa_ref[d])) + ba_ref[d:d + 1])
        gate_x = _sigmoid(_mm3(xs, _split(wx_ref[d])) + bx_ref[d:d + 1])
        log_a = -C_RG * gate_a * _softplus(-lam_ref[d:d + 1])
        a = jnp.exp(log_a)
        u = jnp.sqrt(-jnp.tanh(log_a) * (a * a + 1.0)) * (gate_x * xc)
        k = 1
        while k < SUBLANES:
            if d == 0:
                keep = sub >= k
                sh = lambda z: pltpu.roll(z, k, 0)
            else:
                keep = sub < SUBLANES - k
                sh = lambda z: pltpu.roll(z, t - k, 0)
            u = a * jnp.where(keep, sh(u), 0.0) + u
            a = a * jnp.where(keep, sh(a), 1.0)
            k *= 2
        a_scr[d] = a
        h_scr[d] = u

    n_grp = t // SUBLANES
    if latent:
        carry0 = (h0_ref[0, 0, 0:1], h0_ref[0, 0, 1:2])
    else:
        carry0 = (jnp.zeros((1, W_BR), F32),) * 2

    def chain(i, carry):
        cf, cb = carry
        rf = pl.ds(pl.multiple_of(i * SUBLANES, SUBLANES), SUBLANES)
        rb = pl.ds(pl.multiple_of((n_grp - 1 - i) * SUBLANES, SUBLANES), SUBLANES)
        hf = h_scr[0, rf, :] + a_scr[0, rf, :] * cf
        hb = h_scr[1, rb, :] + a_scr[1, rb, :] * cb
        h_scr[0, rf, :] = hf
        h_scr[1, rb, :] = hb
        return hf[SUBLANES - 1:SUBLANES], hb[0:1]

    cf, cb = lax.fori_loop(0, n_grp, chain, carry0, unroll=4)
    y_ref[0] = h_scr[0] + h_scr[1]
    if not latent:
        st_ref[0] = jnp.concatenate([cf, cb], axis=0)


def _lru(pc, cw, cb, wa, ba, wx, bx, lam, state, layer):
    b, t, _ = pc.shape
    latent = state is not None
    n_blk = W_BR // HEAD
    full = lambda shape: pl.BlockSpec(shape, lambda i: (0,) * len(shape))
    in_specs = [pl.BlockSpec((1, t, WC), lambda i: (i, 0, 0)),
                full((4, W_BR)), full((1, W_BR)), full((2, W_BR, W_BR)), full((2, W_BR)),
                full((2, W_BR, W_BR)), full((2, W_BR)), full((2, W_BR))]

    def block_diag(w):
        out = jnp.zeros((2, W_BR, W_BR), F32)
        for n in range(n_blk):
            out = out.at[:, n * HEAD:(n + 1) * HEAD, n * HEAD:(n + 1) * HEAD].set(w[:, n])
        return out

    args = [pc, cw, cb, block_diag(wa), ba, block_diag(wx), bx, lam]
    out_specs = [pl.BlockSpec((1, t, W_BR), lambda i: (i, 0, 0))]
    out_shape = [jax.ShapeDtypeStruct((b, t, W_BR), F32)]
    if latent:
        in_specs.append(pl.BlockSpec((1, 1, 2, W_BR), lambda i: (i, layer, 0, 0)))
        args.append(state)
    else:
        out_specs.append(pl.BlockSpec((1, 2, W_BR), lambda i: (i, 0, 0)))
        out_shape.append(jax.ShapeDtypeStruct((b, 2, W_BR), F32))
    return pl.pallas_call(
        functools.partial(_lru_kernel, latent=latent),
        grid=(b,),
        in_specs=in_specs,
        out_specs=out_specs,
        out_shape=out_shape,
        scratch_shapes=[pltpu.VMEM((2, t, W_BR), F32), pltpu.VMEM((2, t, W_BR), F32)],
        compiler_params=_params(("parallel",)),
        name="lru",
    )(*args)


def _softmax_pv(blocks, sink):
    m = None
    for s, _ in blocks:
        mi = jnp.max(s, axis=-1, keepdims=True)
        m = mi if m is None else jnp.maximum(m, mi)
    if sink is not None:
        m = jnp.maximum(m, sink)
    den = None if sink is None else jnp.exp(sink - m)
    acc = None
    for s, v in blocks:
        e = jnp.exp(s - m)
        di = jnp.sum(e, axis=-1, keepdims=True)
        den = di if den is None else den + di
        o = _dot_bf(e, v)
        acc = o if acc is None else acc + o
    return acc, den


def _win_kernel(*refs, latent):
    scale = HEAD ** -0.5
    if latent:
        sink_ref, q_ref, kp_ref, kc_ref, kn_ref, ck_ref, cv_ref, o_ref = refs
        j = pl.program_id(1)
        nq = pl.num_programs(1)
        qi = lax.broadcasted_iota(jnp.int32, (Q_BLK, Q_BLK), 0)
        ki = lax.broadcasted_iota(jnp.int32, (Q_BLK, Q_BLK), 1)
        kv_blocks = [(kp_ref[0], (ki >= qi) & (j > 0)), (kc_ref[0], None),
                     (kn_ref[0], (ki <= qi) & (j < nq - 1))]
        ctx_k, ctx_v = ck_ref[0, 0], cv_ref[0, 0]
    else:
        sink_ref, x_ref, o_ref = refs
    q = q_ref[0] if latent else x_ref[0, :, :256]
    outs = []
    for h in range(2):
        hs = slice(h * HEAD, (h + 1) * HEAD)
        for g in range(2):
            qg = q[:, (2 * h + g) * HEAD:(2 * h + g + 1) * HEAD]
            blocks = []
            if latent:
                for kv, mask in kv_blocks:
                    s = _dot_nt_bf(qg, kv[:, hs]) * scale
                    if mask is not None:
                        s = jnp.where(mask, s, NEG_INF)
                    blocks.append((s, kv[:, 128 + h * HEAD:128 + (h + 1) * HEAD]))
                blocks.append((_dot_nt_bf(qg, ctx_k[:, hs]) * scale, ctx_v[:, hs]))
            else:
                k = x_ref[0, :, 256 + h * HEAD:256 + (h + 1) * HEAD]
                v = x_ref[0, :, 384 + h * HEAD:384 + (h + 1) * HEAD]
                blocks.append((_dot_nt_bf(qg, k) * scale, v))
            acc, den = _softmax_pv(blocks, sink_ref[2 * h + g])
            outs.append(acc / den)
    o_ref[0] = jnp.concatenate(outs, axis=-1)


def _win(pb, sink, cache_k, cache_v, layer):
    b, t, _ = pb.shape
    latent = cache_k is not None
    smem = pl.BlockSpec(memory_space=pltpu.SMEM)
    if latent:
        nq = t // Q_BLK
        grid = (b, nq)
        in_specs = [smem,
                    pl.BlockSpec((1, Q_BLK, 256), lambda i, j: (i, j, 0)),
                    pl.BlockSpec((1, Q_BLK, 256), lambda i, j: (i, jnp.maximum(j - 1, 0), 1)),
                    pl.BlockSpec((1, Q_BLK, 256), lambda i, j: (i, j, 1)),
                    pl.BlockSpec((1, Q_BLK, 256), lambda i, j: (i, jnp.minimum(j + 1, nq - 1), 1)),
                    pl.BlockSpec((1, 1) + cache_k.shape[2:], lambda i, j: (i, layer, 0, 0)),
                    pl.BlockSpec((1, 1) + cache_v.shape[2:], lambda i, j: (i, layer, 0, 0))]
        args = [sink, pb, pb, pb, pb, cache_k, cache_v]
        out_spec = pl.BlockSpec((1, Q_BLK, W_BR), lambda i, j: (i, j, 0))
        sem = ("parallel", "parallel")
    else:
        grid = (b,)
        in_specs = [smem, pl.BlockSpec((1, t, WB), lambda i: (i, 0, 0))]
        args = [sink, pb]
        out_spec = pl.BlockSpec((1, t, W_BR), lambda i: (i, 0, 0))
        sem = ("parallel",)
    return pl.pallas_call(
        functools.partial(_win_kernel, latent=latent),
        grid=grid,
        in_specs=in_specs,
        out_specs=out_spec,
        out_shape=jax.ShapeDtypeStruct((b, t, W_BR), F32),
        compiler_params=_params(sem),
        name="win",
    )(*args)


def _diff_kernel(*refs, latent, lam_init):
    if latent:
        lam_ref, g_ref, q_ref, k_ref, v_ref, ck_ref, cv_ref, o_ref = refs
    else:
        lam_ref, g_ref, q_ref, k_ref, v_ref, o_ref = refs
    scale = DQ_D ** -0.5
    lp = lam_ref[...]
    lam = (jnp.exp(jnp.sum(lp[0:1] * lp[1:2], axis=-1, keepdims=True))
           - jnp.exp(jnp.sum(lp[2:3] * lp[3:4], axis=-1, keepdims=True)) + lam_init)
    q, k, v = q_ref[0], k_ref[0], v_ref[0]
    key_sets = [(k, v)]
    if latent:
        key_sets.append((ck_ref[0, 0], cv_ref[0, 0]))
    outs = []
    for h in range(W_BR // HEAD):
        hs = slice(h * HEAD, (h + 1) * HEAD)
        probs = []
        for m in range(2):
            ms = slice(h * HEAD + m * DQ_D, h * HEAD + (m + 1) * DQ_D)
            logits = [_dot_nt_bf(q[:, ms], kk[:, ms]) * scale for kk, _ in key_sets]
            mx = functools.reduce(jnp.maximum, [jnp.max(s, axis=-1, keepdims=True) for s in logits])
            es = [jnp.exp(s - mx) for s in logits]
            den = functools.reduce(lambda a, b: a + b, [jnp.sum(e, axis=-1, keepdims=True) for e in es])
            inv = 1.0 / den
            probs.append([e * inv for e in es])
        o = None
        for i, (_, vv) in enumerate(key_sets):
            oi = _dot_bf(probs[0][i] - lam * probs[1][i], vv[:, hs])
            o = oi if o is None else o + oi
        o = o * lax.rsqrt(jnp.mean(o * o, -1, keepdims=True) + NORM_EPS) * g_ref[...]
        outs.append(o * (1.0 - lam_init))
    o_ref[0] = jnp.concatenate(outs, axis=-1)


def _diff(pd, lam_p, subln_g, cache_k, cache_v, layer, lam_init):
    b, t, _ = pd.shape
    latent = cache_k is not None
    tq = 256
    in_specs = [pl.BlockSpec((4, DQ_D), lambda i, j: (0, 0)),
                pl.BlockSpec((1, HEAD), lambda i, j: (0, 0)),
                pl.BlockSpec((1, tq, 256), lambda i, j: (i, j, 0)),
                pl.BlockSpec((1, t, 256), lambda i, j: (i, 0, 1)),
                pl.BlockSpec((1, t, 256), lambda i, j: (i, 0, 2))]
    args = [lam_p, subln_g, pd, pd, pd]
    if latent:
        in_specs += [pl.BlockSpec((1, 1) + cache_k.shape[2:], lambda i, j: (i, layer, 0, 0)),
                     pl.BlockSpec((1, 1) + cache_v.shape[2:], lambda i, j: (i, layer, 0, 0))]
        args += [cache_k, cache_v]
    return pl.pallas_call(
        functools.partial(_diff_kernel, latent=latent, lam_init=lam_init),
        grid=(b, t // tq),
        in_specs=in_specs,
        out_specs=pl.BlockSpec((1, tq, W_BR), lambda i, j: (i, j, 0)),
        out_shape=jax.ShapeDtypeStruct((b, t, W_BR), F32),
        compiler_params=_params(("parallel", "parallel")),
        name="diff",
    )(*args)


def _out_kernel(x_ref, mod_ref, gpost_ref, w_ref, pa_ref, yf_ref, yb_ref, gng_ref, gnb_ref, rk_ref,
                ywin_ref, bg_ref, ylru_ref, cg_ref, ydiff_ref, dg_ref, o_ref):
    ones = _head_ones(W_BR)
    pa = pa_ref[0]
    r, k, v, ag = pa[:, 0:256], pa[:, 256:512], pa[:, 512:768], pa[:, 896:1152]
    y = yf_ref[0] + yb_ref[0]
    mu = _mm2(y, ones) * (1.0 / HEAD)
    yc = y - mu
    var = _mm2(yc * yc, ones) * (1.0 / HEAD)
    ya = yc * lax.rsqrt(var + GN_EPS) * gng_ref[...] + gnb_ref[...]
    ya = ya + _mm2(r * k * rk_ref[...], ones) * v
    mix = jnp.concatenate([ya * _silu(ag), ywin_ref[0] * _silu(bg_ref[0]),
                           ylru_ref[0] * _silu(cg_ref[0]), ydiff_ref[0] * _silu(dg_ref[0])], axis=-1)
    o = jnp.dot(mix.astype(BF16), w_ref[0], preferred_element_type=F32)
    o = o * lax.rsqrt(jnp.mean(o * o, -1, keepdims=True) + NORM_EPS) * gpost_ref[...]
    o_ref[0] = x_ref[0] + mod_ref[0][:, 2 * D_MODEL:] * o


def _out(x, mod_l, g_post, w_out_bf, layer, latent, pa, yf, yb, gn_g, gn_b, r_k, ywin, pb, ylru, pc, ydiff, pd):
    b, t, _ = x.shape
    tm = ROW_TILE
    mod_idx = (lambda i, j: (1 + i, 0, 0)) if latent else (lambda i, j: (0, 0, 0))
    rows = lambda w, c=0: pl.BlockSpec((1, tm, w), lambda i, j: (i, j, c))
    vec = pl.BlockSpec((1, W_BR), lambda i, j: (0, 0))
    in_specs = [rows(D_MODEL),
                pl.BlockSpec((1, 1, 3 * D_MODEL), mod_idx),
                pl.BlockSpec((1, D_MODEL), lambda i, j: (0, 0)),
                pl.BlockSpec((1, D_MODEL, D_MODEL), lambda i, j: (layer, 0, 0)),
                rows(WA), rows(W_BR), rows(W_BR), vec, vec, vec,
                rows(W_BR), rows(W_BR, 2), rows(W_BR), rows(W_BR, 1), rows(W_BR), rows(W_BR, 3)]
    return pl.pallas_call(
        _out_kernel,
        grid=(b, t // tm),
        in_specs=in_specs,
        out_specs=rows(D_MODEL),
        out_shape=jax.ShapeDtypeStruct((b, t, D_MODEL), F32),
        compiler_params=_params(("parallel", "parallel")),
        name="out",
    )(x, mod_l, g_post, w_out_bf, pa, yf, yb, gn_g, gn_b, r_k, ywin, pb, ylru, pc, ydiff, pd)


def _layer(x, mod_l, layer, lam_init, wts, cache, tables):
    latent = cache is not None
    row = lambda a: a[layer].reshape(1, -1)
    pa, pb, pc, pd = _project(x, mod_l, row(wts['g_pre']), wts['w_in_bf'], layer, tables)
    rw = _rwkv(pa, wts['rwkv_w0'][layer], wts['rwkv_w_up'][layer], wts['rwkv_a0'][layer],
               wts['rwkv_a_up'][layer], row(wts['rwkv_k_k']), row(wts['rwkv_k_a']),
               cache['rwkv'] if latent else None, layer)
    lr = _lru(pc, wts['lru_conv_w'][layer], row(wts['lru_conv_b']), wts['lru_wa'][layer],
              wts['lru_ba'][layer], wts['lru_wx'][layer], wts['lru_bx'][layer], wts['lru_lambda'][layer],
              cache['lru'] if latent else None, layer)
    ywin = _win(pb, wts['win_sink'][layer], cache['win_k'] if latent else None,
                cache['win_v'] if latent else None, layer)
    ydiff = _diff(pd, wts['diff_lambda'][layer], row(wts['diff_subln_g']),
                  cache['diff_k'] if latent else None, cache['diff_v'] if latent else None, layer, lam_init)
    y = _out(x, mod_l, row(wts['g_post']), wts['w_out_bf'], layer, latent, pa, rw[0], rw[1],
             row(wts['rwkv_gn_g']), row(wts['rwkv_gn_b']), row(wts['rwkv_r_k']),
             ywin, pb, lr[0], pc, ydiff, pd)
    new_cache = None if latent else (pb, pd, rw[2], lr[1])
    return y, new_cache


def kernel(x_prompt, x_sample, c, cache_win_k, cache_win_v, cache_diff_k, cache_diff_v, state_rwkv, state_lru,
           c_ctx, w_mod, b_mod, g_pre, g_post, w_in, w_out,
           rwkv_w0, rwkv_w_up, rwkv_a0, rwkv_a_up, rwkv_k_k, rwkv_k_a, rwkv_r_k, rwkv_gn_g, rwkv_gn_b,
           win_sink, lru_conv_w, lru_conv_b, lru_wa, lru_ba, lru_wx, lru_bx, lru_lambda,
           diff_lambda, diff_subln_g):
    n_b, seq = x_prompt.shape[:2]
    n_dec, dec_seq = x_sample.shape[:2]
    past = cache_win_k.shape[2]
    wts = dict(g_pre=g_pre, g_post=g_post, w_in_bf=w_in.astype(BF16), w_out_bf=w_out.astype(BF16),
               rwkv_w0=rwkv_w0, rwkv_w_up=rwkv_w_up, rwkv_a0=rwkv_a0, rwkv_a_up=rwkv_a_up,
               rwkv_k_k=rwkv_k_k, rwkv_k_a=rwkv_k_a, rwkv_r_k=rwkv_r_k, rwkv_gn_g=rwkv_gn_g,
               rwkv_gn_b=rwkv_gn_b, win_sink=win_sink, lru_conv_w=lru_conv_w, lru_conv_b=lru_conv_b,
               lru_wa=lru_wa, lru_ba=lru_ba, lru_wx=lru_wx, lru_bx=lru_bx, lru_lambda=lru_lambda,
               diff_lambda=diff_lambda, diff_subln_g=diff_subln_g)
    cache = dict(win_k=cache_win_k.reshape(n_dec, DEPTH, past, 128),
                 win_v=cache_win_v.reshape(n_dec, DEPTH, past, 128),
                 diff_k=cache_diff_k.reshape(n_dec, DEPTH, past, 256),
                 diff_v=cache_diff_v.reshape(n_dec, DEPTH, past, 256),
                 rwkv=state_rwkv, lru=state_lru)
    cvec = jnp.concatenate([c_ctx[None], c, jnp.zeros((8 - 1 - n_dec, D_MODEL), F32)], axis=0)
    mod = _modulation(cvec, w_mod, b_mod)
    cos_b, sin_b = _rope_tables(dec_seq, HEAD, 384)
    cos_d, sin_d = _rope_tables(dec_seq, DQ_D, 512)
    tables = (cos_b, sin_b, cos_d, sin_d)

    y_p, y_s = x_prompt, x_sample
    ctx = []
    for l in range(DEPTH):
        lam_init = 0.8 - 0.6 * math.exp(-0.3 * l)
        mod_l = mod[l].reshape(8, 1, 3 * D_MODEL)
        y_p, nc = _layer(y_p, mod_l, l, lam_init, wts, None, None)
        ctx.append(nc)
        y_s, _ = _layer(y_s, mod_l, l, lam_init, wts, cache, tables)
    stack = lambda f: jnp.stack([f(ct) for ct in ctx], axis=1)
    new_win_k = stack(lambda ct: ct[0][:, :, 256:384].reshape(n_b, seq, 2, HEAD))
    new_win_v = stack(lambda ct: ct[0][:, :, 384:512].reshape(n_b, seq, 2, HEAD))
    new_diff_k = stack(lambda ct: ct[1][:, :, 256:512].reshape(n_b, seq, 4, 2, DQ_D))
    new_diff_v = stack(lambda ct: ct[1][:, :, 512:768].reshape(n_b, seq, 4, HEAD))
    new_state_rwkv = stack(lambda ct: ct[2])
    new_state_lru = stack(lambda ct: ct[3])
    return (y_p, y_s, new_win_k, new_win_v, new_diff_k, new_diff_v, new_state_rwkv, new_state_lru)
```

```python
import functools
import math

import numpy as np
import jax
import jax.numpy as jnp
from jax import lax
from jax.experimental import pallas as pl
from jax.experimental.pallas import tpu as pltpu

F32 = jnp.float32
BF16 = jnp.bfloat16
HI = lax.Precision.HIGHEST

D_MODEL = 1024
DEPTH = 2
GRID_W = 64
ROPE_BASE = 10000.0
NORM_EPS = 1e-6
NEG_INF = -1e30
GN_EPS = 64e-5
C_RG = 8.0
W_BR = 256
HEAD = 64
SUBLANES = 8
LORA = 64
DQ_D = 32
WINDOW = 128
Q_BLK = 128
P_TOTAL = 3456
WA, WB, WC, WD = 1152, 768, 512, 1024
CHUNK = 64
RWKV_BATCH = 2
ROW_TILE = 256
VMEM_LIMIT = 48 * 1024 * 1024


_NN = (((1,), (0,)), ((), ()))
_NT = (((1,), (1,)), ((), ()))
_TN = (((0,), (0,)), ((), ()))


def _dot_hi(a, b):
    return jnp.dot(a, b, precision=HI, preferred_element_type=F32)


def _bf(x):
    return x.astype(BF16)


def _dotf(x, y, dims=_NN):
    return lax.dot_general(x, y, dims, preferred_element_type=F32)


def _split(x):
    hi = x.astype(BF16)
    return hi, (x - hi.astype(F32)).astype(BF16)


def _mm3(a, b, dims=_NN):
    return _dotf(a[0], b[0], dims) + (_dotf(a[0], b[1], dims) + _dotf(a[1], b[0], dims))


def _mm2(x, m01, left=False):
    hi, lo = _split(x)
    if left:
        return _dotf(m01, hi) + _dotf(m01, lo)
    return _dotf(hi, m01) + _dotf(lo, m01)


def _dot_bf(a, b):
    return jnp.dot(a.astype(BF16), b.astype(BF16), preferred_element_type=F32)


def _dot_nt_bf(a, b):
    return lax.dot_general(a.astype(BF16), b.astype(BF16), (((1,), (1,)), ((), ())),
                           preferred_element_type=F32)


def _sigmoid(x):
    return 1.0 / (1.0 + jnp.exp(-x))


def _silu(x):
    return x * _sigmoid(x)


def _softplus(x):
    return jnp.maximum(x, 0.0) + jnp.log1p(jnp.exp(-jnp.abs(x)))


def _head_ones(n):
    r = lax.broadcasted_iota(jnp.int32, (n, n), 0) // HEAD
    c = lax.broadcasted_iota(jnp.int32, (n, n), 1) // HEAD
    return jnp.where(r == c, 1.0, 0.0).astype(BF16)


def _params(sem):
    return pltpu.CompilerParams(dimension_semantics=sem, vmem_limit_bytes=VMEM_LIMIT)


def _mod_kernel(c_ref, w_ref, b_ref, o_ref):
    o_ref[0] = _dot_hi(_silu(c_ref[...]), w_ref[0]) + b_ref[0]


def _modulation(cvec, w_mod, b_mod):
    n_l = w_mod.shape[0]
    tn = 512
    return pl.pallas_call(
        _mod_kernel,
        grid=(n_l, 3 * D_MODEL // tn),
        in_specs=[pl.BlockSpec((8, D_MODEL), lambda l, j: (0, 0)),
                  pl.BlockSpec((1, D_MODEL, tn), lambda l, j: (l, 0, j)),
                  pl.BlockSpec((1, 1, tn), lambda l, j: (l, 0, j))],
        out_specs=pl.BlockSpec((1, 8, tn), lambda l, j: (l, 0, j)),
        out_shape=jax.ShapeDtypeStruct((n_l, 8, 3 * D_MODEL), F32),
        compiler_params=_params(("parallel", "parallel")),
        name="mod",
    )(cvec, w_mod, b_mod.reshape(n_l, 1, 3 * D_MODEL))


def _rope(x, cos, sin_signed, off):
    w = x.shape[-1]
    lane = lax.broadcasted_iota(jnp.int32, x.shape, 1)
    first = (lane % (2 * off)) < off
    partner = jnp.where(first, pltpu.roll(x, w - off, 1), pltpu.roll(x, off, 1))
    return x * cos + partner * sin_signed


def _proj_kernel(*refs, latent):
    if latent:
        x_ref, mod_ref, g_ref, w_ref, cb_ref, sb_ref, cd_ref, sd_ref, oa, ob, oc, od = refs
    else:
        x_ref, mod_ref, g_ref, w_ref, oa, ob, oc, od = refs
    x = x_ref[0]
    y = x * lax.rsqrt(jnp.mean(x * x, -1, keepdims=True) + NORM_EPS) * g_ref[...]
    m = mod_ref[0]
    h = y * (1.0 + m[:, D_MODEL:2 * D_MODEL]) + m[:, :D_MODEL]
    p = jnp.dot(h.astype(BF16), w_ref[0], preferred_element_type=F32)
    oa[0] = p[:, :WA]
    pb = p[:, WA:WA + WB]
    pd = p[:, WA + WB + WC:]
    if latent:
        ob[0, :, :384] = _rope(pb[:, :384], cb_ref[...], sb_ref[...], 16)
        ob[0, :, 384:] = pb[:, 384:]
        od[0, :, :512] = _rope(pd[:, :512], cd_ref[...], sd_ref[...], 8)
        od[0, :, 512:] = pd[:, 512:]
    else:
        ob[0] = pb
        od[0] = pd
    oc[0] = p[:, WA + WB:WA + WB + WC]


def _project(x, mod_l, g_pre, w_in_bf, layer, tables):
    b, t, _ = x.shape
    latent = tables is not None
    tm = ROW_TILE
    mod_idx = (lambda i, j: (1 + i, 0, 0)) if latent else (lambda i, j: (0, 0, 0))
    in_specs = [pl.BlockSpec((1, tm, D_MODEL), lambda i, j: (i, j, 0)),
                pl.BlockSpec((1, 1, 3 * D_MODEL), mod_idx),
                pl.BlockSpec((1, D_MODEL), lambda i, j: (0, 0)),
                pl.BlockSpec((1, D_MODEL, P_TOTAL), lambda i, j: (layer, 0, 0))]
    args = [x, mod_l, g_pre, w_in_bf]
    if latent:
        for tab in tables:
            in_specs.append(pl.BlockSpec((tm, tab.shape[1]), lambda i, j: (j, 0)))
            args.append(tab)
    widths = (WA, WB, WC, WD)
    return pl.pallas_call(
        functools.partial(_proj_kernel, latent=latent),
        grid=(b, t // tm),
        in_specs=in_specs,
        out_specs=[pl.BlockSpec((1, tm, w), lambda i, j: (i, j, 0)) for w in widths],
        out_shape=[jax.ShapeDtypeStruct((b, t, w), F32) for w in widths],
        compiler_params=_params(("parallel", "parallel")),
        name="proj",
    )(*args)


def _rope_tables(t, head_dim, n_lanes):
    half = head_dim // 2
    quarter = half // 2
    pos = jnp.arange(t)
    row = (pos // GRID_W).astype(F32)
    col = (pos % GRID_W).astype(F32)
    inv = ROPE_BASE ** (-jnp.arange(0, half, 2, dtype=F32) / half)
    lane = np.arange(n_lanes) % head_dim
    in_half = lane % half
    p = jnp.where(jnp.asarray(lane < half)[None, :], row[:, None], col[:, None])
    ang = p * inv[in_half % quarter][None, :]
    sign = jnp.asarray(np.where(in_half < quarter, -1.0, 1.0), F32)[None, :]
    return jnp.cos(ang), jnp.sin(ang) * sign


_PREP_FIELDS = (('lhs', 2, 1), ('nm', 1, 1), ('mm', 1, 1), ('pq', 1, 2), ('v_hi', 1, 1), ('v_lo', 1, 1),
                ('bk_hi', 2, 1), ('bk_lo', 2, 1))


def _prep_scratch(slots, n_streams, n):
    shapes = [pltpu.VMEM((slots, n_streams, rows * n, lanes * n), BF16) for _, rows, lanes in _PREP_FIELDS]
    return shapes + [pltpu.VMEM((slots, n_streams, SUBLANES, W_BR), F32)]


class _SlotView:
    def __init__(self, ref, slot):
        self.ref, self.slot, self.shape = ref, slot, ref.shape[1:]

    def __getitem__(self, idx):
        return self.ref[(self.slot,) + (idx if isinstance(idx, tuple) else (idx,))]

    def __setitem__(self, idx, val):
        self.ref[(self.slot,) + (idx if isinstance(idx, tuple) else (idx,))] = val


def _rwkv_prepare(xs, dirs, w0_ref, wup_ref, a0_ref, aup_ref, k_k, k_a, m_ref, out):
    c = xs[0].shape[0]
    n_h = W_BR // HEAD
    n = n_h * c
    ones = _head_ones(W_BR)
    ti = lax.broadcasted_iota(jnp.int32, (c, c), 0)
    si = lax.broadcasted_iota(jnp.int32, (c, c), 1)
    cums = [jnp.where(si <= ti, 1.0, 0.0).astype(BF16), jnp.where(si >= ti, 1.0, 0.0).astype(BF16)]
    same = m_ref[_M_SAME]
    tile = lambda x: jnp.concatenate([x] * n_h, axis=0)
    expand = lambda x: tile(x) * same

    st = []
    for xc, d in zip(xs, dirs):
        r, k, v = xc[:, 0:256], xc[:, 256:512], xc[:, 512:768]
        wd, ad = xc[:, 768:832], xc[:, 832:896]
        kk = k * k_k
        kk = kk * lax.rsqrt(_mm2(kk * kk, ones) + 1e-12)
        z = w0_ref[d:d + 1] + _mm3(_split(jnp.tanh(wd)), _split(wup_ref[d]))
        e = jnp.exp(-_softplus(-z) - 0.5)
        a = _sigmoid(a0_ref[d:d + 1] + _mm3(_split(ad), _split(aup_ref[d])))
        st.append(dict(r=r, v=v, e=e, kd=k * (1.0 + (a - 1.0) * k_a), alpha=-kk, beta=kk * a))
    yield
    for s, d in zip(st, dirs):
        l_incl = _mm2(s['e'], cums[d], left=True)
        l_tot = jnp.sum(s['e'], axis=0, keepdims=True)
        grow = jnp.exp(l_incl)
        tail = jnp.exp(l_incl - l_tot)
        s.update(a_t=_bf(s['alpha'] * jnp.exp(s['e'] - l_incl)), r_t=_bf(s['r'] * jnp.exp(-l_incl)),
                 b_t=_bf(s['beta'] * grow), k_t=_bf(s['kd'] * grow),
                 b_h=_split(s['beta'] * tail), k_h=_split(s['kd'] * tail), g_c=jnp.exp(-l_tot))
    yield
    for g, s in enumerate(st):
        out['lhs'][g] = jnp.concatenate([expand(s['a_t']), expand(s['r_t'])], axis=0)
        s['rhs_t'] = jnp.concatenate([tile(s['b_t']), tile(s['k_t'])], axis=0)
        out['g_c'][g] = jnp.broadcast_to(s['g_c'], (SUBLANES, W_BR))
    yield
    for g, (s, d) in enumerate(zip(st, dirs)):
        g4 = _dotf(out['lhs'][g], s['rhs_t'], _NT)
        strict = m_ref[_M_STRICT + 2 * d]
        incl = m_ref[_M_INCL + 2 * d]
        out['nm'][g] = _bf(g4[:n, :n]) * strict
        out['mm'][g] = _bf(g4[:n, n:]) * strict
        out['pq'][g, :, :n] = _bf(g4[n:, :n]) * incl
        out['pq'][g, :, n:] = _bf(g4[n:, n:]) * incl
    yield
    for g, s in enumerate(st):
        v_hi, v_lo = _split(s['v'])
        out['v_hi'][g] = expand(v_hi)
        out['v_lo'][g] = expand(v_lo)
        out['bk_hi'][g] = jnp.concatenate([expand(s['b_h'][0]), expand(s['k_h'][0])], axis=0)
        out['bk_lo'][g] = jnp.concatenate([expand(s['b_h'][1]), expand(s['k_h'][1])], axis=0)


def _rwkv_advance(p, m_ref, s_ref, y_stores):
    n_streams, n = p['nm'].shape[0], p['nm'].shape[1]
    c = n // (W_BR // HEAD)
    gs = range(n_streams)
    invs = [(p['nm'][g] * m_ref[_M_PAIR]).astype(F32) + m_ref[_M_EYE].astype(F32) for g in gs]
    for lvl in range(int(math.log2(c)) - 1):
        inv_b = [_bf(t) for t in invs]
        half = [_bf(_dotf(p['nm'][g] * m_ref[_M_OFF + lvl], inv_b[g])) for g in gs]
        invs = [invs[g] + _dotf(inv_b[g], half[g]) for g in gs]
        yield
    s0 = [s_ref[g] for g in gs]
    as0 = [_dotf(p['lhs'][g], _bf(s0[g]), _NT) for g in gs]
    rhs = [_bf(as0[g][:n] + _dotf(p['mm'][g], p['v_hi'][g])) for g in gs]
    yield
    u = [_split(_dotf(_bf(invs[g]), rhs[g])) for g in gs]
    uv = [(jnp.concatenate([u[g][0], p['v_hi'][g]], axis=0), jnp.concatenate([u[g][1], p['v_lo'][g]], axis=0))
          for g in gs]
    yield
    for g in gs:
        s_ref[g] = s0[g] * p['g_c'][g, 0:1, :] + _mm3(uv[g], (p['bk_hi'][g], p['bk_lo'][g]), _TN)
    yield
    for g in gs:
        y_exp = as0[g][n:] + _dotf(p['pq'][g], uv[g][0])
        y = y_exp[0:c]
        for h in range(1, n // c):
            y = y + y_exp[h * c:(h + 1) * c]
        y_stores[g](y)


def _interleave(*gens):
    live = list(gens)
    while live:
        for gen in list(live):
            try:
                next(gen)
            except StopIteration:
                live.remove(gen)


_M_SAME, _M_STRICT, _M_INCL, _M_EYE, _M_PAIR, _M_OFF = 0, 1, 2, 5, 6, 7


def _rwkv_masks(c, n_h):
    n = c * n_h
    r = np.arange(n)[:, None]
    l = np.arange(n)[None, :]
    same = (r // c) == (l // c)
    t, s = r % c, l % c
    masks = [same, same & (s < t), same & (s <= t), same & (s > t), same & (s >= t),
             r == l, ((r // 2) == (l // 2)) & (r != l)]
    b = 2
    while b < c:
        masks.append(((r // (2 * b)) == (l // (2 * b))) & ((r // b) != (l // b)))
        b *= 2
    return jnp.asarray(np.stack(masks).astype(np.float32), BF16)


def _rwkv_kernel(*refs, latent, n_b):
    (xf_ref, xb_ref, xfn_ref, xbn_ref, w0_ref, wup_ref, a0_ref, aup_ref, kk_ref, ka_ref, m_ref) = refs[:11]
    if latent:
        s0_ref, yf_ref, yb_ref = refs[11:14]
        scr = refs[14:]
    else:
        yf_ref, yb_ref, st_ref = refs[11:14]
        scr = refs[14:]
    s_scr = scr[0]
    names = [f[0] for f in _PREP_FIELDS] + ['g_c']
    ci = pl.program_id(1)
    view = lambda refs, slot: {k: _SlotView(r, slot) for k, r in zip(names, refs)}
    cur = view(scr[1:1 + len(names)], ci % 2)
    nxt = view(scr[1:1 + len(names)], 1 - ci % 2)
    mid = view(scr[1 + len(names):], 0)
    n_h = W_BR // HEAD
    c = CHUNK
    diag = lambda h: (slice(h * HEAD, (h + 1) * HEAD),) * 2
    streams = [(bi, d) for bi in range(n_b) for d in range(2)]
    dirs = [d for _, d in streams]
    params = (w0_ref, wup_ref, a0_ref, aup_ref, kk_ref[...], ka_ref[...], m_ref)
    first = lambda d: slice(0, c) if d == 0 else slice(c, 2 * c)
    second = lambda d: slice(c, 2 * c) if d == 0 else slice(0, c)
    x_refs, xn_refs, y_refs = (xf_ref, xb_ref), (xfn_ref, xbn_ref), (yf_ref, yb_ref)

    def y_store(bi, d, rows):
        def store(y):
            y_refs[d][bi, rows, :] = y
        return store

    @pl.when(ci == 0)
    def _():
        s_scr[...] = jnp.zeros(s_scr.shape, F32)
        if latent:
            for g, (bi, d) in enumerate(streams):
                for h in range(n_h):
                    s_scr[(g,) + diag(h)] = s0_ref[bi, 0, d, h]
        _interleave(_rwkv_prepare([x_refs[d][bi, first(d), :] for bi, d in streams], dirs, *params, cur))

    _interleave(_rwkv_advance(cur, m_ref, s_scr, [y_store(bi, d, first(d)) for bi, d in streams]),
                _rwkv_prepare([x_refs[d][bi, second(d), :] for bi, d in streams], dirs, *params, mid))
    _interleave(_rwkv_advance(mid, m_ref, s_scr, [y_store(bi, d, second(d)) for bi, d in streams]),
                _rwkv_prepare([xn_refs[d][bi, first(d), :] for bi, d in streams], dirs, *params, nxt))

    if not latent:
        @pl.when(ci == pl.num_programs(1) - 1)
        def _():
            for g, (bi, d) in enumerate(streams):
                for h in range(n_h):
                    st_ref[bi, d, h] = s_scr[(g,) + diag(h)]


def _rwkv(pa, w0, wup, a0, aup, k_k, k_a, state, layer):
    b, t, _ = pa.shape
    latent = state is not None
    ns = t // (2 * CHUNK)
    n_h = W_BR // HEAD
    n_b = RWKV_BATCH
    masks = _rwkv_masks(CHUNK, n_h)
    full = lambda shape: pl.BlockSpec(shape, lambda i, j: (0,) * len(shape))
    rows = lambda w, idx: pl.BlockSpec((n_b, 2 * CHUNK, w), idx)
    in_specs = [rows(WA, lambda i, j: (i, j, 0)),
                rows(WA, lambda i, j: (i, ns - 1 - j, 0)),
                rows(WA, lambda i, j: (i, jnp.minimum(j + 1, ns - 1), 0)),
                rows(WA, lambda i, j: (i, jnp.maximum(ns - 2 - j, 0), 0)),
                full((2, W_BR)), full((2, LORA, W_BR)), full((2, W_BR)), full((2, LORA, W_BR)),
                full((1, W_BR)), full((1, W_BR)), full(masks.shape)]
    args = [pa, pa, pa, pa, w0, wup, a0, aup, k_k, k_a, masks]
    out_specs = [rows(W_BR, lambda i, j: (i, j, 0)), rows(W_BR, lambda i, j: (i, ns - 1 - j, 0))]
    out_shape = [jax.ShapeDtypeStruct((b, t, W_BR), F32)] * 2
    if latent:
        in_specs.append(pl.BlockSpec((n_b, 1, 2, n_h, HEAD, HEAD), lambda i, j: (i, layer, 0, 0, 0, 0)))
        args.append(state)
    else:
        out_specs.append(pl.BlockSpec((n_b, 2, n_h, HEAD, HEAD), lambda i, j: (i, 0, 0, 0, 0)))
        out_shape.append(jax.ShapeDtypeStruct((b, 2, n_h, HEAD, HEAD), F32))
    return pl.pallas_call(
        functools.partial(_rwkv_kernel, latent=latent, n_b=n_b),
        grid=(b // n_b, ns),
        in_specs=in_specs,
        out_specs=out_specs,
        out_shape=out_shape,
        scratch_shapes=([pltpu.VMEM((2 * n_b, W_BR, W_BR), F32)]
                        + _prep_scratch(2, 2 * n_b, n_h * CHUNK) + _prep_scratch(1, 2 * n_b, n_h * CHUNK)),
        compiler_params=_params(("parallel", "arbitrary")),
        name="rwkv",
    )(*args)


def _lru_kernel(*refs, latent):
    if latent:
        (x_ref, cw_ref, cb_ref, wa_ref, ba_ref, wx_ref, bx_ref, lam_ref, h0_ref, y_ref,
         a_scr, h_scr) = refs
    else:
        (x_ref, cw_ref, cb_ref, wa_ref, ba_ref, wx_ref, bx_ref, lam_ref, y_ref, st_ref,
         a_scr, h_scr) = refs
    x = x_ref[0, :, :W_BR]
    t = x.shape[0]
    row = lax.broadcasted_iota(jnp.int32, x.shape, 0)

    def shift_dn(z, k, fill):
        return jnp.where(row >= k, pltpu.roll(z, k, 0), fill)

    def shift_up(z, k, fill):
        return jnp.where(row < t - k, pltpu.roll(z, t - k, 0), fill)

    cw = cw_ref[...]
    xc = (cb_ref[...] + shift_dn(x, 2, 0.0) * cw[0:1] + shift_dn(x, 1, 0.0) * cw[1:2]
          + x * cw[2:3] + shift_up(x, 1, 0.0) * cw[3:4])
    xs = _split(xc)
    sub = row % SUBLANES
    for d in range(2):
        gate_a = _sigmoid(_mm3(xs, _split(wa_ref[d])) + ba_ref[d:d + 1])
        gate_x = _sigmoid(_mm3(xs, _split(wx_ref[d])) + bx_ref[d:d + 1])
        log_a = -C_RG * gate_a * _softplus(-lam_ref[d:d + 1])
        a = jnp.exp(log_a)
        u = jnp.sqrt(-jnp.tanh(log_a) * (a * a + 1.0)) * (gate_x * xc)
        k = 1
        while k < SUBLANES:
            if d == 0:
                keep = sub >= k
                sh = lambda z: pltpu.roll(z, k, 0)
            else:
                keep = sub < SUBLANES - k
                sh = lambda z: pltpu.roll(z, t - k, 0)
            u = a * jnp.where(keep, sh(u), 0.0) + u
            a = a * jnp.where(keep, sh(a), 1.0)
            k *= 2
        a_scr[d] = a
        h_scr[d] = u

    n_grp = t // SUBLANES
    if latent:
        carry0 = (h0_ref[0, 0, 0:1], h0_ref[0, 0, 1:2])
    else:
        carry0 = (jnp.zeros((1, W_BR), F32),) * 2

    def chain(i, carry):
        cf, cb = carry
        rf = pl.ds(pl.multiple_of(i * SUBLANES, SUBLANES), SUBLANES)
        rb = pl.ds(pl.multiple_of((n_grp - 1 - i) * SUBLANES, SUBLANES), SUBLANES)
        hf = h_scr[0, rf, :] + a_scr[0, rf, :] * cf
        hb = h_scr[1, rb, :] + a_scr[1, rb, :] * cb
        h_scr[0, rf, :] = hf
        h_scr[1, rb, :] = hb
        return hf[SUBLANES - 1:SUBLANES], hb[0:1]

    cf, cb = lax.fori_loop(0, n_grp, chain, carry0, unroll=4)
    y_ref[0] = h_scr[0] + h_scr[1]
    if not latent:
        st_ref[0] = jnp.concatenate([cf, cb], axis=0)


def _lru(pc, cw, cb, wa, ba, wx, bx, lam, state, layer):
    b, t, _ = pc.shape
    latent = state is not None
    n_blk = W_BR // HEAD
    full = lambda shape: pl.BlockSpec(shape, lambda i: (0,) * len(shape))
    in_specs = [pl.BlockSpec((1, t, WC), lambda i: (i, 0, 0)),
                full((4, W_BR)), full((1, W_BR)), full((2, W_BR, W_BR)), full((2, W_BR)),
                full((2, W_BR, W_BR)), full((2, W_BR)), full((2, W_BR))]

    def block_diag(w):
        out = jnp.zeros((2, W_BR, W_BR), F32)
        for n in range(n_blk):
            out = out.at[:, n * HEAD:(n + 1) * HEAD, n * HEAD:(n + 1) * HEAD].set(w[:, n])
        return out

    args = [pc, cw, cb, block_diag(wa), ba, block_diag(wx), bx, lam]
    out_specs = [pl.BlockSpec((1, t, W_BR), lambda i: (i, 0, 0))]
    out_shape = [jax.ShapeDtypeStruct((b, t, W_BR), F32)]
    if latent:
        in_specs.append(pl.BlockSpec((1, 1, 2, W_BR), lambda i: (i, layer, 0, 0)))
        args.append(state)
    else:
        out_specs.append(pl.BlockSpec((1, 2, W_BR), lambda i: (i, 0, 0)))
        out_shape.append(jax.ShapeDtypeStruct((b, 2, W_BR), F32))
    return pl.pallas_call(
        functools.partial(_lru_kernel, latent=latent),
        grid=(b,),
        in_specs=in_specs,
        out_specs=out_specs,
        out_shape=out_shape,
        scratch_shapes=[pltpu.VMEM((2, t, W_BR), F32), pltpu.VMEM((2, t, W_BR), F32)],
        compiler_params=_params(("parallel",)),
        name="lru",
    )(*args)


def _softmax_pv(blocks, sink):
    m = None
    for s, _ in blocks:
        mi = jnp.max(s, axis=-1, keepdims=True)
        m = mi if m is None else jnp.maximum(m, mi)
    if sink is not None:
        m = jnp.maximum(m, sink)
    den = None if sink is None else jnp.exp(sink - m)
    acc = None
    for s, v in blocks:
        e = jnp.exp(s - m)
        di = jnp.sum(e, axis=-1, keepdims=True)
        den = di if den is None else den + di
        o = _dot_bf(e, v)
        acc = o if acc is None else acc + o
    return acc, den


def _win_kernel(*refs, latent):
    scale = HEAD ** -0.5
    if latent:
        sink_ref, q_ref, kp_ref, kc_ref, kn_ref, ck_ref, cv_ref, o_ref = refs
        j = pl.program_id(1)
        nq = pl.num_programs(1)
        qi = lax.broadcasted_iota(jnp.int32, (Q_BLK, Q_BLK), 0)
        ki = lax.broadcasted_iota(jnp.int32, (Q_BLK, Q_BLK), 1)
        kv_blocks = [(kp_ref[0], (ki >= qi) & (j > 0)), (kc_ref[0], None),
                     (kn_ref[0], (ki <= qi) & (j < nq - 1))]
        ctx_k, ctx_v = ck_ref[0, 0], cv_ref[0, 0]
    else:
        sink_ref, x_ref, o_ref = refs
    q = q_ref[0] if latent else x_ref[0, :, :256]
    outs = []
    for h in range(2):
        hs = slice(h * HEAD, (h + 1) * HEAD)
        for g in range(2):
            qg = q[:, (2 * h + g) * HEAD:(2 * h + g + 1) * HEAD]
            blocks = []
            if latent:
                for kv, mask in kv_blocks:
                    s = _dot_nt_bf(qg, kv[:, hs]) * scale
                    if mask is not None:
                        s = jnp.where(mask, s, NEG_INF)
                    blocks.append((s, kv[:, 128 + h * HEAD:128 + (h + 1) * HEAD]))
                blocks.append((_dot_nt_bf(qg, ctx_k[:, hs]) * scale, ctx_v[:, hs]))
            else:
                k = x_ref[0, :, 256 + h * HEAD:256 + (h + 1) * HEAD]
                v = x_ref[0, :, 384 + h * HEAD:384 + (h + 1) * HEAD]
                blocks.append((_dot_nt_bf(qg, k) * scale, v))
            acc, den = _softmax_pv(blocks, sink_ref[2 * h + g])
            outs.append(acc / den)
    o_ref[0] = jnp.concatenate(outs, axis=-1)


def _win(pb, sink, cache_k, cache_v, layer):
    b, t, _ = pb.shape
    latent = cache_k is not None
    smem = pl.BlockSpec(memory_space=pltpu.SMEM)
    if latent:
        nq = t // Q_BLK
        grid = (b, nq)
        in_specs = [smem,
                    pl.BlockSpec((1, Q_BLK, 256), lambda i, j: (i, j, 0)),
                    pl.BlockSpec((1, Q_BLK, 256), lambda i, j: (i, jnp.maximum(j - 1, 0), 1)),
                    pl.BlockSpec((1, Q_BLK, 256), lambda i, j: (i, j, 1)),
                    pl.BlockSpec((1, Q_BLK, 256), lambda i, j: (i, jnp.minimum(j + 1, nq - 1), 1)),
                    pl.BlockSpec((1, 1) + cache_k.shape[2:], lambda i, j: (i, layer, 0, 0)),
                    pl.BlockSpec((1, 1) + cache_v.shape[2:], lambda i, j: (i, layer, 0, 0))]
        args = [sink, pb, pb, pb, pb, cache_k, cache_v]
        out_spec = pl.BlockSpec((1, Q_BLK, W_BR), lambda i, j: (i, j, 0))
        sem = ("parallel", "parallel")
    else:
        grid = (b,)
        in_specs = [smem, pl.BlockSpec((1, t, WB), lambda i: (i, 0, 0))]
        args = [sink, pb]
        out_spec = pl.BlockSpec((1, t, W_BR), lambda i: (i, 0, 0))
        sem = ("parallel",)
    return pl.pallas_call(
        functools.partial(_win_kernel, latent=latent),
        grid=grid,
        in_specs=in_specs,
        out_specs=out_spec,
        out_shape=jax.ShapeDtypeStruct((b, t, W_BR), F32),
        compiler_params=_params(sem),
        name="win",
    )(*args)


def _diff_kernel(*refs, latent, lam_init):
    if latent:
        lam_ref, g_ref, q_ref, k_ref, v_ref, ck_ref, cv_ref, o_ref = refs
    else:
        lam_ref, g_ref, q_ref, k_ref, v_ref, o_ref = refs
    scale = DQ_D ** -0.5
    lp = lam_ref[...]
    lam = (jnp.exp(jnp.sum(lp[0:1] * lp[1:2], axis=-1, keepdims=True))
           - jnp.exp(jnp.sum(lp[2:3] * lp[3:4], axis=-1, keepdims=True)) + lam_init)
    q, k, v = q_ref[0], k_ref[0], v_ref[0]
    key_sets = [(k, v)]
    if latent:
        key_sets.append((ck_ref[0, 0], cv_ref[0, 0]))
    outs = []
    for h in range(W_BR // HEAD):
        hs = slice(h * HEAD, (h + 1) * HEAD)
        probs = []
        for m in range(2):
            ms = slice(h * HEAD + m * DQ_D, h * HEAD + (m + 1) * DQ_D)
            logits = [_dot_nt_bf(q[:, ms], kk[:, ms]) * scale for kk, _ in key_sets]
            mx = functools.reduce(jnp.maximum, [jnp.max(s, axis=-1, keepdims=True) for s in logits])
            es = [jnp.exp(s - mx) for s in logits]
            den = functools.reduce(lambda a, b: a + b, [jnp.sum(e, axis=-1, keepdims=True) for e in es])
            inv = 1.0 / den
            probs.append([e * inv for e in es])
        o = None
        for i, (_, vv) in enumerate(key_sets):
            oi = _dot_bf(probs[0][i] - lam * probs[1][i], vv[:, hs])
            o = oi if o is None else o + oi
        o = o * lax.rsqrt(jnp.mean(o * o, -1, keepdims=True) + NORM_EPS) * g_ref[...]
        outs.append(o * (1.0 - lam_init))
    o_ref[0] = jnp.concatenate(outs, axis=-1)


def _diff(pd, lam_p, subln_g, cache_k, cache_v, layer, lam_init):
    b, t, _ = pd.shape
    latent = cache_k is not None
    tq = 256
    in_specs = [pl.BlockSpec((4, DQ_D), lambda i, j: (0, 0)),
                pl.BlockSpec((1, HEAD), lambda i, j: (0, 0)),
                pl.BlockSpec((1, tq, 256), lambda i, j: (i, j, 0)),
                pl.BlockSpec((1, t, 256), lambda i, j: (i, 0, 1)),
                pl.BlockSpec((1, t, 256), lambda i, j: (i, 0, 2))]
    args = [lam_p, subln_g, pd, pd, pd]
    if latent:
        in_specs += [pl.BlockSpec((1, 1) + cache_k.shape[2:], lambda i, j: (i, layer, 0, 0)),
                     pl.BlockSpec((1, 1) + cache_v.shape[2:], lambda i, j: (i, layer, 0, 0))]
        args += [cache_k, cache_v]
    return pl.pallas_call(
        functools.partial(_diff_kernel, latent=latent, lam_init=lam_init),
        grid=(b, t // tq),
        in_specs=in_specs,
        out_specs=pl.BlockSpec((1, tq, W_BR), lambda i, j: (i, j, 0)),
        out_shape=jax.ShapeDtypeStruct((b, t, W_BR), F32),
        compiler_params=_params(("parallel", "parallel")),
        name="diff",
    )(*args)


def _out_kernel(x_ref, mod_ref, gpost_ref, w_ref, pa_ref, yf_ref, yb_ref, gng_ref, gnb_ref, rk_ref,
                ywin_ref, bg_ref, ylru_ref, cg_ref, ydiff_ref, dg_ref, o_ref):
    ones = _head_ones(W_BR)
    pa = pa_ref[0]
    r, k, v, ag = pa[:, 0:256], pa[:, 256:512], pa[:, 512:768], pa[:, 896:1152]
    y = yf_ref[0] + yb_ref[0]
    mu = _mm2(y, ones) * (1.0 / HEAD)
    yc = y - mu
    var = _mm2(yc * yc, ones) * (1.0 / HEAD)
    ya = yc * lax.rsqrt(var + GN_EPS) * gng_ref[...] + gnb_ref[...]
    ya = ya + _mm2(r * k * rk_ref[...], ones) * v
    mix = jnp.concatenate([ya * _silu(ag), ywin_ref[0] * _silu(bg_ref[0]),
                           ylru_ref[0] * _silu(cg_ref[0]), ydiff_ref[0] * _silu(dg_ref[0])], axis=-1)
    o = jnp.dot(mix.astype(BF16), w_ref[0], preferred_element_type=F32)
    o = o * lax.rsqrt(jnp.mean(o * o, -1, keepdims=True) + NORM_EPS) * gpost_ref[...]
    o_ref[0] = x_ref[0] + mod_ref[0][:, 2 * D_MODEL:] * o


def _out(x, mod_l, g_post, w_out_bf, layer, latent, pa, yf, yb, gn_g, gn_b, r_k, ywin, pb, ylru, pc, ydiff, pd):
    b, t, _ = x.shape
    tm = ROW_TILE
    mod_idx = (lambda i, j: (1 + i, 0, 0)) if latent else (lambda i, j: (0, 0, 0))
    rows = lambda w, c=0: pl.BlockSpec((1, tm, w), lambda i, j: (i, j, c))
    vec = pl.BlockSpec((1, W_BR), lambda i, j: (0, 0))
    in_specs = [rows(D_MODEL),
                pl.BlockSpec((1, 1, 3 * D_MODEL), mod_idx),
                pl.BlockSpec((1, D_MODEL), lambda i, j: (0, 0)),
                pl.BlockSpec((1, D_MODEL, D_MODEL), lambda i, j: (layer, 0, 0)),
                rows(WA), rows(W_BR), rows(W_BR), vec, vec, vec,
                rows(W_BR), rows(W_BR, 2), rows(W_BR), rows(W_BR, 1), rows(W_BR), rows(W_BR, 3)]
    return pl.pallas_call(
        _out_kernel,
        grid=(b, t // tm),
        in_specs=in_specs,
        out_specs=rows(D_MODEL),
        out_shape=jax.ShapeDtypeStruct((b, t, D_MODEL), F32),
        compiler_params=_params(("parallel", "parallel")),
        name="out",
    )(x, mod_l, g_post, w_out_bf, pa, yf, yb, gn_g, gn_b, r_k, ywin, pb, ylru, pc, ydiff, pd)


def _layer(x, mod_l, layer, lam_init, wts, cache, tables):
    latent = cache is not None
    row = lambda a: a[layer].reshape(1, -1)
    pa, pb, pc, pd = _project(x, mod_l, row(wts['g_pre']), wts['w_in_bf'], layer, tables)
    rw = _rwkv(pa, wts['rwkv_w0'][layer], wts['rwkv_w_up'][layer], wts['rwkv_a0'][layer],
               wts['rwkv_a_up'][layer], row(wts['rwkv_k_k']), row(wts['rwkv_k_a']),
               cache['rwkv'] if latent else None, layer)
    lr = _lru(pc, wts['lru_conv_w'][layer], row(wts['lru_conv_b']), wts['lru_wa'][layer],
              wts['lru_ba'][layer], wts['lru_wx'][layer], wts['lru_bx'][layer], wts['lru_lambda'][layer],
              cache['lru'] if latent else None, layer)
    ywin = _win(pb, wts['win_sink'][layer], cache['win_k'] if latent else None,
                cache['win_v'] if latent else None, layer)
    ydiff = _diff(pd, wts['diff_lambda'][layer], row(wts['diff_subln_g']),
                  cache['diff_k'] if latent else None, cache['diff_v'] if latent else None, layer, lam_init)
    y = _out(x, mod_l, row(wts['g_post']), wts['w_out_bf'], layer, latent, pa, rw[0], rw[1],
             row(wts['rwkv_gn_g']), row(wts['rwkv_gn_b']), row(wts['rwkv_r_k']),
             ywin, pb, lr[0], pc, ydiff, pd)
    new_cache = None if latent else (pb, pd, rw[2], lr[1])
    return y, new_cache


def kernel(x_prompt, x_sample, c, cache_win_k, cache_win_v, cache_diff_k, cache_diff_v, state_rwkv, state_lru,
           c_ctx, w_mod, b_mod, g_pre, g_post, w_in, w_out,
           rwkv_w0, rwkv_w_up, rwkv_a0, rwkv_a_up, rwkv_k_k, rwkv_k_a, rwkv_r_k, rwkv_gn_g, rwkv_gn_b,
           win_sink, lru_conv_w, lru_conv_b, lru_wa, lru_ba, lru_wx, lru_bx, lru_lambda,
           diff_lambda, diff_subln_g):
    n_b, seq = x_prompt.shape[:2]
    n_dec, dec_seq = x_sample.shape[:2]
    past = cache_win_k.shape[2]
    wts = dict(g_pre=g_pre, g_post=g_post, w_in_bf=w_in.astype(BF16), w_out_bf=w_out.astype(BF16),
               rwkv_w0=rwkv_w0, rwkv_w_up=rwkv_w_up, rwkv_a0=rwkv_a0, rwkv_a_up=rwkv_a_up,
               rwkv_k_k=rwkv_k_k, rwkv_k_a=rwkv_k_a, rwkv_r_k=rwkv_r_k, rwkv_gn_g=rwkv_gn_g,
               rwkv_gn_b=rwkv_gn_b, win_sink=win_sink, lru_conv_w=lru_conv_w, lru_conv_b=lru_conv_b,
               lru_wa=lru_wa, lru_ba=lru_ba, lru_wx=lru_wx, lru_bx=lru_bx, lru_lambda=lru_lambda,
               diff_lambda=diff_lambda, diff_subln_g=diff_subln_g)
    cache = dict(win_k=cache_win_k.reshape(n_dec, DEPTH, past, 128),
                 win_v=cache_win_v.reshape(n_dec, DEPTH, past, 128),
                 diff_k=cache_diff_k.reshape(n_dec, DEPTH, past, 256),
                 diff_v=cache_diff_v.reshape(n_dec, DEPTH, past, 256),
                 rwkv=state_rwkv, lru=state_lru)
    cvec = jnp.concatenate([c_ctx[None], c, jnp.zeros((8 - 1 - n_dec, D_MODEL), F32)], axis=0)
    mod = _modulation(cvec, w_mod, b_mod)
    cos_b, sin_b = _rope_tables(dec_seq, HEAD, 384)
    cos_d, sin_d = _rope_tables(dec_seq, DQ_D, 512)
    tables = (cos_b, sin_b, cos_d, sin_d)

    y_p, y_s = x_prompt, x_sample
    ctx = []
    for l in range(DEPTH):
        lam_init = 0.8 - 0.6 * math.exp(-0.3 * l)
        mod_l = mod[l].reshape(8, 1, 3 * D_MODEL)
        y_p, nc = _layer(y_p, mod_l, l, lam_init, wts, None, None)
        ctx.append(nc)
        y_s, _ = _layer(y_s, mod_l, l, lam_init, wts, cache, tables)
    stack = lambda f: jnp.stack([f(ct) for ct in ctx], axis=1)
    new_win_k = stack(lambda ct: ct[0][:, :, 256:384].reshape(n_b, seq, 2, HEAD))
    new_win_v = stack(lambda ct: ct[0][:, :, 384:512].reshape(n_b, seq, 2, HEAD))
    new_diff_k = stack(lambda ct: ct[1][:, :, 256:512].reshape(n_b, seq, 4, 2, DQ_D))
    new_diff_v = stack(lambda ct: ct[1][:, :, 512:768].reshape(n_b, seq, 4, HEAD))
    new_state_rwkv = stack(lambda ct: ct[2])
    new_state_lru = stack(lambda ct: ct[3])
    return (y_p, y_s, new_win_k, new_win_v, new_diff_k, new_diff_v, new_state_rwkv, new_state_lru)
```

```python
import functools
import math

import numpy as np
import jax
import jax.numpy as jnp
from jax import lax
from jax.experimental import pallas as pl
from jax.experimental.pallas import tpu as pltpu

F32 = jnp.float32
BF16 = jnp.bfloat16
HI = lax.Precision.HIGHEST

D_MODEL = 1024
DEPTH = 2
GRID_W = 64
ROPE_BASE = 10000.0
NORM_EPS = 1e-6
NEG_INF = -1e30
GN_EPS = 64e-5
C_RG = 8.0
W_BR = 256
HEAD = 64
SUBLANES = 8
LORA = 64
DQ_D = 32
WINDOW = 128
Q_BLK = 128
P_TOTAL = 3456
WA, WB, WC, WD = 1152, 768, 512, 1024
CHUNK = 64
RWKV_BATCH = 2
ROW_TILE = 256
VMEM_LIMIT = 48 * 1024 * 1024


_NN = (((1,), (0,)), ((), ()))
_NT = (((1,), (1,)), ((), ()))
_TN = (((0,), (0,)), ((), ()))


def _dot_hi(a, b):
    return jnp.dot(a, b, precision=HI, preferred_element_type=F32)


def _bf(x):
    return x.astype(BF16)


def _dotf(x, y, dims=_NN):
    return lax.dot_general(x, y, dims, preferred_element_type=F32)


def _split(x):
    hi = x.astype(BF16)
    return hi, (x - hi.astype(F32)).astype(BF16)


def _mm3(a, b, dims=_NN):
    return _dotf(a[0], b[0], dims) + (_dotf(a[0], b[1], dims) + _dotf(a[1], b[0], dims))


def _mm2(x, m01, left=False):
    hi, lo = _split(x)
    if left:
        return _dotf(m01, hi) + _dotf(m01, lo)
    return _dotf(hi, m01) + _dotf(lo, m01)


def _dot_bf(a, b):
    return jnp.dot(a.astype(BF16), b.astype(BF16), preferred_element_type=F32)


def _dot_nt_bf(a, b):
    return lax.dot_general(a.astype(BF16), b.astype(BF16), (((1,), (1,)), ((), ())),
                           preferred_element_type=F32)


def _sigmoid(x):
    return 1.0 / (1.0 + jnp.exp(-x))


def _silu(x):
    return x * _sigmoid(x)


def _softplus(x):
    return jnp.maximum(x, 0.0) + jnp.log1p(jnp.exp(-jnp.abs(x)))


def _head_ones(n):
    r = lax.broadcasted_iota(jnp.int32, (n, n), 0) // HEAD
    c = lax.broadcasted_iota(jnp.int32, (n, n), 1) // HEAD
    return jnp.where(r == c, 1.0, 0.0).astype(BF16)


def _params(sem):
    return pltpu.CompilerParams(dimension_semantics=sem, vmem_limit_bytes=VMEM_LIMIT)


def _mod_kernel(c_ref, w_ref, b_ref, o_ref):
    o_ref[0] = _dot_hi(_silu(c_ref[...]), w_ref[0]) + b_ref[0]


def _modulation(cvec, w_mod, b_mod):
    n_l = w_mod.shape[0]
    tn = 512
    return pl.pallas_call(
        _mod_kernel,
        grid=(n_l, 3 * D_MODEL // tn),
        in_specs=[pl.BlockSpec((8, D_MODEL), lambda l, j: (0, 0)),
                  pl.BlockSpec((1, D_MODEL, tn), lambda l, j: (l, 0, j)),
                  pl.BlockSpec((1, 1, tn), lambda l, j: (l, 0, j))],
        out_specs=pl.BlockSpec((1, 8, tn), lambda l, j: (l, 0, j)),
        out_shape=jax.ShapeDtypeStruct((n_l, 8, 3 * D_MODEL), F32),
        compiler_params=_params(("parallel", "parallel")),
        name="mod",
    )(cvec, w_mod, b_mod.reshape(n_l, 1, 3 * D_MODEL))


def _rope(x, cos, sin_signed, off):
    w = x.shape[-1]
    lane = lax.broadcasted_iota(jnp.int32, x.shape, 1)
    first = (lane % (2 * off)) < off
    partner = jnp.where(first, pltpu.roll(x, w - off, 1), pltpu.roll(x, off, 1))
    return x * cos + partner * sin_signed


def _proj_kernel(*refs, latent):
    if latent:
        x_ref, mod_ref, g_ref, w_ref, cb_ref, sb_ref, cd_ref, sd_ref, oa, ob, oc, od = refs
    else:
        x_ref, mod_ref, g_ref, w_ref, oa, ob, oc, od = refs
    x = x_ref[0]
    y = x * lax.rsqrt(jnp.mean(x * x, -1, keepdims=True) + NORM_EPS) * g_ref[...]
    m = mod_ref[0]
    h = y * (1.0 + m[:, D_MODEL:2 * D_MODEL]) + m[:, :D_MODEL]
    p = jnp.dot(h.astype(BF16), w_ref[0], preferred_element_type=F32)
    oa[0] = p[:, :WA]
    pb = p[:, WA:WA + WB]
    pd = p[:, WA + WB + WC:]
    if latent:
        ob[0, :, :384] = _rope(pb[:, :384], cb_ref[...], sb_ref[...], 16)
        ob[0, :, 384:] = pb[:, 384:]
        od[0, :, :512] = _rope(pd[:, :512], cd_ref[...], sd_ref[...], 8)
        od[0, :, 512:] = pd[:, 512:]
    else:
        ob[0] = pb
        od[0] = pd
    oc[0] = p[:, WA + WB:WA + WB + WC]


def _project(x, mod_l, g_pre, w_in_bf, layer, tables):
    b, t, _ = x.shape
    latent = tables is not None
    tm = ROW_TILE
    mod_idx = (lambda i, j: (1 + i, 0, 0)) if latent else (lambda i, j: (0, 0, 0))
    in_specs = [pl.BlockSpec((1, tm, D_MODEL), lambda i, j: (i, j, 0)),
                pl.BlockSpec((1, 1, 3 * D_MODEL), mod_idx),
                pl.BlockSpec((1, D_MODEL), lambda i, j: (0, 0)),
                pl.BlockSpec((1, D_MODEL, P_TOTAL), lambda i, j: (layer, 0, 0))]
    args = [x, mod_l, g_pre, w_in_bf]
    if latent:
        for tab in tables:
            in_specs.append(pl.BlockSpec((tm, tab.shape[1]), lambda i, j: (j, 0)))
            args.append(tab)
    widths = (WA, WB, WC, WD)
    return pl.pallas_call(
        functools.partial(_proj_kernel, latent=latent),
        grid=(b, t // tm),
        in_specs=in_specs,
        out_specs=[pl.BlockSpec((1, tm, w), lambda i, j: (i, j, 0)) for w in widths],
        out_shape=[jax.ShapeDtypeStruct((b, t, w), F32) for w in widths],
        compiler_params=_params(("parallel", "parallel")),
        name="proj",
    )(*args)


def _rope_tables(t, head_dim, n_lanes):
    half = head_dim // 2
    quarter = half // 2
    pos = jnp.arange(t)
    row = (pos // GRID_W).astype(F32)
    col = (pos % GRID_W).astype(F32)
    inv = ROPE_BASE ** (-jnp.arange(0, half, 2, dtype=F32) / half)
    lane = np.arange(n_lanes) % head_dim
    in_half = lane % half
    p = jnp.where(jnp.asarray(lane < half)[None, :], row[:, None], col[:, None])
    ang = p * inv[in_half % quarter][None, :]
    sign = jnp.asarray(np.where(in_half < quarter, -1.0, 1.0), F32)[None, :]
    return jnp.cos(ang), jnp.sin(ang) * sign


_PREP_FIELDS = (('ar', 2, 1, BF16), ('nm', 1, 1, BF16), ('mm', 1, 1, BF16), ('pq', 1, 2, BF16),
                ('v_hi', 1, 1, BF16), ('v_lo', 1, 1, BF16), ('bkt_hi', 1, 2, BF16), ('bkt_lo', 1, 2, BF16),
                ('gct', 1, 1, F32))


def _prep_scratch(slots, n_streams):
    return [pltpu.VMEM((slots, n_streams, rows * CHUNK, lanes * W_BR), dt) for _, rows, lanes, dt in _PREP_FIELDS]


class _SlotView:
    def __init__(self, ref, slot):
        self.ref, self.slot, self.shape = ref, slot, ref.shape[1:]

    def __getitem__(self, idx):
        return self.ref[(self.slot,) + (idx if isinstance(idx, tuple) else (idx,))]

    def __setitem__(self, idx, val):
        self.ref[(self.slot,) + (idx if isinstance(idx, tuple) else (idx,))] = val


def _head_transpose(x):
    xt = x.T
    return jnp.concatenate([xt[h * HEAD:(h + 1) * HEAD] for h in range(x.shape[1] // HEAD)], axis=1)


def _expand(x, same):
    return jnp.concatenate([x] * (same.shape[0] // x.shape[0]), axis=0) * same


def _rwkv_prepare(xs, dirs, w0_ref, wup_ref, a0_ref, aup_ref, k_k, k_a, m_ref, same_ref, out):
    c = xs[0].shape[0]
    ones = _head_ones(W_BR)
    ti = lax.broadcasted_iota(jnp.int32, (c, c), 0)
    si = lax.broadcasted_iota(jnp.int32, (c, c), 1)
    cums = [jnp.where(si <= ti, 1.0, 0.0).astype(BF16), jnp.where(si >= ti, 1.0, 0.0).astype(BF16)]
    same = same_ref[...]

    st = []
    for xc, d in zip(xs, dirs):
        r, k, v = xc[:, 0:256], xc[:, 256:512], xc[:, 512:768]
        wd, ad = xc[:, 768:832], xc[:, 832:896]
        kk = k * k_k
        kk = kk * lax.rsqrt(_mm2(kk * kk, ones) + 1e-12)
        z = w0_ref[d:d + 1] + _mm3(_split(jnp.tanh(wd)), _split(wup_ref[d]))
        e = jnp.exp(-_softplus(-z) - 0.5)
        a = _sigmoid(a0_ref[d:d + 1] + _mm3(_split(ad), _split(aup_ref[d])))
        st.append(dict(r=r, v=v, e=e, kd=k * (1.0 + (a - 1.0) * k_a), alpha=-kk, beta=kk * a))
    yield
    for s, d in zip(st, dirs):
        l_incl = _mm2(s['e'], cums[d], left=True)
        l_tot = jnp.sum(s['e'], axis=0, keepdims=True)
        grow = jnp.exp(l_incl)
        tail = jnp.exp(l_incl - l_tot)
        s.update(ar=jnp.concatenate([_bf(s['alpha'] * jnp.exp(s['e'] - l_incl)),
                                     _bf(s['r'] * jnp.exp(-l_incl))], axis=0),
                 b_t=_bf(s['beta'] * grow), k_t=_bf(s['kd'] * grow),
                 b_h=s['beta'] * tail, k_h=s['kd'] * tail,
                 g_c=jnp.broadcast_to(jnp.exp(-l_tot), (c, W_BR)))
    yield
    for g, (s, d) in enumerate(zip(st, dirs)):
        out['ar'][g] = s['ar']
        rhs_t = jnp.concatenate([_expand(s['b_t'], same), _expand(s['k_t'], same)], axis=0)
        g4 = _dotf(s['ar'], rhs_t, _NT)
        strict = m_ref[_M_STRICT + 2 * d]
        incl = m_ref[_M_INCL + 2 * d]
        out['nm'][g] = _bf(g4[:c, :W_BR]) * strict
        out['mm'][g] = _bf(g4[:c, W_BR:]) * strict
        out['pq'][g, :, :W_BR] = _bf(g4[c:, :W_BR]) * incl
        out['pq'][g, :, W_BR:] = _bf(g4[c:, W_BR:]) * incl
    yield
    for g, s in enumerate(st):
        v_hi, v_lo = _split(s['v'])
        out['v_hi'][g] = v_hi
        out['v_lo'][g] = v_lo
        b_hi, b_lo = _split(_head_transpose(s['b_h']))
        k_hi, k_lo = _split(_head_transpose(s['k_h']))
        out['bkt_hi'][g] = jnp.concatenate([b_hi, k_hi], axis=1)
        out['bkt_lo'][g] = jnp.concatenate([b_lo, k_lo], axis=1)
        out['gct'][g] = _head_transpose(s['g_c'])


def _rwkv_advance(p, m_ref, same_ref, z_ref, y_stores):
    n_streams, c = p['nm'].shape[0], p['nm'].shape[1]
    gs = range(n_streams)
    same = same_ref[...]
    ex = lambda x: _expand(_bf(x), same)
    invs = [(p['nm'][g] * m_ref[_M_PAIR]).astype(F32) + m_ref[_M_EYE].astype(F32) for g in gs]
    for lvl in range(int(math.log2(c)) - 1):
        inv_x = [ex(t) for t in invs]
        half = [_dotf(p['nm'][g] * m_ref[_M_OFF + lvl], inv_x[g]) for g in gs]
        invs = [invs[g] + _dotf(_bf(invs[g]), ex(half[g])) for g in gs]
        yield
    z0 = [z_ref[g] for g in gs]
    as0 = [_dotf(p['ar'][g], ex(z0[g])) for g in gs]
    vx = [(_expand(p['v_hi'][g], same), _expand(p['v_lo'][g], same)) for g in gs]
    rhs = [as0[g][:c] + _dotf(p['mm'][g], vx[g][0]) for g in gs]
    yield
    u = [_split(_dotf(_bf(invs[g]), ex(rhs[g]))) for g in gs]
    uv = [(jnp.concatenate([_expand(u[g][0], same), vx[g][0]], axis=0),
           jnp.concatenate([_expand(u[g][1], same), vx[g][1]], axis=0)) for g in gs]
    yield
    for g in gs:
        z_ref[g] = z0[g] * p['gct'][g] + _mm3((p['bkt_hi'][g], p['bkt_lo'][g]), uv[g])
    yield
    for g in gs:
        y_stores[g](as0[g][c:] + _dotf(p['pq'][g], uv[g][0]))


def _interleave(*gens):
    live = list(gens)
    while live:
        for gen in list(live):
            try:
                next(gen)
            except StopIteration:
                live.remove(gen)


_M_STRICT, _M_INCL, _M_EYE, _M_PAIR, _M_OFF = 0, 1, 4, 5, 6


def _rwkv_masks(c, n_h):
    t = np.arange(c)[:, None]
    s = np.arange(c)[None, :]
    masks = [s < t, s <= t, s > t, s >= t, s == t, ((t // 2) == (s // 2)) & (t != s)]
    b = 2
    while b < c:
        masks.append(((t // (2 * b)) == (s // (2 * b))) & ((t // b) != (s // b)))
        b *= 2
    masks = np.tile(np.stack(masks).astype(np.float32), (1, 1, n_h))
    r = np.arange(n_h * c)[:, None] // c
    l = np.arange(n_h * HEAD)[None, :] // HEAD
    return jnp.asarray(masks, BF16), jnp.asarray((r == l).astype(np.float32), BF16)


def _rwkv_kernel(*refs, latent, n_b):
    (xf_ref, xb_ref, xfn_ref, xbn_ref, w0_ref, wup_ref, a0_ref, aup_ref, kk_ref, ka_ref,
     m_ref, same_ref) = refs[:12]
    if latent:
        s0_ref, yf_ref, yb_ref = refs[12:15]
    else:
        yf_ref, yb_ref, st_ref = refs[12:15]
    scr = refs[15:]
    z_scr = scr[0]
    names = [f[0] for f in _PREP_FIELDS]
    ci = pl.program_id(1)
    view = lambda refs_, slot: {k: _SlotView(r, slot) for k, r in zip(names, refs_)}
    cur = view(scr[1:1 + len(names)], ci % 2)
    nxt = view(scr[1:1 + len(names)], 1 - ci % 2)
    mid = view(scr[1 + len(names):], 0)
    n_h = W_BR // HEAD
    c = CHUNK
    head = lambda h: slice(h * HEAD, (h + 1) * HEAD)
    streams = [(bi, d) for bi in range(n_b) for d in range(2)]
    dirs = [d for _, d in streams]
    params = (w0_ref, wup_ref, a0_ref, aup_ref, kk_ref[...], ka_ref[...], m_ref, same_ref)
    first = lambda d: slice(0, c) if d == 0 else slice(c, 2 * c)
    second = lambda d: slice(c, 2 * c) if d == 0 else slice(0, c)
    x_refs, xn_refs, y_refs = (xf_ref, xb_ref), (xfn_ref, xbn_ref), (yf_ref, yb_ref)

    def y_store(bi, d, rows):
        def store(y):
            y_refs[d][bi, rows, :] = y
        return store

    @pl.when(ci == 0)
    def _():
        for g, (bi, d) in enumerate(streams):
            if latent:
                z_scr[g] = jnp.concatenate([s0_ref[bi, 0, d, h].T for h in range(n_h)], axis=1)
            else:
                z_scr[g] = jnp.zeros(z_scr.shape[1:], F32)
        _interleave(_rwkv_prepare([x_refs[d][bi, first(d), :] for bi, d in streams], dirs, *params, cur))

    _interleave(_rwkv_advance(cur, m_ref, same_ref, z_scr, [y_store(bi, d, first(d)) for bi, d in streams]),
                _rwkv_prepare([x_refs[d][bi, second(d), :] for bi, d in streams], dirs, *params, mid))
    _interleave(_rwkv_advance(mid, m_ref, same_ref, z_scr, [y_store(bi, d, second(d)) for bi, d in streams]),
                _rwkv_prepare([xn_refs[d][bi, first(d), :] for bi, d in streams], dirs, *params, nxt))

    if not latent:
        @pl.when(ci == pl.num_programs(1) - 1)
        def _():
            for g, (bi, d) in enumerate(streams):
                z = z_scr[g]
                for h in range(n_h):
                    st_ref[bi, d, h] = z[:, head(h)].T


def _rwkv(pa, w0, wup, a0, aup, k_k, k_a, state, layer):
    b, t, _ = pa.shape
    latent = state is not None
    ns = t // (2 * CHUNK)
    n_h = W_BR // HEAD
    n_b = RWKV_BATCH
    masks, same = _rwkv_masks(CHUNK, n_h)
    full = lambda shape: pl.BlockSpec(shape, lambda i, j: (0,) * len(shape))
    rows = lambda w, idx: pl.BlockSpec((n_b, 2 * CHUNK, w), idx)
    in_specs = [rows(WA, lambda i, j: (i, j, 0)),
                rows(WA, lambda i, j: (i, ns - 1 - j, 0)),
                rows(WA, lambda i, j: (i, jnp.minimum(j + 1, ns - 1), 0)),
                rows(WA, lambda i, j: (i, jnp.maximum(ns - 2 - j, 0), 0)),
                full((2, W_BR)), full((2, LORA, W_BR)), full((2, W_BR)), full((2, LORA, W_BR)),
                full((1, W_BR)), full((1, W_BR)), full(masks.shape), full(same.shape)]
    args = [pa, pa, pa, pa, w0, wup, a0, aup, k_k, k_a, masks, same]
    out_specs = [rows(W_BR, lambda i, j: (i, j, 0)), rows(W_BR, lambda i, j: (i, ns - 1 - j, 0))]
    out_shape = [jax.ShapeDtypeStruct((b, t, W_BR), F32)] * 2
    if latent:
        in_specs.append(pl.BlockSpec((n_b, 1, 2, n_h, HEAD, HEAD), lambda i, j: (i, layer, 0, 0, 0, 0)))
        args.append(state)
    else:
        out_specs.append(pl.BlockSpec((n_b, 2, n_h, HEAD, HEAD), lambda i, j: (i, 0, 0, 0, 0)))
        out_shape.append(jax.ShapeDtypeStruct((b, 2, n_h, HEAD, HEAD), F32))
    return pl.pallas_call(
        functools.partial(_rwkv_kernel, latent=latent, n_b=n_b),
        grid=(b // n_b, ns),
        in_specs=in_specs,
        out_specs=out_specs,
        out_shape=out_shape,
        scratch_shapes=([pltpu.VMEM((2 * n_b, CHUNK, W_BR), F32)]
                        + _prep_scratch(2, 2 * n_b) + _prep_scratch(1, 2 * n_b)),
        compiler_params=_params(("parallel", "arbitrary")),
        name="rwkv",
    )(*args)


def _lru_kernel(*refs, latent):
    if latent:
        (x_ref, cw_ref, cb_ref, wa_ref, ba_ref, wx_ref, bx_ref, lam_ref, h0_ref, y_ref,
         a_scr, h_scr) = refs
    else:
        (x_ref, cw_ref, cb_ref, wa_ref, ba_ref, wx_ref, bx_ref, lam_ref, y_ref, st_ref,
         a_scr, h_scr) = refs
    x = x_ref[0, :, :W_BR]
    t = x.shape[0]
    row = lax.broadcasted_iota(jnp.int32, x.shape, 0)

    def shift_dn(z, k, fill):
        return jnp.where(row >= k, pltpu.roll(z, k, 0), fill)

    def shift_up(z, k, fill):
        return jnp.where(row < t - k, pltpu.roll(z, t - k, 0), fill)

    cw = cw_ref[...]
    xc = (cb_ref[...] + shift_dn(x, 2, 0.0) * cw[0:1] + shift_dn(x, 1, 0.0) * cw[1:2]
          + x * cw[2:3] + shift_up(x, 1, 0.0) * cw[3:4])
    xs = _split(xc)
    sub = row % SUBLANES
    for d in range(2):
        gate_a = _sigmoid(_mm3(xs, _split(wa_ref[d])) + ba_ref[d:d + 1])
        gate_x = _sigmoid(_mm3(xs, _split(wx_ref[d])) + bx_ref[d:d + 1])
        log_a = -C_RG * gate_a * _softplus(-lam_ref[d:d + 1])
        a = jnp.exp(log_a)
        u = jnp.sqrt(-jnp.tanh(log_a) * (a * a + 1.0)) * (gate_x * xc)
        k = 1
        while k < SUBLANES:
            if d == 0:
                keep = sub >= k
                sh = lambda z: pltpu.roll(z, k, 0)
            else:
                keep = sub < SUBLANES - k
                sh = lambda z: pltpu.roll(z, t - k, 0)
            u = a * jnp.where(keep, sh(u), 0.0) + u
            a = a * jnp.where(keep, sh(a), 1.0)
            k *= 2
        a_scr[d] = a
        h_scr[d] = u

    n_grp = t // SUBLANES
    if latent:
        carry0 = (h0_ref[0, 0, 0:1], h0_ref[0, 0, 1:2])
    else:
        carry0 = (jnp.zeros((1, W_BR), F32),) * 2

    def chain(i, carry):
        cf, cb = carry
        rf = pl.ds(pl.multiple_of(i * SUBLANES, SUBLANES), SUBLANES)
        rb = pl.ds(pl.multiple_of((n_grp - 1 - i) * SUBLANES, SUBLANES), SUBLANES)
        hf = h_scr[0, rf, :] + a_scr[0, rf, :] * cf
        hb = h_scr[1, rb, :] + a_scr[1, rb, :] * cb
        h_scr[0, rf, :] = hf
        h_scr[1, rb, :] = hb
        return hf[SUBLANES - 1:SUBLANES], hb[0:1]

    cf, cb = lax.fori_loop(0, n_grp, chain, carry0, unroll=4)
    y_ref[0] = h_scr[0] + h_scr[1]
    if not latent:
        st_ref[0] = jnp.concatenate([cf, cb], axis=0)


def _lru(pc, cw, cb, wa, ba, wx, bx, lam, state, layer):
    b, t, _ = pc.shape
    latent = state is not None
    n_blk = W_BR // HEAD
    full = lambda shape: pl.BlockSpec(shape, lambda i: (0,) * len(shape))
    in_specs = [pl.BlockSpec((1, t, WC), lambda i: (i, 0, 0)),
                full((4, W_BR)), full((1, W_BR)), full((2, W_BR, W_BR)), full((2, W_BR)),
                full((2, W_BR, W_BR)), full((2, W_BR)), full((2, W_BR))]

    def block_diag(w):
        out = jnp.zeros((2, W_BR, W_BR), F32)
        for n in range(n_blk):
            out = out.at[:, n * HEAD:(n + 1) * HEAD, n * HEAD:(n + 1) * HEAD].set(w[:, n])
        return out

    args = [pc, cw, cb, block_diag(wa), ba, block_diag(wx), bx, lam]
    out_specs = [pl.BlockSpec((1, t, W_BR), lambda i: (i, 0, 0))]
    out_shape = [jax.ShapeDtypeStruct((b, t, W_BR), F32)]
    if latent:
        in_specs.append(pl.BlockSpec((1, 1, 2, W_BR), lambda i: (i, layer, 0, 0)))
        args.append(state)
    else:
        out_specs.append(pl.BlockSpec((1, 2, W_BR), lambda i: (i, 0, 0)))
        out_shape.append(jax.ShapeDtypeStruct((b, 2, W_BR), F32))
    return pl.pallas_call(
        functools.partial(_lru_kernel, latent=latent),
        grid=(b,),
        in_specs=in_specs,
        out_specs=out_specs,
        out_shape=out_shape,
        scratch_shapes=[pltpu.VMEM((2, t, W_BR), F32), pltpu.VMEM((2, t, W_BR), F32)],
        compiler_params=_params(("parallel",)),
        name="lru",
    )(*args)


def _softmax_pv(blocks, sink):
    m = None
    for s, _ in blocks:
        mi = jnp.max(s, axis=-1, keepdims=True)
        m = mi if m is None else jnp.maximum(m, mi)
    if sink is not None:
        m = jnp.maximum(m, sink)
    den = None if sink is None else jnp.exp(sink - m)
    acc = None
    for s, v in blocks:
        e = jnp.exp(s - m)
        di = jnp.sum(e, axis=-1, keepdims=True)
        den = di if den is None else den + di
        o = _dot_bf(e, v)
        acc = o if acc is None else acc + o
    return acc, den


def _win_kernel(*refs, latent):
    scale = HEAD ** -0.5
    if latent:
        sink_ref, q_ref, kp_ref, kc_ref, kn_ref, ck_ref, cv_ref, o_ref = refs
        j = pl.program_id(1)
        nq = pl.num_programs(1)
        qi = lax.broadcasted_iota(jnp.int32, (Q_BLK, Q_BLK), 0)
        ki = lax.broadcasted_iota(jnp.int32, (Q_BLK, Q_BLK), 1)
        kv_blocks = [(kp_ref[0], (ki >= qi) & (j > 0)), (kc_ref[0], None),
                     (kn_ref[0], (ki <= qi) & (j < nq - 1))]
        ctx_k, ctx_v = ck_ref[0, 0], cv_ref[0, 0]
    else:
        sink_ref, x_ref, o_ref = refs
    q = q_ref[0] if latent else x_ref[0, :, :256]
    outs = []
    for h in range(2):
        hs = slice(h * HEAD, (h + 1) * HEAD)
        for g in range(2):
            qg = q[:, (2 * h + g) * HEAD:(2 * h + g + 1) * HEAD]
            blocks = []
            if latent:
                for kv, mask in kv_blocks:
                    s = _dot_nt_bf(qg, kv[:, hs]) * scale
                    if mask is not None:
                        s = jnp.where(mask, s, NEG_INF)
                    blocks.append((s, kv[:, 128 + h * HEAD:128 + (h + 1) * HEAD]))
                blocks.append((_dot_nt_bf(qg, ctx_k[:, hs]) * scale, ctx_v[:, hs]))
            else:
                k = x_ref[0, :, 256 + h * HEAD:256 + (h + 1) * HEAD]
                v = x_ref[0, :, 384 + h * HEAD:384 + (h + 1) * HEAD]
                blocks.append((_dot_nt_bf(qg, k) * scale, v))
            acc, den = _softmax_pv(blocks, sink_ref[2 * h + g])
            outs.append(acc / den)
    o_ref[0] = jnp.concatenate(outs, axis=-1)


def _win(pb, sink, cache_k, cache_v, layer):
    b, t, _ = pb.shape
    latent = cache_k is not None
    smem = pl.BlockSpec(memory_space=pltpu.SMEM)
    if latent:
        nq = t // Q_BLK
        grid = (b, nq)
        in_specs = [smem,
                    pl.BlockSpec((1, Q_BLK, 256), lambda i, j: (i, j, 0)),
                    pl.BlockSpec((1, Q_BLK, 256), lambda i, j: (i, jnp.maximum(j - 1, 0), 1)),
                    pl.BlockSpec((1, Q_BLK, 256), lambda i, j: (i, j, 1)),
                    pl.BlockSpec((1, Q_BLK, 256), lambda i, j: (i, jnp.minimum(j + 1, nq - 1), 1)),
                    pl.BlockSpec((1, 1) + cache_k.shape[2:], lambda i, j: (i, layer, 0, 0)),
                    pl.BlockSpec((1, 1) + cache_v.shape[2:], lambda i, j: (i, layer, 0, 0))]
        args = [sink, pb, pb, pb, pb, cache_k, cache_v]
        out_spec = pl.BlockSpec((1, Q_BLK, W_BR), lambda i, j: (i, j, 0))
        sem = ("parallel", "parallel")
    else:
        grid = (b,)
        in_specs = [smem, pl.BlockSpec((1, t, WB), lambda i: (i, 0, 0))]
        args = [sink, pb]
        out_spec = pl.BlockSpec((1, t, W_BR), lambda i: (i, 0, 0))
        sem = ("parallel",)
    return pl.pallas_call(
        functools.partial(_win_kernel, latent=latent),
        grid=grid,
        in_specs=in_specs,
        out_specs=out_spec,
        out_shape=jax.ShapeDtypeStruct((b, t, W_BR), F32),
        compiler_params=_params(sem),
        name="win",
    )(*args)


def _diff_kernel(*refs, latent, lam_init):
    if latent:
        lam_ref, g_ref, q_ref, k_ref, v_ref, ck_ref, cv_ref, o_ref = refs
    else:
        lam_ref, g_ref, q_ref, k_ref, v_ref, o_ref = refs
    scale = DQ_D ** -0.5
    lp = lam_ref[...]
    lam = (jnp.exp(jnp.sum(lp[0:1] * lp[1:2], axis=-1, keepdims=True))
           - jnp.exp(jnp.sum(lp[2:3] * lp[3:4], axis=-1, keepdims=True)) + lam_init)
    q, k, v = q_ref[0], k_ref[0], v_ref[0]
    key_sets = [(k, v)]
    if latent:
        key_sets.append((ck_ref[0, 0], cv_ref[0, 0]))
    outs = []
    for h in range(W_BR // HEAD):
        hs = slice(h * HEAD, (h + 1) * HEAD)
        probs = []
        for m in range(2):
            ms = slice(h * HEAD + m * DQ_D, h * HEAD + (m + 1) * DQ_D)
            logits = [_dot_nt_bf(q[:, ms], kk[:, ms]) * scale for kk, _ in key_sets]
            mx = functools.reduce(jnp.maximum, [jnp.max(s, axis=-1, keepdims=True) for s in logits])
            es = [jnp.exp(s - mx) for s in logits]
            den = functools.reduce(lambda a, b: a + b, [jnp.sum(e, axis=-1, keepdims=True) for e in es])
            inv = 1.0 / den
            probs.append([e * inv for e in es])
        o = None
        for i, (_, vv) in enumerate(key_sets):
            oi = _dot_bf(probs[0][i] - lam * probs[1][i], vv[:, hs])
            o = oi if o is None else o + oi
        o = o * lax.rsqrt(jnp.mean(o * o, -1, keepdims=True) + NORM_EPS) * g_ref[...]
        outs.append(o * (1.0 - lam_init))
    o_ref[0] = jnp.concatenate(outs, axis=-1)


def _diff(pd, lam_p, subln_g, cache_k, cache_v, layer, lam_init):
    b, t, _ = pd.shape
    latent = cache_k is not None
    tq = 256
    in_specs = [pl.BlockSpec((4, DQ_D), lambda i, j: (0, 0)),
                pl.BlockSpec((1, HEAD), lambda i, j: (0, 0)),
                pl.BlockSpec((1, tq, 256), lambda i, j: (i, j, 0)),
                pl.BlockSpec((1, t, 256), lambda i, j: (i, 0, 1)),
                pl.BlockSpec((1, t, 256), lambda i, j: (i, 0, 2))]
    args = [lam_p, subln_g, pd, pd, pd]
    if latent:
        in_specs += [pl.BlockSpec((1, 1) + cache_k.shape[2:], lambda i, j: (i, layer, 0, 0)),
                     pl.BlockSpec((1, 1) + cache_v.shape[2:], lambda i, j: (i, layer, 0, 0))]
        args += [cache_k, cache_v]
    return pl.pallas_call(
        functools.partial(_diff_kernel, latent=latent, lam_init=lam_init),
        grid=(b, t // tq),
        in_specs=in_specs,
        out_specs=pl.BlockSpec((1, tq, W_BR), lambda i, j: (i, j, 0)),
        out_shape=jax.ShapeDtypeStruct((b, t, W_BR), F32),
        compiler_params=_params(("parallel", "parallel")),
        name="diff",
    )(*args)


def _out_kernel(x_ref, mod_ref, gpost_ref, w_ref, pa_ref, yf_ref, yb_ref, gng_ref, gnb_ref, rk_ref,
                ywin_ref, bg_ref, ylru_ref, cg_ref, ydiff_ref, dg_ref, o_ref):
    ones = _head_ones(W_BR)
    pa = pa_ref[0]
    r, k, v, ag = pa[:, 0:256], pa[:, 256:512], pa[:, 512:768], pa[:, 896:1152]
    y = yf_ref[0] + yb_ref[0]
    mu = _mm2(y, ones) * (1.0 / HEAD)
    yc = y - mu
    var = _mm2(yc * yc, ones) * (1.0 / HEAD)
    ya = yc * lax.rsqrt(var + GN_EPS) * gng_ref[...] + gnb_ref[...]
    ya = ya + _mm2(r * k * rk_ref[...], ones) * v
    mix = jnp.concatenate([ya * _silu(ag), ywin_ref[0] * _silu(bg_ref[0]),
                           ylru_ref[0] * _silu(cg_ref[0]), ydiff_ref[0] * _silu(dg_ref[0])], axis=-1)
    o = jnp.dot(mix.astype(BF16), w_ref[0], preferred_element_type=F32)
    o = o * lax.rsqrt(jnp.mean(o * o, -1, keepdims=True) + NORM_EPS) * gpost_ref[...]
    o_ref[0] = x_ref[0] + mod_ref[0][:, 2 * D_MODEL:] * o


def _out(x, mod_l, g_post, w_out_bf, layer, latent, pa, yf, yb, gn_g, gn_b, r_k, ywin, pb, ylru, pc, ydiff, pd):
    b, t, _ = x.shape
    tm = ROW_TILE
    mod_idx = (lambda i, j: (1 + i, 0, 0)) if latent else (lambda i, j: (0, 0, 0))
    rows = lambda w, c=0: pl.BlockSpec((1, tm, w), lambda i, j: (i, j, c))
    vec = pl.BlockSpec((1, W_BR), lambda i, j: (0, 0))
    in_specs = [rows(D_MODEL),
                pl.BlockSpec((1, 1, 3 * D_MODEL), mod_idx),
                pl.BlockSpec((1, D_MODEL), lambda i, j: (0, 0)),
                pl.BlockSpec((1, D_MODEL, D_MODEL), lambda i, j: (layer, 0, 0)),
                rows(WA), rows(W_BR), rows(W_BR), vec, vec, vec,
                rows(W_BR), rows(W_BR, 2), rows(W_BR), rows(W_BR, 1), rows(W_BR), rows(W_BR, 3)]
    return pl.pallas_call(
        _out_kernel,
        grid=(b, t // tm),
        in_specs=in_specs,
        out_specs=rows(D_MODEL),
        out_shape=jax.ShapeDtypeStruct((b, t, D_MODEL), F32),
        compiler_params=_params(("parallel", "parallel")),
        name="out",
    )(x, mod_l, g_post, w_out_bf, pa, yf, yb, gn_g, gn_b, r_k, ywin, pb, ylru, pc, ydiff, pd)


def _layer(x, mod_l, layer, lam_init, wts, cache, tables):
    latent = cache is not None
    row = lambda a: a[layer].reshape(1, -1)
    pa, pb, pc, pd = _project(x, mod_l, row(wts['g_pre']), wts['w_in_bf'], layer, tables)
    rw = _rwkv(pa, wts['rwkv_w0'][layer], wts['rwkv_w_up'][layer], wts['rwkv_a0'][layer],
               wts['rwkv_a_up'][layer], row(wts['rwkv_k_k']), row(wts['rwkv_k_a']),
               cache['rwkv'] if latent else None, layer)
    lr = _lru(pc, wts['lru_conv_w'][layer], row(wts['lru_conv_b']), wts['lru_wa'][layer],
              wts['lru_ba'][layer], wts['lru_wx'][layer], wts['lru_bx'][layer], wts['lru_lambda'][layer],
              cache['lru'] if latent else None, layer)
    ywin = _win(pb, wts['win_sink'][layer], cache['win_k'] if latent else None,
                cache['win_v'] if latent else None, layer)
    ydiff = _diff(pd, wts['diff_lambda'][layer], row(wts['diff_subln_g']),
                  cache['diff_k'] if latent else None, cache['diff_v'] if latent else None, layer, lam_init)
    y = _out(x, mod_l, row(wts['g_post']), wts['w_out_bf'], layer, latent, pa, rw[0], rw[1],
             row(wts['rwkv_gn_g']), row(wts['rwkv_gn_b']), row(wts['rwkv_r_k']),
             ywin, pb, lr[0], pc, ydiff, pd)
    new_cache = None if latent else (pb, pd, rw[2], lr[1])
    return y, new_cache


def kernel(x_prompt, x_sample, c, cache_win_k, cache_win_v, cache_diff_k, cache_diff_v, state_rwkv, state_lru,
           c_ctx, w_mod, b_mod, g_pre, g_post, w_in, w_out,
           rwkv_w0, rwkv_w_up, rwkv_a0, rwkv_a_up, rwkv_k_k, rwkv_k_a, rwkv_r_k, rwkv_gn_g, rwkv_gn_b,
           win_sink, lru_conv_w, lru_conv_b, lru_wa, lru_ba, lru_wx, lru_bx, lru_lambda,
           diff_lambda, diff_subln_g):
    n_b, seq = x_prompt.shape[:2]
    n_dec, dec_seq = x_sample.shape[:2]
    past = cache_win_k.shape[2]
    wts = dict(g_pre=g_pre, g_post=g_post, w_in_bf=w_in.astype(BF16), w_out_bf=w_out.astype(BF16),
               rwkv_w0=rwkv_w0, rwkv_w_up=rwkv_w_up, rwkv_a0=rwkv_a0, rwkv_a_up=rwkv_a_up,
               rwkv_k_k=rwkv_k_k, rwkv_k_a=rwkv_k_a, rwkv_r_k=rwkv_r_k, rwkv_gn_g=rwkv_gn_g,
               rwkv_gn_b=rwkv_gn_b, win_sink=win_sink, lru_conv_w=lru_conv_w, lru_conv_b=lru_conv_b,
               lru_wa=lru_wa, lru_ba=lru_ba, lru_wx=lru_wx, lru_bx=lru_bx, lru_lambda=lru_lambda,
               diff_lambda=diff_lambda, diff_subln_g=diff_subln_g)
    cache = dict(win_k=cache_win_k.reshape(n_dec, DEPTH, past, 128),
                 win_v=cache_win_v.reshape(n_dec, DEPTH, past, 128),
                 diff_k=cache_diff_k.reshape(n_dec, DEPTH, past, 256),
                 diff_v=cache_diff_v.reshape(n_dec, DEPTH, past, 256),
                 rwkv=state_rwkv, lru=state_lru)
    cvec = jnp.concatenate([c_ctx[None], c, jnp.zeros((8 - 1 - n_dec, D_MODEL), F32)], axis=0)
    mod = _modulation(cvec, w_mod, b_mod)
    cos_b, sin_b = _rope_tables(dec_seq, HEAD, 384)
    cos_d, sin_d = _rope_tables(dec_seq, DQ_D, 512)
    tables = (cos_b, sin_b, cos_d, sin_d)

    y_p, y_s = x_prompt, x_sample
    ctx = []
    for l in range(DEPTH):
        lam_init = 0.8 - 0.6 * math.exp(-0.3 * l)
        mod_l = mod[l].reshape(8, 1, 3 * D_MODEL)
        y_p, nc = _layer(y_p, mod_l, l, lam_init, wts, None, None)
        ctx.append(nc)
        y_s, _ = _layer(y_s, mod_l, l, lam_init, wts, cache, tables)
    stack = lambda f: jnp.stack([f(ct) for ct in ctx], axis=1)
    new_win_k = stack(lambda ct: ct[0][:, :, 256:384].reshape(n_b, seq, 2, HEAD))
    new_win_v = stack(lambda ct: ct[0][:, :, 384:512].reshape(n_b, seq, 2, HEAD))
    new_diff_k = stack(lambda ct: ct[1][:, :, 256:512].reshape(n_b, seq, 4, 2, DQ_D))
    new_diff_v = stack(lambda ct: ct[1][:, :, 512:768].reshape(n_b, seq, 4, HEAD))
    new_state_rwkv = stack(lambda ct: ct[2])
    new_state_lru = stack(lambda ct: ct[3])
    return (y_p, y_s, new_win_k, new_win_v, new_diff_k, new_diff_v, new_state_rwkv, new_state_lru)
```

```python
import functools
import math

import numpy as np
import jax
import jax.numpy as jnp
from jax import lax
from jax.experimental import pallas as pl
from jax.experimental.pallas import tpu as pltpu

F32 = jnp.float32
BF16 = jnp.bfloat16
HI = lax.Precision.HIGHEST

D_MODEL = 1024
DEPTH = 2
GRID_W = 64
ROPE_BASE = 10000.0
NORM_EPS = 1e-6
NEG_INF = -1e30
GN_EPS = 64e-5
C_RG = 8.0
W_BR = 256
HEAD = 64
SUBLANES = 8
LORA = 64
DQ_D = 32
WINDOW = 128
Q_BLK = 128
P_TOTAL = 3456
WA, WB, WC, WD = 1152, 768, 512, 1024
CHUNK = 64
RWKV_BATCH = 4
ROW_TILE = 256
VMEM_LIMIT = 48 * 1024 * 1024


_NN = (((1,), (0,)), ((), ()))
_NT = (((1,), (1,)), ((), ()))
_TN = (((0,), (0,)), ((), ()))


def _dot_hi(a, b):
    return jnp.dot(a, b, precision=HI, preferred_element_type=F32)


def _bf(x):
    return x.astype(BF16)


def _dotf(x, y, dims=_NN):
    return lax.dot_general(x, y, dims, preferred_element_type=F32)


def _split(x):
    hi = x.astype(BF16)
    return hi, (x - hi.astype(F32)).astype(BF16)


def _mm3(a, b, dims=_NN):
    return _dotf(a[0], b[0], dims) + (_dotf(a[0], b[1], dims) + _dotf(a[1], b[0], dims))


def _mm2(x, m01, left=False):
    hi, lo = _split(x)
    if left:
        return _dotf(m01, hi) + _dotf(m01, lo)
    return _dotf(hi, m01) + _dotf(lo, m01)


def _dot_bf(a, b):
    return jnp.dot(a.astype(BF16), b.astype(BF16), preferred_element_type=F32)


def _dot_nt_bf(a, b):
    return lax.dot_general(a.astype(BF16), b.astype(BF16), (((1,), (1,)), ((), ())),
                           preferred_element_type=F32)


def _sigmoid(x):
    return 1.0 / (1.0 + jnp.exp(-x))


def _silu(x):
    return x * _sigmoid(x)


def _softplus(x):
    return jnp.maximum(x, 0.0) + jnp.log1p(jnp.exp(-jnp.abs(x)))


def _head_ones(n):
    r = lax.broadcasted_iota(jnp.int32, (n, n), 0) // HEAD
    c = lax.broadcasted_iota(jnp.int32, (n, n), 1) // HEAD
    return jnp.where(r == c, 1.0, 0.0).astype(BF16)


def _params(sem):
    return pltpu.CompilerParams(dimension_semantics=sem, vmem_limit_bytes=VMEM_LIMIT)


def _mod_kernel(c_ref, w_ref, b_ref, o_ref):
    o_ref[0] = _dot_hi(_silu(c_ref[...]), w_ref[0]) + b_ref[0]


def _modulation(cvec, w_mod, b_mod):
    n_l = w_mod.shape[0]
    tn = 512
    return pl.pallas_call(
        _mod_kernel,
        grid=(n_l, 3 * D_MODEL // tn),
        in_specs=[pl.BlockSpec((8, D_MODEL), lambda l, j: (0, 0)),
                  pl.BlockSpec((1, D_MODEL, tn), lambda l, j: (l, 0, j)),
                  pl.BlockSpec((1, 1, tn), lambda l, j: (l, 0, j))],
        out_specs=pl.BlockSpec((1, 8, tn), lambda l, j: (l, 0, j)),
        out_shape=jax.ShapeDtypeStruct((n_l, 8, 3 * D_MODEL), F32),
        compiler_params=_params(("parallel", "parallel")),
        name="mod",
    )(cvec, w_mod, b_mod.reshape(n_l, 1, 3 * D_MODEL))


def _rope(x, cos, sin_signed, off):
    w = x.shape[-1]
    lane = lax.broadcasted_iota(jnp.int32, x.shape, 1)
    first = (lane % (2 * off)) < off
    partner = jnp.where(first, pltpu.roll(x, w - off, 1), pltpu.roll(x, off, 1))
    return x * cos + partner * sin_signed


def _proj_kernel(*refs, latent):
    if latent:
        x_ref, mod_ref, g_ref, w_ref, cb_ref, sb_ref, cd_ref, sd_ref, oa, ob, oc, od = refs
    else:
        x_ref, mod_ref, g_ref, w_ref, oa, ob, oc, od = refs
    x = x_ref[0]
    y = x * lax.rsqrt(jnp.mean(x * x, -1, keepdims=True) + NORM_EPS) * g_ref[...]
    m = mod_ref[0]
    h = y * (1.0 + m[:, D_MODEL:2 * D_MODEL]) + m[:, :D_MODEL]
    p = jnp.dot(h.astype(BF16), w_ref[0], preferred_element_type=F32)
    oa[0] = p[:, :WA]
    pb = p[:, WA:WA + WB]
    pd = p[:, WA + WB + WC:]
    if latent:
        ob[0, :, :384] = _rope(pb[:, :384], cb_ref[...], sb_ref[...], 16)
        ob[0, :, 384:] = pb[:, 384:]
        od[0, :, :512] = _rope(pd[:, :512], cd_ref[...], sd_ref[...], 8)
        od[0, :, 512:] = pd[:, 512:]
    else:
        ob[0] = pb
        od[0] = pd
    oc[0] = p[:, WA + WB:WA + WB + WC]


def _project(x, mod_l, g_pre, w_in_bf, layer, tables):
    b, t, _ = x.shape
    latent = tables is not None
    tm = ROW_TILE
    mod_idx = (lambda i, j: (1 + i, 0, 0)) if latent else (lambda i, j: (0, 0, 0))
    in_specs = [pl.BlockSpec((1, tm, D_MODEL), lambda i, j: (i, j, 0)),
                pl.BlockSpec((1, 1, 3 * D_MODEL), mod_idx),
                pl.BlockSpec((1, D_MODEL), lambda i, j: (0, 0)),
                pl.BlockSpec((1, D_MODEL, P_TOTAL), lambda i, j: (layer, 0, 0))]
    args = [x, mod_l, g_pre, w_in_bf]
    if latent:
        for tab in tables:
            in_specs.append(pl.BlockSpec((tm, tab.shape[1]), lambda i, j: (j, 0)))
            args.append(tab)
    widths = (WA, WB, WC, WD)
    return pl.pallas_call(
        functools.partial(_proj_kernel, latent=latent),
        grid=(b, t // tm),
        in_specs=in_specs,
        out_specs=[pl.BlockSpec((1, tm, w), lambda i, j: (i, j, 0)) for w in widths],
        out_shape=[jax.ShapeDtypeStruct((b, t, w), F32) for w in widths],
        compiler_params=_params(("parallel", "parallel")),
        name="proj",
    )(*args)


def _rope_tables(t, head_dim, n_lanes):
    half = head_dim // 2
    quarter = half // 2
    pos = jnp.arange(t)
    row = (pos // GRID_W).astype(F32)
    col = (pos % GRID_W).astype(F32)
    inv = ROPE_BASE ** (-jnp.arange(0, half, 2, dtype=F32) / half)
    lane = np.arange(n_lanes) % head_dim
    in_half = lane % half
    p = jnp.where(jnp.asarray(lane < half)[None, :], row[:, None], col[:, None])
    ang = p * inv[in_half % quarter][None, :]
    sign = jnp.asarray(np.where(in_half < quarter, -1.0, 1.0), F32)[None, :]
    return jnp.cos(ang), jnp.sin(ang) * sign


_PREP_FIELDS = (('ar', 2, 1, BF16), ('nm', 1, 1, BF16), ('mm', 1, 1, BF16), ('pq', 1, 2, BF16),
                ('v_hi', 1, 1, BF16), ('v_lo', 1, 1, BF16), ('bkt_hi', 1, 2, BF16), ('bkt_lo', 1, 2, BF16),
                ('gct', 1, 1, F32))


def _prep_scratch(slots, n_streams):
    return [pltpu.VMEM((slots, n_streams, rows * CHUNK, lanes * W_BR), dt) for _, rows, lanes, dt in _PREP_FIELDS]


class _SlotView:
    def __init__(self, ref, slot):
        self.ref, self.slot, self.shape = ref, slot, ref.shape[1:]

    def __getitem__(self, idx):
        return self.ref[(self.slot,) + (idx if isinstance(idx, tuple) else (idx,))]

    def __setitem__(self, idx, val):
        self.ref[(self.slot,) + (idx if isinstance(idx, tuple) else (idx,))] = val


def _head_transpose(x):
    xt = x.T
    return jnp.concatenate([xt[h * HEAD:(h + 1) * HEAD] for h in range(x.shape[1] // HEAD)], axis=1)


def _expand(x, same):
    return jnp.concatenate([x] * (same.shape[0] // x.shape[0]), axis=0) * same


def _rwkv_prepare(xs, dirs, w0_ref, wup_ref, a0_ref, aup_ref, k_k, k_a, m_ref, same_ref, out):
    c = xs[0].shape[0]
    ones = _head_ones(W_BR)
    ti = lax.broadcasted_iota(jnp.int32, (c, c), 0)
    si = lax.broadcasted_iota(jnp.int32, (c, c), 1)
    cums = [jnp.where(si <= ti, 1.0, 0.0).astype(BF16), jnp.where(si >= ti, 1.0, 0.0).astype(BF16)]
    same = same_ref[...]

    st = []
    for xc, d in zip(xs, dirs):
        r, k, v = xc[:, 0:256], xc[:, 256:512], xc[:, 512:768]
        wd, ad = xc[:, 768:832], xc[:, 832:896]
        kk = k * k_k
        kk = kk * lax.rsqrt(_mm2(kk * kk, ones) + 1e-12)
        z = w0_ref[d:d + 1] + _mm3(_split(jnp.tanh(wd)), _split(wup_ref[d]))
        e = jnp.exp(-_softplus(-z) - 0.5)
        a = _sigmoid(a0_ref[d:d + 1] + _mm3(_split(ad), _split(aup_ref[d])))
        st.append(dict(r=r, v=v, e=e, kd=k * (1.0 + (a - 1.0) * k_a), alpha=-kk, beta=kk * a))
    yield
    for s, d in zip(st, dirs):
        l_incl = _mm2(s['e'], cums[d], left=True)
        l_tot = jnp.sum(s['e'], axis=0, keepdims=True)
        grow = jnp.exp(l_incl)
        tail = jnp.exp(l_incl - l_tot)
        s.update(ar=jnp.concatenate([_bf(s['alpha'] * jnp.exp(s['e'] - l_incl)),
                                     _bf(s['r'] * jnp.exp(-l_incl))], axis=0),
                 b_t=_bf(s['beta'] * grow), k_t=_bf(s['kd'] * grow),
                 b_h=s['beta'] * tail, k_h=s['kd'] * tail,
                 g_c=jnp.broadcast_to(jnp.exp(-l_tot), (c, W_BR)))
    yield
    for g, (s, d) in enumerate(zip(st, dirs)):
        out['ar'][g] = s['ar']
        rhs_t = jnp.concatenate([_expand(s['b_t'], same), _expand(s['k_t'], same)], axis=0)
        g4 = _dotf(s['ar'], rhs_t, _NT)
        strict = m_ref[_M_STRICT + 2 * d]
        incl = m_ref[_M_INCL + 2 * d]
        out['nm'][g] = _bf(g4[:c, :W_BR]) * strict
        out['mm'][g] = _bf(g4[:c, W_BR:]) * strict
        out['pq'][g, :, :W_BR] = _bf(g4[c:, :W_BR]) * incl
        out['pq'][g, :, W_BR:] = _bf(g4[c:, W_BR:]) * incl
    yield
    for g, s in enumerate(st):
        v_hi, v_lo = _split(s['v'])
        out['v_hi'][g] = v_hi
        out['v_lo'][g] = v_lo
        b_hi, b_lo = _split(_head_transpose(s['b_h']))
        k_hi, k_lo = _split(_head_transpose(s['k_h']))
        out['bkt_hi'][g] = jnp.concatenate([b_hi, k_hi], axis=1)
        out['bkt_lo'][g] = jnp.concatenate([b_lo, k_lo], axis=1)
        out['gct'][g] = _head_transpose(s['g_c'])


def _rwkv_advance(p, m_ref, same_ref, z_ref, y_stores):
    n_streams, c = p['nm'].shape[0], p['nm'].shape[1]
    gs = range(n_streams)
    same = same_ref[...]
    ex = lambda x: _expand(_bf(x), same)
    invs = [(p['nm'][g] * m_ref[_M_PAIR]).astype(F32) + m_ref[_M_EYE].astype(F32) for g in gs]
    for lvl in range(int(math.log2(c)) - 1):
        inv_x = [ex(t) for t in invs]
        half = [_dotf(p['nm'][g] * m_ref[_M_OFF + lvl], inv_x[g]) for g in gs]
        invs = [invs[g] + _dotf(_bf(invs[g]), ex(half[g])) for g in gs]
        yield
    z0 = [z_ref[g] for g in gs]
    as0 = [_dotf(p['ar'][g], ex(z0[g])) for g in gs]
    vx = [(_expand(p['v_hi'][g], same), _expand(p['v_lo'][g], same)) for g in gs]
    rhs = [as0[g][:c] + _dotf(p['mm'][g], vx[g][0]) for g in gs]
    yield
    u = [_split(_dotf(_bf(invs[g]), ex(rhs[g]))) for g in gs]
    uv = [(jnp.concatenate([_expand(u[g][0], same), vx[g][0]], axis=0),
           jnp.concatenate([_expand(u[g][1], same), vx[g][1]], axis=0)) for g in gs]
    yield
    for g in gs:
        z_ref[g] = z0[g] * p['gct'][g] + _mm3((p['bkt_hi'][g], p['bkt_lo'][g]), uv[g])
    yield
    for g in gs:
        y_stores[g](as0[g][c:] + _dotf(p['pq'][g], uv[g][0]))


def _interleave(*gens):
    live = list(gens)
    while live:
        for gen in list(live):
            try:
                next(gen)
            except StopIteration:
                live.remove(gen)


_M_STRICT, _M_INCL, _M_EYE, _M_PAIR, _M_OFF = 0, 1, 4, 5, 6


def _rwkv_masks(c, n_h):
    t = np.arange(c)[:, None]
    s = np.arange(c)[None, :]
    masks = [s < t, s <= t, s > t, s >= t, s == t, ((t // 2) == (s // 2)) & (t != s)]
    b = 2
    while b < c:
        masks.append(((t // (2 * b)) == (s // (2 * b))) & ((t // b) != (s // b)))
        b *= 2
    masks = np.tile(np.stack(masks).astype(np.float32), (1, 1, n_h))
    r = np.arange(n_h * c)[:, None] // c
    l = np.arange(n_h * HEAD)[None, :] // HEAD
    return jnp.asarray(masks, BF16), jnp.asarray((r == l).astype(np.float32), BF16)


def _rwkv_kernel(*refs, latent, n_b):
    (xf_ref, xb_ref, xfn_ref, xbn_ref, w0_ref, wup_ref, a0_ref, aup_ref, kk_ref, ka_ref,
     m_ref, same_ref) = refs[:12]
    if latent:
        s0_ref, yf_ref, yb_ref = refs[12:15]
    else:
        yf_ref, yb_ref, st_ref = refs[12:15]
    scr = refs[15:]
    z_scr = scr[0]
    names = [f[0] for f in _PREP_FIELDS]
    ci = pl.program_id(1)
    view = lambda refs_, slot: {k: _SlotView(r, slot) for k, r in zip(names, refs_)}
    cur = view(scr[1:1 + len(names)], ci % 2)
    nxt = view(scr[1:1 + len(names)], 1 - ci % 2)
    mid = view(scr[1 + len(names):], 0)
    n_h = W_BR // HEAD
    c = CHUNK
    head = lambda h: slice(h * HEAD, (h + 1) * HEAD)
    streams = [(bi, d) for bi in range(n_b) for d in range(2)]
    dirs = [d for _, d in streams]
    params = (w0_ref, wup_ref, a0_ref, aup_ref, kk_ref[...], ka_ref[...], m_ref, same_ref)
    first = lambda d: slice(0, c) if d == 0 else slice(c, 2 * c)
    second = lambda d: slice(c, 2 * c) if d == 0 else slice(0, c)
    x_refs, xn_refs, y_refs = (xf_ref, xb_ref), (xfn_ref, xbn_ref), (yf_ref, yb_ref)

    def y_store(bi, d, rows):
        def store(y):
            y_refs[d][bi, rows, :] = y
        return store

    @pl.when(ci == 0)
    def _():
        for g, (bi, d) in enumerate(streams):
            if latent:
                z_scr[g] = jnp.concatenate([s0_ref[bi, 0, d, h].T for h in range(n_h)], axis=1)
            else:
                z_scr[g] = jnp.zeros(z_scr.shape[1:], F32)
        _interleave(_rwkv_prepare([x_refs[d][bi, first(d), :] for bi, d in streams], dirs, *params, cur))

    _interleave(_rwkv_advance(cur, m_ref, same_ref, z_scr, [y_store(bi, d, first(d)) for bi, d in streams]),
                _rwkv_prepare([x_refs[d][bi, second(d), :] for bi, d in streams], dirs, *params, mid))
    _interleave(_rwkv_advance(mid, m_ref, same_ref, z_scr, [y_store(bi, d, second(d)) for bi, d in streams]),
                _rwkv_prepare([xn_refs[d][bi, first(d), :] for bi, d in streams], dirs, *params, nxt))

    if not latent:
        @pl.when(ci == pl.num_programs(1) - 1)
        def _():
            for g, (bi, d) in enumerate(streams):
                z = z_scr[g]
                for h in range(n_h):
                    st_ref[bi, d, h] = z[:, head(h)].T


def _rwkv(pa, w0, wup, a0, aup, k_k, k_a, state, layer):
    b, t, _ = pa.shape
    latent = state is not None
    ns = t // (2 * CHUNK)
    n_h = W_BR // HEAD
    n_b = RWKV_BATCH
    masks, same = _rwkv_masks(CHUNK, n_h)
    full = lambda shape: pl.BlockSpec(shape, lambda i, j: (0,) * len(shape))
    rows = lambda w, idx: pl.BlockSpec((n_b, 2 * CHUNK, w), idx)
    in_specs = [rows(WA, lambda i, j: (i, j, 0)),
                rows(WA, lambda i, j: (i, ns - 1 - j, 0)),
                rows(WA, lambda i, j: (i, jnp.minimum(j + 1, ns - 1), 0)),
                rows(WA, lambda i, j: (i, jnp.maximum(ns - 2 - j, 0), 0)),
                full((2, W_BR)), full((2, LORA, W_BR)), full((2, W_BR)), full((2, LORA, W_BR)),
                full((1, W_BR)), full((1, W_BR)), full(masks.shape), full(same.shape)]
    args = [pa, pa, pa, pa, w0, wup, a0, aup, k_k, k_a, masks, same]
    out_specs = [rows(W_BR, lambda i, j: (i, j, 0)), rows(W_BR, lambda i, j: (i, ns - 1 - j, 0))]
    out_shape = [jax.ShapeDtypeStruct((b, t, W_BR), F32)] * 2
    if latent:
        in_specs.append(pl.BlockSpec((n_b, 1, 2, n_h, HEAD, HEAD), lambda i, j: (i, layer, 0, 0, 0, 0)))
        args.append(state)
    else:
        out_specs.append(pl.BlockSpec((n_b, 2, n_h, HEAD, HEAD), lambda i, j: (i, 0, 0, 0, 0)))
        out_shape.append(jax.ShapeDtypeStruct((b, 2, n_h, HEAD, HEAD), F32))
    return pl.pallas_call(
        functools.partial(_rwkv_kernel, latent=latent, n_b=n_b),
        grid=(b // n_b, ns),
        in_specs=in_specs,
        out_specs=out_specs,
        out_shape=out_shape,
        scratch_shapes=([pltpu.VMEM((2 * n_b, CHUNK, W_BR), F32)]
                        + _prep_scratch(2, 2 * n_b) + _prep_scratch(1, 2 * n_b)),
        compiler_params=_params(("parallel", "arbitrary")),
        name="rwkv",
    )(*args)


def _lru_kernel(*refs, latent):
    if latent:
        (x_ref, cw_ref, cb_ref, wa_ref, ba_ref, wx_ref, bx_ref, lam_ref, h0_ref, y_ref,
         a_scr, h_scr) = refs
    else:
        (x_ref, cw_ref, cb_ref, wa_ref, ba_ref, wx_ref, bx_ref, lam_ref, y_ref, st_ref,
         a_scr, h_scr) = refs
    x = x_ref[0, :, :W_BR]
    t = x.shape[0]
    row = lax.broadcasted_iota(jnp.int32, x.shape, 0)

    def shift_dn(z, k, fill):
        return jnp.where(row >= k, pltpu.roll(z, k, 0), fill)

    def shift_up(z, k, fill):
        return jnp.where(row < t - k, pltpu.roll(z, t - k, 0), fill)

    cw = cw_ref[...]
    xc = (cb_ref[...] + shift_dn(x, 2, 0.0) * cw[0:1] + shift_dn(x, 1, 0.0) * cw[1:2]
          + x * cw[2:3] + shift_up(x, 1, 0.0) * cw[3:4])
    xs = _split(xc)
    sub = row % SUBLANES
    for d in range(2):
        gate_a = _sigmoid(_mm3(xs, _split(wa_ref[d])) + ba_ref[d:d + 1])
        gate_x = _sigmoid(_mm3(xs, _split(wx_ref[d])) + bx_ref[d:d + 1])
        log_a = -C_RG * gate_a * _softplus(-lam_ref[d:d + 1])
        a = jnp.exp(log_a)
        u = jnp.sqrt(-jnp.tanh(log_a) * (a * a + 1.0)) * (gate_x * xc)
        k = 1
        while k < SUBLANES:
            if d == 0:
                keep = sub >= k
                sh = lambda z: pltpu.roll(z, k, 0)
            else:
                keep = sub < SUBLANES - k
                sh = lambda z: pltpu.roll(z, t - k, 0)
            u = a * jnp.where(keep, sh(u), 0.0) + u
            a = a * jnp.where(keep, sh(a), 1.0)
            k *= 2
        a_scr[d] = a
        h_scr[d] = u

    n_grp = t // SUBLANES
    if latent:
        carry0 = (h0_ref[0, 0, 0:1], h0_ref[0, 0, 1:2])
    else:
        carry0 = (jnp.zeros((1, W_BR), F32),) * 2

    def chain(i, carry):
        cf, cb = carry
        rf = pl.ds(pl.multiple_of(i * SUBLANES, SUBLANES), SUBLANES)
        rb = pl.ds(pl.multiple_of((n_grp - 1 - i) * SUBLANES, SUBLANES), SUBLANES)
        hf = h_scr[0, rf, :] + a_scr[0, rf, :] * cf
        hb = h_scr[1, rb, :] + a_scr[1, rb, :] * cb
        h_scr[0, rf, :] = hf
        h_scr[1, rb, :] = hb
        return hf[SUBLANES - 1:SUBLANES], hb[0:1]

    cf, cb = lax.fori_loop(0, n_grp, chain, carry0, unroll=4)
    y_ref[0] = h_scr[0] + h_scr[1]
    if not latent:
        st_ref[0] = jnp.concatenate([cf, cb], axis=0)


def _lru(pc, cw, cb, wa, ba, wx, bx, lam, state, layer):
    b, t, _ = pc.shape
    latent = state is not None
    n_blk = W_BR // HEAD
    full = lambda shape: pl.BlockSpec(shape, lambda i: (0,) * len(shape))
    in_specs = [pl.BlockSpec((1, t, WC), lambda i: (i, 0, 0)),
                full((4, W_BR)), full((1, W_BR)), full((2, W_BR, W_BR)), full((2, W_BR)),
                full((2, W_BR, W_BR)), full((2, W_BR)), full((2, W_BR))]

    def block_diag(w):
        out = jnp.zeros((2, W_BR, W_BR), F32)
        for n in range(n_blk):
            out = out.at[:, n * HEAD:(n + 1) * HEAD, n * HEAD:(n + 1) * HEAD].set(w[:, n])
        return out

    args = [pc, cw, cb, block_diag(wa), ba, block_diag(wx), bx, lam]
    out_specs = [pl.BlockSpec((1, t, W_BR), lambda i: (i, 0, 0))]
    out_shape = [jax.ShapeDtypeStruct((b, t, W_BR), F32)]
    if latent:
        in_specs.append(pl.BlockSpec((1, 1, 2, W_BR), lambda i: (i, layer, 0, 0)))
        args.append(state)
    else:
        out_specs.append(pl.BlockSpec((1, 2, W_BR), lambda i: (i, 0, 0)))
        out_shape.append(jax.ShapeDtypeStruct((b, 2, W_BR), F32))
    return pl.pallas_call(
        functools.partial(_lru_kernel, latent=latent),
        grid=(b,),
        in_specs=in_specs,
        out_specs=out_specs,
        out_shape=out_shape,
        scratch_shapes=[pltpu.VMEM((2, t, W_BR), F32), pltpu.VMEM((2, t, W_BR), F32)],
        compiler_params=_params(("parallel",)),
        name="lru",
    )(*args)


def _softmax_pv(blocks, sink):
    m = None
    for s, _ in blocks:
        mi = jnp.max(s, axis=-1, keepdims=True)
        m = mi if m is None else jnp.maximum(m, mi)
    if sink is not None:
        m = jnp.maximum(m, sink)
    den = None if sink is None else jnp.exp(sink - m)
    acc = None
    for s, v in blocks:
        e = jnp.exp(s - m)
        di = jnp.sum(e, axis=-1, keepdims=True)
        den = di if den is None else den + di
        o = _dot_bf(e, v)
        acc = o if acc is None else acc + o
    return acc, den


def _win_kernel(*refs, latent):
    scale = HEAD ** -0.5
    if latent:
        sink_ref, q_ref, kp_ref, kc_ref, kn_ref, ck_ref, cv_ref, o_ref = refs
        j = pl.program_id(1)
        nq = pl.num_programs(1)
        qi = lax.broadcasted_iota(jnp.int32, (Q_BLK, Q_BLK), 0)
        ki = lax.broadcasted_iota(jnp.int32, (Q_BLK, Q_BLK), 1)
        kv_blocks = [(kp_ref[0], (ki >= qi) & (j > 0)), (kc_ref[0], None),
                     (kn_ref[0], (ki <= qi) & (j < nq - 1))]
        ctx_k, ctx_v = ck_ref[0, 0], cv_ref[0, 0]
    else:
        sink_ref, x_ref, o_ref = refs
    q = q_ref[0] if latent else x_ref[0, :, :256]
    outs = []
    for h in range(2):
        hs = slice(h * HEAD, (h + 1) * HEAD)
        for g in range(2):
            qg = q[:, (2 * h + g) * HEAD:(2 * h + g + 1) * HEAD]
            blocks = []
            if latent:
                for kv, mask in kv_blocks:
                    s = _dot_nt_bf(qg, kv[:, hs]) * scale
                    if mask is not None:
                        s = jnp.where(mask, s, NEG_INF)
                    blocks.append((s, kv[:, 128 + h * HEAD:128 + (h + 1) * HEAD]))
                blocks.append((_dot_nt_bf(qg, ctx_k[:, hs]) * scale, ctx_v[:, hs]))
            else:
                k = x_ref[0, :, 256 + h * HEAD:256 + (h + 1) * HEAD]
                v = x_ref[0, :, 384 + h * HEAD:384 + (h + 1) * HEAD]
                blocks.append((_dot_nt_bf(qg, k) * scale, v))
            acc, den = _softmax_pv(blocks, sink_ref[2 * h + g])
            outs.append(acc / den)
    o_ref[0] = jnp.concatenate(outs, axis=-1)


def _win(pb, sink, cache_k, cache_v, layer):
    b, t, _ = pb.shape
    latent = cache_k is not None
    smem = pl.BlockSpec(memory_space=pltpu.SMEM)
    if latent:
        nq = t // Q_BLK
        grid = (b, nq)
        in_specs = [smem,
                    pl.BlockSpec((1, Q_BLK, 256), lambda i, j: (i, j, 0)),
                    pl.BlockSpec((1, Q_BLK, 256), lambda i, j: (i, jnp.maximum(j - 1, 0), 1)),
                    pl.BlockSpec((1, Q_BLK, 256), lambda i, j: (i, j, 1)),
                    pl.BlockSpec((1, Q_BLK, 256), lambda i, j: (i, jnp.minimum(j + 1, nq - 1), 1)),
                    pl.BlockSpec((1, 1) + cache_k.shape[2:], lambda i, j: (i, layer, 0, 0)),
                    pl.BlockSpec((1, 1) + cache_v.shape[2:], lambda i, j: (i, layer, 0, 0))]
        args = [sink, pb, pb, pb, pb, cache_k, cache_v]
        out_spec = pl.BlockSpec((1, Q_BLK, W_BR), lambda i, j: (i, j, 0))
        sem = ("parallel", "parallel")
    else:
        grid = (b,)
        in_specs = [smem, pl.BlockSpec((1, t, WB), lambda i: (i, 0, 0))]
        args = [sink, pb]
        out_spec = pl.BlockSpec((1, t, W_BR), lambda i: (i, 0, 0))
        sem = ("parallel",)
    return pl.pallas_call(
        functools.partial(_win_kernel, latent=latent),
        grid=grid,
        in_specs=in_specs,
        out_specs=out_spec,
        out_shape=jax.ShapeDtypeStruct((b, t, W_BR), F32),
        compiler_params=_params(sem),
        name="win",
    )(*args)


def _diff_kernel(*refs, latent, lam_init):
    if latent:
        lam_ref, g_ref, q_ref, k_ref, v_ref, ck_ref, cv_ref, o_ref = refs
    else:
        lam_ref, g_ref, q_ref, k_ref, v_ref, o_ref = refs
    scale = DQ_D ** -0.5
    lp = lam_ref[...]
    lam = (jnp.exp(jnp.sum(lp[0:1] * lp[1:2], axis=-1, keepdims=True))
           - jnp.exp(jnp.sum(lp[2:3] * lp[3:4], axis=-1, keepdims=True)) + lam_init)
    q, k, v = q_ref[0], k_ref[0], v_ref[0]
    key_sets = [(k, v)]
    if latent:
        key_sets.append((ck_ref[0, 0], cv_ref[0, 0]))
    outs = []
    for h in range(W_BR // HEAD):
        hs = slice(h * HEAD, (h + 1) * HEAD)
        probs = []
        for m in range(2):
            ms = slice(h * HEAD + m * DQ_D, h * HEAD + (m + 1) * DQ_D)
            logits = [_dot_nt_bf(q[:, ms], kk[:, ms]) * scale for kk, _ in key_sets]
            mx = functools.reduce(jnp.maximum, [jnp.max(s, axis=-1, keepdims=True) for s in logits])
            es = [jnp.exp(s - mx) for s in logits]
            den = functools.reduce(lambda a, b: a + b, [jnp.sum(e, axis=-1, keepdims=True) for e in es])
            inv = 1.0 / den
            probs.append([e * inv for e in es])
        o = None
        for i, (_, vv) in enumerate(key_sets):
            oi = _dot_bf(probs[0][i] - lam * probs[1][i], vv[:, hs])
            o = oi if o is None else o + oi
        o = o * lax.rsqrt(jnp.mean(o * o, -1, keepdims=True) + NORM_EPS) * g_ref[...]
        outs.append(o * (1.0 - lam_init))
    o_ref[0] = jnp.concatenate(outs, axis=-1)


def _diff(pd, lam_p, subln_g, cache_k, cache_v, layer, lam_init):
    b, t, _ = pd.shape
    latent = cache_k is not None
    tq = 256
    in_specs = [pl.BlockSpec((4, DQ_D), lambda i, j: (0, 0)),
                pl.BlockSpec((1, HEAD), lambda i, j: (0, 0)),
                pl.BlockSpec((1, tq, 256), lambda i, j: (i, j, 0)),
                pl.BlockSpec((1, t, 256), lambda i, j: (i, 0, 1)),
                pl.BlockSpec((1, t, 256), lambda i, j: (i, 0, 2))]
    args = [lam_p, subln_g, pd, pd, pd]
    if latent:
        in_specs += [pl.BlockSpec((1, 1) + cache_k.shape[2:], lambda i, j: (i, layer, 0, 0)),
                     pl.BlockSpec((1, 1) + cache_v.shape[2:], lambda i, j: (i, layer, 0, 0))]
        args += [cache_k, cache_v]
    return pl.pallas_call(
        functools.partial(_diff_kernel, latent=latent, lam_init=lam_init),
        grid=(b, t // tq),
        in_specs=in_specs,
        out_specs=pl.BlockSpec((1, tq, W_BR), lambda i, j: (i, j, 0)),
        out_shape=jax.ShapeDtypeStruct((b, t, W_BR), F32),
        compiler_params=_params(("parallel", "parallel")),
        name="diff",
    )(*args)


def _out_kernel(x_ref, mod_ref, gpost_ref, w_ref, pa_ref, yf_ref, yb_ref, gng_ref, gnb_ref, rk_ref,
                ywin_ref, bg_ref, ylru_ref, cg_ref, ydiff_ref, dg_ref, o_ref):
    ones = _head_ones(W_BR)
    pa = pa_ref[0]
    r, k, v, ag = pa[:, 0:256], pa[:, 256:512], pa[:, 512:768], pa[:, 896:1152]
    y = yf_ref[0] + yb_ref[0]
    mu = _mm2(y, ones) * (1.0 / HEAD)
    yc = y - mu
    var = _mm2(yc * yc, ones) * (1.0 / HEAD)
    ya = yc * lax.rsqrt(var + GN_EPS) * gng_ref[...] + gnb_ref[...]
    ya = ya + _mm2(r * k * rk_ref[...], ones) * v
    mix = jnp.concatenate([ya * _silu(ag), ywin_ref[0] * _silu(bg_ref[0]),
                           ylru_ref[0] * _silu(cg_ref[0]), ydiff_ref[0] * _silu(dg_ref[0])], axis=-1)
    o = jnp.dot(mix.astype(BF16), w_ref[0], preferred_element_type=F32)
    o = o * lax.rsqrt(jnp.mean(o * o, -1, keepdims=True) + NORM_EPS) * gpost_ref[...]
    o_ref[0] = x_ref[0] + mod_ref[0][:, 2 * D_MODEL:] * o


def _out(x, mod_l, g_post, w_out_bf, layer, latent, pa, yf, yb, gn_g, gn_b, r_k, ywin, pb, ylru, pc, ydiff, pd):
    b, t, _ = x.shape
    tm = ROW_TILE
    mod_idx = (lambda i, j: (1 + i, 0, 0)) if latent else (lambda i, j: (0, 0, 0))
    rows = lambda w, c=0: pl.BlockSpec((1, tm, w), lambda i, j: (i, j, c))
    vec = pl.BlockSpec((1, W_BR), lambda i, j: (0, 0))
    in_specs = [rows(D_MODEL),
                pl.BlockSpec((1, 1, 3 * D_MODEL), mod_idx),
                pl.BlockSpec((1, D_MODEL), lambda i, j: (0, 0)),
                pl.BlockSpec((1, D_MODEL, D_MODEL), lambda i, j: (layer, 0, 0)),
                rows(WA), rows(W_BR), rows(W_BR), vec, vec, vec,
                rows(W_BR), rows(W_BR, 2), rows(W_BR), rows(W_BR, 1), rows(W_BR), rows(W_BR, 3)]
    return pl.pallas_call(
        _out_kernel,
        grid=(b, t // tm),
        in_specs=in_specs,
        out_specs=rows(D_MODEL),
        out_shape=jax.ShapeDtypeStruct((b, t, D_MODEL), F32),
        compiler_params=_params(("parallel", "parallel")),
        name="out",
    )(x, mod_l, g_post, w_out_bf, pa, yf, yb, gn_g, gn_b, r_k, ywin, pb, ylru, pc, ydiff, pd)


def _layer(x, mod_l, layer, lam_init, wts, cache, tables):
    latent = cache is not None
    row = lambda a: a[layer].reshape(1, -1)
    pa, pb, pc, pd = _project(x, mod_l, row(wts['g_pre']), wts['w_in_bf'], layer, tables)
    rw = _rwkv(pa, wts['rwkv_w0'][layer], wts['rwkv_w_up'][layer], wts['rwkv_a0'][layer],
               wts['rwkv_a_up'][layer], row(wts['rwkv_k_k']), row(wts['rwkv_k_a']),
               cache['rwkv'] if latent else None, layer)
    lr = _lru(pc, wts['lru_conv_w'][layer], row(wts['lru_conv_b']), wts['lru_wa'][layer],
              wts['lru_ba'][layer], wts['lru_wx'][layer], wts['lru_bx'][layer], wts['lru_lambda'][layer],
              cache['lru'] if latent else None, layer)
    ywin = _win(pb, wts['win_sink'][layer], cache['win_k'] if latent else None,
                cache['win_v'] if latent else None, layer)
    ydiff = _diff(pd, wts['diff_lambda'][layer], row(wts['diff_subln_g']),
                  cache['diff_k'] if latent else None, cache['diff_v'] if latent else None, layer, lam_init)
    y = _out(x, mod_l, row(wts['g_post']), wts['w_out_bf'], layer, latent, pa, rw[0], rw[1],
             row(wts['rwkv_gn_g']), row(wts['rwkv_gn_b']), row(wts['rwkv_r_k']),
             ywin, pb, lr[0], pc, ydiff, pd)
    new_cache = None if latent else (pb, pd, rw[2], lr[1])
    return y, new_cache


def kernel(x_prompt, x_sample, c, cache_win_k, cache_win_v, cache_diff_k, cache_diff_v, state_rwkv, state_lru,
           c_ctx, w_mod, b_mod, g_pre, g_post, w_in, w_out,
           rwkv_w0, rwkv_w_up, rwkv_a0, rwkv_a_up, rwkv_k_k, rwkv_k_a, rwkv_r_k, rwkv_gn_g, rwkv_gn_b,
           win_sink, lru_conv_w, lru_conv_b, lru_wa, lru_ba, lru_wx, lru_bx, lru_lambda,
           diff_lambda, diff_subln_g):
    n_b, seq = x_prompt.shape[:2]
    n_dec, dec_seq = x_sample.shape[:2]
    past = cache_win_k.shape[2]
    wts = dict(g_pre=g_pre, g_post=g_post, w_in_bf=w_in.astype(BF16), w_out_bf=w_out.astype(BF16),
               rwkv_w0=rwkv_w0, rwkv_w_up=rwkv_w_up, rwkv_a0=rwkv_a0, rwkv_a_up=rwkv_a_up,
               rwkv_k_k=rwkv_k_k, rwkv_k_a=rwkv_k_a, rwkv_r_k=rwkv_r_k, rwkv_gn_g=rwkv_gn_g,
               rwkv_gn_b=rwkv_gn_b, win_sink=win_sink, lru_conv_w=lru_conv_w, lru_conv_b=lru_conv_b,
               lru_wa=lru_wa, lru_ba=lru_ba, lru_wx=lru_wx, lru_bx=lru_bx, lru_lambda=lru_lambda,
               diff_lambda=diff_lambda, diff_subln_g=diff_subln_g)
    cache = dict(win_k=cache_win_k.reshape(n_dec, DEPTH, past, 128),
                 win_v=cache_win_v.reshape(n_dec, DEPTH, past, 128),
                 diff_k=cache_diff_k.reshape(n_dec, DEPTH, past, 256),
                 diff_v=cache_diff_v.reshape(n_dec, DEPTH, past, 256),
                 rwkv=state_rwkv, lru=state_lru)
    cvec = jnp.concatenate([c_ctx[None], c, jnp.zeros((8 - 1 - n_dec, D_MODEL), F32)], axis=0)
    mod = _modulation(cvec, w_mod, b_mod)
    cos_b, sin_b = _rope_tables(dec_seq, HEAD, 384)
    cos_d, sin_d = _rope_tables(dec_seq, DQ_D, 512)
    tables = (cos_b, sin_b, cos_d, sin_d)

    y_p, y_s = x_prompt, x_sample
    ctx = []
    for l in range(DEPTH):
        lam_init = 0.8 - 0.6 * math.exp(-0.3 * l)
        mod_l = mod[l].reshape(8, 1, 3 * D_MODEL)
        y_p, nc = _layer(y_p, mod_l, l, lam_init, wts, None, None)
        ctx.append(nc)
        y_s, _ = _layer(y_s, mod_l, l, lam_init, wts, cache, tables)
    stack = lambda f: jnp.stack([f(ct) for ct in ctx], axis=1)
    new_win_k = stack(lambda ct: ct[0][:, :, 256:384].reshape(n_b, seq, 2, HEAD))
    new_win_v = stack(lambda ct: ct[0][:, :, 384:512].reshape(n_b, seq, 2, HEAD))
    new_diff_k = stack(lambda ct: ct[1][:, :, 256:512].reshape(n_b, seq, 4, 2, DQ_D))
    new_diff_v = stack(lambda ct: ct[1][:, :, 512:768].reshape(n_b, seq, 4, HEAD))
    new_state_rwkv = stack(lambda ct: ct[2])
    new_state_lru = stack(lambda ct: ct[3])
    return (y_p, y_s, new_win_k, new_win_v, new_diff_k, new_diff_v, new_state_rwkv, new_state_lru)
```

```python
import functools
import math

import numpy as np
import jax
import jax.numpy as jnp
from jax import lax
from jax.experimental import pallas as pl
from jax.experimental.pallas import tpu as pltpu

F32 = jnp.float32
BF16 = jnp.bfloat16
HI = lax.Precision.HIGHEST

D_MODEL = 1024
DEPTH = 2
GRID_W = 64
ROPE_BASE = 10000.0
NORM_EPS = 1e-6
NEG_INF = -1e30
LOG2_E = math.log2(math.e)
GN_EPS = 64e-5
C_RG = 8.0
W_BR = 256
HEAD = 64
SUBLANES = 8
LORA = 64
DQ_D = 32
WINDOW = 128
Q_BLK = 128
P_TOTAL = 3456
WA, WB, WC, WD = 1152, 768, 512, 1024
CHUNK = 64
RWKV_BATCH = 4
ROW_TILE = 256
VMEM_LIMIT = 48 * 1024 * 1024


_NN = (((1,), (0,)), ((), ()))
_NT = (((1,), (1,)), ((), ()))
_TN = (((0,), (0,)), ((), ()))


def _dot_hi(a, b):
    return jnp.dot(a, b, precision=HI, preferred_element_type=F32)


def _bf(x):
    return x.astype(BF16)


def _dotf(x, y, dims=_NN):
    return lax.dot_general(x, y, dims, preferred_element_type=F32)


def _split(x):
    hi = x.astype(BF16)
    return hi, (x - hi.astype(F32)).astype(BF16)


def _mm3(a, b, dims=_NN):
    return _dotf(a[0], b[0], dims) + (_dotf(a[0], b[1], dims) + _dotf(a[1], b[0], dims))


def _mm2(x, m01, left=False):
    hi, lo = _split(x)
    if left:
        return _dotf(m01, hi) + _dotf(m01, lo)
    return _dotf(hi, m01) + _dotf(lo, m01)


def _dot_bf(a, b):
    return jnp.dot(a.astype(BF16), b.astype(BF16), preferred_element_type=F32)


def _dot_nt_bf(a, b):
    return lax.dot_general(a.astype(BF16), b.astype(BF16), (((1,), (1,)), ((), ())),
                           preferred_element_type=F32)


def _sigmoid(x):
    return 1.0 / (1.0 + jnp.exp(-x))


def _silu(x):
    return x * _sigmoid(x)


def _softplus(x):
    return jnp.maximum(x, 0.0) + jnp.log1p(jnp.exp(-jnp.abs(x)))


def _head_ones(n):
    r = lax.broadcasted_iota(jnp.int32, (n, n), 0) // HEAD
    c = lax.broadcasted_iota(jnp.int32, (n, n), 1) // HEAD
    return jnp.where(r == c, 1.0, 0.0).astype(BF16)


def _params(sem):
    return pltpu.CompilerParams(dimension_semantics=sem, vmem_limit_bytes=VMEM_LIMIT)


def _mod_kernel(c_ref, w_ref, b_ref, o_ref):
    o_ref[0] = _dot_hi(_silu(c_ref[...]), w_ref[0]) + b_ref[0]


def _modulation(cvec, w_mod, b_mod):
    n_l = w_mod.shape[0]
    tn = 512
    return pl.pallas_call(
        _mod_kernel,
        grid=(n_l, 3 * D_MODEL // tn),
        in_specs=[pl.BlockSpec((8, D_MODEL), lambda l, j: (0, 0)),
                  pl.BlockSpec((1, D_MODEL, tn), lambda l, j: (l, 0, j)),
                  pl.BlockSpec((1, 1, tn), lambda l, j: (l, 0, j))],
        out_specs=pl.BlockSpec((1, 8, tn), lambda l, j: (l, 0, j)),
        out_shape=jax.ShapeDtypeStruct((n_l, 8, 3 * D_MODEL), F32),
        compiler_params=_params(("parallel", "parallel")),
        name="mod",
    )(cvec, w_mod, b_mod.reshape(n_l, 1, 3 * D_MODEL))


def _rope(x, cos, sin_signed, off):
    w = x.shape[-1]
    lane = lax.broadcasted_iota(jnp.int32, x.shape, 1)
    first = (lane % (2 * off)) < off
    partner = jnp.where(first, pltpu.roll(x, w - off, 1), pltpu.roll(x, off, 1))
    return x * cos + partner * sin_signed


def _proj_kernel(*refs, latent):
    if latent:
        x_ref, mod_ref, g_ref, w_ref, cb_ref, sb_ref, cd_ref, sd_ref, oa, ob, oc, od = refs
    else:
        x_ref, mod_ref, g_ref, w_ref, oa, ob, oc, od = refs
    x = x_ref[0]
    y = x * lax.rsqrt(jnp.mean(x * x, -1, keepdims=True) + NORM_EPS) * g_ref[...]
    m = mod_ref[0]
    h = y * (1.0 + m[:, D_MODEL:2 * D_MODEL]) + m[:, :D_MODEL]
    p = jnp.dot(h.astype(BF16), w_ref[0], preferred_element_type=F32)
    oa[0] = p[:, :WA]
    pb = p[:, WA:WA + WB]
    pd = p[:, WA + WB + WC:]
    if latent:
        ob[0, :, :384] = _rope(pb[:, :384], cb_ref[...], sb_ref[...], 16)
        ob[0, :, 384:] = pb[:, 384:]
        od[0, :, :512] = _rope(pd[:, :512], cd_ref[...], sd_ref[...], 8)
        od[0, :, 512:] = pd[:, 512:]
    else:
        ob[0] = pb
        od[0] = pd
    oc[0] = p[:, WA + WB:WA + WB + WC]


def _project(x, mod_l, g_pre, w_in_bf, layer, tables):
    b, t, _ = x.shape
    latent = tables is not None
    tm = ROW_TILE
    mod_idx = (lambda i, j: (1 + i, 0, 0)) if latent else (lambda i, j: (0, 0, 0))
    in_specs = [pl.BlockSpec((1, tm, D_MODEL), lambda i, j: (i, j, 0)),
                pl.BlockSpec((1, 1, 3 * D_MODEL), mod_idx),
                pl.BlockSpec((1, D_MODEL), lambda i, j: (0, 0)),
                pl.BlockSpec((1, D_MODEL, P_TOTAL), lambda i, j: (layer, 0, 0))]
    args = [x, mod_l, g_pre, w_in_bf]
    if latent:
        for tab in tables:
            in_specs.append(pl.BlockSpec((tm, tab.shape[1]), lambda i, j: (j, 0)))
            args.append(tab)
    widths = (WA, WB, WC, WD)
    return pl.pallas_call(
        functools.partial(_proj_kernel, latent=latent),
        grid=(b, t // tm),
        in_specs=in_specs,
        out_specs=[pl.BlockSpec((1, tm, w), lambda i, j: (i, j, 0)) for w in widths],
        out_shape=[jax.ShapeDtypeStruct((b, t, w), F32) for w in widths],
        compiler_params=_params(("parallel", "parallel")),
        name="proj",
    )(*args)


def _rope_tables(t, head_dim, n_lanes):
    half = head_dim // 2
    quarter = half // 2
    pos = jnp.arange(t)
    row = (pos // GRID_W).astype(F32)
    col = (pos % GRID_W).astype(F32)
    inv = ROPE_BASE ** (-jnp.arange(0, half, 2, dtype=F32) / half)
    lane = np.arange(n_lanes) % head_dim
    in_half = lane % half
    p = jnp.where(jnp.asarray(lane < half)[None, :], row[:, None], col[:, None])
    ang = p * inv[in_half % quarter][None, :]
    sign = jnp.asarray(np.where(in_half < quarter, -1.0, 1.0), F32)[None, :]
    return jnp.cos(ang), jnp.sin(ang) * sign


_PREP_FIELDS = (('ar', 2, 1, BF16), ('nm', 1, 1, BF16), ('mm', 1, 1, BF16), ('pq', 1, 2, BF16),
                ('v_hi', 1, 1, BF16), ('v_lo', 1, 1, BF16), ('bkt_hi', 1, 2, BF16), ('bkt_lo', 1, 2, BF16),
                ('gct', 1, 1, F32))


def _prep_scratch(slots, n_streams):
    return [pltpu.VMEM((slots, n_streams, rows * CHUNK, lanes * W_BR), dt) for _, rows, lanes, dt in _PREP_FIELDS]


class _SlotView:
    def __init__(self, ref, slot):
        self.ref, self.slot, self.shape = ref, slot, ref.shape[1:]

    def __getitem__(self, idx):
        return self.ref[(self.slot,) + (idx if isinstance(idx, tuple) else (idx,))]

    def __setitem__(self, idx, val):
        self.ref[(self.slot,) + (idx if isinstance(idx, tuple) else (idx,))] = val


def _head_transpose(x):
    xt = x.T
    return jnp.concatenate([xt[h * HEAD:(h + 1) * HEAD] for h in range(x.shape[1] // HEAD)], axis=1)


def _expand(x, same):
    return jnp.concatenate([x] * (same.shape[0] // x.shape[0]), axis=0) * same


def _rwkv_prepare(xs, dirs, w0_ref, wup_ref, a0_ref, aup_ref, k_k, k_a, m_ref, same_ref, out):
    c = xs[0].shape[0]
    ones = _head_ones(W_BR)
    ti = lax.broadcasted_iota(jnp.int32, (c, c), 0)
    si = lax.broadcasted_iota(jnp.int32, (c, c), 1)
    cums = [jnp.where(si <= ti, 1.0, 0.0).astype(BF16), jnp.where(si >= ti, 1.0, 0.0).astype(BF16)]
    same = same_ref[...]

    st = []
    for xc, d in zip(xs, dirs):
        r, k, v = xc[:, 0:256], xc[:, 256:512], xc[:, 512:768]
        wd, ad = xc[:, 768:832], xc[:, 832:896]
        kk = k * k_k
        kk = kk * lax.rsqrt(_mm2(kk * kk, ones) + 1e-12)
        z = w0_ref[d:d + 1] + _mm3(_split(jnp.tanh(wd)), _split(wup_ref[d]))
        e = jnp.exp(-_softplus(-z) - 0.5)
        a = _sigmoid(a0_ref[d:d + 1] + _mm3(_split(ad), _split(aup_ref[d])))
        st.append(dict(r=r, v=v, e=e, kd=k * (1.0 + (a - 1.0) * k_a), alpha=-kk, beta=kk * a))
    yield
    for s, d in zip(st, dirs):
        l_incl = _mm2(s['e'], cums[d], left=True)
        l_tot = jnp.sum(s['e'], axis=0, keepdims=True)
        grow = jnp.exp(l_incl)
        tail = jnp.exp(l_incl - l_tot)
        s.update(ar=jnp.concatenate([_bf(s['alpha'] * jnp.exp(s['e'] - l_incl)),
                                     _bf(s['r'] * jnp.exp(-l_incl))], axis=0),
                 b_t=_bf(s['beta'] * grow), k_t=_bf(s['kd'] * grow),
                 b_h=s['beta'] * tail, k_h=s['kd'] * tail,
                 g_c=jnp.broadcast_to(jnp.exp(-l_tot), (c, W_BR)))
    yield
    for g, (s, d) in enumerate(zip(st, dirs)):
        out['ar'][g] = s['ar']
        rhs_t = jnp.concatenate([_expand(s['b_t'], same), _expand(s['k_t'], same)], axis=0)
        g4 = _dotf(s['ar'], rhs_t, _NT)
        strict = m_ref[_M_STRICT + 2 * d]
        incl = m_ref[_M_INCL + 2 * d]
        out['nm'][g] = _bf(g4[:c, :W_BR]) * strict
        out['mm'][g] = _bf(g4[:c, W_BR:]) * strict
        out['pq'][g, :, :W_BR] = _bf(g4[c:, :W_BR]) * incl
        out['pq'][g, :, W_BR:] = _bf(g4[c:, W_BR:]) * incl
    yield
    for g, s in enumerate(st):
        v_hi, v_lo = _split(s['v'])
        out['v_hi'][g] = v_hi
        out['v_lo'][g] = v_lo
        b_hi, b_lo = _split(_head_transpose(s['b_h']))
        k_hi, k_lo = _split(_head_transpose(s['k_h']))
        out['bkt_hi'][g] = jnp.concatenate([b_hi, k_hi], axis=1)
        out['bkt_lo'][g] = jnp.concatenate([b_lo, k_lo], axis=1)
        out['gct'][g] = _head_transpose(s['g_c'])


def _rwkv_advance(p, m_ref, same_ref, z_ref, y_stores):
    n_streams, c = p['nm'].shape[0], p['nm'].shape[1]
    gs = range(n_streams)
    same = same_ref[...]
    ex = lambda x: _expand(_bf(x), same)
    invs = [(p['nm'][g] * m_ref[_M_PAIR]).astype(F32) + m_ref[_M_EYE].astype(F32) for g in gs]
    for lvl in range(int(math.log2(c)) - 1):
        inv_x = [ex(t) for t in invs]
        half = [_dotf(p['nm'][g] * m_ref[_M_OFF + lvl], inv_x[g]) for g in gs]
        invs = [invs[g] + _dotf(_bf(invs[g]), ex(half[g])) for g in gs]
        yield
    z0 = [z_ref[g] for g in gs]
    as0 = [_dotf(p['ar'][g], ex(z0[g])) for g in gs]
    vx = [(_expand(p['v_hi'][g], same), _expand(p['v_lo'][g], same)) for g in gs]
    rhs = [as0[g][:c] + _dotf(p['mm'][g], vx[g][0]) for g in gs]
    yield
    u = [_split(_dotf(_bf(invs[g]), ex(rhs[g]))) for g in gs]
    uv = [(jnp.concatenate([_expand(u[g][0], same), vx[g][0]], axis=0),
           jnp.concatenate([_expand(u[g][1], same), vx[g][1]], axis=0)) for g in gs]
    yield
    for g in gs:
        z_ref[g] = z0[g] * p['gct'][g] + _mm3((p['bkt_hi'][g], p['bkt_lo'][g]), uv[g])
    yield
    for g in gs:
        y_stores[g](as0[g][c:] + _dotf(p['pq'][g], uv[g][0]))


def _interleave(*gens):
    live = list(gens)
    while live:
        for gen in list(live):
            try:
                next(gen)
            except StopIteration:
                live.remove(gen)


_M_STRICT, _M_INCL, _M_EYE, _M_PAIR, _M_OFF = 0, 1, 4, 5, 6


def _rwkv_masks(c, n_h):
    t = np.arange(c)[:, None]
    s = np.arange(c)[None, :]
    masks = [s < t, s <= t, s > t, s >= t, s == t, ((t // 2) == (s // 2)) & (t != s)]
    b = 2
    while b < c:
        masks.append(((t // (2 * b)) == (s // (2 * b))) & ((t // b) != (s // b)))
        b *= 2
    masks = np.tile(np.stack(masks).astype(np.float32), (1, 1, n_h))
    r = np.arange(n_h * c)[:, None] // c
    l = np.arange(n_h * HEAD)[None, :] // HEAD
    return jnp.asarray(masks, BF16), jnp.asarray((r == l).astype(np.float32), BF16)


def _rwkv_kernel(*refs, latent, n_b):
    (xf_ref, xb_ref, xfn_ref, xbn_ref, w0_ref, wup_ref, a0_ref, aup_ref, kk_ref, ka_ref,
     m_ref, same_ref) = refs[:12]
    if latent:
        s0_ref, yf_ref, yb_ref = refs[12:15]
    else:
        yf_ref, yb_ref, st_ref = refs[12:15]
    scr = refs[15:]
    z_scr = scr[0]
    names = [f[0] for f in _PREP_FIELDS]
    ci = pl.program_id(1)
    view = lambda refs_, slot: {k: _SlotView(r, slot) for k, r in zip(names, refs_)}
    cur = view(scr[1:1 + len(names)], ci % 2)
    nxt = view(scr[1:1 + len(names)], 1 - ci % 2)
    mid = view(scr[1 + len(names):], 0)
    n_h = W_BR // HEAD
    c = CHUNK
    head = lambda h: slice(h * HEAD, (h + 1) * HEAD)
    streams = [(bi, d) for bi in range(n_b) for d in range(2)]
    dirs = [d for _, d in streams]
    params = (w0_ref, wup_ref, a0_ref, aup_ref, kk_ref[...], ka_ref[...], m_ref, same_ref)
    first = lambda d: slice(0, c) if d == 0 else slice(c, 2 * c)
    second = lambda d: slice(c, 2 * c) if d == 0 else slice(0, c)
    x_refs, xn_refs, y_refs = (xf_ref, xb_ref), (xfn_ref, xbn_ref), (yf_ref, yb_ref)

    def y_store(bi, d, rows):
        def store(y):
            y_refs[d][bi, rows, :] = y
        return store

    @pl.when(ci == 0)
    def _():
        for g, (bi, d) in enumerate(streams):
            if latent:
                z_scr[g] = jnp.concatenate([s0_ref[bi, 0, d, h].T for h in range(n_h)], axis=1)
            else:
                z_scr[g] = jnp.zeros(z_scr.shape[1:], F32)
        _interleave(_rwkv_prepare([x_refs[d][bi, first(d), :] for bi, d in streams], dirs, *params, cur))

    _interleave(_rwkv_advance(cur, m_ref, same_ref, z_scr, [y_store(bi, d, first(d)) for bi, d in streams]),
                _rwkv_prepare([x_refs[d][bi, second(d), :] for bi, d in streams], dirs, *params, mid))
    _interleave(_rwkv_advance(mid, m_ref, same_ref, z_scr, [y_store(bi, d, second(d)) for bi, d in streams]),
                _rwkv_prepare([xn_refs[d][bi, first(d), :] for bi, d in streams], dirs, *params, nxt))

    if not latent:
        @pl.when(ci == pl.num_programs(1) - 1)
        def _():
            for g, (bi, d) in enumerate(streams):
                z = z_scr[g]
                for h in range(n_h):
                    st_ref[bi, d, h] = z[:, head(h)].T


def _rwkv(pa, w0, wup, a0, aup, k_k, k_a, state, layer):
    b, t, _ = pa.shape
    latent = state is not None
    ns = t // (2 * CHUNK)
    n_h = W_BR // HEAD
    n_b = RWKV_BATCH
    masks, same = _rwkv_masks(CHUNK, n_h)
    full = lambda shape: pl.BlockSpec(shape, lambda i, j: (0,) * len(shape))
    rows = lambda w, idx: pl.BlockSpec((n_b, 2 * CHUNK, w), idx)
    in_specs = [rows(WA, lambda i, j: (i, j, 0)),
                rows(WA, lambda i, j: (i, ns - 1 - j, 0)),
                rows(WA, lambda i, j: (i, jnp.minimum(j + 1, ns - 1), 0)),
                rows(WA, lambda i, j: (i, jnp.maximum(ns - 2 - j, 0), 0)),
                full((2, W_BR)), full((2, LORA, W_BR)), full((2, W_BR)), full((2, LORA, W_BR)),
                full((1, W_BR)), full((1, W_BR)), full(masks.shape), full(same.shape)]
    args = [pa, pa, pa, pa, w0, wup, a0, aup, k_k, k_a, masks, same]
    out_specs = [rows(W_BR, lambda i, j: (i, j, 0)), rows(W_BR, lambda i, j: (i, ns - 1 - j, 0))]
    out_shape = [jax.ShapeDtypeStruct((b, t, W_BR), F32)] * 2
    if latent:
        in_specs.append(pl.BlockSpec((n_b, 1, 2, n_h, HEAD, HEAD), lambda i, j: (i, layer, 0, 0, 0, 0)))
        args.append(state)
    else:
        out_specs.append(pl.BlockSpec((n_b, 2, n_h, HEAD, HEAD), lambda i, j: (i, 0, 0, 0, 0)))
        out_shape.append(jax.ShapeDtypeStruct((b, 2, n_h, HEAD, HEAD), F32))
    return pl.pallas_call(
        functools.partial(_rwkv_kernel, latent=latent, n_b=n_b),
        grid=(b // n_b, ns),
        in_specs=in_specs,
        out_specs=out_specs,
        out_shape=out_shape,
        scratch_shapes=([pltpu.VMEM((2 * n_b, CHUNK, W_BR), F32)]
                        + _prep_scratch(2, 2 * n_b) + _prep_scratch(1, 2 * n_b)),
        compiler_params=_params(("parallel", "arbitrary")),
        name="rwkv",
    )(*args)


def _lru_kernel(*refs, latent):
    if latent:
        (x_ref, cw_ref, cb_ref, wa_ref, ba_ref, wx_ref, bx_ref, lam_ref, h0_ref, y_ref,
         a_scr, h_scr) = refs
    else:
        (x_ref, cw_ref, cb_ref, wa_ref, ba_ref, wx_ref, bx_ref, lam_ref, y_ref, st_ref,
         a_scr, h_scr) = refs
    x = x_ref[0, :, :W_BR]
    t = x.shape[0]
    row = lax.broadcasted_iota(jnp.int32, x.shape, 0)

    def shift_dn(z, k, fill):
        return jnp.where(row >= k, pltpu.roll(z, k, 0), fill)

    def shift_up(z, k, fill):
        return jnp.where(row < t - k, pltpu.roll(z, t - k, 0), fill)

    cw = cw_ref[...]
    xc = (cb_ref[...] + shift_dn(x, 2, 0.0) * cw[0:1] + shift_dn(x, 1, 0.0) * cw[1:2]
          + x * cw[2:3] + shift_up(x, 1, 0.0) * cw[3:4])
    xs = _split(xc)
    sub = row % SUBLANES
    for d in range(2):
        gate_a = _sigmoid(_mm3(xs, _split(wa_ref[d])) + ba_ref[d:d + 1])
        gate_x = _sigmoid(_mm3(xs, _split(wx_ref[d])) + bx_ref[d:d + 1])
        log_a = -C_RG * gate_a * _softplus(-lam_ref[d:d + 1])
        a = jnp.exp(log_a)
        u = jnp.sqrt(-jnp.tanh(log_a) * (a * a + 1.0)) * (gate_x * xc)
        k = 1
        while k < SUBLANES:
            if d == 0:
                keep = sub >= k
                sh = lambda z: pltpu.roll(z, k, 0)
            else:
                keep = sub < SUBLANES - k
                sh = lambda z: pltpu.roll(z, t - k, 0)
            u = a * jnp.where(keep, sh(u), 0.0) + u
            a = a * jnp.where(keep, sh(a), 1.0)
            k *= 2
        a_scr[d] = a
        h_scr[d] = u

    n_grp = t // SUBLANES
    if latent:
        carry0 = (h0_ref[0, 0, 0:1], h0_ref[0, 0, 1:2])
    else:
        carry0 = (jnp.zeros((1, W_BR), F32),) * 2

    def chain(i, carry):
        cf, cb = carry
        rf = pl.ds(pl.multiple_of(i * SUBLANES, SUBLANES), SUBLANES)
        rb = pl.ds(pl.multiple_of((n_grp - 1 - i) * SUBLANES, SUBLANES), SUBLANES)
        hf = h_scr[0, rf, :] + a_scr[0, rf, :] * cf
        hb = h_scr[1, rb, :] + a_scr[1, rb, :] * cb
        h_scr[0, rf, :] = hf
        h_scr[1, rb, :] = hb
        return hf[SUBLANES - 1:SUBLANES], hb[0:1]

    cf, cb = lax.fori_loop(0, n_grp, chain, carry0, unroll=4)
    y_ref[0] = h_scr[0] + h_scr[1]
    if not latent:
        st_ref[0] = jnp.concatenate([cf, cb], axis=0)


def _lru(pc, cw, cb, wa, ba, wx, bx, lam, state, layer):
    b, t, _ = pc.shape
    latent = state is not None
    n_blk = W_BR // HEAD
    full = lambda shape: pl.BlockSpec(shape, lambda i: (0,) * len(shape))
    in_specs = [pl.BlockSpec((1, t, WC), lambda i: (i, 0, 0)),
                full((4, W_BR)), full((1, W_BR)), full((2, W_BR, W_BR)), full((2, W_BR)),
                full((2, W_BR, W_BR)), full((2, W_BR)), full((2, W_BR))]

    def block_diag(w):
        out = jnp.zeros((2, W_BR, W_BR), F32)
        for n in range(n_blk):
            out = out.at[:, n * HEAD:(n + 1) * HEAD, n * HEAD:(n + 1) * HEAD].set(w[:, n])
        return out

    args = [pc, cw, cb, block_diag(wa), ba, block_diag(wx), bx, lam]
    out_specs = [pl.BlockSpec((1, t, W_BR), lambda i: (i, 0, 0))]
    out_shape = [jax.ShapeDtypeStruct((b, t, W_BR), F32)]
    if latent:
        in_specs.append(pl.BlockSpec((1, 1, 2, W_BR), lambda i: (i, layer, 0, 0)))
        args.append(state)
    else:
        out_specs.append(pl.BlockSpec((1, 2, W_BR), lambda i: (i, 0, 0)))
        out_shape.append(jax.ShapeDtypeStruct((b, 2, W_BR), F32))
    return pl.pallas_call(
        functools.partial(_lru_kernel, latent=latent),
        grid=(b,),
        in_specs=in_specs,
        out_specs=out_specs,
        out_shape=out_shape,
        scratch_shapes=[pltpu.VMEM((2, t, W_BR), F32), pltpu.VMEM((2, t, W_BR), F32)],
        compiler_params=_params(("parallel",)),
        name="lru",
    )(*args)


def _win_kernel(*refs, latent):
    if latent:
        sink_ref, q_ref, kp_ref, kc_ref, kn_ref, ck_ref, cv_ref, o_ref = refs
        j = pl.program_id(1)
        nq = pl.num_programs(1)
        qi = lax.broadcasted_iota(jnp.int32, (Q_BLK, Q_BLK), 0)
        ki = lax.broadcasted_iota(jnp.int32, (Q_BLK, Q_BLK), 1)
        q = q_ref[0]
        ctx_k, ctx_v = _bf(ck_ref[0, 0]), _bf(cv_ref[0, 0])
        pieces = [(_bf(r[0, :, :128]), _bf(r[0, :, 128:]), m) for r, m in
                  ((kp_ref, (ki >= qi) & (j > 0)), (kc_ref, None), (kn_ref, (ki <= qi) & (j < nq - 1)))]
        pieces += [(ctx_k[i:i + Q_BLK], ctx_v[i:i + Q_BLK], None) for i in range(0, ctx_k.shape[0], Q_BLK)]
    else:
        sink_ref, x_ref, o_ref = refs
        q = x_ref[0, :, :256]
        k, v = _bf(x_ref[0, :, 256:384]), _bf(x_ref[0, :, 384:512])
        pieces = [(k[i:i + Q_BLK], v[i:i + Q_BLK], None) for i in range(0, k.shape[0], Q_BLK)]
    q = _bf(q * (HEAD ** -0.5 * LOG2_E))
    outs = []
    for h in range(2):
        hs = slice(h * HEAD, (h + 1) * HEAD)
        for g in range(2):
            qg = q[:, (2 * h + g) * HEAD:(2 * h + g + 1) * HEAD]
            sink = sink_ref[2 * h + g] * LOG2_E
            logits = []
            for kp, _, mask in pieces:
                s = _dotf(qg, kp[:, hs], _NT)
                logits.append(s if mask is None else jnp.where(mask, s, NEG_INF))
            mx = jnp.maximum(jnp.max(functools.reduce(jnp.maximum, logits), axis=-1, keepdims=True), sink)
            es = [jnp.exp2(s - mx) for s in logits]
            den = jnp.sum(functools.reduce(lambda a, b: a + b, es), axis=-1, keepdims=True) + jnp.exp2(sink - mx)
            acc = functools.reduce(lambda a, b: a + b,
                                   [_dotf(_bf(e), vp[:, hs]) for e, (_, vp, _) in zip(es, pieces)])
            outs.append(acc / den)
    o_ref[0] = jnp.concatenate(outs, axis=-1)


def _win(pb, sink, cache_k, cache_v, layer):
    b, t, _ = pb.shape
    latent = cache_k is not None
    smem = pl.BlockSpec(memory_space=pltpu.SMEM)
    if latent:
        nq = t // Q_BLK
        grid = (b, nq)
        in_specs = [smem,
                    pl.BlockSpec((1, Q_BLK, 256), lambda i, j: (i, j, 0)),
                    pl.BlockSpec((1, Q_BLK, 256), lambda i, j: (i, jnp.maximum(j - 1, 0), 1)),
                    pl.BlockSpec((1, Q_BLK, 256), lambda i, j: (i, j, 1)),
                    pl.BlockSpec((1, Q_BLK, 256), lambda i, j: (i, jnp.minimum(j + 1, nq - 1), 1)),
                    pl.BlockSpec((1, 1) + cache_k.shape[2:], lambda i, j: (i, layer, 0, 0)),
                    pl.BlockSpec((1, 1) + cache_v.shape[2:], lambda i, j: (i, layer, 0, 0))]
        args = [sink, pb, pb, pb, pb, cache_k, cache_v]
        out_spec = pl.BlockSpec((1, Q_BLK, W_BR), lambda i, j: (i, j, 0))
        sem = ("parallel", "parallel")
    else:
        grid = (b,)
        in_specs = [smem, pl.BlockSpec((1, t, WB), lambda i: (i, 0, 0))]
        args = [sink, pb]
        out_spec = pl.BlockSpec((1, t, W_BR), lambda i: (i, 0, 0))
        sem = ("parallel",)
    return pl.pallas_call(
        functools.partial(_win_kernel, latent=latent),
        grid=grid,
        in_specs=in_specs,
        out_specs=out_spec,
        out_shape=jax.ShapeDtypeStruct((b, t, W_BR), F32),
        compiler_params=_params(sem),
        name="win",
    )(*args)


def _diff_kernel(*refs, latent, lam_init):
    if latent:
        lam_ref, g_ref, q_ref, k_ref, v_ref, ck_ref, cv_ref, o_ref = refs
    else:
        lam_ref, g_ref, q_ref, k_ref, v_ref, o_ref = refs
    lp = lam_ref[...]
    lam = (jnp.exp(jnp.sum(lp[0:1] * lp[1:2], axis=-1, keepdims=True))
           - jnp.exp(jnp.sum(lp[2:3] * lp[3:4], axis=-1, keepdims=True)) + lam_init)
    q = _bf(q_ref[0] * (DQ_D ** -0.5 * LOG2_E))
    key_sets = [(_bf(k_ref[0]), _bf(v_ref[0]))]
    if latent:
        key_sets.append((_bf(ck_ref[0, 0]), _bf(cv_ref[0, 0])))
    lane = lax.broadcasted_iota(jnp.int32, (1, W_BR), 1)
    out = jnp.zeros(q.shape, F32)
    for h in range(W_BR // HEAD):
        probs = []
        for m in range(2):
            qm = jnp.where(lane // DQ_D == 2 * h + m, q, jnp.zeros((), BF16))
            logits = [_dotf(qm, kk, _NT) for kk, _ in key_sets]
            mx = functools.reduce(jnp.maximum, [jnp.max(s, axis=-1, keepdims=True) for s in logits])
            es = [jnp.exp2(s - mx) for s in logits]
            den = functools.reduce(lambda a, b: a + b, [jnp.sum(e, axis=-1, keepdims=True) for e in es])
            inv = 1.0 / den
            probs.append([e * inv for e in es])
        o = None
        for i, (_, vv) in enumerate(key_sets):
            oi = _dotf(_bf(probs[0][i] - lam * probs[1][i]), vv)
            o = oi if o is None else o + oi
        out = jnp.where(lane // HEAD == h, o, out)
    ms = _mm2(out * out, _head_ones(W_BR)) * (1.0 / HEAD)
    o_ref[0] = out * lax.rsqrt(ms + NORM_EPS) * g_ref[...] * (1.0 - lam_init)


def _diff(pd, lam_p, subln_g, cache_k, cache_v, layer, lam_init):
    b, t, _ = pd.shape
    latent = cache_k is not None
    tq = 256
    in_specs = [pl.BlockSpec((4, DQ_D), lambda i, j: (0, 0)),
                pl.BlockSpec((1, W_BR), lambda i, j: (0, 0)),
                pl.BlockSpec((1, tq, 256), lambda i, j: (i, j, 0)),
                pl.BlockSpec((1, t, 256), lambda i, j: (i, 0, 1)),
                pl.BlockSpec((1, t, 256), lambda i, j: (i, 0, 2))]
    args = [lam_p, jnp.tile(subln_g, (1, W_BR // HEAD)), pd, pd, pd]
    if latent:
        in_specs += [pl.BlockSpec((1, 1) + cache_k.shape[2:], lambda i, j: (i, layer, 0, 0)),
                     pl.BlockSpec((1, 1) + cache_v.shape[2:], lambda i, j: (i, layer, 0, 0))]
        args += [cache_k, cache_v]
    return pl.pallas_call(
        functools.partial(_diff_kernel, latent=latent, lam_init=lam_init),
        grid=(b, t // tq),
        in_specs=in_specs,
        out_specs=pl.BlockSpec((1, tq, W_BR), lambda i, j: (i, j, 0)),
        out_shape=jax.ShapeDtypeStruct((b, t, W_BR), F32),
        compiler_params=_params(("parallel", "parallel")),
        name="diff",
    )(*args)


def _out_kernel(x_ref, mod_ref, gpost_ref, w_ref, pa_ref, yf_ref, yb_ref, gng_ref, gnb_ref, rk_ref,
                ywin_ref, bg_ref, ylru_ref, cg_ref, ydiff_ref, dg_ref, o_ref):
    ones = _head_ones(W_BR)
    pa = pa_ref[0]
    r, k, v, ag = pa[:, 0:256], pa[:, 256:512], pa[:, 512:768], pa[:, 896:1152]
    y = yf_ref[0] + yb_ref[0]
    mu = _mm2(y, ones) * (1.0 / HEAD)
    yc = y - mu
    var = _mm2(yc * yc, ones) * (1.0 / HEAD)
    ya = yc * lax.rsqrt(var + GN_EPS) * gng_ref[...] + gnb_ref[...]
    ya = ya + _mm2(r * k * rk_ref[...], ones) * v
    mix = jnp.concatenate([ya * _silu(ag), ywin_ref[0] * _silu(bg_ref[0]),
                           ylru_ref[0] * _silu(cg_ref[0]), ydiff_ref[0] * _silu(dg_ref[0])], axis=-1)
    o = jnp.dot(mix.astype(BF16), w_ref[0], preferred_element_type=F32)
    o = o * lax.rsqrt(jnp.mean(o * o, -1, keepdims=True) + NORM_EPS) * gpost_ref[...]
    o_ref[0] = x_ref[0] + mod_ref[0][:, 2 * D_MODEL:] * o


def _out(x, mod_l, g_post, w_out_bf, layer, latent, pa, yf, yb, gn_g, gn_b, r_k, ywin, pb, ylru, pc, ydiff, pd):
    b, t, _ = x.shape
    tm = ROW_TILE
    mod_idx = (lambda i, j: (1 + i, 0, 0)) if latent else (lambda i, j: (0, 0, 0))
    rows = lambda w, c=0: pl.BlockSpec((1, tm, w), lambda i, j: (i, j, c))
    vec = pl.BlockSpec((1, W_BR), lambda i, j: (0, 0))
    in_specs = [rows(D_MODEL),
                pl.BlockSpec((1, 1, 3 * D_MODEL), mod_idx),
                pl.BlockSpec((1, D_MODEL), lambda i, j: (0, 0)),
                pl.BlockSpec((1, D_MODEL, D_MODEL), lambda i, j: (layer, 0, 0)),
                rows(WA), rows(W_BR), rows(W_BR), vec, vec, vec,
                rows(W_BR), rows(W_BR, 2), rows(W_BR), rows(W_BR, 1), rows(W_BR), rows(W_BR, 3)]
    return pl.pallas_call(
        _out_kernel,
        grid=(b, t // tm),
        in_specs=in_specs,
        out_specs=rows(D_MODEL),
        out_shape=jax.ShapeDtypeStruct((b, t, D_MODEL), F32),
        compiler_params=_params(("parallel", "parallel")),
        name="out",
    )(x, mod_l, g_post, w_out_bf, pa, yf, yb, gn_g, gn_b, r_k, ywin, pb, ylru, pc, ydiff, pd)


def _layer(x, mod_l, layer, lam_init, wts, cache, tables):
    latent = cache is not None
    row = lambda a: a[layer].reshape(1, -1)
    pa, pb, pc, pd = _project(x, mod_l, row(wts['g_pre']), wts['w_in_bf'], layer, tables)
    rw = _rwkv(pa, wts['rwkv_w0'][layer], wts['rwkv_w_up'][layer], wts['rwkv_a0'][layer],
               wts['rwkv_a_up'][layer], row(wts['rwkv_k_k']), row(wts['rwkv_k_a']),
               cache['rwkv'] if latent else None, layer)
    lr = _lru(pc, wts['lru_conv_w'][layer], row(wts['lru_conv_b']), wts['lru_wa'][layer],
              wts['lru_ba'][layer], wts['lru_wx'][layer], wts['lru_bx'][layer], wts['lru_lambda'][layer],
              cache['lru'] if latent else None, layer)
    ywin = _win(pb, wts['win_sink'][layer], cache['win_k'] if latent else None,
                cache['win_v'] if latent else None, layer)
    ydiff = _diff(pd, wts['diff_lambda'][layer], row(wts['diff_subln_g']),
                  cache['diff_k'] if latent else None, cache['diff_v'] if latent else None, layer, lam_init)
    y = _out(x, mod_l, row(wts['g_post']), wts['w_out_bf'], layer, latent, pa, rw[0], rw[1],
             row(wts['rwkv_gn_g']), row(wts['rwkv_gn_b']), row(wts['rwkv_r_k']),
             ywin, pb, lr[0], pc, ydiff, pd)
    new_cache = None if latent else (pb, pd, rw[2], lr[1])
    return y, new_cache


def kernel(x_prompt, x_sample, c, cache_win_k, cache_win_v, cache_diff_k, cache_diff_v, state_rwkv, state_lru,
           c_ctx, w_mod, b_mod, g_pre, g_post, w_in, w_out,
           rwkv_w0, rwkv_w_up, rwkv_a0, rwkv_a_up, rwkv_k_k, rwkv_k_a, rwkv_r_k, rwkv_gn_g, rwkv_gn_b,
           win_sink, lru_conv_w, lru_conv_b, lru_wa, lru_ba, lru_wx, lru_bx, lru_lambda,
           diff_lambda, diff_subln_g):
    n_b, seq = x_prompt.shape[:2]
    n_dec, dec_seq = x_sample.shape[:2]
    past = cache_win_k.shape[2]
    wts = dict(g_pre=g_pre, g_post=g_post, w_in_bf=w_in.astype(BF16), w_out_bf=w_out.astype(BF16),
               rwkv_w0=rwkv_w0, rwkv_w_up=rwkv_w_up, rwkv_a0=rwkv_a0, rwkv_a_up=rwkv_a_up,
               rwkv_k_k=rwkv_k_k, rwkv_k_a=rwkv_k_a, rwkv_r_k=rwkv_r_k, rwkv_gn_g=rwkv_gn_g,
               rwkv_gn_b=rwkv_gn_b, win_sink=win_sink, lru_conv_w=lru_conv_w, lru_conv_b=lru_conv_b,
               lru_wa=lru_wa, lru_ba=lru_ba, lru_wx=lru_wx, lru_bx=lru_bx, lru_lambda=lru_lambda,
               diff_lambda=diff_lambda, diff_subln_g=diff_subln_g)
    cache = dict(win_k=cache_win_k.reshape(n_dec, DEPTH, past, 128),
                 win_v=cache_win_v.reshape(n_dec, DEPTH, past, 128),
                 diff_k=cache_diff_k.reshape(n_dec, DEPTH, past, 256),
                 diff_v=cache_diff_v.reshape(n_dec, DEPTH, past, 256),
                 rwkv=state_rwkv, lru=state_lru)
    cvec = jnp.concatenate([c_ctx[None], c, jnp.zeros((8 - 1 - n_dec, D_MODEL), F32)], axis=0)
    mod = _modulation(cvec, w_mod, b_mod)
    cos_b, sin_b = _rope_tables(dec_seq, HEAD, 384)
    cos_d, sin_d = _rope_tables(dec_seq, DQ_D, 512)
    tables = (cos_b, sin_b, cos_d, sin_d)

    y_p, y_s = x_prompt, x_sample
    ctx = []
    for l in range(DEPTH):
        lam_init = 0.8 - 0.6 * math.exp(-0.3 * l)
        mod_l = mod[l].reshape(8, 1, 3 * D_MODEL)
        y_p, nc = _layer(y_p, mod_l, l, lam_init, wts, None, None)
        ctx.append(nc)
        y_s, _ = _layer(y_s, mod_l, l, lam_init, wts, cache, tables)
    stack = lambda f: jnp.stack([f(ct) for ct in ctx], axis=1)
    new_win_k = stack(lambda ct: ct[0][:, :, 256:384].reshape(n_b, seq, 2, HEAD))
    new_win_v = stack(lambda ct: ct[0][:, :, 384:512].reshape(n_b, seq, 2, HEAD))
    new_diff_k = stack(lambda ct: ct[1][:, :, 256:512].reshape(n_b, seq, 4, 2, DQ_D))
    new_diff_v = stack(lambda ct: ct[1][:, :, 512:768].reshape(n_b, seq, 4, HEAD))
    new_state_rwkv = stack(lambda ct: ct[2])
    new_state_lru = stack(lambda ct: ct[3])
    return (y_p, y_s, new_win_k, new_win_v, new_diff_k, new_diff_v, new_state_rwkv, new_state_lru)
```

```python
import functools
import math

import numpy as np
import jax
import jax.numpy as jnp
from jax import lax
from jax.experimental import pallas as pl
from jax.experimental.pallas import tpu as pltpu

F32 = jnp.float32
BF16 = jnp.bfloat16
HI = lax.Precision.HIGHEST

D_MODEL = 1024
DEPTH = 2
GRID_W = 64
ROPE_BASE = 10000.0
NORM_EPS = 1e-6
NEG_INF = -1e30
LOG2_E = math.log2(math.e)
GN_EPS = 64e-5
C_RG = 8.0
W_BR = 256
HEAD = 64
SUBLANES = 8
LORA = 64
DQ_D = 32
WINDOW = 128
Q_BLK = 128
P_TOTAL = 3456
WA, WB, WC, WD = 1152, 768, 512, 1024
CHUNK = 64
RWKV_BATCH = 4
ROW_TILE = 256
VMEM_LIMIT = 48 * 1024 * 1024


_NN = (((1,), (0,)), ((), ()))
_NT = (((1,), (1,)), ((), ()))
_TN = (((0,), (0,)), ((), ()))


def _dot_hi(a, b):
    return jnp.dot(a, b, precision=HI, preferred_element_type=F32)


def _bf(x):
    return x.astype(BF16)


def _dotf(x, y, dims=_NN):
    return lax.dot_general(x, y, dims, preferred_element_type=F32)


def _split(x):
    hi = x.astype(BF16)
    return hi, (x - hi.astype(F32)).astype(BF16)


def _mm3(a, b, dims=_NN):
    return _dotf(a[0], b[0], dims) + (_dotf(a[0], b[1], dims) + _dotf(a[1], b[0], dims))


def _mm2(x, m01, left=False):
    hi, lo = _split(x)
    if left:
        return _dotf(m01, hi) + _dotf(m01, lo)
    return _dotf(hi, m01) + _dotf(lo, m01)


def _dot_bf(a, b):
    return jnp.dot(a.astype(BF16), b.astype(BF16), preferred_element_type=F32)


def _dot_nt_bf(a, b):
    return lax.dot_general(a.astype(BF16), b.astype(BF16), (((1,), (1,)), ((), ())),
                           preferred_element_type=F32)


def _sigmoid(x):
    return 1.0 / (1.0 + jnp.exp(-x))


def _silu(x):
    return x * _sigmoid(x)


def _softplus(x):
    return jnp.maximum(x, 0.0) + jnp.log1p(jnp.exp(-jnp.abs(x)))


def _head_ones(n):
    r = lax.broadcasted_iota(jnp.int32, (n, n), 0) // HEAD
    c = lax.broadcasted_iota(jnp.int32, (n, n), 1) // HEAD
    return jnp.where(r == c, 1.0, 0.0).astype(BF16)


def _params(sem):
    return pltpu.CompilerParams(dimension_semantics=sem, vmem_limit_bytes=VMEM_LIMIT)


def _mod_kernel(c_ref, w_ref, b_ref, o_ref):
    o_ref[0] = _dot_hi(_silu(c_ref[...]), w_ref[0]) + b_ref[0]


def _modulation(cvec, w_mod, b_mod):
    n_l = w_mod.shape[0]
    tn = 512
    return pl.pallas_call(
        _mod_kernel,
        grid=(n_l, 3 * D_MODEL // tn),
        in_specs=[pl.BlockSpec((8, D_MODEL), lambda l, j: (0, 0)),
                  pl.BlockSpec((1, D_MODEL, tn), lambda l, j: (l, 0, j)),
                  pl.BlockSpec((1, 1, tn), lambda l, j: (l, 0, j))],
        out_specs=pl.BlockSpec((1, 8, tn), lambda l, j: (l, 0, j)),
        out_shape=jax.ShapeDtypeStruct((n_l, 8, 3 * D_MODEL), F32),
        compiler_params=_params(("parallel", "parallel")),
        name="mod",
    )(cvec, w_mod, b_mod.reshape(n_l, 1, 3 * D_MODEL))


def _rope(x, cos, sin_signed, off):
    w = x.shape[-1]
    lane = lax.broadcasted_iota(jnp.int32, x.shape, 1)
    first = (lane % (2 * off)) < off
    partner = jnp.where(first, pltpu.roll(x, w - off, 1), pltpu.roll(x, off, 1))
    return x * cos + partner * sin_signed


def _proj_kernel(*refs, latent, layer):
    if latent:
        x_ref, mod_ref, g_ref, w_ref, cb_ref, sb_ref, cd_ref, sd_ref, oa, ob, oc, od = refs
    else:
        x_ref, mod_ref, g_ref, w_ref, oa, ob, oc, od, o_wk, o_wv, o_dk, o_dv = refs
    x = x_ref[0]
    y = x * lax.rsqrt(jnp.mean(x * x, -1, keepdims=True) + NORM_EPS) * g_ref[layer:layer + 1]
    m = mod_ref[0]
    h = y * (1.0 + m[:, D_MODEL:2 * D_MODEL]) + m[:, :D_MODEL]
    p = jnp.dot(h.astype(BF16), w_ref[0], preferred_element_type=F32)
    oa[0] = p[:, :WA]
    pb = p[:, WA:WA + WB]
    pd = p[:, WA + WB + WC:]
    if latent:
        ob[0, :, :384] = _rope(pb[:, :384], cb_ref[...], sb_ref[...], 16)
        ob[0, :, 384:] = pb[:, 384:]
        od[0, :, :512] = _rope(pd[:, :512], cd_ref[...], sd_ref[...], 8)
        od[0, :, 512:] = pd[:, 512:]
    else:
        ob[0] = pb
        od[0] = pd
        o_wk[0] = pb[:, 256:384].T
        o_wv[0] = pb[:, 384:512].T
        o_dk[0] = pd[:, 256:512].T
        o_dv[0] = pd[:, 512:768].T
    oc[0] = p[:, WA + WB:WA + WB + WC]


def _project(x, mod_l, g_pre, w_in_bf, layer, tables):
    b, t, _ = x.shape
    latent = tables is not None
    tm = ROW_TILE
    mod_idx = (lambda i, j: (1 + i, 0, 0)) if latent else (lambda i, j: (0, 0, 0))
    in_specs = [pl.BlockSpec((1, tm, D_MODEL), lambda i, j: (i, j, 0)),
                pl.BlockSpec((1, 1, 3 * D_MODEL), mod_idx),
                pl.BlockSpec(g_pre.shape, lambda i, j: (0, 0)),
                pl.BlockSpec((1, D_MODEL, P_TOTAL), lambda i, j: (layer, 0, 0))]
    args = [x, mod_l, g_pre, w_in_bf]
    if latent:
        for tab in tables:
            in_specs.append(pl.BlockSpec((tm, tab.shape[1]), lambda i, j: (j, 0)))
            args.append(tab)
    widths = (WA, WB, WC, WD)
    out_specs = [pl.BlockSpec((1, tm, w), lambda i, j: (i, j, 0)) for w in widths]
    out_shape = [jax.ShapeDtypeStruct((b, t, w), F32) for w in widths]
    if not latent:
        for w in (128, 128, W_BR, W_BR):
            out_specs.append(pl.BlockSpec((1, w, tm), lambda i, j: (i, 0, j)))
            out_shape.append(jax.ShapeDtypeStruct((b, w, t), F32))
    return pl.pallas_call(
        functools.partial(_proj_kernel, latent=latent, layer=layer),
        grid=(b, t // tm),
        in_specs=in_specs,
        out_specs=out_specs,
        out_shape=out_shape,
        compiler_params=_params(("parallel", "parallel")),
        name="proj",
    )(*args)


def _rope_tables(t, head_dim, n_lanes):
    half = head_dim // 2
    quarter = half // 2
    pos = np.arange(t)
    row = (pos // GRID_W).astype(np.float32)
    col = (pos % GRID_W).astype(np.float32)
    inv = np.float32(ROPE_BASE) ** (-np.arange(0, half, 2, dtype=np.float32) / np.float32(half))
    lane = np.arange(n_lanes) % head_dim
    in_half = lane % half
    p = np.where((lane < half)[None, :], row[:, None], col[:, None])
    ang = (p * inv[in_half % quarter][None, :]).astype(np.float64)
    sign = np.where(in_half < quarter, -1.0, 1.0)[None, :]
    return jnp.asarray(np.cos(ang), F32), jnp.asarray(np.sin(ang) * sign, F32)


_PREP_FIELDS = (('ar', 2, 1, BF16), ('nm', 1, 1, BF16), ('mm', 1, 1, BF16), ('pq', 1, 2, BF16),
                ('v_hi', 1, 1, BF16), ('v_lo', 1, 1, BF16), ('bkt_hi', 1, 2, BF16), ('bkt_lo', 1, 2, BF16),
                ('gct', 1, 1, F32))


def _prep_scratch(slots, n_streams):
    return [pltpu.VMEM((slots, n_streams, rows * CHUNK, lanes * W_BR), dt) for _, rows, lanes, dt in _PREP_FIELDS]


class _SlotView:
    def __init__(self, ref, slot):
        self.ref, self.slot, self.shape = ref, slot, ref.shape[1:]

    def __getitem__(self, idx):
        return self.ref[(self.slot,) + (idx if isinstance(idx, tuple) else (idx,))]

    def __setitem__(self, idx, val):
        self.ref[(self.slot,) + (idx if isinstance(idx, tuple) else (idx,))] = val


def _head_transpose(x):
    xt = x.T
    return jnp.concatenate([xt[h * HEAD:(h + 1) * HEAD] for h in range(x.shape[1] // HEAD)], axis=1)


def _expand(x, same):
    return jnp.concatenate([x] * (same.shape[0] // x.shape[0]), axis=0) * same


def _rwkv_prepare(xs, dirs, w0_ref, wup_ref, a0_ref, aup_ref, k_k, k_a, m_ref, same_ref, out):
    c = xs[0].shape[0]
    ones = _head_ones(W_BR)
    ti = lax.broadcasted_iota(jnp.int32, (c, c), 0)
    si = lax.broadcasted_iota(jnp.int32, (c, c), 1)
    cums = [jnp.where(si <= ti, 1.0, 0.0).astype(BF16), jnp.where(si >= ti, 1.0, 0.0).astype(BF16)]
    same = same_ref[...]

    st = []
    for xc, d in zip(xs, dirs):
        r, k, v = xc[:, 0:256], xc[:, 256:512], xc[:, 512:768]
        wd, ad = xc[:, 768:832], xc[:, 832:896]
        kk = k * k_k
        kk = kk * lax.rsqrt(_mm2(kk * kk, ones) + 1e-12)
        z = w0_ref[d:d + 1] + _mm3(_split(jnp.tanh(wd)), _split(wup_ref[d]))
        e = jnp.exp(-_softplus(-z) - 0.5)
        a = _sigmoid(a0_ref[d:d + 1] + _mm3(_split(ad), _split(aup_ref[d])))
        st.append(dict(r=r, v=v, e=e, kd=k * (1.0 + (a - 1.0) * k_a), alpha=-kk, beta=kk * a))
    yield
    for s, d in zip(st, dirs):
        l_incl = _mm2(s['e'], cums[d], left=True)
        l_tot = jnp.sum(s['e'], axis=0, keepdims=True)
        grow = jnp.exp(l_incl)
        tail = jnp.exp(l_incl - l_tot)
        s.update(ar=jnp.concatenate([_bf(s['alpha'] * jnp.exp(s['e'] - l_incl)),
                                     _bf(s['r'] * jnp.exp(-l_incl))], axis=0),
                 b_t=_bf(s['beta'] * grow), k_t=_bf(s['kd'] * grow),
                 b_h=s['beta'] * tail, k_h=s['kd'] * tail,
                 g_c=jnp.broadcast_to(jnp.exp(-l_tot), (c, W_BR)))
    yield
    for g, (s, d) in enumerate(zip(st, dirs)):
        out['ar'][g] = s['ar']
        rhs_t = jnp.concatenate([_expand(s['b_t'], same), _expand(s['k_t'], same)], axis=0)
        g4 = _dotf(s['ar'], rhs_t, _NT)
        strict = m_ref[_M_STRICT + 2 * d]
        incl = m_ref[_M_INCL + 2 * d]
        out['nm'][g] = _bf(g4[:c, :W_BR]) * strict
        out['mm'][g] = _bf(g4[:c, W_BR:]) * strict
        out['pq'][g, :, :W_BR] = _bf(g4[c:, :W_BR]) * incl
        out['pq'][g, :, W_BR:] = _bf(g4[c:, W_BR:]) * incl
    yield
    for g, s in enumerate(st):
        v_hi, v_lo = _split(s['v'])
        out['v_hi'][g] = v_hi
        out['v_lo'][g] = v_lo
        b_hi, b_lo = _split(_head_transpose(s['b_h']))
        k_hi, k_lo = _split(_head_transpose(s['k_h']))
        out['bkt_hi'][g] = jnp.concatenate([b_hi, k_hi], axis=1)
        out['bkt_lo'][g] = jnp.concatenate([b_lo, k_lo], axis=1)
        out['gct'][g] = _head_transpose(s['g_c'])


def _rwkv_advance(p, m_ref, same_ref, z_ref, y_stores):
    n_streams, c = p['nm'].shape[0], p['nm'].shape[1]
    gs = range(n_streams)
    same = same_ref[...]
    ex = lambda x: _expand(_bf(x), same)
    invs = [(p['nm'][g] * m_ref[_M_PAIR]).astype(F32) + m_ref[_M_EYE].astype(F32) for g in gs]
    for lvl in range(int(math.log2(c)) - 1):
        inv_x = [ex(t) for t in invs]
        half = [_dotf(p['nm'][g] * m_ref[_M_OFF + lvl], inv_x[g]) for g in gs]
        invs = [invs[g] + _dotf(_bf(invs[g]), ex(half[g])) for g in gs]
        yield
    z0 = [z_ref[g] for g in gs]
    as0 = [_dotf(p['ar'][g], ex(z0[g])) for g in gs]
    vx = [(_expand(p['v_hi'][g], same), _expand(p['v_lo'][g], same)) for g in gs]
    rhs = [as0[g][:c] + _dotf(p['mm'][g], vx[g][0]) for g in gs]
    yield
    u = [_split(_dotf(_bf(invs[g]), ex(rhs[g]))) for g in gs]
    uv = [(jnp.concatenate([_expand(u[g][0], same), vx[g][0]], axis=0),
           jnp.concatenate([_expand(u[g][1], same), vx[g][1]], axis=0)) for g in gs]
    yield
    for g in gs:
        z_ref[g] = z0[g] * p['gct'][g] + _mm3((p['bkt_hi'][g], p['bkt_lo'][g]), uv[g])
    yield
    for g in gs:
        y_stores[g](as0[g][c:] + _dotf(p['pq'][g], uv[g][0]))


def _interleave(*gens):
    live = list(gens)
    while live:
        for gen in list(live):
            try:
                next(gen)
            except StopIteration:
                live.remove(gen)


_M_STRICT, _M_INCL, _M_EYE, _M_PAIR, _M_OFF = 0, 1, 4, 5, 6


def _rwkv_masks(c, n_h):
    t = np.arange(c)[:, None]
    s = np.arange(c)[None, :]
    masks = [s < t, s <= t, s > t, s >= t, s == t, ((t // 2) == (s // 2)) & (t != s)]
    b = 2
    while b < c:
        masks.append(((t // (2 * b)) == (s // (2 * b))) & ((t // b) != (s // b)))
        b *= 2
    masks = np.tile(np.stack(masks).astype(np.float32), (1, 1, n_h))
    r = np.arange(n_h * c)[:, None] // c
    l = np.arange(n_h * HEAD)[None, :] // HEAD
    return jnp.asarray(masks, BF16), jnp.asarray((r == l).astype(np.float32), BF16)


def _rwkv_kernel(*refs, latent, n_b, layer):
    (xf_ref, xb_ref, xfn_ref, xbn_ref, w0_ref, wup_ref, a0_ref, aup_ref, kk_ref, ka_ref,
     m_ref, same_ref) = refs[:12]
    if latent:
        s0_ref, yf_ref, yb_ref = refs[12:15]
    else:
        yf_ref, yb_ref, st_ref = refs[12:15]
    scr = refs[15:]
    z_scr = scr[0]
    names = [f[0] for f in _PREP_FIELDS]
    ci = pl.program_id(1)
    view = lambda refs_, slot: {k: _SlotView(r, slot) for k, r in zip(names, refs_)}
    cur = view(scr[1:1 + len(names)], ci % 2)
    nxt = view(scr[1:1 + len(names)], 1 - ci % 2)
    mid = view(scr[1 + len(names):], 0)
    n_h = W_BR // HEAD
    c = CHUNK
    head = lambda h: slice(h * HEAD, (h + 1) * HEAD)
    streams = [(bi, d) for bi in range(n_b) for d in range(2)]
    dirs = [d for _, d in streams]
    params = (w0_ref[layer], wup_ref[0], a0_ref[layer], aup_ref[0], kk_ref[layer:layer + 1],
              ka_ref[layer:layer + 1], m_ref, same_ref)
    first = lambda d: slice(0, c) if d == 0 else slice(c, 2 * c)
    second = lambda d: slice(c, 2 * c) if d == 0 else slice(0, c)
    x_refs, xn_refs, y_refs = (xf_ref, xb_ref), (xfn_ref, xbn_ref), (yf_ref, yb_ref)

    def y_store(bi, d, rows):
        def store(y):
            y_refs[d][bi, rows, :] = y
        return store

    @pl.when(ci == 0)
    def _():
        for g, (bi, d) in enumerate(streams):
            if latent:
                z_scr[g] = jnp.concatenate([s0_ref[bi, 0, d, h].T for h in range(n_h)], axis=1)
            else:
                z_scr[g] = jnp.zeros(z_scr.shape[1:], F32)
        _interleave(_rwkv_prepare([x_refs[d][bi, first(d), :] for bi, d in streams], dirs, *params, cur))

    _interleave(_rwkv_advance(cur, m_ref, same_ref, z_scr, [y_store(bi, d, first(d)) for bi, d in streams]),
                _rwkv_prepare([x_refs[d][bi, second(d), :] for bi, d in streams], dirs, *params, mid))
    _interleave(_rwkv_advance(mid, m_ref, same_ref, z_scr, [y_store(bi, d, second(d)) for bi, d in streams]),
                _rwkv_prepare([xn_refs[d][bi, first(d), :] for bi, d in streams], dirs, *params, nxt))

    if not latent:
        @pl.when(ci == pl.num_programs(1) - 1)
        def _():
            for g, (bi, d) in enumerate(streams):
                z = z_scr[g]
                for h in range(n_h):
                    st_ref[bi, d, h] = z[:, head(h)].T


def _rwkv(pa, w0, wup, a0, aup, k_k, k_a, state, layer):
    b, t, _ = pa.shape
    latent = state is not None
    ns = t // (2 * CHUNK)
    n_h = W_BR // HEAD
    n_b = RWKV_BATCH
    masks, same = _rwkv_masks(CHUNK, n_h)
    full = lambda shape: pl.BlockSpec(shape, lambda i, j: (0,) * len(shape))
    one_layer = lambda shape: pl.BlockSpec((1,) + shape[1:], lambda i, j: (layer,) + (0,) * (len(shape) - 1))
    rows = lambda w, idx: pl.BlockSpec((n_b, 2 * CHUNK, w), idx)
    in_specs = [rows(WA, lambda i, j: (i, j, 0)),
                rows(WA, lambda i, j: (i, ns - 1 - j, 0)),
                rows(WA, lambda i, j: (i, jnp.minimum(j + 1, ns - 1), 0)),
                rows(WA, lambda i, j: (i, jnp.maximum(ns - 2 - j, 0), 0)),
                full(w0.shape), one_layer(wup.shape), full(a0.shape), one_layer(aup.shape),
                full(k_k.shape), full(k_a.shape), full(masks.shape), full(same.shape)]
    args = [pa, pa, pa, pa, w0, wup, a0, aup, k_k, k_a, masks, same]
    out_specs = [rows(W_BR, lambda i, j: (i, j, 0)), rows(W_BR, lambda i, j: (i, ns - 1 - j, 0))]
    out_shape = [jax.ShapeDtypeStruct((b, t, W_BR), F32)] * 2
    if latent:
        in_specs.append(pl.BlockSpec((n_b, 1, 2, n_h, HEAD, HEAD), lambda i, j: (i, layer, 0, 0, 0, 0)))
        args.append(state)
    else:
        out_specs.append(pl.BlockSpec((n_b, 2, n_h, HEAD, HEAD), lambda i, j: (i, 0, 0, 0, 0)))
        out_shape.append(jax.ShapeDtypeStruct((b, 2, n_h, HEAD, HEAD), F32))
    return pl.pallas_call(
        functools.partial(_rwkv_kernel, latent=latent, n_b=n_b, layer=layer),
        grid=(b // n_b, ns),
        in_specs=in_specs,
        out_specs=out_specs,
        out_shape=out_shape,
        scratch_shapes=([pltpu.VMEM((2 * n_b, CHUNK, W_BR), F32)]
                        + _prep_scratch(2, 2 * n_b) + _prep_scratch(1, 2 * n_b)),
        compiler_params=_params(("parallel", "arbitrary")),
        name="rwkv",
    )(*args)


def _lru_kernel(*refs, latent, layer):
    if latent:
        (x_ref, cw_ref, cb_ref, wa_ref, ba_ref, wx_ref, bx_ref, lam_ref, h0_ref, y_ref,
         a_scr, h_scr) = refs
    else:
        (x_ref, cw_ref, cb_ref, wa_ref, ba_ref, wx_ref, bx_ref, lam_ref, y_ref, st_ref,
         a_scr, h_scr) = refs
    x = x_ref[0, :, :W_BR]
    t = x.shape[0]
    row = lax.broadcasted_iota(jnp.int32, x.shape, 0)

    def shift_dn(z, k, fill):
        return jnp.where(row >= k, pltpu.roll(z, k, 0), fill)

    def shift_up(z, k, fill):
        return jnp.where(row < t - k, pltpu.roll(z, t - k, 0), fill)

    cw = cw_ref[layer]
    xc = (cb_ref[layer:layer + 1] + shift_dn(x, 2, 0.0) * cw[0:1] + shift_dn(x, 1, 0.0) * cw[1:2]
          + x * cw[2:3] + shift_up(x, 1, 0.0) * cw[3:4])
    xs = _split(xc)
    sub = row % SUBLANES
    n_blk = W_BR // HEAD
    zero = jnp.zeros((HEAD, HEAD), F32)

    def block_diag(w_ref, d):
        return jnp.concatenate([jnp.concatenate([w_ref[0, d, n] if m == n else zero for m in range(n_blk)], axis=1)
                                for n in range(n_blk)], axis=0)

    for d in range(2):
        gate_a = _sigmoid(_mm3(xs, _split(block_diag(wa_ref, d))) + ba_ref[layer, d:d + 1])
        gate_x = _sigmoid(_mm3(xs, _split(block_diag(wx_ref, d))) + bx_ref[layer, d:d + 1])
        log_a = -C_RG * gate_a * _softplus(-lam_ref[layer, d:d + 1])
        a = jnp.exp(log_a)
        u = jnp.sqrt(-jnp.tanh(log_a) * (a * a + 1.0)) * (gate_x * xc)
        k = 1
        while k < SUBLANES:
            if d == 0:
                keep = sub >= k
                sh = lambda z: pltpu.roll(z, k, 0)
            else:
                keep = sub < SUBLANES - k
                sh = lambda z: pltpu.roll(z, t - k, 0)
            u = a * jnp.where(keep, sh(u), 0.0) + u
            a = a * jnp.where(keep, sh(a), 1.0)
            k *= 2
        a_scr[d] = a
        h_scr[d] = u

    n_grp = t // SUBLANES
    if latent:
        carry0 = (h0_ref[0, 0, 0:1], h0_ref[0, 0, 1:2])
    else:
        carry0 = (jnp.zeros((1, W_BR), F32),) * 2

    def chain(i, carry):
        cf, cb = carry
        rf = pl.ds(pl.multiple_of(i * SUBLANES, SUBLANES), SUBLANES)
        rb = pl.ds(pl.multiple_of((n_grp - 1 - i) * SUBLANES, SUBLANES), SUBLANES)
        hf = h_scr[0, rf, :] + a_scr[0, rf, :] * cf
        hb = h_scr[1, rb, :] + a_scr[1, rb, :] * cb
        h_scr[0, rf, :] = hf
        h_scr[1, rb, :] = hb
        return hf[SUBLANES - 1:SUBLANES], hb[0:1]

    cf, cb = lax.fori_loop(0, n_grp, chain, carry0, unroll=4)
    y_ref[0] = h_scr[0] + h_scr[1]
    if not latent:
        st_ref[0] = jnp.concatenate([cf, cb], axis=0)


def _lru(pc, cw, cb, wa, ba, wx, bx, lam, state, layer):
    b, t, _ = pc.shape
    latent = state is not None
    full = lambda shape: pl.BlockSpec(shape, lambda i: (0,) * len(shape))
    one_layer = lambda shape: pl.BlockSpec((1,) + shape[1:], lambda i: (layer,) + (0,) * (len(shape) - 1))
    in_specs = [pl.BlockSpec((1, t, WC), lambda i: (i, 0, 0)),
                full(cw.shape), full(cb.shape), one_layer(wa.shape), full(ba.shape),
                one_layer(wx.shape), full(bx.shape), full(lam.shape)]
    args = [pc, cw, cb, wa, ba, wx, bx, lam]
    out_specs = [pl.BlockSpec((1, t, W_BR), lambda i: (i, 0, 0))]
    out_shape = [jax.ShapeDtypeStruct((b, t, W_BR), F32)]
    if latent:
        in_specs.append(pl.BlockSpec((1, 1, 2, W_BR), lambda i: (i, layer, 0, 0)))
        args.append(state)
    else:
        out_specs.append(pl.BlockSpec((1, 2, W_BR), lambda i: (i, 0, 0)))
        out_shape.append(jax.ShapeDtypeStruct((b, 2, W_BR), F32))
    return pl.pallas_call(
        functools.partial(_lru_kernel, latent=latent, layer=layer),
        grid=(b,),
        in_specs=in_specs,
        out_specs=out_specs,
        out_shape=out_shape,
        scratch_shapes=[pltpu.VMEM((2, t, W_BR), F32), pltpu.VMEM((2, t, W_BR), F32)],
        compiler_params=_params(("parallel",)),
        name="lru",
    )(*args)


def _win_kernel(*refs, latent, layer):
    if latent:
        sink_ref, q_ref, kp_ref, kc_ref, kn_ref, ck_ref, cv_ref, o_ref = refs
        j = pl.program_id(1)
        nq = pl.num_programs(1)
        qi = lax.broadcasted_iota(jnp.int32, (Q_BLK, Q_BLK), 0)
        ki = lax.broadcasted_iota(jnp.int32, (Q_BLK, Q_BLK), 1)
        q = q_ref[0]
        ctx_k, ctx_v = _bf(ck_ref[0, 0]), _bf(cv_ref[0, 0])
        pieces = [(_bf(r[0, :, :128]), _bf(r[0, :, 128:]), m, False) for r, m in
                  ((kp_ref, (ki >= qi) & (j > 0)), (kc_ref, None), (kn_ref, (ki <= qi) & (j < nq - 1)))]
        pieces += [(ctx_k[:, i:i + Q_BLK], ctx_v[:, i:i + Q_BLK], None, True)
                   for i in range(0, ctx_k.shape[1], Q_BLK)]
    else:
        sink_ref, x_ref, o_ref = refs
        q = x_ref[0, :, :256]
        k, v = _bf(x_ref[0, :, 256:384]), _bf(x_ref[0, :, 384:512])
        pieces = [(k[i:i + Q_BLK], v[i:i + Q_BLK], None, False) for i in range(0, k.shape[0], Q_BLK)]
    q = _bf(q * (HEAD ** -0.5 * LOG2_E))
    outs = []
    for h in range(2):
        hs = slice(h * HEAD, (h + 1) * HEAD)
        for g in range(2):
            qg = q[:, (2 * h + g) * HEAD:(2 * h + g + 1) * HEAD]
            sink = sink_ref[layer, 2 * h + g] * LOG2_E
            logits = []
            for kp, _, mask, fm in pieces:
                s = _dotf(qg, kp[hs]) if fm else _dotf(qg, kp[:, hs], _NT)
                logits.append(s if mask is None else jnp.where(mask, s, NEG_INF))
            mx = jnp.maximum(jnp.max(functools.reduce(jnp.maximum, logits), axis=-1, keepdims=True), sink)
            es = [jnp.exp2(s - mx) for s in logits]
            den = jnp.sum(functools.reduce(lambda a, b: a + b, es), axis=-1, keepdims=True) + jnp.exp2(sink - mx)
            acc = functools.reduce(lambda a, b: a + b,
                                   [_dotf(_bf(e), vp[hs], _NT) if fm else _dotf(_bf(e), vp[:, hs])
                                    for e, (_, vp, _, fm) in zip(es, pieces)])
            outs.append(acc / den)
    o_ref[0] = jnp.concatenate(outs, axis=-1)


def _win(pb, sink, cache_k, cache_v, layer):
    b, t, _ = pb.shape
    latent = cache_k is not None
    smem = pl.BlockSpec(memory_space=pltpu.SMEM)
    if latent:
        nq = t // Q_BLK
        grid = (b, nq)
        in_specs = [smem,
                    pl.BlockSpec((1, Q_BLK, 256), lambda i, j: (i, j, 0)),
                    pl.BlockSpec((1, Q_BLK, 256), lambda i, j: (i, jnp.maximum(j - 1, 0), 1)),
                    pl.BlockSpec((1, Q_BLK, 256), lambda i, j: (i, j, 1)),
                    pl.BlockSpec((1, Q_BLK, 256), lambda i, j: (i, jnp.minimum(j + 1, nq - 1), 1)),
                    pl.BlockSpec((1, 1) + cache_k.shape[2:], lambda i, j: (i, layer, 0, 0)),
                    pl.BlockSpec((1, 1) + cache_v.shape[2:], lambda i, j: (i, layer, 0, 0))]
        args = [sink, pb, pb, pb, pb, cache_k, cache_v]
        out_spec = pl.BlockSpec((1, Q_BLK, W_BR), lambda i, j: (i, j, 0))
        sem = ("parallel", "parallel")
    else:
        grid = (b,)
        in_specs = [smem, pl.BlockSpec((1, t, WB), lambda i: (i, 0, 0))]
        args = [sink, pb]
        out_spec = pl.BlockSpec((1, t, W_BR), lambda i: (i, 0, 0))
        sem = ("parallel",)
    return pl.pallas_call(
        functools.partial(_win_kernel, latent=latent, layer=layer),
        grid=grid,
        in_specs=in_specs,
        out_specs=out_spec,
        out_shape=jax.ShapeDtypeStruct((b, t, W_BR), F32),
        compiler_params=_params(sem),
        name="win",
    )(*args)


def _diff_kernel(*refs, latent, lam_init, layer):
    if latent:
        lam_ref, g_ref, q_ref, k_ref, v_ref, ck_ref, cv_ref, o_ref = refs
    else:
        lam_ref, g_ref, q_ref, k_ref, v_ref, o_ref = refs
    lp = lam_ref[layer]
    lam = (jnp.exp(jnp.sum(lp[0:1] * lp[1:2], axis=-1, keepdims=True))
           - jnp.exp(jnp.sum(lp[2:3] * lp[3:4], axis=-1, keepdims=True)) + lam_init)
    q = _bf(q_ref[0] * (DQ_D ** -0.5 * LOG2_E))
    key_sets = [(_bf(k_ref[0]), _bf(v_ref[0]), False)]
    if latent:
        key_sets.append((_bf(ck_ref[0, 0]), _bf(cv_ref[0, 0]), True))
    lane = lax.broadcasted_iota(jnp.int32, (1, W_BR), 1)
    out = jnp.zeros(q.shape, F32)
    for h in range(W_BR // HEAD):
        probs = []
        for m in range(2):
            qm = jnp.where(lane // DQ_D == 2 * h + m, q, jnp.zeros((), BF16))
            logits = [_dotf(qm, kk) if fm else _dotf(qm, kk, _NT) for kk, _, fm in key_sets]
            mx = functools.reduce(jnp.maximum, [jnp.max(s, axis=-1, keepdims=True) for s in logits])
            es = [jnp.exp2(s - mx) for s in logits]
            den = functools.reduce(lambda a, b: a + b, [jnp.sum(e, axis=-1, keepdims=True) for e in es])
            inv = 1.0 / den
            probs.append([e * inv for e in es])
        o = None
        for i, (_, vv, fm) in enumerate(key_sets):
            pi = _bf(probs[0][i] - lam * probs[1][i])
            oi = _dotf(pi, vv, _NT) if fm else _dotf(pi, vv)
            o = oi if o is None else o + oi
        out = jnp.where(lane // HEAD == h, o, out)
    ms = _mm2(out * out, _head_ones(W_BR)) * (1.0 / HEAD)
    gain = jnp.concatenate([g_ref[layer:layer + 1]] * (W_BR // HEAD), axis=1)
    o_ref[0] = out * lax.rsqrt(ms + NORM_EPS) * gain * (1.0 - lam_init)


def _diff(pd, lam_p, subln_g, cache_k, cache_v, layer, lam_init):
    b, t, _ = pd.shape
    latent = cache_k is not None
    tq = 256
    in_specs = [pl.BlockSpec(lam_p.shape, lambda i, j: (0, 0, 0)),
                pl.BlockSpec(subln_g.shape, lambda i, j: (0, 0)),
                pl.BlockSpec((1, tq, 256), lambda i, j: (i, j, 0)),
                pl.BlockSpec((1, t, 256), lambda i, j: (i, 0, 1)),
                pl.BlockSpec((1, t, 256), lambda i, j: (i, 0, 2))]
    args = [lam_p, subln_g, pd, pd, pd]
    if latent:
        in_specs += [pl.BlockSpec((1, 1) + cache_k.shape[2:], lambda i, j: (i, layer, 0, 0)),
                     pl.BlockSpec((1, 1) + cache_v.shape[2:], lambda i, j: (i, layer, 0, 0))]
        args += [cache_k, cache_v]
    return pl.pallas_call(
        functools.partial(_diff_kernel, latent=latent, lam_init=lam_init, layer=layer),
        grid=(b, t // tq),
        in_specs=in_specs,
        out_specs=pl.BlockSpec((1, tq, W_BR), lambda i, j: (i, j, 0)),
        out_shape=jax.ShapeDtypeStruct((b, t, W_BR), F32),
        compiler_params=_params(("parallel", "parallel")),
        name="diff",
    )(*args)


def _out_kernel(x_ref, mod_ref, gpost_ref, w_ref, pa_ref, yf_ref, yb_ref, gng_ref, gnb_ref, rk_ref,
                ywin_ref, bg_ref, ylru_ref, cg_ref, ydiff_ref, dg_ref, o_ref, *, layer):
    lrow = slice(layer, layer + 1)
    ones = _head_ones(W_BR)
    pa = pa_ref[0]
    r, k, v, ag = pa[:, 0:256], pa[:, 256:512], pa[:, 512:768], pa[:, 896:1152]
    y = yf_ref[0] + yb_ref[0]
    mu = _mm2(y, ones) * (1.0 / HEAD)
    yc = y - mu
    var = _mm2(yc * yc, ones) * (1.0 / HEAD)
    ya = yc * lax.rsqrt(var + GN_EPS) * gng_ref[lrow] + gnb_ref[lrow]
    ya = ya + _mm2(r * k * rk_ref[lrow], ones) * v
    mix = jnp.concatenate([ya * _silu(ag), ywin_ref[0] * _silu(bg_ref[0]),
                           ylru_ref[0] * _silu(cg_ref[0]), ydiff_ref[0] * _silu(dg_ref[0])], axis=-1)
    o = jnp.dot(mix.astype(BF16), w_ref[0], preferred_element_type=F32)
    o = o * lax.rsqrt(jnp.mean(o * o, -1, keepdims=True) + NORM_EPS) * gpost_ref[lrow]
    o_ref[0] = x_ref[0] + mod_ref[0][:, 2 * D_MODEL:] * o


def _out(x, mod_l, g_post, w_out_bf, layer, latent, pa, yf, yb, gn_g, gn_b, r_k, ywin, pb, ylru, pc, ydiff, pd):
    b, t, _ = x.shape
    tm = ROW_TILE
    mod_idx = (lambda i, j: (1 + i, 0, 0)) if latent else (lambda i, j: (0, 0, 0))
    rows = lambda w, c=0: pl.BlockSpec((1, tm, w), lambda i, j: (i, j, c))
    vec = pl.BlockSpec(gn_g.shape, lambda i, j: (0, 0))
    in_specs = [rows(D_MODEL),
                pl.BlockSpec((1, 1, 3 * D_MODEL), mod_idx),
                pl.BlockSpec(g_post.shape, lambda i, j: (0, 0)),
                pl.BlockSpec((1, D_MODEL, D_MODEL), lambda i, j: (layer, 0, 0)),
                rows(WA), rows(W_BR), rows(W_BR), vec, vec, vec,
                rows(W_BR), rows(W_BR, 2), rows(W_BR), rows(W_BR, 1), rows(W_BR), rows(W_BR, 3)]
    return pl.pallas_call(
        functools.partial(_out_kernel, layer=layer),
        grid=(b, t // tm),
        in_specs=in_specs,
        out_specs=rows(D_MODEL),
        out_shape=jax.ShapeDtypeStruct((b, t, D_MODEL), F32),
        compiler_params=_params(("parallel", "parallel")),
        name="out",
    )(x, mod_l, g_post, w_out_bf, pa, yf, yb, gn_g, gn_b, r_k, ywin, pb, ylru, pc, ydiff, pd)


def _layer(x, mod_l, layer, lam_init, wts, cache, tables):
    latent = cache is not None
    pa, pb, pc, pd, *ctx_t = _project(x, mod_l, wts['g_pre'], wts['w_in_bf'], layer, tables)
    rw = _rwkv(pa, wts['rwkv_w0'], wts['rwkv_w_up'], wts['rwkv_a0'], wts['rwkv_a_up'], wts['rwkv_k_k'],
               wts['rwkv_k_a'], cache['rwkv'] if latent else None, layer)
    lr = _lru(pc, wts['lru_conv_w'], wts['lru_conv_b'], wts['lru_wa'], wts['lru_ba'], wts['lru_wx'],
              wts['lru_bx'], wts['lru_lambda'], cache['lru'] if latent else None, layer)
    ywin = _win(pb, wts['win_sink'], cache['win_k'] if latent else None,
                cache['win_v'] if latent else None, layer)
    ydiff = _diff(pd, wts['diff_lambda'], wts['diff_subln_g'],
                  cache['diff_k'] if latent else None, cache['diff_v'] if latent else None, layer, lam_init)
    y = _out(x, mod_l, wts['g_post'], wts['w_out_bf'], layer, latent, pa, rw[0], rw[1],
             wts['rwkv_gn_g'], wts['rwkv_gn_b'], wts['rwkv_r_k'], ywin, pb, lr[0], pc, ydiff, pd)
    new_cache = None if latent else (ctx_t, rw[2], lr[1])
    return y, new_cache


def kernel(x_prompt, x_sample, c, cache_win_k, cache_win_v, cache_diff_k, cache_diff_v, state_rwkv, state_lru,
           c_ctx, w_mod, b_mod, g_pre, g_post, w_in, w_out,
           rwkv_w0, rwkv_w_up, rwkv_a0, rwkv_a_up, rwkv_k_k, rwkv_k_a, rwkv_r_k, rwkv_gn_g, rwkv_gn_b,
           win_sink, lru_conv_w, lru_conv_b, lru_wa, lru_ba, lru_wx, lru_bx, lru_lambda,
           diff_lambda, diff_subln_g):
    n_b, seq = x_prompt.shape[:2]
    n_dec, dec_seq = x_sample.shape[:2]
    past = cache_win_k.shape[2]
    wts = dict(g_pre=g_pre, g_post=g_post, w_in_bf=w_in.astype(BF16), w_out_bf=w_out.astype(BF16),
               rwkv_w0=rwkv_w0, rwkv_w_up=rwkv_w_up, rwkv_a0=rwkv_a0, rwkv_a_up=rwkv_a_up,
               rwkv_k_k=rwkv_k_k, rwkv_k_a=rwkv_k_a, rwkv_r_k=rwkv_r_k.reshape(DEPTH, W_BR), rwkv_gn_g=rwkv_gn_g,
               rwkv_gn_b=rwkv_gn_b, win_sink=win_sink, lru_conv_w=lru_conv_w, lru_conv_b=lru_conv_b,
               lru_wa=lru_wa, lru_ba=lru_ba, lru_wx=lru_wx, lru_bx=lru_bx, lru_lambda=lru_lambda,
               diff_lambda=diff_lambda, diff_subln_g=diff_subln_g)
    fm = lambda a: jnp.moveaxis(a, 2, -1).reshape(n_dec, DEPTH, -1, past)
    cache = dict(win_k=fm(cache_win_k), win_v=fm(cache_win_v), diff_k=fm(cache_diff_k), diff_v=fm(cache_diff_v),
                 rwkv=state_rwkv, lru=state_lru)
    cvec = jnp.concatenate([c_ctx[None], c, jnp.zeros((8 - 1 - n_dec, D_MODEL), F32)], axis=0)
    mod = _modulation(cvec, w_mod, b_mod)
    cos_b, sin_b = _rope_tables(dec_seq, HEAD, 384)
    cos_d, sin_d = _rope_tables(dec_seq, DQ_D, 512)
    tables = (cos_b, sin_b, cos_d, sin_d)

    y_p, y_s = x_prompt, x_sample
    ctx = []
    for l in range(DEPTH):
        lam_init = 0.8 - 0.6 * math.exp(-0.3 * l)
        mod_l = mod[l].reshape(8, 1, 3 * D_MODEL)
        y_p, nc = _layer(y_p, mod_l, l, lam_init, wts, None, None)
        ctx.append(nc)
        y_s, _ = _layer(y_s, mod_l, l, lam_init, wts, cache, tables)
    stack = lambda f: jnp.stack([f(ct) for ct in ctx], axis=1)
    tm_ = lambda i, dims: jnp.moveaxis(stack(lambda ct: ct[0][i]).reshape((n_b, DEPTH) + dims + (seq,)), -1, 2)
    new_win_k = tm_(0, (2, HEAD))
    new_win_v = tm_(1, (2, HEAD))
    new_diff_k = tm_(2, (4, 2, DQ_D))
    new_diff_v = tm_(3, (4, HEAD))
    new_state_rwkv = stack(lambda ct: ct[1])
    new_state_lru = stack(lambda ct: ct[2])
    return (y_p, y_s, new_win_k, new_win_v, new_diff_k, new_diff_v, new_state_rwkv, new_state_lru)
```

```python
import functools
import math

import numpy as np
import jax
import jax.numpy as jnp
from jax import lax
from jax.experimental import pallas as pl
from jax.experimental.pallas import tpu as pltpu

F32 = jnp.float32
BF16 = jnp.bfloat16
HI = lax.Precision.HIGHEST

D_MODEL = 1024
DEPTH = 2
GRID_W = 64
ROPE_BASE = 10000.0
NORM_EPS = 1e-6
NEG_INF = -1e30
LOG2_E = math.log2(math.e)
GN_EPS = 64e-5
C_RG = 8.0
W_BR = 256
HEAD = 64
SUBLANES = 8
LORA = 64
DQ_D = 32
WINDOW = 128
Q_BLK = 128
P_TOTAL = 3456
WA, WB, WC, WD = 1152, 768, 512, 1024
CHUNK = 64
RWKV_BATCH = 4
ROW_TILE = 256
VMEM_LIMIT = 48 * 1024 * 1024


_NN = (((1,), (0,)), ((), ()))
_NT = (((1,), (1,)), ((), ()))
_TN = (((0,), (0,)), ((), ()))


def _dot_hi(a, b):
    return jnp.dot(a, b, precision=HI, preferred_element_type=F32)


def _bf(x):
    return x.astype(BF16)


def _dotf(x, y, dims=_NN):
    return lax.dot_general(x, y, dims, preferred_element_type=F32)


def _split(x):
    hi = x.astype(BF16)
    return hi, (x - hi.astype(F32)).astype(BF16)


def _mm3(a, b, dims=_NN):
    return _dotf(a[0], b[0], dims) + (_dotf(a[0], b[1], dims) + _dotf(a[1], b[0], dims))


def _mm2(x, m01, left=False):
    hi, lo = _split(x)
    if left:
        return _dotf(m01, hi) + _dotf(m01, lo)
    return _dotf(hi, m01) + _dotf(lo, m01)


def _dot_bf(a, b):
    return jnp.dot(a.astype(BF16), b.astype(BF16), preferred_element_type=F32)


def _dot_nt_bf(a, b):
    return lax.dot_general(a.astype(BF16), b.astype(BF16), (((1,), (1,)), ((), ())),
                           preferred_element_type=F32)


def _sigmoid(x):
    return 1.0 / (1.0 + jnp.exp(-x))


def _silu(x):
    return x * _sigmoid(x)


def _softplus(x):
    return jnp.maximum(x, 0.0) + jnp.log1p(jnp.exp(-jnp.abs(x)))


def _head_ones(n):
    r = lax.broadcasted_iota(jnp.int32, (n, n), 0) // HEAD
    c = lax.broadcasted_iota(jnp.int32, (n, n), 1) // HEAD
    return jnp.where(r == c, 1.0, 0.0).astype(BF16)


def _params(sem):
    return pltpu.CompilerParams(dimension_semantics=sem, vmem_limit_bytes=VMEM_LIMIT)


def _mod_kernel(c_ref, w_ref, b_ref, o_ref):
    o_ref[0] = _mm3(_split(_silu(c_ref[...])), _split(w_ref[0])) + b_ref[0]


def _modulation(cvec, w_mod, b_mod):
    n_l = w_mod.shape[0]
    tn = 512
    return pl.pallas_call(
        _mod_kernel,
        grid=(n_l, 3 * D_MODEL // tn),
        in_specs=[pl.BlockSpec((8, D_MODEL), lambda l, j: (0, 0)),
                  pl.BlockSpec((1, D_MODEL, tn), lambda l, j: (l, 0, j)),
                  pl.BlockSpec((1, 1, tn), lambda l, j: (l, 0, j))],
        out_specs=pl.BlockSpec((1, 8, tn), lambda l, j: (l, 0, j)),
        out_shape=jax.ShapeDtypeStruct((n_l, 8, 3 * D_MODEL), F32),
        compiler_params=_params(("parallel", "parallel")),
        name="mod",
    )(cvec, w_mod, b_mod.reshape(n_l, 1, 3 * D_MODEL))


def _rope(x, cos, sin_signed, off):
    w = x.shape[-1]
    lane = lax.broadcasted_iota(jnp.int32, x.shape, 1)
    first = (lane % (2 * off)) < off
    partner = jnp.where(first, pltpu.roll(x, w - off, 1), pltpu.roll(x, off, 1))
    return x * cos + partner * sin_signed


def _proj_kernel(*refs, latent, layer):
    if latent:
        x_ref, mod_ref, g_ref, w_ref, cb_ref, sb_ref, cd_ref, sd_ref, oa, ob, oc, od = refs
    else:
        x_ref, mod_ref, g_ref, w_ref, oa, ob, oc, od, o_wk, o_wv, o_dk, o_dv = refs
    x = x_ref[0]
    y = x * lax.rsqrt(jnp.mean(x * x, -1, keepdims=True) + NORM_EPS) * g_ref[layer:layer + 1]
    m = mod_ref[0]
    h = y * (1.0 + m[:, D_MODEL:2 * D_MODEL]) + m[:, :D_MODEL]
    p = jnp.dot(h.astype(BF16), w_ref[0], preferred_element_type=F32)
    oa[0] = p[:, :WA]
    pb = p[:, WA:WA + WB]
    pd = p[:, WA + WB + WC:]
    if latent:
        ob[0, :, :384] = _rope(pb[:, :384], cb_ref[...], sb_ref[...], 16)
        ob[0, :, 384:] = pb[:, 384:]
        od[0, :, :512] = _rope(pd[:, :512], cd_ref[...], sd_ref[...], 8)
        od[0, :, 512:] = pd[:, 512:]
    else:
        ob[0] = pb
        od[0] = pd
        o_wk[0] = pb[:, 256:384].T
        o_wv[0] = pb[:, 384:512].T
        o_dk[0] = pd[:, 256:512].T
        o_dv[0] = pd[:, 512:768].T
    oc[0] = p[:, WA + WB:WA + WB + WC]


def _project(x, mod_l, g_pre, w_in_bf, layer, tables):
    b, t, _ = x.shape
    latent = tables is not None
    tm = ROW_TILE
    mod_idx = (lambda i, j: (1 + i, 0, 0)) if latent else (lambda i, j: (0, 0, 0))
    in_specs = [pl.BlockSpec((1, tm, D_MODEL), lambda i, j: (i, j, 0)),
                pl.BlockSpec((1, 1, 3 * D_MODEL), mod_idx),
                pl.BlockSpec(g_pre.shape, lambda i, j: (0, 0)),
                pl.BlockSpec((1, D_MODEL, P_TOTAL), lambda i, j: (layer, 0, 0))]
    args = [x, mod_l, g_pre, w_in_bf]
    if latent:
        for tab in tables:
            in_specs.append(pl.BlockSpec((tm, tab.shape[1]), lambda i, j: (j, 0)))
            args.append(tab)
    widths = (WA, WB, WC, WD)
    out_specs = [pl.BlockSpec((1, tm, w), lambda i, j: (i, j, 0)) for w in widths]
    out_shape = [jax.ShapeDtypeStruct((b, t, w), F32) for w in widths]
    if not latent:
        for w in (128, 128, W_BR, W_BR):
            out_specs.append(pl.BlockSpec((1, w, tm), lambda i, j: (i, 0, j)))
            out_shape.append(jax.ShapeDtypeStruct((b, w, t), F32))
    return pl.pallas_call(
        functools.partial(_proj_kernel, latent=latent, layer=layer),
        grid=(b, t // tm),
        in_specs=in_specs,
        out_specs=out_specs,
        out_shape=out_shape,
        compiler_params=_params(("parallel", "parallel")),
        name="proj",
    )(*args)


def _rope_tables(t, head_dim, n_lanes):
    half = head_dim // 2
    quarter = half // 2
    pos = np.arange(t)
    row = (pos // GRID_W).astype(np.float32)
    col = (pos % GRID_W).astype(np.float32)
    inv = np.float32(ROPE_BASE) ** (-np.arange(0, half, 2, dtype=np.float32) / np.float32(half))
    lane = np.arange(n_lanes) % head_dim
    in_half = lane % half
    p = np.where((lane < half)[None, :], row[:, None], col[:, None])
    ang = (p * inv[in_half % quarter][None, :]).astype(np.float64)
    sign = np.where(in_half < quarter, -1.0, 1.0)[None, :]
    return jnp.asarray(np.cos(ang), F32), jnp.asarray(np.sin(ang) * sign, F32)


_PREP_FIELDS = (('ar', 2, 1, BF16), ('nm', 1, 1, BF16), ('mm', 1, 1, BF16), ('pq', 1, 2, BF16),
                ('v', 1, 1, BF16), ('bkt', 1, 2, BF16), ('gct', 1, 1, F32))


def _prep_scratch(slots, n_streams):
    return [pltpu.VMEM((slots, n_streams, rows * CHUNK, lanes * W_BR), dt) for _, rows, lanes, dt in _PREP_FIELDS]


class _SlotView:
    def __init__(self, ref, slot):
        self.ref, self.slot, self.shape = ref, slot, ref.shape[1:]

    def __getitem__(self, idx):
        return self.ref[(self.slot,) + (idx if isinstance(idx, tuple) else (idx,))]

    def __setitem__(self, idx, val):
        self.ref[(self.slot,) + (idx if isinstance(idx, tuple) else (idx,))] = val


def _head_transpose(x):
    xt = x.T
    return jnp.concatenate([xt[h * HEAD:(h + 1) * HEAD] for h in range(x.shape[1] // HEAD)], axis=1)


def _expand(x, same):
    return jnp.concatenate([x] * (same.shape[0] // x.shape[0]), axis=0) * same


def _rwkv_prepare(xs, dirs, w0_ref, wup_ref, a0_ref, aup_ref, k_k, k_a, m_ref, same_ref, out):
    c = xs[0].shape[0]
    ones = _head_ones(W_BR)
    ti = lax.broadcasted_iota(jnp.int32, (c, c), 0)
    si = lax.broadcasted_iota(jnp.int32, (c, c), 1)
    cums = [jnp.where(si <= ti, 1.0, 0.0).astype(BF16), jnp.where(si >= ti, 1.0, 0.0).astype(BF16)]
    same = same_ref[...]

    st = []
    for xc, d in zip(xs, dirs):
        r, k, v = xc[:, 0:256], xc[:, 256:512], xc[:, 512:768]
        wd, ad = xc[:, 768:832], xc[:, 832:896]
        kk = k * k_k
        kk = kk * lax.rsqrt(_mm2(kk * kk, ones) + 1e-12)
        z = w0_ref[d:d + 1] + _mm3(_split(jnp.tanh(wd)), _split(wup_ref[d]))
        e = jnp.exp(-_softplus(-z) - 0.5)
        a = _sigmoid(a0_ref[d:d + 1] + _mm3(_split(ad), _split(aup_ref[d])))
        st.append(dict(r=r, v=v, e=e, kd=k * (1.0 + (a - 1.0) * k_a), alpha=-kk, beta=kk * a))
    yield
    for s, d in zip(st, dirs):
        l_incl = _mm2(s['e'], cums[d], left=True)
        l_tot = jnp.sum(s['e'], axis=0, keepdims=True)
        grow = jnp.exp(l_incl)
        tail = jnp.exp(l_incl - l_tot)
        s.update(ar=jnp.concatenate([_bf(s['alpha'] * jnp.exp(s['e'] - l_incl)),
                                     _bf(s['r'] * jnp.exp(-l_incl))], axis=0),
                 b_t=_bf(s['beta'] * grow), k_t=_bf(s['kd'] * grow),
                 b_h=s['beta'] * tail, k_h=s['kd'] * tail,
                 g_c=jnp.broadcast_to(jnp.exp(-l_tot), (c, W_BR)))
    yield
    for g, (s, d) in enumerate(zip(st, dirs)):
        out['ar'][g] = s['ar']
        rhs_t = jnp.concatenate([_expand(s['b_t'], same), _expand(s['k_t'], same)], axis=0)
        g4 = _dotf(s['ar'], rhs_t, _NT)
        strict = m_ref[_M_STRICT + 2 * d]
        incl = m_ref[_M_INCL + 2 * d]
        out['nm'][g] = _bf(g4[:c, :W_BR]) * strict
        out['mm'][g] = _bf(g4[:c, W_BR:]) * strict
        out['pq'][g, :, :W_BR] = _bf(g4[c:, :W_BR]) * incl
        out['pq'][g, :, W_BR:] = _bf(g4[c:, W_BR:]) * incl
    yield
    for g, s in enumerate(st):
        out['v'][g] = _bf(s['v'])
        out['bkt'][g] = jnp.concatenate([_bf(_head_transpose(s['b_h'])), _bf(_head_transpose(s['k_h']))], axis=1)
        out['gct'][g] = _head_transpose(s['g_c'])


def _rwkv_advance(p, m_ref, same_ref, z_ref, y_stores):
    n_streams, c = p['nm'].shape[0], p['nm'].shape[1]
    gs = range(n_streams)
    same = same_ref[...]
    ex = lambda x: _expand(_bf(x), same)
    invs = [(p['nm'][g] * m_ref[_M_PAIR]).astype(F32) + m_ref[_M_EYE].astype(F32) for g in gs]
    for lvl in range(int(math.log2(c)) - 1):
        inv_x = [ex(t) for t in invs]
        half = [_dotf(p['nm'][g] * m_ref[_M_OFF + lvl], inv_x[g]) for g in gs]
        invs = [invs[g] + _dotf(_bf(invs[g]), ex(half[g])) for g in gs]
        yield
    z0 = [z_ref[g] for g in gs]
    as0 = [_dotf(p['ar'][g], ex(z0[g])) for g in gs]
    vx = [_expand(p['v'][g], same) for g in gs]
    rhs = [as0[g][:c] + _dotf(p['mm'][g], vx[g]) for g in gs]
    yield
    uv = [jnp.concatenate([ex(_dotf(_bf(invs[g]), ex(rhs[g]))), vx[g]], axis=0) for g in gs]
    yield
    for g in gs:
        z_ref[g] = z0[g] * p['gct'][g] + _dotf(p['bkt'][g], uv[g])
    yield
    for g in gs:
        y_stores[g](as0[g][c:] + _dotf(p['pq'][g], uv[g]))


def _interleave(*gens):
    live = list(gens)
    while live:
        for gen in list(live):
            try:
                next(gen)
            except StopIteration:
                live.remove(gen)


_M_STRICT, _M_INCL, _M_EYE, _M_PAIR, _M_OFF = 0, 1, 4, 5, 6


def _rwkv_masks(c, n_h):
    t = np.arange(c)[:, None]
    s = np.arange(c)[None, :]
    masks = [s < t, s <= t, s > t, s >= t, s == t, ((t // 2) == (s // 2)) & (t != s)]
    b = 2
    while b < c:
        masks.append(((t // (2 * b)) == (s // (2 * b))) & ((t // b) != (s // b)))
        b *= 2
    masks = np.tile(np.stack(masks).astype(np.float32), (1, 1, n_h))
    r = np.arange(n_h * c)[:, None] // c
    l = np.arange(n_h * HEAD)[None, :] // HEAD
    return jnp.asarray(masks, BF16), jnp.asarray((r == l).astype(np.float32), BF16)


def _rwkv_kernel(*refs, latent, n_b, layer):
    (xf_ref, xb_ref, xfn_ref, xbn_ref, w0_ref, wup_ref, a0_ref, aup_ref, kk_ref, ka_ref,
     m_ref, same_ref) = refs[:12]
    if latent:
        s0_ref, yf_ref, yb_ref = refs[12:15]
    else:
        yf_ref, yb_ref, st_ref = refs[12:15]
    scr = refs[15:]
    z_scr = scr[0]
    names = [f[0] for f in _PREP_FIELDS]
    ci = pl.program_id(1)
    view = lambda refs_, slot: {k: _SlotView(r, slot) for k, r in zip(names, refs_)}
    cur = view(scr[1:1 + len(names)], ci % 2)
    nxt = view(scr[1:1 + len(names)], 1 - ci % 2)
    mid = view(scr[1 + len(names):], 0)
    n_h = W_BR // HEAD
    c = CHUNK
    head = lambda h: slice(h * HEAD, (h + 1) * HEAD)
    streams = [(bi, d) for bi in range(n_b) for d in range(2)]
    dirs = [d for _, d in streams]
    params = (w0_ref[layer], wup_ref[0], a0_ref[layer], aup_ref[0], kk_ref[layer:layer + 1],
              ka_ref[layer:layer + 1], m_ref, same_ref)
    first = lambda d: slice(0, c) if d == 0 else slice(c, 2 * c)
    second = lambda d: slice(c, 2 * c) if d == 0 else slice(0, c)
    x_refs, xn_refs, y_refs = (xf_ref, xb_ref), (xfn_ref, xbn_ref), (yf_ref, yb_ref)

    def y_store(bi, d, rows):
        def store(y):
            y_refs[d][bi, rows, :] = y
        return store

    @pl.when(ci == 0)
    def _():
        for g, (bi, d) in enumerate(streams):
            if latent:
                z_scr[g] = jnp.concatenate([s0_ref[bi, 0, d, h].T for h in range(n_h)], axis=1)
            else:
                z_scr[g] = jnp.zeros(z_scr.shape[1:], F32)
        _interleave(_rwkv_prepare([x_refs[d][bi, first(d), :] for bi, d in streams], dirs, *params, cur))

    _interleave(_rwkv_advance(cur, m_ref, same_ref, z_scr, [y_store(bi, d, first(d)) for bi, d in streams]),
                _rwkv_prepare([x_refs[d][bi, second(d), :] for bi, d in streams], dirs, *params, mid))
    _interleave(_rwkv_advance(mid, m_ref, same_ref, z_scr, [y_store(bi, d, second(d)) for bi, d in streams]),
                _rwkv_prepare([xn_refs[d][bi, first(d), :] for bi, d in streams], dirs, *params, nxt))

    if not latent:
        @pl.when(ci == pl.num_programs(1) - 1)
        def _():
            for g, (bi, d) in enumerate(streams):
                z = z_scr[g]
                for h in range(n_h):
                    st_ref[bi, d, h] = z[:, head(h)].T


def _rwkv(pa, w0, wup, a0, aup, k_k, k_a, state, layer):
    b, t, _ = pa.shape
    latent = state is not None
    ns = t // (2 * CHUNK)
    n_h = W_BR // HEAD
    n_b = RWKV_BATCH
    masks, same = _rwkv_masks(CHUNK, n_h)
    full = lambda shape: pl.BlockSpec(shape, lambda i, j: (0,) * len(shape))
    one_layer = lambda shape: pl.BlockSpec((1,) + shape[1:], lambda i, j: (layer,) + (0,) * (len(shape) - 1))
    rows = lambda w, idx: pl.BlockSpec((n_b, 2 * CHUNK, w), idx)
    in_specs = [rows(WA, lambda i, j: (i, j, 0)),
                rows(WA, lambda i, j: (i, ns - 1 - j, 0)),
                rows(WA, lambda i, j: (i, jnp.minimum(j + 1, ns - 1), 0)),
                rows(WA, lambda i, j: (i, jnp.maximum(ns - 2 - j, 0), 0)),
                full(w0.shape), one_layer(wup.shape), full(a0.shape), one_layer(aup.shape),
                full(k_k.shape), full(k_a.shape), full(masks.shape), full(same.shape)]
    args = [pa, pa, pa, pa, w0, wup, a0, aup, k_k, k_a, masks, same]
    out_specs = [rows(W_BR, lambda i, j: (i, j, 0)), rows(W_BR, lambda i, j: (i, ns - 1 - j, 0))]
    out_shape = [jax.ShapeDtypeStruct((b, t, W_BR), F32)] * 2
    if latent:
        in_specs.append(pl.BlockSpec((n_b, 1, 2, n_h, HEAD, HEAD), lambda i, j: (i, layer, 0, 0, 0, 0)))
        args.append(state)
    else:
        out_specs.append(pl.BlockSpec((n_b, 2, n_h, HEAD, HEAD), lambda i, j: (i, 0, 0, 0, 0)))
        out_shape.append(jax.ShapeDtypeStruct((b, 2, n_h, HEAD, HEAD), F32))
    return pl.pallas_call(
        functools.partial(_rwkv_kernel, latent=latent, n_b=n_b, layer=layer),
        grid=(b // n_b, ns),
        in_specs=in_specs,
        out_specs=out_specs,
        out_shape=out_shape,
        scratch_shapes=([pltpu.VMEM((2 * n_b, CHUNK, W_BR), F32)]
                        + _prep_scratch(2, 2 * n_b) + _prep_scratch(1, 2 * n_b)),
        compiler_params=_params(("parallel", "arbitrary")),
        name="rwkv",
    )(*args)


def _lru_kernel(*refs, latent, layer):
    if latent:
        (x_ref, cw_ref, cb_ref, wa_ref, ba_ref, wx_ref, bx_ref, lam_ref, h0_ref, y_ref,
         a_scr, h_scr) = refs
    else:
        (x_ref, cw_ref, cb_ref, wa_ref, ba_ref, wx_ref, bx_ref, lam_ref, y_ref, st_ref,
         a_scr, h_scr) = refs
    x = x_ref[0, :, :W_BR]
    t = x.shape[0]
    row = lax.broadcasted_iota(jnp.int32, x.shape, 0)

    def shift_dn(z, k, fill):
        return jnp.where(row >= k, pltpu.roll(z, k, 0), fill)

    def shift_up(z, k, fill):
        return jnp.where(row < t - k, pltpu.roll(z, t - k, 0), fill)

    cw = cw_ref[layer]
    xc = (cb_ref[layer:layer + 1] + shift_dn(x, 2, 0.0) * cw[0:1] + shift_dn(x, 1, 0.0) * cw[1:2]
          + x * cw[2:3] + shift_up(x, 1, 0.0) * cw[3:4])
    xs = _split(xc)
    sub = row % SUBLANES
    n_blk = W_BR // HEAD
    zero = jnp.zeros((HEAD, HEAD), F32)

    def block_diag(w_ref, d):
        return jnp.concatenate([jnp.concatenate([w_ref[0, d, n] if m == n else zero for m in range(n_blk)], axis=1)
                                for n in range(n_blk)], axis=0)

    for d in range(2):
        gate_a = _sigmoid(_mm3(xs, _split(block_diag(wa_ref, d))) + ba_ref[layer, d:d + 1])
        gate_x = _sigmoid(_mm3(xs, _split(block_diag(wx_ref, d))) + bx_ref[layer, d:d + 1])
        log_a = -C_RG * gate_a * _softplus(-lam_ref[layer, d:d + 1])
        a = jnp.exp(log_a)
        u = jnp.sqrt(-jnp.tanh(log_a) * (a * a + 1.0)) * (gate_x * xc)
        k = 1
        while k < SUBLANES:
            if d == 0:
                keep = sub >= k
                sh = lambda z: pltpu.roll(z, k, 0)
            else:
                keep = sub < SUBLANES - k
                sh = lambda z: pltpu.roll(z, t - k, 0)
            u = a * jnp.where(keep, sh(u), 0.0) + u
            a = a * jnp.where(keep, sh(a), 1.0)
            k *= 2
        a_scr[d] = a
        h_scr[d] = u

    n_grp = t // SUBLANES
    if latent:
        carry0 = (h0_ref[0, 0, 0:1], h0_ref[0, 0, 1:2])
    else:
        carry0 = (jnp.zeros((1, W_BR), F32),) * 2

    def chain(i, carry):
        cf, cb = carry
        rf = pl.ds(pl.multiple_of(i * SUBLANES, SUBLANES), SUBLANES)
        rb = pl.ds(pl.multiple_of((n_grp - 1 - i) * SUBLANES, SUBLANES), SUBLANES)
        hf = h_scr[0, rf, :] + a_scr[0, rf, :] * cf
        hb = h_scr[1, rb, :] + a_scr[1, rb, :] * cb
        h_scr[0, rf, :] = hf
        h_scr[1, rb, :] = hb
        return hf[SUBLANES - 1:SUBLANES], hb[0:1]

    cf, cb = lax.fori_loop(0, n_grp, chain, carry0, unroll=4)
    y_ref[0] = h_scr[0] + h_scr[1]
    if not latent:
        st_ref[0] = jnp.concatenate([cf, cb], axis=0)


def _lru(pc, cw, cb, wa, ba, wx, bx, lam, state, layer):
    b, t, _ = pc.shape
    latent = state is not None
    full = lambda shape: pl.BlockSpec(shape, lambda i: (0,) * len(shape))
    one_layer = lambda shape: pl.BlockSpec((1,) + shape[1:], lambda i: (layer,) + (0,) * (len(shape) - 1))
    in_specs = [pl.BlockSpec((1, t, WC), lambda i: (i, 0, 0)),
                full(cw.shape), full(cb.shape), one_layer(wa.shape), full(ba.shape),
                one_layer(wx.shape), full(bx.shape), full(lam.shape)]
    args = [pc, cw, cb, wa, ba, wx, bx, lam]
    out_specs = [pl.BlockSpec((1, t, W_BR), lambda i: (i, 0, 0))]
    out_shape = [jax.ShapeDtypeStruct((b, t, W_BR), F32)]
    if latent:
        in_specs.append(pl.BlockSpec((1, 1, 2, W_BR), lambda i: (i, layer, 0, 0)))
        args.append(state)
    else:
        out_specs.append(pl.BlockSpec((1, 2, W_BR), lambda i: (i, 0, 0)))
        out_shape.append(jax.ShapeDtypeStruct((b, 2, W_BR), F32))
    return pl.pallas_call(
        functools.partial(_lru_kernel, latent=latent, layer=layer),
        grid=(b,),
        in_specs=in_specs,
        out_specs=out_specs,
        out_shape=out_shape,
        scratch_shapes=[pltpu.VMEM((2, t, W_BR), F32), pltpu.VMEM((2, t, W_BR), F32)],
        compiler_params=_params(("parallel",)),
        name="lru",
    )(*args)


def _win_kernel(*refs, latent, layer):
    if latent:
        sink_ref, q_ref, kp_ref, kc_ref, kn_ref, ck_ref, cv_ref, o_ref = refs
        j = pl.program_id(1)
        nq = pl.num_programs(1)
        qi = lax.broadcasted_iota(jnp.int32, (Q_BLK, Q_BLK), 0)
        ki = lax.broadcasted_iota(jnp.int32, (Q_BLK, Q_BLK), 1)
        q = q_ref[0]
        ctx_k, ctx_v = _bf(ck_ref[0, 0]), _bf(cv_ref[0, 0])
        pieces = [(_bf(r[0, :, :128]), _bf(r[0, :, 128:]), m, False) for r, m in
                  ((kp_ref, (ki >= qi) & (j > 0)), (kc_ref, None), (kn_ref, (ki <= qi) & (j < nq - 1)))]
        pieces += [(ctx_k[:, i:i + Q_BLK], ctx_v[:, i:i + Q_BLK], None, True)
                   for i in range(0, ctx_k.shape[1], Q_BLK)]
    else:
        sink_ref, x_ref, o_ref = refs
        q = x_ref[0, :, :256]
        k, v = _bf(x_ref[0, :, 256:384]), _bf(x_ref[0, :, 384:512])
        pieces = [(k[i:i + Q_BLK], v[i:i + Q_BLK], None, False) for i in range(0, k.shape[0], Q_BLK)]
    q = _bf(q * (HEAD ** -0.5 * LOG2_E))
    outs = []
    for h in range(2):
        hs = slice(h * HEAD, (h + 1) * HEAD)
        for g in range(2):
            qg = q[:, (2 * h + g) * HEAD:(2 * h + g + 1) * HEAD]
            sink = sink_ref[layer, 2 * h + g] * LOG2_E
            logits = []
            for kp, _, mask, fm in pieces:
                s = _dotf(qg, kp[hs]) if fm else _dotf(qg, kp[:, hs], _NT)
                logits.append(s if mask is None else jnp.where(mask, s, NEG_INF))
            mx = jnp.maximum(jnp.max(functools.reduce(jnp.maximum, logits), axis=-1, keepdims=True), sink)
            es = [jnp.exp2(s - mx) for s in logits]
            den = jnp.sum(functools.reduce(lambda a, b: a + b, es), axis=-1, keepdims=True) + jnp.exp2(sink - mx)
            acc = functools.reduce(lambda a, b: a + b,
                                   [_dotf(_bf(e), vp[hs], _NT) if fm else _dotf(_bf(e), vp[:, hs])
                                    for e, (_, vp, _, fm) in zip(es, pieces)])
            outs.append(acc / den)
    o_ref[0] = jnp.concatenate(outs, axis=-1)


def _win(pb, sink, cache_k, cache_v, layer):
    b, t, _ = pb.shape
    latent = cache_k is not None
    smem = pl.BlockSpec(memory_space=pltpu.SMEM)
    if latent:
        nq = t // Q_BLK
        grid = (b, nq)
        in_specs = [smem,
                    pl.BlockSpec((1, Q_BLK, 256), lambda i, j: (i, j, 0)),
                    pl.BlockSpec((1, Q_BLK, 256), lambda i, j: (i, jnp.maximum(j - 1, 0), 1)),
                    pl.BlockSpec((1, Q_BLK, 256), lambda i, j: (i, j, 1)),
                    pl.BlockSpec((1, Q_BLK, 256), lambda i, j: (i, jnp.minimum(j + 1, nq - 1), 1)),
                    pl.BlockSpec((1, 1) + cache_k.shape[2:], lambda i, j: (i, layer, 0, 0)),
                    pl.BlockSpec((1, 1) + cache_v.shape[2:], lambda i, j: (i, layer, 0, 0))]
        args = [sink, pb, pb, pb, pb, cache_k, cache_v]
        out_spec = pl.BlockSpec((1, Q_BLK, W_BR), lambda i, j: (i, j, 0))
        sem = ("parallel", "parallel")
    else:
        grid = (b,)
        in_specs = [smem, pl.BlockSpec((1, t, WB), lambda i: (i, 0, 0))]
        args = [sink, pb]
        out_spec = pl.BlockSpec((1, t, W_BR), lambda i: (i, 0, 0))
        sem = ("parallel",)
    return pl.pallas_call(
        functools.partial(_win_kernel, latent=latent, layer=layer),
        grid=grid,
        in_specs=in_specs,
        out_specs=out_spec,
        out_shape=jax.ShapeDtypeStruct((b, t, W_BR), F32),
        compiler_params=_params(sem),
        name="win",
    )(*args)


def _diff_kernel(*refs, latent, lam_init, layer):
    if latent:
        lam_ref, g_ref, q_ref, k_ref, v_ref, ck_ref, cv_ref, o_ref = refs
    else:
        lam_ref, g_ref, q_ref, k_ref, v_ref, o_ref = refs
    lp = lam_ref[layer]
    lam = (jnp.exp(jnp.sum(lp[0:1] * lp[1:2], axis=-1, keepdims=True))
           - jnp.exp(jnp.sum(lp[2:3] * lp[3:4], axis=-1, keepdims=True)) + lam_init)
    q = _bf(q_ref[0] * (DQ_D ** -0.5 * LOG2_E))
    key_sets = [(_bf(k_ref[0]), _bf(v_ref[0]), False)]
    if latent:
        key_sets.append((_bf(ck_ref[0, 0]), _bf(cv_ref[0, 0]), True))
    lane = lax.broadcasted_iota(jnp.int32, (1, W_BR), 1)
    out = jnp.zeros(q.shape, F32)
    for h in range(W_BR // HEAD):
        probs = []
        for m in range(2):
            qm = jnp.where(lane // DQ_D == 2 * h + m, q, jnp.zeros((), BF16))
            logits = [_dotf(qm, kk) if fm else _dotf(qm, kk, _NT) for kk, _, fm in key_sets]
            mx = functools.reduce(jnp.maximum, [jnp.max(s, axis=-1, keepdims=True) for s in logits])
            es = [jnp.exp2(s - mx) for s in logits]
            den = functools.reduce(lambda a, b: a + b, [jnp.sum(e, axis=-1, keepdims=True) for e in es])
            inv = 1.0 / den
            probs.append([e * inv for e in es])
        o = None
        for i, (_, vv, fm) in enumerate(key_sets):
            pi = _bf(probs[0][i] - lam * probs[1][i])
            oi = _dotf(pi, vv, _NT) if fm else _dotf(pi, vv)
            o = oi if o is None else o + oi
        out = jnp.where(lane // HEAD == h, o, out)
    ms = _mm2(out * out, _head_ones(W_BR)) * (1.0 / HEAD)
    gain = jnp.concatenate([g_ref[layer:layer + 1]] * (W_BR // HEAD), axis=1)
    o_ref[0] = out * lax.rsqrt(ms + NORM_EPS) * gain * (1.0 - lam_init)


def _diff(pd, lam_p, subln_g, cache_k, cache_v, layer, lam_init):
    b, t, _ = pd.shape
    latent = cache_k is not None
    tq = 256
    in_specs = [pl.BlockSpec(lam_p.shape, lambda i, j: (0, 0, 0)),
                pl.BlockSpec(subln_g.shape, lambda i, j: (0, 0)),
                pl.BlockSpec((1, tq, 256), lambda i, j: (i, j, 0)),
                pl.BlockSpec((1, t, 256), lambda i, j: (i, 0, 1)),
                pl.BlockSpec((1, t, 256), lambda i, j: (i, 0, 2))]
    args = [lam_p, subln_g, pd, pd, pd]
    if latent:
        in_specs += [pl.BlockSpec((1, 1) + cache_k.shape[2:], lambda i, j: (i, layer, 0, 0)),
                     pl.BlockSpec((1, 1) + cache_v.shape[2:], lambda i, j: (i, layer, 0, 0))]
        args += [cache_k, cache_v]
    return pl.pallas_call(
        functools.partial(_diff_kernel, latent=latent, lam_init=lam_init, layer=layer),
        grid=(b, t // tq),
        in_specs=in_specs,
        out_specs=pl.BlockSpec((1, tq, W_BR), lambda i, j: (i, j, 0)),
        out_shape=jax.ShapeDtypeStruct((b, t, W_BR), F32),
        compiler_params=_params(("parallel", "parallel")),
        name="diff",
    )(*args)


def _out_kernel(x_ref, mod_ref, gpost_ref, w_ref, pa_ref, yf_ref, yb_ref, gng_ref, gnb_ref, rk_ref,
                ywin_ref, bg_ref, ylru_ref, cg_ref, ydiff_ref, dg_ref, o_ref, *, layer):
    lrow = slice(layer, layer + 1)
    ones = _head_ones(W_BR)
    pa = pa_ref[0]
    r, k, v, ag = pa[:, 0:256], pa[:, 256:512], pa[:, 512:768], pa[:, 896:1152]
    y = yf_ref[0] + yb_ref[0]
    mu = _mm2(y, ones) * (1.0 / HEAD)
    yc = y - mu
    var = _mm2(yc * yc, ones) * (1.0 / HEAD)
    ya = yc * lax.rsqrt(var + GN_EPS) * gng_ref[lrow] + gnb_ref[lrow]
    ya = ya + _mm2(r * k * rk_ref[lrow], ones) * v
    mix = jnp.concatenate([ya * _silu(ag), ywin_ref[0] * _silu(bg_ref[0]),
                           ylru_ref[0] * _silu(cg_ref[0]), ydiff_ref[0] * _silu(dg_ref[0])], axis=-1)
    o = jnp.dot(mix.astype(BF16), w_ref[0], preferred_element_type=F32)
    o = o * lax.rsqrt(jnp.mean(o * o, -1, keepdims=True) + NORM_EPS) * gpost_ref[lrow]
    o_ref[0] = x_ref[0] + mod_ref[0][:, 2 * D_MODEL:] * o


def _out(x, mod_l, g_post, w_out_bf, layer, latent, pa, yf, yb, gn_g, gn_b, r_k, ywin, pb, ylru, pc, ydiff, pd):
    b, t, _ = x.shape
    tm = ROW_TILE
    mod_idx = (lambda i, j: (1 + i, 0, 0)) if latent else (lambda i, j: (0, 0, 0))
    rows = lambda w, c=0: pl.BlockSpec((1, tm, w), lambda i, j: (i, j, c))
    vec = pl.BlockSpec(gn_g.shape, lambda i, j: (0, 0))
    in_specs = [rows(D_MODEL),
                pl.BlockSpec((1, 1, 3 * D_MODEL), mod_idx),
                pl.BlockSpec(g_post.shape, lambda i, j: (0, 0)),
                pl.BlockSpec((1, D_MODEL, D_MODEL), lambda i, j: (layer, 0, 0)),
                rows(WA), rows(W_BR), rows(W_BR), vec, vec, vec,
                rows(W_BR), rows(W_BR, 2), rows(W_BR), rows(W_BR, 1), rows(W_BR), rows(W_BR, 3)]
    return pl.pallas_call(
        functools.partial(_out_kernel, layer=layer),
        grid=(b, t // tm),
        in_specs=in_specs,
        out_specs=rows(D_MODEL),
        out_shape=jax.ShapeDtypeStruct((b, t, D_MODEL), F32),
        compiler_params=_params(("parallel", "parallel")),
        name="out",
    )(x, mod_l, g_post, w_out_bf, pa, yf, yb, gn_g, gn_b, r_k, ywin, pb, ylru, pc, ydiff, pd)


def _layer(x, mod_l, layer, lam_init, wts, cache, tables):
    latent = cache is not None
    pa, pb, pc, pd, *ctx_t = _project(x, mod_l, wts['g_pre'], wts['w_in_bf'], layer, tables)
    rw = _rwkv(pa, wts['rwkv_w0'], wts['rwkv_w_up'], wts['rwkv_a0'], wts['rwkv_a_up'], wts['rwkv_k_k'],
               wts['rwkv_k_a'], cache['rwkv'] if latent else None, layer)
    lr = _lru(pc, wts['lru_conv_w'], wts['lru_conv_b'], wts['lru_wa'], wts['lru_ba'], wts['lru_wx'],
              wts['lru_bx'], wts['lru_lambda'], cache['lru'] if latent else None, layer)
    ywin = _win(pb, wts['win_sink'], cache['win_k'] if latent else None,
                cache['win_v'] if latent else None, layer)
    ydiff = _diff(pd, wts['diff_lambda'], wts['diff_subln_g'],
                  cache['diff_k'] if latent else None, cache['diff_v'] if latent else None, layer, lam_init)
    y = _out(x, mod_l, wts['g_post'], wts['w_out_bf'], layer, latent, pa, rw[0], rw[1],
             wts['rwkv_gn_g'], wts['rwkv_gn_b'], wts['rwkv_r_k'], ywin, pb, lr[0], pc, ydiff, pd)
    new_cache = None if latent else (ctx_t, rw[2], lr[1])
    return y, new_cache


def kernel(x_prompt, x_sample, c, cache_win_k, cache_win_v, cache_diff_k, cache_diff_v, state_rwkv, state_lru,
           c_ctx, w_mod, b_mod, g_pre, g_post, w_in, w_out,
           rwkv_w0, rwkv_w_up, rwkv_a0, rwkv_a_up, rwkv_k_k, rwkv_k_a, rwkv_r_k, rwkv_gn_g, rwkv_gn_b,
           win_sink, lru_conv_w, lru_conv_b, lru_wa, lru_ba, lru_wx, lru_bx, lru_lambda,
           diff_lambda, diff_subln_g):
    n_b, seq = x_prompt.shape[:2]
    n_dec, dec_seq = x_sample.shape[:2]
    past = cache_win_k.shape[2]
    wts = dict(g_pre=g_pre, g_post=g_post, w_in_bf=w_in.astype(BF16), w_out_bf=w_out.astype(BF16),
               rwkv_w0=rwkv_w0, rwkv_w_up=rwkv_w_up, rwkv_a0=rwkv_a0, rwkv_a_up=rwkv_a_up,
               rwkv_k_k=rwkv_k_k, rwkv_k_a=rwkv_k_a, rwkv_r_k=rwkv_r_k.reshape(DEPTH, W_BR), rwkv_gn_g=rwkv_gn_g,
               rwkv_gn_b=rwkv_gn_b, win_sink=win_sink, lru_conv_w=lru_conv_w, lru_conv_b=lru_conv_b,
               lru_wa=lru_wa, lru_ba=lru_ba, lru_wx=lru_wx, lru_bx=lru_bx, lru_lambda=lru_lambda,
               diff_lambda=diff_lambda, diff_subln_g=diff_subln_g)
    fm = lambda a: jnp.moveaxis(a, 2, -1).reshape(n_dec, DEPTH, -1, past)
    cache = dict(win_k=fm(cache_win_k), win_v=fm(cache_win_v), diff_k=fm(cache_diff_k), diff_v=fm(cache_diff_v),
                 rwkv=state_rwkv, lru=state_lru)
    cvec = jnp.concatenate([c_ctx[None], c, jnp.zeros((8 - 1 - n_dec, D_MODEL), F32)], axis=0)
    mod = _modulation(cvec, w_mod, b_mod)
    cos_b, sin_b = _rope_tables(dec_seq, HEAD, 384)
    cos_d, sin_d = _rope_tables(dec_seq, DQ_D, 512)
    tables = (cos_b, sin_b, cos_d, sin_d)

    y_p, y_s = x_prompt, x_sample
    ctx = []
    for l in range(DEPTH):
        lam_init = 0.8 - 0.6 * math.exp(-0.3 * l)
        mod_l = mod[l].reshape(8, 1, 3 * D_MODEL)
        y_p, nc = _layer(y_p, mod_l, l, lam_init, wts, None, None)
        ctx.append(nc)
        y_s, _ = _layer(y_s, mod_l, l, lam_init, wts, cache, tables)
    stack = lambda f: jnp.stack([f(ct) for ct in ctx], axis=1)
    tm_ = lambda i, dims: jnp.moveaxis(stack(lambda ct: ct[0][i]).reshape((n_b, DEPTH) + dims + (seq,)), -1, 2)
    new_win_k = tm_(0, (2, HEAD))
    new_win_v = tm_(1, (2, HEAD))
    new_diff_k = tm_(2, (4, 2, DQ_D))
    new_diff_v = tm_(3, (4, HEAD))
    new_state_rwkv = stack(lambda ct: ct[1])
    new_state_lru = stack(lambda ct: ct[2])
    return (y_p, y_s, new_win_k, new_win_v, new_diff_k, new_diff_v, new_state_rwkv, new_state_lru)
```

```python
import functools
import math

import numpy as np
import jax
import jax.numpy as jnp
from jax import lax
from jax.experimental import pallas as pl
from jax.experimental.pallas import tpu as pltpu

F32 = jnp.float32
BF16 = jnp.bfloat16
HI = lax.Precision.HIGHEST

D_MODEL = 1024
DEPTH = 2
GRID_W = 64
ROPE_BASE = 10000.0
NORM_EPS = 1e-6
NEG_INF = -1e30
LOG2_E = math.log2(math.e)
GN_EPS = 64e-5
C_RG = 8.0
W_BR = 256
HEAD = 64
SUBLANES = 8
LORA = 64
DQ_D = 32
WINDOW = 128
Q_BLK = 128
P_TOTAL = 3456
WA, WB, WC, WD = 1152, 768, 512, 1024
CHUNK = 64
RWKV_BATCH = 4
ROW_TILE = 256
VMEM_LIMIT = 48 * 1024 * 1024


_NN = (((1,), (0,)), ((), ()))
_NT = (((1,), (1,)), ((), ()))
_TN = (((0,), (0,)), ((), ()))


def _dot_hi(a, b):
    return jnp.dot(a, b, precision=HI, preferred_element_type=F32)


def _bf(x):
    return x.astype(BF16)


def _dotf(x, y, dims=_NN):
    return lax.dot_general(x, y, dims, preferred_element_type=F32)


def _split(x):
    hi = x.astype(BF16)
    return hi, (x - hi.astype(F32)).astype(BF16)


def _mm3(a, b, dims=_NN):
    return _dotf(a[0], b[0], dims) + (_dotf(a[0], b[1], dims) + _dotf(a[1], b[0], dims))


def _mm2(x, m01, left=False):
    hi, lo = _split(x)
    if left:
        return _dotf(m01, hi) + _dotf(m01, lo)
    return _dotf(hi, m01) + _dotf(lo, m01)


def _dot_bf(a, b):
    return jnp.dot(a.astype(BF16), b.astype(BF16), preferred_element_type=F32)


def _dot_nt_bf(a, b):
    return lax.dot_general(a.astype(BF16), b.astype(BF16), (((1,), (1,)), ((), ())),
                           preferred_element_type=F32)


def _sigmoid(x):
    return 1.0 / (1.0 + jnp.exp(-x))


def _silu(x):
    return x * _sigmoid(x)


def _softplus(x):
    return jnp.maximum(x, 0.0) + jnp.log1p(jnp.exp(-jnp.abs(x)))


def _head_ones(n):
    r = lax.broadcasted_iota(jnp.int32, (n, n), 0) // HEAD
    c = lax.broadcasted_iota(jnp.int32, (n, n), 1) // HEAD
    return jnp.where(r == c, 1.0, 0.0).astype(BF16)


def _params(sem):
    return pltpu.CompilerParams(dimension_semantics=sem, vmem_limit_bytes=VMEM_LIMIT)


def _mod_kernel(c_ref, w_ref, b_ref, o_ref):
    o_ref[0] = _mm3(_split(_silu(c_ref[...])), _split(w_ref[0])) + b_ref[0]


def _modulation(cvec, w_mod, b_mod):
    n_l = w_mod.shape[0]
    tn = 512
    return pl.pallas_call(
        _mod_kernel,
        grid=(n_l, 3 * D_MODEL // tn),
        in_specs=[pl.BlockSpec((8, D_MODEL), lambda l, j: (0, 0)),
                  pl.BlockSpec((1, D_MODEL, tn), lambda l, j: (l, 0, j)),
                  pl.BlockSpec((1, 1, tn), lambda l, j: (l, 0, j))],
        out_specs=pl.BlockSpec((1, 8, tn), lambda l, j: (l, 0, j)),
        out_shape=jax.ShapeDtypeStruct((n_l, 8, 3 * D_MODEL), F32),
        compiler_params=_params(("parallel", "parallel")),
        name="mod",
    )(cvec, w_mod, b_mod.reshape(n_l, 1, 3 * D_MODEL))


def _rope(x, cos, sin_signed, off):
    w = x.shape[-1]
    lane = lax.broadcasted_iota(jnp.int32, x.shape, 1)
    first = (lane % (2 * off)) < off
    partner = jnp.where(first, pltpu.roll(x, w - off, 1), pltpu.roll(x, off, 1))
    return x * cos + partner * sin_signed


def _proj_kernel(*refs, latent, layer):
    if latent:
        x_ref, mod_ref, g_ref, w_ref, cb_ref, sb_ref, cd_ref, sd_ref, oa, ob, oc, od = refs
    else:
        x_ref, mod_ref, g_ref, w_ref, oa, ob, oc, od, o_wk, o_wv, o_dk, o_dv = refs
    x = x_ref[0]
    y = x * lax.rsqrt(jnp.mean(x * x, -1, keepdims=True) + NORM_EPS) * g_ref[layer:layer + 1]
    m = mod_ref[0]
    h = y * (1.0 + m[:, D_MODEL:2 * D_MODEL]) + m[:, :D_MODEL]
    p = jnp.dot(h.astype(BF16), w_ref[0], preferred_element_type=F32)
    oa[0] = p[:, :WA]
    pb = p[:, WA:WA + WB]
    pd = p[:, WA + WB + WC:]
    if latent:
        ob[0, :, :384] = _rope(pb[:, :384], cb_ref[...], sb_ref[...], 16)
        ob[0, :, 384:] = pb[:, 384:]
        od[0, :, :512] = _rope(pd[:, :512], cd_ref[...], sd_ref[...], 8)
        od[0, :, 512:] = pd[:, 512:]
    else:
        ob[0] = pb
        od[0] = pd
        o_wk[0] = pb[:, 256:384].T
        o_wv[0] = pb[:, 384:512].T
        o_dk[0] = pd[:, 256:512].T
        o_dv[0] = pd[:, 512:768].T
    oc[0] = p[:, WA + WB:WA + WB + WC]


def _project(x, mod_l, g_pre, w_in_bf, layer, tables):
    b, t, _ = x.shape
    latent = tables is not None
    tm = ROW_TILE
    mod_idx = (lambda i, j: (1 + i, 0, 0)) if latent else (lambda i, j: (0, 0, 0))
    in_specs = [pl.BlockSpec((1, tm, D_MODEL), lambda i, j: (i, j, 0)),
                pl.BlockSpec((1, 1, 3 * D_MODEL), mod_idx),
                pl.BlockSpec(g_pre.shape, lambda i, j: (0, 0)),
                pl.BlockSpec((1, D_MODEL, P_TOTAL), lambda i, j: (layer, 0, 0))]
    args = [x, mod_l, g_pre, w_in_bf]
    if latent:
        for tab in tables:
            in_specs.append(pl.BlockSpec((tm, tab.shape[1]), lambda i, j: (j, 0)))
            args.append(tab)
    widths = (WA, WB, WC, WD)
    out_specs = [pl.BlockSpec((1, tm, w), lambda i, j: (i, j, 0)) for w in widths]
    out_shape = [jax.ShapeDtypeStruct((b, t, w), F32) for w in widths]
    if not latent:
        for w in (128, 128, W_BR, W_BR):
            out_specs.append(pl.BlockSpec((1, w, tm), lambda i, j: (i, 0, j)))
            out_shape.append(jax.ShapeDtypeStruct((b, w, t), F32))
    return pl.pallas_call(
        functools.partial(_proj_kernel, latent=latent, layer=layer),
        grid=(b, t // tm),
        in_specs=in_specs,
        out_specs=out_specs,
        out_shape=out_shape,
        compiler_params=_params(("parallel", "parallel")),
        name="proj",
    )(*args)


def _rope_tables(t, head_dim, n_lanes):
    half = head_dim // 2
    quarter = half // 2
    pos = np.arange(t)
    row = (pos // GRID_W).astype(np.float32)
    col = (pos % GRID_W).astype(np.float32)
    inv = np.float32(ROPE_BASE) ** (-np.arange(0, half, 2, dtype=np.float32) / np.float32(half))
    lane = np.arange(n_lanes) % head_dim
    in_half = lane % half
    p = np.where((lane < half)[None, :], row[:, None], col[:, None])
    ang = (p * inv[in_half % quarter][None, :]).astype(np.float64)
    sign = np.where(in_half < quarter, -1.0, 1.0)[None, :]
    return jnp.asarray(np.cos(ang), F32), jnp.asarray(np.sin(ang) * sign, F32)


_PREP_FIELDS = (('ar', 2, 1, BF16), ('nm', 1, 1, BF16), ('mm', 1, 1, BF16), ('pq', 1, 2, BF16),
                ('v', 1, 1, BF16), ('bkt', 1, 2, BF16), ('gct', 1, 1, F32))


def _prep_scratch(slots, n_streams):
    return [pltpu.VMEM((slots, n_streams, rows * CHUNK, lanes * W_BR), dt) for _, rows, lanes, dt in _PREP_FIELDS]


class _SlotView:
    def __init__(self, ref, slot):
        self.ref, self.slot, self.shape = ref, slot, ref.shape[1:]

    def __getitem__(self, idx):
        return self.ref[(self.slot,) + (idx if isinstance(idx, tuple) else (idx,))]

    def __setitem__(self, idx, val):
        self.ref[(self.slot,) + (idx if isinstance(idx, tuple) else (idx,))] = val


def _head_transpose(x):
    xt = x.T
    return jnp.concatenate([xt[h * HEAD:(h + 1) * HEAD] for h in range(x.shape[1] // HEAD)], axis=1)


def _expand(x, same):
    return jnp.concatenate([x] * (same.shape[0] // x.shape[0]), axis=0) * same


def _rwkv_prepare(xs, dirs, w0_ref, wup_ref, a0_ref, aup_ref, k_k, k_a, m_ref, same_ref, out):
    c = xs[0].shape[0]
    ones = _head_ones(W_BR)
    ti = lax.broadcasted_iota(jnp.int32, (c, c), 0)
    si = lax.broadcasted_iota(jnp.int32, (c, c), 1)
    cums = [jnp.where(si <= ti, 1.0, 0.0).astype(BF16), jnp.where(si >= ti, 1.0, 0.0).astype(BF16)]
    same = same_ref[...]

    st = []
    for xc, d in zip(xs, dirs):
        r, k, v = xc[:, 0:256], xc[:, 256:512], xc[:, 512:768]
        wd, ad = xc[:, 768:832], xc[:, 832:896]
        kk = k * k_k
        kk = kk * lax.rsqrt(_mm2(kk * kk, ones) + 1e-12)
        z = w0_ref[d:d + 1] + _mm3(_split(jnp.tanh(wd)), _split(wup_ref[d]))
        e = jnp.exp(-_softplus(-z) - 0.5)
        a = _sigmoid(a0_ref[d:d + 1] + _mm3(_split(ad), _split(aup_ref[d])))
        st.append(dict(r=r, v=v, e=e, kd=k * (1.0 + (a - 1.0) * k_a), alpha=-kk, beta=kk * a))
    yield
    for s, d in zip(st, dirs):
        l_incl = _mm2(s['e'], cums[d], left=True)
        l_tot = jnp.sum(s['e'], axis=0, keepdims=True)
        grow = jnp.exp(l_incl)
        tail = jnp.exp(l_incl - l_tot)
        s.update(ar=jnp.concatenate([_bf(s['alpha'] * jnp.exp(s['e'] - l_incl)),
                                     _bf(s['r'] * jnp.exp(-l_incl))], axis=0),
                 b_t=_bf(s['beta'] * grow), k_t=_bf(s['kd'] * grow),
                 b_h=s['beta'] * tail, k_h=s['kd'] * tail,
                 g_c=jnp.broadcast_to(jnp.exp(-l_tot), (c, W_BR)))
    yield
    for g, (s, d) in enumerate(zip(st, dirs)):
        out['ar'][g] = s['ar']
        rhs_t = jnp.concatenate([_expand(s['b_t'], same), _expand(s['k_t'], same)], axis=0)
        g4 = _dotf(s['ar'], rhs_t, _NT)
        strict = m_ref[_M_STRICT + 2 * d]
        incl = m_ref[_M_INCL + 2 * d]
        out['nm'][g] = _bf(g4[:c, :W_BR]) * strict
        out['mm'][g] = _bf(g4[:c, W_BR:]) * strict
        out['pq'][g, :, :W_BR] = _bf(g4[c:, :W_BR]) * incl
        out['pq'][g, :, W_BR:] = _bf(g4[c:, W_BR:]) * incl
    yield
    for g, s in enumerate(st):
        out['v'][g] = _bf(s['v'])
        out['bkt'][g] = jnp.concatenate([_bf(_head_transpose(s['b_h'])), _bf(_head_transpose(s['k_h']))], axis=1)
        out['gct'][g] = _head_transpose(s['g_c'])


def _rwkv_advance(p, m_ref, same_ref, z_ref, y_stores):
    n_streams, c = p['nm'].shape[0], p['nm'].shape[1]
    gs = range(n_streams)
    same = same_ref[...]
    ex = lambda x: _expand(_bf(x), same)
    invs = [(p['nm'][g] * m_ref[_M_PAIR]).astype(F32) + m_ref[_M_EYE].astype(F32) for g in gs]
    for lvl in range(int(math.log2(c)) - 1):
        inv_x = [ex(t) for t in invs]
        half = [_dotf(p['nm'][g] * m_ref[_M_OFF + lvl], inv_x[g]) for g in gs]
        invs = [invs[g] + _dotf(_bf(invs[g]), ex(half[g])) for g in gs]
        yield
    z0 = [z_ref[g] for g in gs]
    as0 = [_dotf(p['ar'][g], ex(z0[g])) for g in gs]
    vx = [_expand(p['v'][g], same) for g in gs]
    rhs = [as0[g][:c] + _dotf(p['mm'][g], vx[g]) for g in gs]
    yield
    uv = [jnp.concatenate([ex(_dotf(_bf(invs[g]), ex(rhs[g]))), vx[g]], axis=0) for g in gs]
    yield
    for g in gs:
        z_ref[g] = z0[g] * p['gct'][g] + _dotf(p['bkt'][g], uv[g])
    yield
    for g in gs:
        y_stores[g](as0[g][c:] + _dotf(p['pq'][g], uv[g]))


def _interleave(*gens):
    live = list(gens)
    while live:
        for gen in list(live):
            try:
                next(gen)
            except StopIteration:
                live.remove(gen)


_M_STRICT, _M_INCL, _M_EYE, _M_PAIR, _M_OFF = 0, 1, 4, 5, 6


def _rwkv_masks(c, n_h):
    t = np.arange(c)[:, None]
    s = np.arange(c)[None, :]
    masks = [s < t, s <= t, s > t, s >= t, s == t, ((t // 2) == (s // 2)) & (t != s)]
    b = 2
    while b < c:
        masks.append(((t // (2 * b)) == (s // (2 * b))) & ((t // b) != (s // b)))
        b *= 2
    masks = np.tile(np.stack(masks).astype(np.float32), (1, 1, n_h))
    r = np.arange(n_h * c)[:, None] // c
    l = np.arange(n_h * HEAD)[None, :] // HEAD
    return jnp.asarray(masks, BF16), jnp.asarray((r == l).astype(np.float32), BF16)


def _rwkv_kernel(*refs, latent, n_b, layer):
    (xf_ref, xb_ref, xfn_ref, xbn_ref, w0_ref, wup_ref, a0_ref, aup_ref, kk_ref, ka_ref,
     m_ref, same_ref) = refs[:12]
    if latent:
        s0_ref, yf_ref, yb_ref = refs[12:15]
    else:
        yf_ref, yb_ref, st_ref = refs[12:15]
    scr = refs[15:]
    z_scr = scr[0]
    names = [f[0] for f in _PREP_FIELDS]
    ci = pl.program_id(1)
    step = pl.program_id(0) * pl.num_programs(1) + ci
    view = lambda refs_, slot: {k: _SlotView(r, slot) for k, r in zip(names, refs_)}
    cur = view(scr[1:1 + len(names)], step % 2)
    nxt = view(scr[1:1 + len(names)], 1 - step % 2)
    mid = view(scr[1 + len(names):], 0)
    n_h = W_BR // HEAD
    c = CHUNK
    head = lambda h: slice(h * HEAD, (h + 1) * HEAD)
    streams = [(bi, d) for bi in range(n_b) for d in range(2)]
    dirs = [d for _, d in streams]
    params = (w0_ref[layer], wup_ref[0], a0_ref[layer], aup_ref[0], kk_ref[layer:layer + 1],
              ka_ref[layer:layer + 1], m_ref, same_ref)
    first = lambda d: slice(0, c) if d == 0 else slice(c, 2 * c)
    second = lambda d: slice(c, 2 * c) if d == 0 else slice(0, c)
    x_refs, xn_refs, y_refs = (xf_ref, xb_ref), (xfn_ref, xbn_ref), (yf_ref, yb_ref)

    def y_store(bi, d, rows):
        def store(y):
            y_refs[d][bi, rows, :] = y
        return store

    @pl.when(ci == 0)
    def _():
        for g, (bi, d) in enumerate(streams):
            if latent:
                z_scr[g] = jnp.concatenate([s0_ref[bi, 0, d, h].T for h in range(n_h)], axis=1)
            else:
                z_scr[g] = jnp.zeros(z_scr.shape[1:], F32)

    @pl.when(step == 0)
    def _():
        _interleave(_rwkv_prepare([x_refs[d][bi, first(d), :] for bi, d in streams], dirs, *params, cur))

    _interleave(_rwkv_advance(cur, m_ref, same_ref, z_scr, [y_store(bi, d, first(d)) for bi, d in streams]),
                _rwkv_prepare([x_refs[d][bi, second(d), :] for bi, d in streams], dirs, *params, mid))
    _interleave(_rwkv_advance(mid, m_ref, same_ref, z_scr, [y_store(bi, d, second(d)) for bi, d in streams]),
                _rwkv_prepare([xn_refs[d][bi, first(d), :] for bi, d in streams], dirs, *params, nxt))

    if not latent:
        @pl.when(ci == pl.num_programs(1) - 1)
        def _():
            for g, (bi, d) in enumerate(streams):
                z = z_scr[g]
                for h in range(n_h):
                    st_ref[bi, d, h] = z[:, head(h)].T


def _rwkv(pa, w0, wup, a0, aup, k_k, k_a, state, layer):
    b, t, _ = pa.shape
    latent = state is not None
    ns = t // (2 * CHUNK)
    n_h = W_BR // HEAD
    n_b = RWKV_BATCH
    masks, same = _rwkv_masks(CHUNK, n_h)
    full = lambda shape: pl.BlockSpec(shape, lambda i, j: (0,) * len(shape))
    one_layer = lambda shape: pl.BlockSpec((1,) + shape[1:], lambda i, j: (layer,) + (0,) * (len(shape) - 1))
    rows = lambda w, idx: pl.BlockSpec((n_b, 2 * CHUNK, w), idx)
    nxt_group = lambda i, j: jnp.minimum(i + (j + 1) // ns, b // n_b - 1)
    in_specs = [rows(WA, lambda i, j: (i, j, 0)),
                rows(WA, lambda i, j: (i, ns - 1 - j, 0)),
                rows(WA, lambda i, j: (nxt_group(i, j), (j + 1) % ns, 0)),
                rows(WA, lambda i, j: (nxt_group(i, j), (2 * ns - 2 - j) % ns, 0)),
                full(w0.shape), one_layer(wup.shape), full(a0.shape), one_layer(aup.shape),
                full(k_k.shape), full(k_a.shape), full(masks.shape), full(same.shape)]
    args = [pa, pa, pa, pa, w0, wup, a0, aup, k_k, k_a, masks, same]
    out_specs = [rows(W_BR, lambda i, j: (i, j, 0)), rows(W_BR, lambda i, j: (i, ns - 1 - j, 0))]
    out_shape = [jax.ShapeDtypeStruct((b, t, W_BR), F32)] * 2
    if latent:
        in_specs.append(pl.BlockSpec((n_b, 1, 2, n_h, HEAD, HEAD), lambda i, j: (i, layer, 0, 0, 0, 0)))
        args.append(state)
    else:
        out_specs.append(pl.BlockSpec((n_b, 2, n_h, HEAD, HEAD), lambda i, j: (i, 0, 0, 0, 0)))
        out_shape.append(jax.ShapeDtypeStruct((b, 2, n_h, HEAD, HEAD), F32))
    return pl.pallas_call(
        functools.partial(_rwkv_kernel, latent=latent, n_b=n_b, layer=layer),
        grid=(b // n_b, ns),
        in_specs=in_specs,
        out_specs=out_specs,
        out_shape=out_shape,
        scratch_shapes=([pltpu.VMEM((2 * n_b, CHUNK, W_BR), F32)]
                        + _prep_scratch(2, 2 * n_b) + _prep_scratch(1, 2 * n_b)),
        compiler_params=_params(("arbitrary", "arbitrary")),
        name="rwkv",
    )(*args)


def _lru_kernel(*refs, latent, layer):
    if latent:
        (x_ref, cw_ref, cb_ref, wa_ref, ba_ref, wx_ref, bx_ref, lam_ref, h0_ref, y_ref,
         a_scr, h_scr) = refs
    else:
        (x_ref, cw_ref, cb_ref, wa_ref, ba_ref, wx_ref, bx_ref, lam_ref, y_ref, st_ref,
         a_scr, h_scr) = refs
    x = x_ref[0, :, :W_BR]
    t = x.shape[0]
    row = lax.broadcasted_iota(jnp.int32, x.shape, 0)

    def shift_dn(z, k, fill):
        return jnp.where(row >= k, pltpu.roll(z, k, 0), fill)

    def shift_up(z, k, fill):
        return jnp.where(row < t - k, pltpu.roll(z, t - k, 0), fill)

    cw = cw_ref[layer]
    xc = (cb_ref[layer:layer + 1] + shift_dn(x, 2, 0.0) * cw[0:1] + shift_dn(x, 1, 0.0) * cw[1:2]
          + x * cw[2:3] + shift_up(x, 1, 0.0) * cw[3:4])
    xs = _split(xc)
    sub = row % SUBLANES
    n_blk = W_BR // HEAD
    zero = jnp.zeros((HEAD, HEAD), F32)

    def block_diag(w_ref, d):
        return jnp.concatenate([jnp.concatenate([w_ref[0, d, n] if m == n else zero for m in range(n_blk)], axis=1)
                                for n in range(n_blk)], axis=0)

    for d in range(2):
        gate_a = _sigmoid(_mm3(xs, _split(block_diag(wa_ref, d))) + ba_ref[layer, d:d + 1])
        gate_x = _sigmoid(_mm3(xs, _split(block_diag(wx_ref, d))) + bx_ref[layer, d:d + 1])
        log_a = -C_RG * gate_a * _softplus(-lam_ref[layer, d:d + 1])
        a = jnp.exp(log_a)
        u = jnp.sqrt(-jnp.tanh(log_a) * (a * a + 1.0)) * (gate_x * xc)
        k = 1
        while k < SUBLANES:
            if d == 0:
                keep = sub >= k
                sh = lambda z: pltpu.roll(z, k, 0)
            else:
                keep = sub < SUBLANES - k
                sh = lambda z: pltpu.roll(z, t - k, 0)
            u = a * jnp.where(keep, sh(u), 0.0) + u
            a = a * jnp.where(keep, sh(a), 1.0)
            k *= 2
        a_scr[d] = a
        h_scr[d] = u

    n_grp = t // SUBLANES
    if latent:
        carry0 = (h0_ref[0, 0, 0:1], h0_ref[0, 0, 1:2])
    else:
        carry0 = (jnp.zeros((1, W_BR), F32),) * 2

    def chain(i, carry):
        cf, cb = carry
        rf = pl.ds(pl.multiple_of(i * SUBLANES, SUBLANES), SUBLANES)
        rb = pl.ds(pl.multiple_of((n_grp - 1 - i) * SUBLANES, SUBLANES), SUBLANES)
        hf = h_scr[0, rf, :] + a_scr[0, rf, :] * cf
        hb = h_scr[1, rb, :] + a_scr[1, rb, :] * cb
        h_scr[0, rf, :] = hf
        h_scr[1, rb, :] = hb
        return hf[SUBLANES - 1:SUBLANES], hb[0:1]

    cf, cb = lax.fori_loop(0, n_grp, chain, carry0, unroll=4)
    y_ref[0] = h_scr[0] + h_scr[1]
    if not latent:
        st_ref[0] = jnp.concatenate([cf, cb], axis=0)


def _lru(pc, cw, cb, wa, ba, wx, bx, lam, state, layer):
    b, t, _ = pc.shape
    latent = state is not None
    full = lambda shape: pl.BlockSpec(shape, lambda i: (0,) * len(shape))
    one_layer = lambda shape: pl.BlockSpec((1,) + shape[1:], lambda i: (layer,) + (0,) * (len(shape) - 1))
    in_specs = [pl.BlockSpec((1, t, WC), lambda i: (i, 0, 0)),
                full(cw.shape), full(cb.shape), one_layer(wa.shape), full(ba.shape),
                one_layer(wx.shape), full(bx.shape), full(lam.shape)]
    args = [pc, cw, cb, wa, ba, wx, bx, lam]
    out_specs = [pl.BlockSpec((1, t, W_BR), lambda i: (i, 0, 0))]
    out_shape = [jax.ShapeDtypeStruct((b, t, W_BR), F32)]
    if latent:
        in_specs.append(pl.BlockSpec((1, 1, 2, W_BR), lambda i: (i, layer, 0, 0)))
        args.append(state)
    else:
        out_specs.append(pl.BlockSpec((1, 2, W_BR), lambda i: (i, 0, 0)))
        out_shape.append(jax.ShapeDtypeStruct((b, 2, W_BR), F32))
    return pl.pallas_call(
        functools.partial(_lru_kernel, latent=latent, layer=layer),
        grid=(b,),
        in_specs=in_specs,
        out_specs=out_specs,
        out_shape=out_shape,
        scratch_shapes=[pltpu.VMEM((2, t, W_BR), F32), pltpu.VMEM((2, t, W_BR), F32)],
        compiler_params=_params(("parallel",)),
        name="lru",
    )(*args)


def _win_kernel(*refs, latent, layer):
    if latent:
        sink_ref, q_ref, kp_ref, kc_ref, kn_ref, ck_ref, cv_ref, o_ref = refs
        j = pl.program_id(1)
        nq = pl.num_programs(1)
        qi = lax.broadcasted_iota(jnp.int32, (Q_BLK, Q_BLK), 0)
        ki = lax.broadcasted_iota(jnp.int32, (Q_BLK, Q_BLK), 1)
        q = q_ref[0]
        ctx_k, ctx_v = _bf(ck_ref[0, 0]), _bf(cv_ref[0, 0])
        pieces = [(_bf(r[0, :, :128]), _bf(r[0, :, 128:]), m, False) for r, m in
                  ((kp_ref, (ki >= qi) & (j > 0)), (kc_ref, None), (kn_ref, (ki <= qi) & (j < nq - 1)))]
        pieces += [(ctx_k[:, i:i + Q_BLK], ctx_v[:, i:i + Q_BLK], None, True)
                   for i in range(0, ctx_k.shape[1], Q_BLK)]
    else:
        sink_ref, x_ref, o_ref = refs
        q = x_ref[0, :, :256]
        k, v = _bf(x_ref[0, :, 256:384]), _bf(x_ref[0, :, 384:512])
        pieces = [(k[i:i + Q_BLK], v[i:i + Q_BLK], None, False) for i in range(0, k.shape[0], Q_BLK)]
    q = _bf(q * (HEAD ** -0.5 * LOG2_E))
    outs = []
    for h in range(2):
        hs = slice(h * HEAD, (h + 1) * HEAD)
        for g in range(2):
            qg = q[:, (2 * h + g) * HEAD:(2 * h + g + 1) * HEAD]
            sink = sink_ref[layer, 2 * h + g] * LOG2_E
            logits = []
            for kp, _, mask, fm in pieces:
                s = _dotf(qg, kp[hs]) if fm else _dotf(qg, kp[:, hs], _NT)
                logits.append(s if mask is None else jnp.where(mask, s, NEG_INF))
            mx = jnp.maximum(jnp.max(functools.reduce(jnp.maximum, logits), axis=-1, keepdims=True), sink)
            es = [jnp.exp2(s - mx) for s in logits]
            den = jnp.sum(functools.reduce(lambda a, b: a + b, es), axis=-1, keepdims=True) + jnp.exp2(sink - mx)
            acc = functools.reduce(lambda a, b: a + b,
                                   [_dotf(_bf(e), vp[hs], _NT) if fm else _dotf(_bf(e), vp[:, hs])
                                    for e, (_, vp, _, fm) in zip(es, pieces)])
            outs.append(acc / den)
    o_ref[0] = jnp.concatenate(outs, axis=-1)


def _win(pb, sink, cache_k, cache_v, layer):
    b, t, _ = pb.shape
    latent = cache_k is not None
    smem = pl.BlockSpec(memory_space=pltpu.SMEM)
    if latent:
        nq = t // Q_BLK
        grid = (b, nq)
        in_specs = [smem,
                    pl.BlockSpec((1, Q_BLK, 256), lambda i, j: (i, j, 0)),
                    pl.BlockSpec((1, Q_BLK, 256), lambda i, j: (i, jnp.maximum(j - 1, 0), 1)),
                    pl.BlockSpec((1, Q_BLK, 256), lambda i, j: (i, j, 1)),
                    pl.BlockSpec((1, Q_BLK, 256), lambda i, j: (i, jnp.minimum(j + 1, nq - 1), 1)),
                    pl.BlockSpec((1, 1) + cache_k.shape[2:], lambda i, j: (i, layer, 0, 0)),
                    pl.BlockSpec((1, 1) + cache_v.shape[2:], lambda i, j: (i, layer, 0, 0))]
        args = [sink, pb, pb, pb, pb, cache_k, cache_v]
        out_spec = pl.BlockSpec((1, Q_BLK, W_BR), lambda i, j: (i, j, 0))
        sem = ("parallel", "parallel")
    else:
        grid = (b,)
        in_specs = [smem, pl.BlockSpec((1, t, WB), lambda i: (i, 0, 0))]
        args = [sink, pb]
        out_spec = pl.BlockSpec((1, t, W_BR), lambda i: (i, 0, 0))
        sem = ("parallel",)
    return pl.pallas_call(
        functools.partial(_win_kernel, latent=latent, layer=layer),
        grid=grid,
        in_specs=in_specs,
        out_specs=out_spec,
        out_shape=jax.ShapeDtypeStruct((b, t, W_BR), F32),
        compiler_params=_params(sem),
        name="win",
    )(*args)


def _diff_kernel(*refs, latent, lam_init, layer):
    if latent:
        lam_ref, g_ref, q_ref, k_ref, v_ref, ck_ref, cv_ref, o_ref = refs
    else:
        lam_ref, g_ref, q_ref, k_ref, v_ref, o_ref = refs
    lp = lam_ref[layer]
    lam = (jnp.exp(jnp.sum(lp[0:1] * lp[1:2], axis=-1, keepdims=True))
           - jnp.exp(jnp.sum(lp[2:3] * lp[3:4], axis=-1, keepdims=True)) + lam_init)
    q = _bf(q_ref[0] * (DQ_D ** -0.5 * LOG2_E))
    key_sets = [(_bf(k_ref[0]), _bf(v_ref[0]), False)]
    if latent:
        key_sets.append((_bf(ck_ref[0, 0]), _bf(cv_ref[0, 0]), True))
    lane = lax.broadcasted_iota(jnp.int32, (1, W_BR), 1)
    out = jnp.zeros(q.shape, F32)
    for h in range(W_BR // HEAD):
        probs = []
        for m in range(2):
            qm = jnp.where(lane // DQ_D == 2 * h + m, q, jnp.zeros((), BF16))
            logits = [_dotf(qm, kk) if fm else _dotf(qm, kk, _NT) for kk, _, fm in key_sets]
            mx = functools.reduce(jnp.maximum, [jnp.max(s, axis=-1, keepdims=True) for s in logits])
            es = [jnp.exp2(s - mx) for s in logits]
            den = functools.reduce(lambda a, b: a + b, [jnp.sum(e, axis=-1, keepdims=True) for e in es])
            inv = 1.0 / den
            probs.append([e * inv for e in es])
        o = None
        for i, (_, vv, fm) in enumerate(key_sets):
            pi = _bf(probs[0][i] - lam * probs[1][i])
            oi = _dotf(pi, vv, _NT) if fm else _dotf(pi, vv)
            o = oi if o is None else o + oi
        out = jnp.where(lane // HEAD == h, o, out)
    ms = _mm2(out * out, _head_ones(W_BR)) * (1.0 / HEAD)
    gain = jnp.concatenate([g_ref[layer:layer + 1]] * (W_BR // HEAD), axis=1)
    o_ref[0] = out * lax.rsqrt(ms + NORM_EPS) * gain * (1.0 - lam_init)


def _diff(pd, lam_p, subln_g, cache_k, cache_v, layer, lam_init):
    b, t, _ = pd.shape
    latent = cache_k is not None
    tq = 256
    in_specs = [pl.BlockSpec(lam_p.shape, lambda i, j: (0, 0, 0)),
                pl.BlockSpec(subln_g.shape, lambda i, j: (0, 0)),
                pl.BlockSpec((1, tq, 256), lambda i, j: (i, j, 0)),
                pl.BlockSpec((1, t, 256), lambda i, j: (i, 0, 1)),
                pl.BlockSpec((1, t, 256), lambda i, j: (i, 0, 2))]
    args = [lam_p, subln_g, pd, pd, pd]
    if latent:
        in_specs += [pl.BlockSpec((1, 1) + cache_k.shape[2:], lambda i, j: (i, layer, 0, 0)),
                     pl.BlockSpec((1, 1) + cache_v.shape[2:], lambda i, j: (i, layer, 0, 0))]
        args += [cache_k, cache_v]
    return pl.pallas_call(
        functools.partial(_diff_kernel, latent=latent, lam_init=lam_init, layer=layer),
        grid=(b, t // tq),
        in_specs=in_specs,
        out_specs=pl.BlockSpec((1, tq, W_BR), lambda i, j: (i, j, 0)),
        out_shape=jax.ShapeDtypeStruct((b, t, W_BR), F32),
        compiler_params=_params(("parallel", "parallel")),
        name="diff",
    )(*args)


def _out_kernel(x_ref, mod_ref, gpost_ref, w_ref, pa_ref, yf_ref, yb_ref, gng_ref, gnb_ref, rk_ref,
                ywin_ref, bg_ref, ylru_ref, cg_ref, ydiff_ref, dg_ref, o_ref, *, layer):
    lrow = slice(layer, layer + 1)
    ones = _head_ones(W_BR)
    pa = pa_ref[0]
    r, k, v, ag = pa[:, 0:256], pa[:, 256:512], pa[:, 512:768], pa[:, 896:1152]
    y = yf_ref[0] + yb_ref[0]
    mu = _mm2(y, ones) * (1.0 / HEAD)
    yc = y - mu
    var = _mm2(yc * yc, ones) * (1.0 / HEAD)
    ya = yc * lax.rsqrt(var + GN_EPS) * gng_ref[lrow] + gnb_ref[lrow]
    ya = ya + _mm2(r * k * rk_ref[lrow], ones) * v
    mix = jnp.concatenate([ya * _silu(ag), ywin_ref[0] * _silu(bg_ref[0]),
                           ylru_ref[0] * _silu(cg_ref[0]), ydiff_ref[0] * _silu(dg_ref[0])], axis=-1)
    o = jnp.dot(mix.astype(BF16), w_ref[0], preferred_element_type=F32)
    o = o * lax.rsqrt(jnp.mean(o * o, -1, keepdims=True) + NORM_EPS) * gpost_ref[lrow]
    o_ref[0] = x_ref[0] + mod_ref[0][:, 2 * D_MODEL:] * o


def _out(x, mod_l, g_post, w_out_bf, layer, latent, pa, yf, yb, gn_g, gn_b, r_k, ywin, pb, ylru, pc, ydiff, pd):
    b, t, _ = x.shape
    tm = ROW_TILE
    mod_idx = (lambda i, j: (1 + i, 0, 0)) if latent else (lambda i, j: (0, 0, 0))
    rows = lambda w, c=0: pl.BlockSpec((1, tm, w), lambda i, j: (i, j, c))
    vec = pl.BlockSpec(gn_g.shape, lambda i, j: (0, 0))
    in_specs = [rows(D_MODEL),
                pl.BlockSpec((1, 1, 3 * D_MODEL), mod_idx),
                pl.BlockSpec(g_post.shape, lambda i, j: (0, 0)),
                pl.BlockSpec((1, D_MODEL, D_MODEL), lambda i, j: (layer, 0, 0)),
                rows(WA), rows(W_BR), rows(W_BR), vec, vec, vec,
                rows(W_BR), rows(W_BR, 2), rows(W_BR), rows(W_BR, 1), rows(W_BR), rows(W_BR, 3)]
    return pl.pallas_call(
        functools.partial(_out_kernel, layer=layer),
        grid=(b, t // tm),
        in_specs=in_specs,
        out_specs=rows(D_MODEL),
        out_shape=jax.ShapeDtypeStruct((b, t, D_MODEL), F32),
        compiler_params=_params(("parallel", "parallel")),
        name="out",
    )(x, mod_l, g_post, w_out_bf, pa, yf, yb, gn_g, gn_b, r_k, ywin, pb, ylru, pc, ydiff, pd)


def _layer(x, mod_l, layer, lam_init, wts, cache, tables):
    latent = cache is not None
    pa, pb, pc, pd, *ctx_t = _project(x, mod_l, wts['g_pre'], wts['w_in_bf'], layer, tables)
    rw = _rwkv(pa, wts['rwkv_w0'], wts['rwkv_w_up'], wts['rwkv_a0'], wts['rwkv_a_up'], wts['rwkv_k_k'],
               wts['rwkv_k_a'], cache['rwkv'] if latent else None, layer)
    lr = _lru(pc, wts['lru_conv_w'], wts['lru_conv_b'], wts['lru_wa'], wts['lru_ba'], wts['lru_wx'],
              wts['lru_bx'], wts['lru_lambda'], cache['lru'] if latent else None, layer)
    ywin = _win(pb, wts['win_sink'], cache['win_k'] if latent else None,
                cache['win_v'] if latent else None, layer)
    ydiff = _diff(pd, wts['diff_lambda'], wts['diff_subln_g'],
                  cache['diff_k'] if latent else None, cache['diff_v'] if latent else None, layer, lam_init)
    y = _out(x, mod_l, wts['g_post'], wts['w_out_bf'], layer, latent, pa, rw[0], rw[1],
             wts['rwkv_gn_g'], wts['rwkv_gn_b'], wts['rwkv_r_k'], ywin, pb, lr[0], pc, ydiff, pd)
    new_cache = None if latent else (ctx_t, rw[2], lr[1])
    return y, new_cache


def kernel(x_prompt, x_sample, c, cache_win_k, cache_win_v, cache_diff_k, cache_diff_v, state_rwkv, state_lru,
           c_ctx, w_mod, b_mod, g_pre, g_post, w_in, w_out,
           rwkv_w0, rwkv_w_up, rwkv_a0, rwkv_a_up, rwkv_k_k, rwkv_k_a, rwkv_r_k, rwkv_gn_g, rwkv_gn_b,
           win_sink, lru_conv_w, lru_conv_b, lru_wa, lru_ba, lru_wx, lru_bx, lru_lambda,
           diff_lambda, diff_subln_g):
    n_b, seq = x_prompt.shape[:2]
    n_dec, dec_seq = x_sample.shape[:2]
    past = cache_win_k.shape[2]
    wts = dict(g_pre=g_pre, g_post=g_post, w_in_bf=w_in.astype(BF16), w_out_bf=w_out.astype(BF16),
               rwkv_w0=rwkv_w0, rwkv_w_up=rwkv_w_up, rwkv_a0=rwkv_a0, rwkv_a_up=rwkv_a_up,
               rwkv_k_k=rwkv_k_k, rwkv_k_a=rwkv_k_a, rwkv_r_k=rwkv_r_k.reshape(DEPTH, W_BR), rwkv_gn_g=rwkv_gn_g,
               rwkv_gn_b=rwkv_gn_b, win_sink=win_sink, lru_conv_w=lru_conv_w, lru_conv_b=lru_conv_b,
               lru_wa=lru_wa, lru_ba=lru_ba, lru_wx=lru_wx, lru_bx=lru_bx, lru_lambda=lru_lambda,
               diff_lambda=diff_lambda, diff_subln_g=diff_subln_g)
    fm = lambda a: jnp.moveaxis(a, 2, -1).reshape(n_dec, DEPTH, -1, past)
    cache = dict(win_k=fm(cache_win_k), win_v=fm(cache_win_v), diff_k=fm(cache_diff_k), diff_v=fm(cache_diff_v),
                 rwkv=state_rwkv, lru=state_lru)
    cvec = jnp.concatenate([c_ctx[None], c, jnp.zeros((8 - 1 - n_dec, D_MODEL), F32)], axis=0)
    mod = _modulation(cvec, w_mod, b_mod)
    cos_b, sin_b = _rope_tables(dec_seq, HEAD, 384)
    cos_d, sin_d = _rope_tables(dec_seq, DQ_D, 512)
    tables = (cos_b, sin_b, cos_d, sin_d)

    y_p, y_s = x_prompt, x_sample
    ctx = []
    for l in range(DEPTH):
        lam_init = 0.8 - 0.6 * math.exp(-0.3 * l)
        mod_l = mod[l].reshape(8, 1, 3 * D_MODEL)
        y_p, nc = _layer(y_p, mod_l, l, lam_init, wts, None, None)
        ctx.append(nc)
        y_s, _ = _layer(y_s, mod_l, l, lam_init, wts, cache, tables)
    stack = lambda f: jnp.stack([f(ct) for ct in ctx], axis=1)
    tm_ = lambda i, dims: jnp.moveaxis(stack(lambda ct: ct[0][i]).reshape((n_b, DEPTH) + dims + (seq,)), -1, 2)
    new_win_k = tm_(0, (2, HEAD))
    new_win_v = tm_(1, (2, HEAD))
    new_diff_k = tm_(2, (4, 2, DQ_D))
    new_diff_v = tm_(3, (4, HEAD))
    new_state_rwkv = stack(lambda ct: ct[1])
    new_state_lru = stack(lambda ct: ct[2])
    return (y_p, y_s, new_win_k, new_win_v, new_diff_k, new_diff_v, new_state_rwkv, new_state_lru)
```

```python
import functools
import math

import numpy as np
import jax
import jax.numpy as jnp
from jax import lax
from jax.experimental import pallas as pl
from jax.experimental.pallas import tpu as pltpu

F32 = jnp.float32
BF16 = jnp.bfloat16
HI = lax.Precision.HIGHEST

D_MODEL = 1024
DEPTH = 2
GRID_W = 64
ROPE_BASE = 10000.0
NORM_EPS = 1e-6
NEG_INF = -1e30
LOG2_E = math.log2(math.e)
GN_EPS = 64e-5
C_RG = 8.0
W_BR = 256
HEAD = 64
SUBLANES = 8
LORA = 64
DQ_D = 32
WINDOW = 128
Q_BLK = 128
P_TOTAL = 3456
WA, WB, WC, WD = 1152, 768, 512, 1024
CHUNK = 64
RWKV_BATCH = 4
ROW_TILE = 256
VMEM_LIMIT = 58 * 1024 * 1024


_NN = (((1,), (0,)), ((), ()))
_NT = (((1,), (1,)), ((), ()))
_TN = (((0,), (0,)), ((), ()))


def _dot_hi(a, b):
    return jnp.dot(a, b, precision=HI, preferred_element_type=F32)


def _bf(x):
    return x.astype(BF16)


def _dotf(x, y, dims=_NN):
    return lax.dot_general(x, y, dims, preferred_element_type=F32)


def _split(x):
    hi = x.astype(BF16)
    return hi, (x - hi.astype(F32)).astype(BF16)


def _mm3(a, b, dims=_NN):
    return _dotf(a[0], b[0], dims) + (_dotf(a[0], b[1], dims) + _dotf(a[1], b[0], dims))


def _mm2(x, m01, left=False):
    hi, lo = _split(x)
    if left:
        return _dotf(m01, hi) + _dotf(m01, lo)
    return _dotf(hi, m01) + _dotf(lo, m01)


def _dot_bf(a, b):
    return jnp.dot(a.astype(BF16), b.astype(BF16), preferred_element_type=F32)


def _dot_nt_bf(a, b):
    return lax.dot_general(a.astype(BF16), b.astype(BF16), (((1,), (1,)), ((), ())),
                           preferred_element_type=F32)


def _sigmoid(x):
    return 1.0 / (1.0 + jnp.exp(-x))


def _silu(x):
    return x * _sigmoid(x)


def _softplus(x):
    return jnp.maximum(x, 0.0) + jnp.log1p(jnp.exp(-jnp.abs(x)))


def _head_ones(n):
    r = lax.broadcasted_iota(jnp.int32, (n, n), 0) // HEAD
    c = lax.broadcasted_iota(jnp.int32, (n, n), 1) // HEAD
    return jnp.where(r == c, 1.0, 0.0).astype(BF16)


def _params(sem):
    return pltpu.CompilerParams(dimension_semantics=sem, vmem_limit_bytes=VMEM_LIMIT)


def _mod_kernel(c_ref, w_ref, b_ref, o_ref):
    o_ref[0] = _mm3(_split(_silu(c_ref[...])), _split(w_ref[0])) + b_ref[0]


def _modulation(cvec, w_mod, b_mod):
    n_l = w_mod.shape[0]
    tn = 512
    return pl.pallas_call(
        _mod_kernel,
        grid=(n_l, 3 * D_MODEL // tn),
        in_specs=[pl.BlockSpec((8, D_MODEL), lambda l, j: (0, 0)),
                  pl.BlockSpec((1, D_MODEL, tn), lambda l, j: (l, 0, j)),
                  pl.BlockSpec((1, 1, tn), lambda l, j: (l, 0, j))],
        out_specs=pl.BlockSpec((1, 8, tn), lambda l, j: (l, 0, j)),
        out_shape=jax.ShapeDtypeStruct((n_l, 8, 3 * D_MODEL), F32),
        compiler_params=_params(("parallel", "parallel")),
        name="mod",
    )(cvec, w_mod, b_mod.reshape(n_l, 1, 3 * D_MODEL))


def _rope(x, cos, sin_signed, off):
    w = x.shape[-1]
    lane = lax.broadcasted_iota(jnp.int32, x.shape, 1)
    first = (lane % (2 * off)) < off
    partner = jnp.where(first, pltpu.roll(x, w - off, 1), pltpu.roll(x, off, 1))
    return x * cos + partner * sin_signed


def _proj_kernel(*refs, latent, layer):
    if latent:
        x_ref, mod_ref, g_ref, w_ref, cb_ref, sb_ref, cd_ref, sd_ref, oa, ob, oc, od = refs
    else:
        x_ref, mod_ref, g_ref, w_ref, oa, ob, oc, od, o_wk, o_wv, o_dk, o_dv = refs
    x = x_ref[0]
    y = x * lax.rsqrt(jnp.mean(x * x, -1, keepdims=True) + NORM_EPS) * g_ref[layer:layer + 1]
    m = mod_ref[0]
    h = y * (1.0 + m[:, D_MODEL:2 * D_MODEL]) + m[:, :D_MODEL]
    p = jnp.dot(h.astype(BF16), w_ref[0], preferred_element_type=F32)
    oa[0] = p[:, :WA]
    pb = p[:, WA:WA + WB]
    pd = p[:, WA + WB + WC:]
    if latent:
        ob[0, :, :384] = _rope(pb[:, :384], cb_ref[...], sb_ref[...], 16)
        ob[0, :, 384:] = pb[:, 384:]
        od[0, :, :512] = _rope(pd[:, :512], cd_ref[...], sd_ref[...], 8)
        od[0, :, 512:] = pd[:, 512:]
    else:
        ob[0] = pb
        od[0] = pd
        o_wk[0] = pb[:, 256:384].T
        o_wv[0] = pb[:, 384:512].T
        o_dk[0] = pd[:, 256:512].T
        o_dv[0] = pd[:, 512:768].T
    oc[0] = p[:, WA + WB:WA + WB + WC]


def _project(x, mod_l, g_pre, w_in_bf, layer, tables):
    b, t, _ = x.shape
    latent = tables is not None
    tm = ROW_TILE
    mod_idx = (lambda i, j: (1 + i, 0, 0)) if latent else (lambda i, j: (0, 0, 0))
    in_specs = [pl.BlockSpec((1, tm, D_MODEL), lambda i, j: (i, j, 0)),
                pl.BlockSpec((1, 1, 3 * D_MODEL), mod_idx),
                pl.BlockSpec(g_pre.shape, lambda i, j: (0, 0)),
                pl.BlockSpec((1, D_MODEL, P_TOTAL), lambda i, j: (layer, 0, 0))]
    args = [x, mod_l, g_pre, w_in_bf]
    if latent:
        for tab in tables:
            in_specs.append(pl.BlockSpec((tm, tab.shape[1]), lambda i, j: (j, 0)))
            args.append(tab)
    widths = (WA, WB, WC, WD)
    out_specs = [pl.BlockSpec((1, tm, w), lambda i, j: (i, j, 0)) for w in widths]
    out_shape = [jax.ShapeDtypeStruct((b, t, w), F32) for w in widths]
    if not latent:
        for w in (128, 128, W_BR, W_BR):
            out_specs.append(pl.BlockSpec((1, w, tm), lambda i, j: (i, 0, j)))
            out_shape.append(jax.ShapeDtypeStruct((b, w, t), F32))
    return pl.pallas_call(
        functools.partial(_proj_kernel, latent=latent, layer=layer),
        grid=(b, t // tm),
        in_specs=in_specs,
        out_specs=out_specs,
        out_shape=out_shape,
        compiler_params=_params(("parallel", "parallel")),
        name="proj",
    )(*args)


def _rope_tables(t, head_dim, n_lanes):
    half = head_dim // 2
    quarter = half // 2
    pos = np.arange(t)
    row = (pos // GRID_W).astype(np.float32)
    col = (pos % GRID_W).astype(np.float32)
    inv = np.float32(ROPE_BASE) ** (-np.arange(0, half, 2, dtype=np.float32) / np.float32(half))
    lane = np.arange(n_lanes) % head_dim
    in_half = lane % half
    p = np.where((lane < half)[None, :], row[:, None], col[:, None])
    ang = (p * inv[in_half % quarter][None, :]).astype(np.float64)
    sign = np.where(in_half < quarter, -1.0, 1.0)[None, :]
    return jnp.asarray(np.cos(ang), F32), jnp.asarray(np.sin(ang) * sign, F32)


_PREP_FIELDS = (('ar', 2, 1, BF16), ('nm', 1, 1, BF16), ('mm', 1, 1, BF16), ('pq', 1, 2, BF16),
                ('v', 1, 1, BF16), ('bkt', 1, 2, BF16), ('gct', 1, 1, F32))


def _prep_scratch(slots, n_streams):
    return [pltpu.VMEM((slots, n_streams, rows * CHUNK, lanes * W_BR), dt) for _, rows, lanes, dt in _PREP_FIELDS]


class _SlotView:
    def __init__(self, ref, slot):
        self.ref, self.slot, self.shape = ref, slot, ref.shape[1:]

    def __getitem__(self, idx):
        return self.ref[(self.slot,) + (idx if isinstance(idx, tuple) else (idx,))]

    def __setitem__(self, idx, val):
        self.ref[(self.slot,) + (idx if isinstance(idx, tuple) else (idx,))] = val


def _head_transpose(x):
    xt = x.T
    return jnp.concatenate([xt[h * HEAD:(h + 1) * HEAD] for h in range(x.shape[1] // HEAD)], axis=1)


def _expand(x, same):
    return jnp.concatenate([x] * (same.shape[0] // x.shape[0]), axis=0) * same


def _rwkv_prepare(xs, dirs, w0_ref, wup_ref, a0_ref, aup_ref, k_k, k_a, m_ref, same_ref, out):
    c = xs[0].shape[0]
    ones = _head_ones(W_BR)
    ti = lax.broadcasted_iota(jnp.int32, (c, c), 0)
    si = lax.broadcasted_iota(jnp.int32, (c, c), 1)
    cums = [jnp.where(si <= ti, 1.0, 0.0).astype(BF16), jnp.where(si >= ti, 1.0, 0.0).astype(BF16)]
    same = same_ref[...]

    st = []
    for xc, d in zip(xs, dirs):
        r, k, v = xc[:, 0:256], xc[:, 256:512], xc[:, 512:768]
        wd, ad = xc[:, 768:832], xc[:, 832:896]
        kk = k * k_k
        kk = kk * lax.rsqrt(_mm2(kk * kk, ones) + 1e-12)
        z = w0_ref[d:d + 1] + _mm3(_split(jnp.tanh(wd)), _split(wup_ref[d]))
        e = jnp.exp(-_softplus(-z) - 0.5)
        a = _sigmoid(a0_ref[d:d + 1] + _mm3(_split(ad), _split(aup_ref[d])))
        st.append(dict(r=r, v=v, e=e, kd=k * (1.0 + (a - 1.0) * k_a), alpha=-kk, beta=kk * a))
    yield
    for s, d in zip(st, dirs):
        l_incl = _mm2(s['e'], cums[d], left=True)
        l_tot = jnp.sum(s['e'], axis=0, keepdims=True)
        grow = jnp.exp(l_incl)
        tail = jnp.exp(l_incl - l_tot)
        s.update(ar=jnp.concatenate([_bf(s['alpha'] * jnp.exp(s['e'] - l_incl)),
                                     _bf(s['r'] * jnp.exp(-l_incl))], axis=0),
                 b_t=_bf(s['beta'] * grow), k_t=_bf(s['kd'] * grow),
                 b_h=s['beta'] * tail, k_h=s['kd'] * tail,
                 g_c=jnp.broadcast_to(jnp.exp(-l_tot), (c, W_BR)))
    yield
    for g, (s, d) in enumerate(zip(st, dirs)):
        out['ar'][g] = s['ar']
        rhs_t = jnp.concatenate([_expand(s['b_t'], same), _expand(s['k_t'], same)], axis=0)
        g4 = _dotf(s['ar'], rhs_t, _NT)
        strict = m_ref[_M_STRICT + 2 * d]
        incl = m_ref[_M_INCL + 2 * d]
        out['nm'][g] = _bf(g4[:c, :W_BR]) * strict
        out['mm'][g] = _bf(g4[:c, W_BR:]) * strict
        out['pq'][g, :, :W_BR] = _bf(g4[c:, :W_BR]) * incl
        out['pq'][g, :, W_BR:] = _bf(g4[c:, W_BR:]) * incl
    yield
    for g, s in enumerate(st):
        out['v'][g] = _bf(s['v'])
        out['bkt'][g] = jnp.concatenate([_bf(_head_transpose(s['b_h'])), _bf(_head_transpose(s['k_h']))], axis=1)
        out['gct'][g] = _head_transpose(s['g_c'])


def _rwkv_advance(p, m_ref, same_ref, z_ref, y_stores):
    n_streams, c = p['nm'].shape[0], p['nm'].shape[1]
    gs = range(n_streams)
    same = same_ref[...]
    ex = lambda x: _expand(_bf(x), same)
    invs = [(p['nm'][g] * m_ref[_M_PAIR]).astype(F32) + m_ref[_M_EYE].astype(F32) for g in gs]
    for lvl in range(int(math.log2(c)) - 1):
        inv_x = [ex(t) for t in invs]
        half = [_dotf(p['nm'][g] * m_ref[_M_OFF + lvl], inv_x[g]) for g in gs]
        invs = [invs[g] + _dotf(_bf(invs[g]), ex(half[g])) for g in gs]
        yield
    z0 = [z_ref[g] for g in gs]
    as0 = [_dotf(p['ar'][g], ex(z0[g])) for g in gs]
    vx = [_expand(p['v'][g], same) for g in gs]
    rhs = [as0[g][:c] + _dotf(p['mm'][g], vx[g]) for g in gs]
    yield
    uv = [jnp.concatenate([ex(_dotf(_bf(invs[g]), ex(rhs[g]))), vx[g]], axis=0) for g in gs]
    yield
    for g in gs:
        z_ref[g] = z0[g] * p['gct'][g] + _dotf(p['bkt'][g], uv[g])
    yield
    for g in gs:
        y_stores[g](as0[g][c:] + _dotf(p['pq'][g], uv[g]))


def _interleave(*gens):
    live = list(gens)
    while live:
        for gen in list(live):
            try:
                next(gen)
            except StopIteration:
                live.remove(gen)


_M_STRICT, _M_INCL, _M_EYE, _M_PAIR, _M_OFF = 0, 1, 4, 5, 6


def _rwkv_masks(c, n_h):
    t = np.arange(c)[:, None]
    s = np.arange(c)[None, :]
    masks = [s < t, s <= t, s > t, s >= t, s == t, ((t // 2) == (s // 2)) & (t != s)]
    b = 2
    while b < c:
        masks.append(((t // (2 * b)) == (s // (2 * b))) & ((t // b) != (s // b)))
        b *= 2
    masks = np.tile(np.stack(masks).astype(np.float32), (1, 1, n_h))
    r = np.arange(n_h * c)[:, None] // c
    l = np.arange(n_h * HEAD)[None, :] // HEAD
    return jnp.asarray(masks, BF16), jnp.asarray((r == l).astype(np.float32), BF16)


def _rwkv_kernel(*refs, latent, n_b, layer):
    (xf_ref, xb_ref, xfn_ref, xbn_ref, w0_ref, wup_ref, a0_ref, aup_ref, kk_ref, ka_ref,
     m_ref, same_ref) = refs[:12]
    if latent:
        s0_ref, yf_ref, yb_ref = refs[12:15]
    else:
        yf_ref, yb_ref, st_ref = refs[12:15]
    scr = refs[15:]
    z_scr = scr[0]
    names = [f[0] for f in _PREP_FIELDS]
    ci = pl.program_id(1)
    step = pl.program_id(0) * pl.num_programs(1) + ci
    view = lambda refs_, slot: {k: _SlotView(r, slot) for k, r in zip(names, refs_)}
    cur = view(scr[1:1 + len(names)], step % 2)
    nxt = view(scr[1:1 + len(names)], 1 - step % 2)
    mid = view(scr[1 + len(names):], 0)
    n_h = W_BR // HEAD
    c = CHUNK
    head = lambda h: slice(h * HEAD, (h + 1) * HEAD)
    streams = [(bi, d) for bi in range(n_b) for d in range(2)]
    dirs = [d for _, d in streams]
    params = (w0_ref[layer], wup_ref[0], a0_ref[layer], aup_ref[0], kk_ref[layer:layer + 1],
              ka_ref[layer:layer + 1], m_ref, same_ref)
    first = lambda d: slice(0, c) if d == 0 else slice(c, 2 * c)
    second = lambda d: slice(c, 2 * c) if d == 0 else slice(0, c)
    x_refs, xn_refs, y_refs = (xf_ref, xb_ref), (xfn_ref, xbn_ref), (yf_ref, yb_ref)

    def y_store(bi, d, rows):
        def store(y):
            y_refs[d][bi, rows, :] = y
        return store

    @pl.when(ci == 0)
    def _():
        for g, (bi, d) in enumerate(streams):
            if latent:
                z_scr[g] = jnp.concatenate([s0_ref[bi, 0, d, h].T for h in range(n_h)], axis=1)
            else:
                z_scr[g] = jnp.zeros(z_scr.shape[1:], F32)

    @pl.when(step == 0)
    def _():
        _interleave(_rwkv_prepare([x_refs[d][bi, first(d), :] for bi, d in streams], dirs, *params, cur))

    _interleave(_rwkv_advance(cur, m_ref, same_ref, z_scr, [y_store(bi, d, first(d)) for bi, d in streams]),
                _rwkv_prepare([x_refs[d][bi, second(d), :] for bi, d in streams], dirs, *params, mid))
    _interleave(_rwkv_advance(mid, m_ref, same_ref, z_scr, [y_store(bi, d, second(d)) for bi, d in streams]),
                _rwkv_prepare([xn_refs[d][bi, first(d), :] for bi, d in streams], dirs, *params, nxt))

    if not latent:
        @pl.when(ci == pl.num_programs(1) - 1)
        def _():
            for g, (bi, d) in enumerate(streams):
                z = z_scr[g]
                for h in range(n_h):
                    st_ref[bi, d, h] = z[:, head(h)].T


def _rwkv(pa, w0, wup, a0, aup, k_k, k_a, state, layer):
    b, t, _ = pa.shape
    latent = state is not None
    ns = t // (2 * CHUNK)
    n_h = W_BR // HEAD
    n_b = RWKV_BATCH
    masks, same = _rwkv_masks(CHUNK, n_h)
    full = lambda shape: pl.BlockSpec(shape, lambda i, j: (0,) * len(shape))
    one_layer = lambda shape: pl.BlockSpec((1,) + shape[1:], lambda i, j: (layer,) + (0,) * (len(shape) - 1))
    rows = lambda w, idx: pl.BlockSpec((n_b, 2 * CHUNK, w), idx)
    nxt_group = lambda i, j: jnp.minimum(i + (j + 1) // ns, b // n_b - 1)
    in_specs = [rows(WA, lambda i, j: (i, j, 0)),
                rows(WA, lambda i, j: (i, ns - 1 - j, 0)),
                rows(WA, lambda i, j: (nxt_group(i, j), (j + 1) % ns, 0)),
                rows(WA, lambda i, j: (nxt_group(i, j), (2 * ns - 2 - j) % ns, 0)),
                full(w0.shape), one_layer(wup.shape), full(a0.shape), one_layer(aup.shape),
                full(k_k.shape), full(k_a.shape), full(masks.shape), full(same.shape)]
    args = [pa, pa, pa, pa, w0, wup, a0, aup, k_k, k_a, masks, same]
    out_specs = [rows(W_BR, lambda i, j: (i, j, 0)), rows(W_BR, lambda i, j: (i, ns - 1 - j, 0))]
    out_shape = [jax.ShapeDtypeStruct((b, t, W_BR), F32)] * 2
    if latent:
        in_specs.append(pl.BlockSpec((n_b, 1, 2, n_h, HEAD, HEAD), lambda i, j: (i, layer, 0, 0, 0, 0)))
        args.append(state)
    else:
        out_specs.append(pl.BlockSpec((n_b, 2, n_h, HEAD, HEAD), lambda i, j: (i, 0, 0, 0, 0)))
        out_shape.append(jax.ShapeDtypeStruct((b, 2, n_h, HEAD, HEAD), F32))
    return pl.pallas_call(
        functools.partial(_rwkv_kernel, latent=latent, n_b=n_b, layer=layer),
        grid=(b // n_b, ns),
        in_specs=in_specs,
        out_specs=out_specs,
        out_shape=out_shape,
        scratch_shapes=([pltpu.VMEM((2 * n_b, CHUNK, W_BR), F32)]
                        + _prep_scratch(2, 2 * n_b) + _prep_scratch(1, 2 * n_b)),
        compiler_params=_params(("arbitrary", "arbitrary")),
        name="rwkv",
    )(*args)


def _lru_kernel(*refs, latent, layer):
    if latent:
        (x_ref, cw_ref, cb_ref, wa_ref, ba_ref, wx_ref, bx_ref, lam_ref, h0_ref, y_ref,
         a_scr, h_scr) = refs
    else:
        (x_ref, cw_ref, cb_ref, wa_ref, ba_ref, wx_ref, bx_ref, lam_ref, y_ref, st_ref,
         a_scr, h_scr) = refs
    x = x_ref[0, :, :W_BR]
    t = x.shape[0]
    row = lax.broadcasted_iota(jnp.int32, x.shape, 0)

    def shift_dn(z, k, fill):
        return jnp.where(row >= k, pltpu.roll(z, k, 0), fill)

    def shift_up(z, k, fill):
        return jnp.where(row < t - k, pltpu.roll(z, t - k, 0), fill)

    cw = cw_ref[layer]
    xc = (cb_ref[layer:layer + 1] + shift_dn(x, 2, 0.0) * cw[0:1] + shift_dn(x, 1, 0.0) * cw[1:2]
          + x * cw[2:3] + shift_up(x, 1, 0.0) * cw[3:4])
    xs = _split(xc)
    sub = row % SUBLANES
    n_blk = W_BR // HEAD
    zero = jnp.zeros((HEAD, HEAD), F32)

    def block_diag(w_ref, d):
        return jnp.concatenate([jnp.concatenate([w_ref[0, d, n] if m == n else zero for m in range(n_blk)], axis=1)
                                for n in range(n_blk)], axis=0)

    for d in range(2):
        gate_a = _sigmoid(_mm3(xs, _split(block_diag(wa_ref, d))) + ba_ref[layer, d:d + 1])
        gate_x = _sigmoid(_mm3(xs, _split(block_diag(wx_ref, d))) + bx_ref[layer, d:d + 1])
        log_a = -C_RG * gate_a * _softplus(-lam_ref[layer, d:d + 1])
        a = jnp.exp(log_a)
        u = jnp.sqrt(-jnp.tanh(log_a) * (a * a + 1.0)) * (gate_x * xc)
        k = 1
        while k < SUBLANES:
            if d == 0:
                keep = sub >= k
                sh = lambda z: pltpu.roll(z, k, 0)
            else:
                keep = sub < SUBLANES - k
                sh = lambda z: pltpu.roll(z, t - k, 0)
            u = a * jnp.where(keep, sh(u), 0.0) + u
            a = a * jnp.where(keep, sh(a), 1.0)
            k *= 2
        a_scr[d] = a
        h_scr[d] = u

    n_grp = t // SUBLANES
    if latent:
        carry0 = (h0_ref[0, 0, 0:1], h0_ref[0, 0, 1:2])
    else:
        carry0 = (jnp.zeros((1, W_BR), F32),) * 2

    def chain(i, carry):
        cf, cb = carry
        rf = pl.ds(pl.multiple_of(i * SUBLANES, SUBLANES), SUBLANES)
        rb = pl.ds(pl.multiple_of((n_grp - 1 - i) * SUBLANES, SUBLANES), SUBLANES)
        hf = h_scr[0, rf, :] + a_scr[0, rf, :] * cf
        hb = h_scr[1, rb, :] + a_scr[1, rb, :] * cb
        h_scr[0, rf, :] = hf
        h_scr[1, rb, :] = hb
        return hf[SUBLANES - 1:SUBLANES], hb[0:1]

    cf, cb = lax.fori_loop(0, n_grp, chain, carry0, unroll=4)
    y_ref[0] = h_scr[0] + h_scr[1]
    if not latent:
        st_ref[0] = jnp.concatenate([cf, cb], axis=0)


def _lru(pc, cw, cb, wa, ba, wx, bx, lam, state, layer):
    b, t, _ = pc.shape
    latent = state is not None
    full = lambda shape: pl.BlockSpec(shape, lambda i: (0,) * len(shape))
    one_layer = lambda shape: pl.BlockSpec((1,) + shape[1:], lambda i: (layer,) + (0,) * (len(shape) - 1))
    in_specs = [pl.BlockSpec((1, t, WC), lambda i: (i, 0, 0)),
                full(cw.shape), full(cb.shape), one_layer(wa.shape), full(ba.shape),
                one_layer(wx.shape), full(bx.shape), full(lam.shape)]
    args = [pc, cw, cb, wa, ba, wx, bx, lam]
    out_specs = [pl.BlockSpec((1, t, W_BR), lambda i: (i, 0, 0))]
    out_shape = [jax.ShapeDtypeStruct((b, t, W_BR), F32)]
    if latent:
        in_specs.append(pl.BlockSpec((1, 1, 2, W_BR), lambda i: (i, layer, 0, 0)))
        args.append(state)
    else:
        out_specs.append(pl.BlockSpec((1, 2, W_BR), lambda i: (i, 0, 0)))
        out_shape.append(jax.ShapeDtypeStruct((b, 2, W_BR), F32))
    return pl.pallas_call(
        functools.partial(_lru_kernel, latent=latent, layer=layer),
        grid=(b,),
        in_specs=in_specs,
        out_specs=out_specs,
        out_shape=out_shape,
        scratch_shapes=[pltpu.VMEM((2, t, W_BR), F32), pltpu.VMEM((2, t, W_BR), F32)],
        compiler_params=_params(("parallel",)),
        name="lru",
    )(*args)


def _win_kernel(*refs, latent, layer):
    if latent:
        sink_ref, q_ref, kp_ref, kc_ref, kn_ref, ck_ref, cv_ref, o_ref = refs
        j = pl.program_id(1)
        nq = pl.num_programs(1)
        qi = lax.broadcasted_iota(jnp.int32, (Q_BLK, Q_BLK), 0)
        ki = lax.broadcasted_iota(jnp.int32, (Q_BLK, Q_BLK), 1)
        q = q_ref[0]
        ctx_k, ctx_v = _bf(ck_ref[0, 0]), _bf(cv_ref[0, 0])
        pieces = [(_bf(r[0, :, :128]), _bf(r[0, :, 128:]), m, False) for r, m in
                  ((kp_ref, (ki >= qi) & (j > 0)), (kc_ref, None), (kn_ref, (ki <= qi) & (j < nq - 1)))]
        pieces += [(ctx_k[:, i:i + Q_BLK], ctx_v[:, i:i + Q_BLK], None, True)
                   for i in range(0, ctx_k.shape[1], Q_BLK)]
    else:
        sink_ref, x_ref, o_ref = refs
        q = x_ref[0, :, :256]
        k, v = _bf(x_ref[0, :, 256:384]), _bf(x_ref[0, :, 384:512])
        pieces = [(k[i:i + Q_BLK], v[i:i + Q_BLK], None, False) for i in range(0, k.shape[0], Q_BLK)]
    q = _bf(q * (HEAD ** -0.5 * LOG2_E))
    outs = []
    for h in range(2):
        hs = slice(h * HEAD, (h + 1) * HEAD)
        for g in range(2):
            qg = q[:, (2 * h + g) * HEAD:(2 * h + g + 1) * HEAD]
            sink = sink_ref[layer, 2 * h + g] * LOG2_E
            logits = []
            for kp, _, mask, fm in pieces:
                s = _dotf(qg, kp[hs]) if fm else _dotf(qg, kp[:, hs], _NT)
                logits.append(s if mask is None else jnp.where(mask, s, NEG_INF))
            mx = jnp.maximum(jnp.max(functools.reduce(jnp.maximum, logits), axis=-1, keepdims=True), sink)
            es = [jnp.exp2(s - mx) for s in logits]
            den = jnp.sum(functools.reduce(lambda a, b: a + b, es), axis=-1, keepdims=True) + jnp.exp2(sink - mx)
            acc = functools.reduce(lambda a, b: a + b,
                                   [_dotf(_bf(e), vp[hs], _NT) if fm else _dotf(_bf(e), vp[:, hs])
                                    for e, (_, vp, _, fm) in zip(es, pieces)])
            outs.append(acc / den)
    o_ref[0] = jnp.concatenate(outs, axis=-1)


def _win(pb, sink, cache_k, cache_v, layer):
    b, t, _ = pb.shape
    latent = cache_k is not None
    smem = pl.BlockSpec(memory_space=pltpu.SMEM)
    if latent:
        nq = t // Q_BLK
        grid = (b, nq)
        in_specs = [smem,
                    pl.BlockSpec((1, Q_BLK, 256), lambda i, j: (i, j, 0)),
                    pl.BlockSpec((1, Q_BLK, 256), lambda i, j: (i, jnp.maximum(j - 1, 0), 1)),
                    pl.BlockSpec((1, Q_BLK, 256), lambda i, j: (i, j, 1)),
                    pl.BlockSpec((1, Q_BLK, 256), lambda i, j: (i, jnp.minimum(j + 1, nq - 1), 1)),
                    pl.BlockSpec((1, 1) + cache_k.shape[2:], lambda i, j: (i, layer, 0, 0)),
                    pl.BlockSpec((1, 1) + cache_v.shape[2:], lambda i, j: (i, layer, 0, 0))]
        args = [sink, pb, pb, pb, pb, cache_k, cache_v]
        out_spec = pl.BlockSpec((1, Q_BLK, W_BR), lambda i, j: (i, j, 0))
        sem = ("parallel", "parallel")
    else:
        grid = (b,)
        in_specs = [smem, pl.BlockSpec((1, t, WB), lambda i: (i, 0, 0))]
        args = [sink, pb]
        out_spec = pl.BlockSpec((1, t, W_BR), lambda i: (i, 0, 0))
        sem = ("parallel",)
    return pl.pallas_call(
        functools.partial(_win_kernel, latent=latent, layer=layer),
        grid=grid,
        in_specs=in_specs,
        out_specs=out_spec,
        out_shape=jax.ShapeDtypeStruct((b, t, W_BR), F32),
        compiler_params=_params(sem),
        name="win",
    )(*args)


def _diff_kernel(*refs, latent, lam_init, layer):
    if latent:
        lam_ref, g_ref, q_ref, k_ref, v_ref, ck_ref, cv_ref, o_ref = refs
    else:
        lam_ref, g_ref, q_ref, k_ref, v_ref, o_ref = refs
    lp = lam_ref[layer]
    lam = (jnp.exp(jnp.sum(lp[0:1] * lp[1:2], axis=-1, keepdims=True))
           - jnp.exp(jnp.sum(lp[2:3] * lp[3:4], axis=-1, keepdims=True)) + lam_init)
    q = _bf(q_ref[0] * (DQ_D ** -0.5 * LOG2_E))
    key_sets = [(_bf(k_ref[0]), _bf(v_ref[0]), False)]
    if latent:
        key_sets.append((_bf(ck_ref[0, 0]), _bf(cv_ref[0, 0]), True))
    lane = lax.broadcasted_iota(jnp.int32, (1, W_BR), 1)
    out = jnp.zeros(q.shape, F32)
    for h in range(W_BR // HEAD):
        probs = []
        for m in range(2):
            qm = jnp.where(lane // DQ_D == 2 * h + m, q, jnp.zeros((), BF16))
            logits = [_dotf(qm, kk) if fm else _dotf(qm, kk, _NT) for kk, _, fm in key_sets]
            mx = functools.reduce(jnp.maximum, [jnp.max(s, axis=-1, keepdims=True) for s in logits])
            es = [jnp.exp2(s - mx) for s in logits]
            den = functools.reduce(lambda a, b: a + b, [jnp.sum(e, axis=-1, keepdims=True) for e in es])
            inv = 1.0 / den
            probs.append([e * inv for e in es])
        o = None
        for i, (_, vv, fm) in enumerate(key_sets):
            pi = _bf(probs[0][i] - lam * probs[1][i])
            oi = _dotf(pi, vv, _NT) if fm else _dotf(pi, vv)
            o = oi if o is None else o + oi
        out = jnp.where(lane // HEAD == h, o, out)
    ms = _mm2(out * out, _head_ones(W_BR)) * (1.0 / HEAD)
    gain = jnp.concatenate([g_ref[layer:layer + 1]] * (W_BR // HEAD), axis=1)
    o_ref[0] = out * lax.rsqrt(ms + NORM_EPS) * gain * (1.0 - lam_init)


def _diff(pd, lam_p, subln_g, cache_k, cache_v, layer, lam_init):
    b, t, _ = pd.shape
    latent = cache_k is not None
    tq = 256
    in_specs = [pl.BlockSpec(lam_p.shape, lambda i, j: (0, 0, 0)),
                pl.BlockSpec(subln_g.shape, lambda i, j: (0, 0)),
                pl.BlockSpec((1, tq, 256), lambda i, j: (i, j, 0)),
                pl.BlockSpec((1, t, 256), lambda i, j: (i, 0, 1)),
                pl.BlockSpec((1, t, 256), lambda i, j: (i, 0, 2))]
    args = [lam_p, subln_g, pd, pd, pd]
    if latent:
        in_specs += [pl.BlockSpec((1, 1) + cache_k.shape[2:], lambda i, j: (i, layer, 0, 0)),
                     pl.BlockSpec((1, 1) + cache_v.shape[2:], lambda i, j: (i, layer, 0, 0))]
        args += [cache_k, cache_v]
    return pl.pallas_call(
        functools.partial(_diff_kernel, latent=latent, lam_init=lam_init, layer=layer),
        grid=(b, t // tq),
        in_specs=in_specs,
        out_specs=pl.BlockSpec((1, tq, W_BR), lambda i, j: (i, j, 0)),
        out_shape=jax.ShapeDtypeStruct((b, t, W_BR), F32),
        compiler_params=_params(("parallel", "parallel")),
        name="diff",
    )(*args)


def _out_kernel(x_ref, mod_ref, gpost_ref, w_ref, pa_ref, yf_ref, yb_ref, gng_ref, gnb_ref, rk_ref,
                ywin_ref, bg_ref, ylru_ref, cg_ref, ydiff_ref, dg_ref, o_ref, *, layer):
    lrow = slice(layer, layer + 1)
    ones = _head_ones(W_BR)
    pa = pa_ref[0]
    r, k, v, ag = pa[:, 0:256], pa[:, 256:512], pa[:, 512:768], pa[:, 896:1152]
    y = yf_ref[0] + yb_ref[0]
    mu = _mm2(y, ones) * (1.0 / HEAD)
    yc = y - mu
    var = _mm2(yc * yc, ones) * (1.0 / HEAD)
    ya = yc * lax.rsqrt(var + GN_EPS) * gng_ref[lrow] + gnb_ref[lrow]
    ya = ya + _mm2(r * k * rk_ref[lrow], ones) * v
    mix = jnp.concatenate([ya * _silu(ag), ywin_ref[0] * _silu(bg_ref[0]),
                           ylru_ref[0] * _silu(cg_ref[0]), ydiff_ref[0] * _silu(dg_ref[0])], axis=-1)
    o = jnp.dot(mix.astype(BF16), w_ref[0], preferred_element_type=F32)
    o = o * lax.rsqrt(jnp.mean(o * o, -1, keepdims=True) + NORM_EPS) * gpost_ref[lrow]
    o_ref[0] = x_ref[0] + mod_ref[0][:, 2 * D_MODEL:] * o


def _out(x, mod_l, g_post, w_out_bf, layer, latent, pa, yf, yb, gn_g, gn_b, r_k, ywin, pb, ylru, pc, ydiff, pd):
    b, t, _ = x.shape
    tm = ROW_TILE
    mod_idx = (lambda i, j: (1 + i, 0, 0)) if latent else (lambda i, j: (0, 0, 0))
    rows = lambda w, c=0: pl.BlockSpec((1, tm, w), lambda i, j: (i, j, c))
    vec = pl.BlockSpec(gn_g.shape, lambda i, j: (0, 0))
    in_specs = [rows(D_MODEL),
                pl.BlockSpec((1, 1, 3 * D_MODEL), mod_idx),
                pl.BlockSpec(g_post.shape, lambda i, j: (0, 0)),
                pl.BlockSpec((1, D_MODEL, D_MODEL), lambda i, j: (layer, 0, 0)),
                rows(WA), rows(W_BR), rows(W_BR), vec, vec, vec,
                rows(W_BR), rows(W_BR, 2), rows(W_BR), rows(W_BR, 1), rows(W_BR), rows(W_BR, 3)]
    return pl.pallas_call(
        functools.partial(_out_kernel, layer=layer),
        grid=(b, t // tm),
        in_specs=in_specs,
        out_specs=rows(D_MODEL),
        out_shape=jax.ShapeDtypeStruct((b, t, D_MODEL), F32),
        compiler_params=_params(("parallel", "parallel")),
        name="out",
    )(x, mod_l, g_post, w_out_bf, pa, yf, yb, gn_g, gn_b, r_k, ywin, pb, ylru, pc, ydiff, pd)


def _layer(x, mod_l, layer, lam_init, wts, cache, tables):
    latent = cache is not None
    pa, pb, pc, pd, *ctx_t = _project(x, mod_l, wts['g_pre'], wts['w_in_bf'], layer, tables)
    rw = _rwkv(pa, wts['rwkv_w0'], wts['rwkv_w_up'], wts['rwkv_a0'], wts['rwkv_a_up'], wts['rwkv_k_k'],
               wts['rwkv_k_a'], cache['rwkv'] if latent else None, layer)
    lr = _lru(pc, wts['lru_conv_w'], wts['lru_conv_b'], wts['lru_wa'], wts['lru_ba'], wts['lru_wx'],
              wts['lru_bx'], wts['lru_lambda'], cache['lru'] if latent else None, layer)
    ywin = _win(pb, wts['win_sink'], cache['win_k'] if latent else None,
                cache['win_v'] if latent else None, layer)
    ydiff = _diff(pd, wts['diff_lambda'], wts['diff_subln_g'],
                  cache['diff_k'] if latent else None, cache['diff_v'] if latent else None, layer, lam_init)
    y = _out(x, mod_l, wts['g_post'], wts['w_out_bf'], layer, latent, pa, rw[0], rw[1],
             wts['rwkv_gn_g'], wts['rwkv_gn_b'], wts['rwkv_r_k'], ywin, pb, lr[0], pc, ydiff, pd)
    new_cache = None if latent else (ctx_t, rw[2], lr[1])
    return y, new_cache


def kernel(x_prompt, x_sample, c, cache_win_k, cache_win_v, cache_diff_k, cache_diff_v, state_rwkv, state_lru,
           c_ctx, w_mod, b_mod, g_pre, g_post, w_in, w_out,
           rwkv_w0, rwkv_w_up, rwkv_a0, rwkv_a_up, rwkv_k_k, rwkv_k_a, rwkv_r_k, rwkv_gn_g, rwkv_gn_b,
           win_sink, lru_conv_w, lru_conv_b, lru_wa, lru_ba, lru_wx, lru_bx, lru_lambda,
           diff_lambda, diff_subln_g):
    n_b, seq = x_prompt.shape[:2]
    n_dec, dec_seq = x_sample.shape[:2]
    past = cache_win_k.shape[2]
    wts = dict(g_pre=g_pre, g_post=g_post, w_in_bf=w_in.astype(BF16), w_out_bf=w_out.astype(BF16),
               rwkv_w0=rwkv_w0, rwkv_w_up=rwkv_w_up, rwkv_a0=rwkv_a0, rwkv_a_up=rwkv_a_up,
               rwkv_k_k=rwkv_k_k, rwkv_k_a=rwkv_k_a, rwkv_r_k=rwkv_r_k.reshape(DEPTH, W_BR), rwkv_gn_g=rwkv_gn_g,
               rwkv_gn_b=rwkv_gn_b, win_sink=win_sink, lru_conv_w=lru_conv_w, lru_conv_b=lru_conv_b,
               lru_wa=lru_wa, lru_ba=lru_ba, lru_wx=lru_wx, lru_bx=lru_bx, lru_lambda=lru_lambda,
               diff_lambda=diff_lambda, diff_subln_g=diff_subln_g)
    fm = lambda a: jnp.moveaxis(a, 2, -1).reshape(n_dec, DEPTH, -1, past)
    cache = dict(win_k=fm(cache_win_k), win_v=fm(cache_win_v), diff_k=fm(cache_diff_k), diff_v=fm(cache_diff_v),
                 rwkv=state_rwkv, lru=state_lru)
    cvec = jnp.concatenate([c_ctx[None], c, jnp.zeros((8 - 1 - n_dec, D_MODEL), F32)], axis=0)
    mod = _modulation(cvec, w_mod, b_mod)
    cos_b, sin_b = _rope_tables(dec_seq, HEAD, 384)
    cos_d, sin_d = _rope_tables(dec_seq, DQ_D, 512)
    tables = (cos_b, sin_b, cos_d, sin_d)

    y_p, y_s = x_prompt, x_sample
    ctx = []
    for l in range(DEPTH):
        lam_init = 0.8 - 0.6 * math.exp(-0.3 * l)
        mod_l = mod[l].reshape(8, 1, 3 * D_MODEL)
        y_p, nc = _layer(y_p, mod_l, l, lam_init, wts, None, None)
        ctx.append(nc)
        y_s, _ = _layer(y_s, mod_l, l, lam_init, wts, cache, tables)
    stack = lambda f: jnp.stack([f(ct) for ct in ctx], axis=1)
    tm_ = lambda i, dims: jnp.moveaxis(stack(lambda ct: ct[0][i]).reshape((n_b, DEPTH) + dims + (seq,)), -1, 2)
    new_win_k = tm_(0, (2, HEAD))
    new_win_v = tm_(1, (2, HEAD))
    new_diff_k = tm_(2, (4, 2, DQ_D))
    new_diff_v = tm_(3, (4, HEAD))
    new_state_rwkv = stack(lambda ct: ct[1])
    new_state_lru = stack(lambda ct: ct[2])
    return (y_p, y_s, new_win_k, new_win_v, new_diff_k, new_diff_v, new_state_rwkv, new_state_lru)
```

```python
import functools
import math

import numpy as np
import jax
import jax.numpy as jnp
from jax import lax
from jax.experimental import pallas as pl
from jax.experimental.pallas import tpu as pltpu

F32 = jnp.float32
BF16 = jnp.bfloat16
HI = lax.Precision.HIGHEST

D_MODEL = 1024
DEPTH = 2
GRID_W = 64
ROPE_BASE = 10000.0
NORM_EPS = 1e-6
NEG_INF = -1e30
LOG2_E = math.log2(math.e)
GN_EPS = 64e-5
C_RG = 8.0
W_BR = 256
HEAD = 64
SUBLANES = 8
LORA = 64
DQ_D = 32
WINDOW = 128
Q_BLK = 128
P_TOTAL = 3456
WA, WB, WC, WD = 1152, 768, 512, 1024
CHUNK = 64
RWKV_BATCH = 4
ROW_TILE = 256
MIB = 1024 * 1024
VMEM_MIB = dict(mod=16, proj=32, rwkv=44, lru=24, win=16, diff=32, out=16)


_NN = (((1,), (0,)), ((), ()))
_NT = (((1,), (1,)), ((), ()))
_TN = (((0,), (0,)), ((), ()))


def _dot_hi(a, b):
    return jnp.dot(a, b, precision=HI, preferred_element_type=F32)


def _bf(x):
    return x.astype(BF16)


def _dotf(x, y, dims=_NN):
    return lax.dot_general(x, y, dims, preferred_element_type=F32)


def _split(x):
    hi = x.astype(BF16)
    return hi, (x - hi.astype(F32)).astype(BF16)


def _mm3(a, b, dims=_NN):
    return _dotf(a[0], b[0], dims) + (_dotf(a[0], b[1], dims) + _dotf(a[1], b[0], dims))


def _mm2(x, m01, left=False):
    hi, lo = _split(x)
    if left:
        return _dotf(m01, hi) + _dotf(m01, lo)
    return _dotf(hi, m01) + _dotf(lo, m01)


def _dot_bf(a, b):
    return jnp.dot(a.astype(BF16), b.astype(BF16), preferred_element_type=F32)


def _dot_nt_bf(a, b):
    return lax.dot_general(a.astype(BF16), b.astype(BF16), (((1,), (1,)), ((), ())),
                           preferred_element_type=F32)


def _sigmoid(x):
    return 1.0 / (1.0 + jnp.exp(-x))


def _silu(x):
    return x * _sigmoid(x)


def _softplus(x):
    return jnp.maximum(x, 0.0) + jnp.log1p(jnp.exp(-jnp.abs(x)))


def _head_ones(n):
    r = lax.broadcasted_iota(jnp.int32, (n, n), 0) // HEAD
    c = lax.broadcasted_iota(jnp.int32, (n, n), 1) // HEAD
    return jnp.where(r == c, 1.0, 0.0).astype(BF16)


def _params(name, sem):
    return pltpu.CompilerParams(dimension_semantics=sem, vmem_limit_bytes=VMEM_MIB[name] * MIB)


def _mod_kernel(c_ref, w_ref, b_ref, o_ref):
    o_ref[0] = _mm3(_split(_silu(c_ref[...])), _split(w_ref[0])) + b_ref[0]


def _modulation(cvec, w_mod, b_mod):
    n_l = w_mod.shape[0]
    tn = 512
    return pl.pallas_call(
        _mod_kernel,
        grid=(n_l, 3 * D_MODEL // tn),
        in_specs=[pl.BlockSpec((8, D_MODEL), lambda l, j: (0, 0)),
                  pl.BlockSpec((1, D_MODEL, tn), lambda l, j: (l, 0, j)),
                  pl.BlockSpec((1, 1, tn), lambda l, j: (l, 0, j))],
        out_specs=pl.BlockSpec((1, 8, tn), lambda l, j: (l, 0, j)),
        out_shape=jax.ShapeDtypeStruct((n_l, 8, 3 * D_MODEL), F32),
        compiler_params=_params("mod", ("parallel", "parallel")),
        name="mod",
    )(cvec, w_mod, b_mod.reshape(n_l, 1, 3 * D_MODEL))


def _rope(x, cos, sin_signed, off):
    w = x.shape[-1]
    lane = lax.broadcasted_iota(jnp.int32, x.shape, 1)
    first = (lane % (2 * off)) < off
    partner = jnp.where(first, pltpu.roll(x, w - off, 1), pltpu.roll(x, off, 1))
    return x * cos + partner * sin_signed


def _proj_kernel(*refs, latent, layer):
    if latent:
        x_ref, mod_ref, g_ref, w_ref, cb_ref, sb_ref, cd_ref, sd_ref, oa, ob, oc, od = refs
    else:
        x_ref, mod_ref, g_ref, w_ref, oa, ob, oc, od, o_wk, o_wv, o_dk, o_dv = refs
    x = x_ref[0]
    y = x * lax.rsqrt(jnp.mean(x * x, -1, keepdims=True) + NORM_EPS) * g_ref[layer:layer + 1]
    m = mod_ref[0]
    h = y * (1.0 + m[:, D_MODEL:2 * D_MODEL]) + m[:, :D_MODEL]
    p = jnp.dot(h.astype(BF16), w_ref[0], preferred_element_type=F32)
    oa[0] = p[:, :WA]
    pb = p[:, WA:WA + WB]
    pd = p[:, WA + WB + WC:]
    if latent:
        ob[0, :, :384] = _rope(pb[:, :384], cb_ref[...], sb_ref[...], 16)
        ob[0, :, 384:] = pb[:, 384:]
        od[0, :, :512] = _rope(pd[:, :512], cd_ref[...], sd_ref[...], 8)
        od[0, :, 512:] = pd[:, 512:]
    else:
        ob[0] = pb
        od[0] = pd
        o_wk[0] = pb[:, 256:384].T
        o_wv[0] = pb[:, 384:512].T
        o_dk[0] = pd[:, 256:512].T
        o_dv[0] = pd[:, 512:768].T
    oc[0] = p[:, WA + WB:WA + WB + WC]


def _project(x, mod_l, g_pre, w_in_bf, layer, tables):
    b, t, _ = x.shape
    latent = tables is not None
    tm = ROW_TILE
    mod_idx = (lambda i, j: (1 + i, 0, 0)) if latent else (lambda i, j: (0, 0, 0))
    in_specs = [pl.BlockSpec((1, tm, D_MODEL), lambda i, j: (i, j, 0)),
                pl.BlockSpec((1, 1, 3 * D_MODEL), mod_idx),
                pl.BlockSpec(g_pre.shape, lambda i, j: (0, 0)),
                pl.BlockSpec((1, D_MODEL, P_TOTAL), lambda i, j: (layer, 0, 0))]
    args = [x, mod_l, g_pre, w_in_bf]
    if latent:
        for tab in tables:
            in_specs.append(pl.BlockSpec((tm, tab.shape[1]), lambda i, j: (j, 0)))
            args.append(tab)
    widths = (WA, WB, WC, WD)
    out_specs = [pl.BlockSpec((1, tm, w), lambda i, j: (i, j, 0)) for w in widths]
    out_shape = [jax.ShapeDtypeStruct((b, t, w), F32) for w in widths]
    if not latent:
        for w in (128, 128, W_BR, W_BR):
            out_specs.append(pl.BlockSpec((1, w, tm), lambda i, j: (i, 0, j)))
            out_shape.append(jax.ShapeDtypeStruct((b, w, t), F32))
    return pl.pallas_call(
        functools.partial(_proj_kernel, latent=latent, layer=layer),
        grid=(b, t // tm),
        in_specs=in_specs,
        out_specs=out_specs,
        out_shape=out_shape,
        compiler_params=_params("proj", ("parallel", "parallel")),
        name="proj",
    )(*args)


def _rope_tables(t, head_dim, n_lanes):
    half = head_dim // 2
    quarter = half // 2
    pos = np.arange(t)
    row = (pos // GRID_W).astype(np.float32)
    col = (pos % GRID_W).astype(np.float32)
    inv = np.float32(ROPE_BASE) ** (-np.arange(0, half, 2, dtype=np.float32) / np.float32(half))
    lane = np.arange(n_lanes) % head_dim
    in_half = lane % half
    p = np.where((lane < half)[None, :], row[:, None], col[:, None])
    ang = (p * inv[in_half % quarter][None, :]).astype(np.float64)
    sign = np.where(in_half < quarter, -1.0, 1.0)[None, :]
    return jnp.asarray(np.cos(ang), F32), jnp.asarray(np.sin(ang) * sign, F32)


_PREP_FIELDS = (('ar', 2, 1, BF16), ('nm', 1, 1, BF16), ('mm', 1, 1, BF16), ('pq', 1, 2, BF16),
                ('v', 1, 1, BF16), ('bkt', 1, 2, BF16), ('gct', 1, 1, F32))


def _prep_scratch(slots, n_streams):
    return [pltpu.VMEM((slots, n_streams, rows * CHUNK, lanes * W_BR), dt) for _, rows, lanes, dt in _PREP_FIELDS]


class _SlotView:
    def __init__(self, ref, slot):
        self.ref, self.slot, self.shape = ref, slot, ref.shape[1:]

    def __getitem__(self, idx):
        return self.ref[(self.slot,) + (idx if isinstance(idx, tuple) else (idx,))]

    def __setitem__(self, idx, val):
        self.ref[(self.slot,) + (idx if isinstance(idx, tuple) else (idx,))] = val


def _head_transpose(x):
    xt = x.T
    return jnp.concatenate([xt[h * HEAD:(h + 1) * HEAD] for h in range(x.shape[1] // HEAD)], axis=1)


def _expand(x, same):
    return jnp.concatenate([x] * (same.shape[0] // x.shape[0]), axis=0) * same


def _rwkv_prepare(xs, dirs, w0_ref, wup_ref, a0_ref, aup_ref, k_k, k_a, m_ref, same_ref, out):
    c = xs[0].shape[0]
    ones = _head_ones(W_BR)
    ti = lax.broadcasted_iota(jnp.int32, (c, c), 0)
    si = lax.broadcasted_iota(jnp.int32, (c, c), 1)
    cums = [jnp.where(si <= ti, 1.0, 0.0).astype(BF16), jnp.where(si >= ti, 1.0, 0.0).astype(BF16)]
    same = same_ref[...]

    st = []
    for xc, d in zip(xs, dirs):
        r, k, v = xc[:, 0:256], xc[:, 256:512], xc[:, 512:768]
        wd, ad = xc[:, 768:832], xc[:, 832:896]
        kk = k * k_k
        kk = kk * lax.rsqrt(_mm2(kk * kk, ones) + 1e-12)
        z = w0_ref[d:d + 1] + _mm3(_split(jnp.tanh(wd)), _split(wup_ref[d]))
        e = jnp.exp(-_softplus(-z) - 0.5)
        a = _sigmoid(a0_ref[d:d + 1] + _mm3(_split(ad), _split(aup_ref[d])))
        st.append(dict(r=r, v=v, e=e, kd=k * (1.0 + (a - 1.0) * k_a), alpha=-kk, beta=kk * a))
    yield
    for s, d in zip(st, dirs):
        l_incl = _mm2(s['e'], cums[d], left=True)
        l_tot = jnp.sum(s['e'], axis=0, keepdims=True)
        grow = jnp.exp(l_incl)
        tail = jnp.exp(l_incl - l_tot)
        s.update(ar=jnp.concatenate([_bf(s['alpha'] * jnp.exp(s['e'] - l_incl)),
                                     _bf(s['r'] * jnp.exp(-l_incl))], axis=0),
                 b_t=_bf(s['beta'] * grow), k_t=_bf(s['kd'] * grow),
                 b_h=s['beta'] * tail, k_h=s['kd'] * tail,
                 g_c=jnp.broadcast_to(jnp.exp(-l_tot), (c, W_BR)))
    yield
    for g, (s, d) in enumerate(zip(st, dirs)):
        out['ar'][g] = s['ar']
        rhs_t = jnp.concatenate([_expand(s['b_t'], same), _expand(s['k_t'], same)], axis=0)
        g4 = _dotf(s['ar'], rhs_t, _NT)
        strict = m_ref[_M_STRICT + 2 * d]
        incl = m_ref[_M_INCL + 2 * d]
        out['nm'][g] = _bf(g4[:c, :W_BR]) * strict
        out['mm'][g] = _bf(g4[:c, W_BR:]) * strict
        out['pq'][g, :, :W_BR] = _bf(g4[c:, :W_BR]) * incl
        out['pq'][g, :, W_BR:] = _bf(g4[c:, W_BR:]) * incl
    yield
    for g, s in enumerate(st):
        out['v'][g] = _bf(s['v'])
        out['bkt'][g] = jnp.concatenate([_bf(_head_transpose(s['b_h'])), _bf(_head_transpose(s['k_h']))], axis=1)
        out['gct'][g] = _head_transpose(s['g_c'])


def _rwkv_advance(p, m_ref, same_ref, z_ref, y_stores):
    n_streams, c = p['nm'].shape[0], p['nm'].shape[1]
    gs = range(n_streams)
    same = same_ref[...]
    ex = lambda x: _expand(_bf(x), same)
    invs = [(p['nm'][g] * m_ref[_M_PAIR]).astype(F32) + m_ref[_M_EYE].astype(F32) for g in gs]
    for lvl in range(int(math.log2(c)) - 1):
        inv_x = [ex(t) for t in invs]
        half = [_dotf(p['nm'][g] * m_ref[_M_OFF + lvl], inv_x[g]) for g in gs]
        invs = [invs[g] + _dotf(_bf(invs[g]), ex(half[g])) for g in gs]
        yield
    z0 = [z_ref[g] for g in gs]
    as0 = [_dotf(p['ar'][g], ex(z0[g])) for g in gs]
    vx = [_expand(p['v'][g], same) for g in gs]
    rhs = [as0[g][:c] + _dotf(p['mm'][g], vx[g]) for g in gs]
    yield
    uv = [jnp.concatenate([ex(_dotf(_bf(invs[g]), ex(rhs[g]))), vx[g]], axis=0) for g in gs]
    yield
    for g in gs:
        z_ref[g] = z0[g] * p['gct'][g] + _dotf(p['bkt'][g], uv[g])
    yield
    for g in gs:
        y_stores[g](as0[g][c:] + _dotf(p['pq'][g], uv[g]))


def _interleave(*gens):
    live = list(gens)
    while live:
        for gen in list(live):
            try:
                next(gen)
            except StopIteration:
                live.remove(gen)


_M_STRICT, _M_INCL, _M_EYE, _M_PAIR, _M_OFF = 0, 1, 4, 5, 6


def _rwkv_masks(c, n_h):
    t = np.arange(c)[:, None]
    s = np.arange(c)[None, :]
    masks = [s < t, s <= t, s > t, s >= t, s == t, ((t // 2) == (s // 2)) & (t != s)]
    b = 2
    while b < c:
        masks.append(((t // (2 * b)) == (s // (2 * b))) & ((t // b) != (s // b)))
        b *= 2
    masks = np.tile(np.stack(masks).astype(np.float32), (1, 1, n_h))
    r = np.arange(n_h * c)[:, None] // c
    l = np.arange(n_h * HEAD)[None, :] // HEAD
    return jnp.asarray(masks, BF16), jnp.asarray((r == l).astype(np.float32), BF16)


def _rwkv_kernel(*refs, latent, n_b, layer):
    (xf_ref, xb_ref, xfn_ref, xbn_ref, w0_ref, wup_ref, a0_ref, aup_ref, kk_ref, ka_ref,
     m_ref, same_ref) = refs[:12]
    if latent:
        s0_ref, yf_ref, yb_ref = refs[12:15]
    else:
        yf_ref, yb_ref, st_ref = refs[12:15]
    scr = refs[15:]
    z_scr = scr[0]
    names = [f[0] for f in _PREP_FIELDS]
    ci = pl.program_id(1)
    step = pl.program_id(0) * pl.num_programs(1) + ci
    view = lambda refs_, slot: {k: _SlotView(r, slot) for k, r in zip(names, refs_)}
    cur = view(scr[1:1 + len(names)], step % 2)
    nxt = view(scr[1:1 + len(names)], 1 - step % 2)
    mid = view(scr[1 + len(names):], 0)
    n_h = W_BR // HEAD
    c = CHUNK
    head = lambda h: slice(h * HEAD, (h + 1) * HEAD)
    streams = [(bi, d) for bi in range(n_b) for d in range(2)]
    dirs = [d for _, d in streams]
    params = (w0_ref[layer], wup_ref[0], a0_ref[layer], aup_ref[0], kk_ref[layer:layer + 1],
              ka_ref[layer:layer + 1], m_ref, same_ref)
    first = lambda d: slice(0, c) if d == 0 else slice(c, 2 * c)
    second = lambda d: slice(c, 2 * c) if d == 0 else slice(0, c)
    x_refs, xn_refs, y_refs = (xf_ref, xb_ref), (xfn_ref, xbn_ref), (yf_ref, yb_ref)

    def y_store(bi, d, rows):
        def store(y):
            y_refs[d][bi, rows, :] = y
        return store

    @pl.when(ci == 0)
    def _():
        for g, (bi, d) in enumerate(streams):
            if latent:
                z_scr[g] = jnp.concatenate([s0_ref[bi, 0, d, h].T for h in range(n_h)], axis=1)
            else:
                z_scr[g] = jnp.zeros(z_scr.shape[1:], F32)

    @pl.when(step == 0)
    def _():
        _interleave(_rwkv_prepare([x_refs[d][bi, first(d), :] for bi, d in streams], dirs, *params, cur))

    _interleave(_rwkv_advance(cur, m_ref, same_ref, z_scr, [y_store(bi, d, first(d)) for bi, d in streams]),
                _rwkv_prepare([x_refs[d][bi, second(d), :] for bi, d in streams], dirs, *params, mid))
    _interleave(_rwkv_advance(mid, m_ref, same_ref, z_scr, [y_store(bi, d, second(d)) for bi, d in streams]),
                _rwkv_prepare([xn_refs[d][bi, first(d), :] for bi, d in streams], dirs, *params, nxt))

    if not latent:
        @pl.when(ci == pl.num_programs(1) - 1)
        def _():
            for g, (bi, d) in enumerate(streams):
                z = z_scr[g]
                for h in range(n_h):
                    st_ref[bi, d, h] = z[:, head(h)].T


def _rwkv(pa, w0, wup, a0, aup, k_k, k_a, state, layer):
    b, t, _ = pa.shape
    latent = state is not None
    ns = t // (2 * CHUNK)
    n_h = W_BR // HEAD
    n_b = RWKV_BATCH
    masks, same = _rwkv_masks(CHUNK, n_h)
    full = lambda shape: pl.BlockSpec(shape, lambda i, j: (0,) * len(shape))
    one_layer = lambda shape: pl.BlockSpec((1,) + shape[1:], lambda i, j: (layer,) + (0,) * (len(shape) - 1))
    rows = lambda w, idx: pl.BlockSpec((n_b, 2 * CHUNK, w), idx)
    nxt_group = lambda i, j: jnp.minimum(i + (j + 1) // ns, b // n_b - 1)
    in_specs = [rows(WA, lambda i, j: (i, j, 0)),
                rows(WA, lambda i, j: (i, ns - 1 - j, 0)),
                rows(WA, lambda i, j: (nxt_group(i, j), (j + 1) % ns, 0)),
                rows(WA, lambda i, j: (nxt_group(i, j), (2 * ns - 2 - j) % ns, 0)),
                full(w0.shape), one_layer(wup.shape), full(a0.shape), one_layer(aup.shape),
                full(k_k.shape), full(k_a.shape), full(masks.shape), full(same.shape)]
    args = [pa, pa, pa, pa, w0, wup, a0, aup, k_k, k_a, masks, same]
    out_specs = [rows(W_BR, lambda i, j: (i, j, 0)), rows(W_BR, lambda i, j: (i, ns - 1 - j, 0))]
    out_shape = [jax.ShapeDtypeStruct((b, t, W_BR), F32)] * 2
    if latent:
        in_specs.append(pl.BlockSpec((n_b, 1, 2, n_h, HEAD, HEAD), lambda i, j: (i, layer, 0, 0, 0, 0)))
        args.append(state)
    else:
        out_specs.append(pl.BlockSpec((n_b, 2, n_h, HEAD, HEAD), lambda i, j: (i, 0, 0, 0, 0)))
        out_shape.append(jax.ShapeDtypeStruct((b, 2, n_h, HEAD, HEAD), F32))
    return pl.pallas_call(
        functools.partial(_rwkv_kernel, latent=latent, n_b=n_b, layer=layer),
        grid=(b // n_b, ns),
        in_specs=in_specs,
        out_specs=out_specs,
        out_shape=out_shape,
        scratch_shapes=([pltpu.VMEM((2 * n_b, CHUNK, W_BR), F32)]
                        + _prep_scratch(2, 2 * n_b) + _prep_scratch(1, 2 * n_b)),
        compiler_params=_params("rwkv", ("arbitrary", "arbitrary")),
        name="rwkv",
    )(*args)


def _lru_kernel(*refs, latent, layer):
    if latent:
        (x_ref, cw_ref, cb_ref, wa_ref, ba_ref, wx_ref, bx_ref, lam_ref, h0_ref, y_ref,
         a_scr, h_scr) = refs
    else:
        (x_ref, cw_ref, cb_ref, wa_ref, ba_ref, wx_ref, bx_ref, lam_ref, y_ref, st_ref,
         a_scr, h_scr) = refs
    x = x_ref[0, :, :W_BR]
    t = x.shape[0]
    row = lax.broadcasted_iota(jnp.int32, x.shape, 0)

    def shift_dn(z, k, fill):
        return jnp.where(row >= k, pltpu.roll(z, k, 0), fill)

    def shift_up(z, k, fill):
        return jnp.where(row < t - k, pltpu.roll(z, t - k, 0), fill)

    cw = cw_ref[layer]
    xc = (cb_ref[layer:layer + 1] + shift_dn(x, 2, 0.0) * cw[0:1] + shift_dn(x, 1, 0.0) * cw[1:2]
          + x * cw[2:3] + shift_up(x, 1, 0.0) * cw[3:4])
    xs = _split(xc)
    sub = row % SUBLANES
    n_blk = W_BR // HEAD
    zero = jnp.zeros((HEAD, HEAD), F32)

    def block_diag(w_ref, d):
        return jnp.concatenate([jnp.concatenate([w_ref[0, d, n] if m == n else zero for m in range(n_blk)], axis=1)
                                for n in range(n_blk)], axis=0)

    for d in range(2):
        gate_a = _sigmoid(_mm3(xs, _split(block_diag(wa_ref, d))) + ba_ref[layer, d:d + 1])
        gate_x = _sigmoid(_mm3(xs, _split(block_diag(wx_ref, d))) + bx_ref[layer, d:d + 1])
        log_a = -C_RG * gate_a * _softplus(-lam_ref[layer, d:d + 1])
        a = jnp.exp(log_a)
        u = jnp.sqrt(-jnp.tanh(log_a) * (a * a + 1.0)) * (gate_x * xc)
        k = 1
        while k < SUBLANES:
            if d == 0:
                keep = sub >= k
                sh = lambda z: pltpu.roll(z, k, 0)
            else:
                keep = sub < SUBLANES - k
                sh = lambda z: pltpu.roll(z, t - k, 0)
            u = a * jnp.where(keep, sh(u), 0.0) + u
            a = a * jnp.where(keep, sh(a), 1.0)
            k *= 2
        a_scr[d] = a
        h_scr[d] = u

    n_grp = t // SUBLANES
    if latent:
        carry0 = (h0_ref[0, 0, 0:1], h0_ref[0, 0, 1:2])
    else:
        carry0 = (jnp.zeros((1, W_BR), F32),) * 2

    def chain(i, carry):
        cf, cb = carry
        rf = pl.ds(pl.multiple_of(i * SUBLANES, SUBLANES), SUBLANES)
        rb = pl.ds(pl.multiple_of((n_grp - 1 - i) * SUBLANES, SUBLANES), SUBLANES)
        hf = h_scr[0, rf, :] + a_scr[0, rf, :] * cf
        hb = h_scr[1, rb, :] + a_scr[1, rb, :] * cb
        h_scr[0, rf, :] = hf
        h_scr[1, rb, :] = hb
        return hf[SUBLANES - 1:SUBLANES], hb[0:1]

    cf, cb = lax.fori_loop(0, n_grp, chain, carry0, unroll=4)
    y_ref[0] = h_scr[0] + h_scr[1]
    if not latent:
        st_ref[0] = jnp.concatenate([cf, cb], axis=0)


def _lru(pc, cw, cb, wa, ba, wx, bx, lam, state, layer):
    b, t, _ = pc.shape
    latent = state is not None
    full = lambda shape: pl.BlockSpec(shape, lambda i: (0,) * len(shape))
    one_layer = lambda shape: pl.BlockSpec((1,) + shape[1:], lambda i: (layer,) + (0,) * (len(shape) - 1))
    in_specs = [pl.BlockSpec((1, t, WC), lambda i: (i, 0, 0)),
                full(cw.shape), full(cb.shape), one_layer(wa.shape), full(ba.shape),
                one_layer(wx.shape), full(bx.shape), full(lam.shape)]
    args = [pc, cw, cb, wa, ba, wx, bx, lam]
    out_specs = [pl.BlockSpec((1, t, W_BR), lambda i: (i, 0, 0))]
    out_shape = [jax.ShapeDtypeStruct((b, t, W_BR), F32)]
    if latent:
        in_specs.append(pl.BlockSpec((1, 1, 2, W_BR), lambda i: (i, layer, 0, 0)))
        args.append(state)
    else:
        out_specs.append(pl.BlockSpec((1, 2, W_BR), lambda i: (i, 0, 0)))
        out_shape.append(jax.ShapeDtypeStruct((b, 2, W_BR), F32))
    return pl.pallas_call(
        functools.partial(_lru_kernel, latent=latent, layer=layer),
        grid=(b,),
        in_specs=in_specs,
        out_specs=out_specs,
        out_shape=out_shape,
        scratch_shapes=[pltpu.VMEM((2, t, W_BR), F32), pltpu.VMEM((2, t, W_BR), F32)],
        compiler_params=_params("lru", ("parallel",)),
        name="lru",
    )(*args)


def _win_kernel(*refs, latent, layer):
    if latent:
        sink_ref, q_ref, kp_ref, kc_ref, kn_ref, ck_ref, cv_ref, o_ref = refs
        j = pl.program_id(1)
        nq = pl.num_programs(1)
        qi = lax.broadcasted_iota(jnp.int32, (Q_BLK, Q_BLK), 0)
        ki = lax.broadcasted_iota(jnp.int32, (Q_BLK, Q_BLK), 1)
        q = q_ref[0]
        ctx_k, ctx_v = _bf(ck_ref[0, 0]), _bf(cv_ref[0, 0])
        pieces = [(_bf(r[0, :, :128]), _bf(r[0, :, 128:]), m, False) for r, m in
                  ((kp_ref, (ki >= qi) & (j > 0)), (kc_ref, None), (kn_ref, (ki <= qi) & (j < nq - 1)))]
        pieces += [(ctx_k[:, i:i + Q_BLK], ctx_v[:, i:i + Q_BLK], None, True)
                   for i in range(0, ctx_k.shape[1], Q_BLK)]
    else:
        sink_ref, x_ref, o_ref = refs
        q = x_ref[0, :, :256]
        k, v = _bf(x_ref[0, :, 256:384]), _bf(x_ref[0, :, 384:512])
        pieces = [(k[i:i + Q_BLK], v[i:i + Q_BLK], None, False) for i in range(0, k.shape[0], Q_BLK)]
    q = _bf(q * (HEAD ** -0.5 * LOG2_E))
    outs = []
    for h in range(2):
        hs = slice(h * HEAD, (h + 1) * HEAD)
        for g in range(2):
            qg = q[:, (2 * h + g) * HEAD:(2 * h + g + 1) * HEAD]
            sink = sink_ref[layer, 2 * h + g] * LOG2_E
            logits = []
            for kp, _, mask, fm in pieces:
                s = _dotf(qg, kp[hs]) if fm else _dotf(qg, kp[:, hs], _NT)
                logits.append(s if mask is None else jnp.where(mask, s, NEG_INF))
            mx = jnp.maximum(jnp.max(functools.reduce(jnp.maximum, logits), axis=-1, keepdims=True), sink)
            es = [jnp.exp2(s - mx) for s in logits]
            den = jnp.sum(functools.reduce(lambda a, b: a + b, es), axis=-1, keepdims=True) + jnp.exp2(sink - mx)
            acc = functools.reduce(lambda a, b: a + b,
                                   [_dotf(_bf(e), vp[hs], _NT) if fm else _dotf(_bf(e), vp[:, hs])
                                    for e, (_, vp, _, fm) in zip(es, pieces)])
            outs.append(acc / den)
    o_ref[0] = jnp.concatenate(outs, axis=-1)


def _win(pb, sink, cache_k, cache_v, layer):
    b, t, _ = pb.shape
    latent = cache_k is not None
    smem = pl.BlockSpec(memory_space=pltpu.SMEM)
    if latent:
        nq = t // Q_BLK
        grid = (b, nq)
        in_specs = [smem,
                    pl.BlockSpec((1, Q_BLK, 256), lambda i, j: (i, j, 0)),
                    pl.BlockSpec((1, Q_BLK, 256), lambda i, j: (i, jnp.maximum(j - 1, 0), 1)),
                    pl.BlockSpec((1, Q_BLK, 256), lambda i, j: (i, j, 1)),
                    pl.BlockSpec((1, Q_BLK, 256), lambda i, j: (i, jnp.minimum(j + 1, nq - 1), 1)),
                    pl.BlockSpec((1, 1) + cache_k.shape[2:], lambda i, j: (i, layer, 0, 0)),
                    pl.BlockSpec((1, 1) + cache_v.shape[2:], lambda i, j: (i, layer, 0, 0))]
        args = [sink, pb, pb, pb, pb, cache_k, cache_v]
        out_spec = pl.BlockSpec((1, Q_BLK, W_BR), lambda i, j: (i, j, 0))
        sem = ("parallel", "parallel")
    else:
        grid = (b,)
        in_specs = [smem, pl.BlockSpec((1, t, WB), lambda i: (i, 0, 0))]
        args = [sink, pb]
        out_spec = pl.BlockSpec((1, t, W_BR), lambda i: (i, 0, 0))
        sem = ("parallel",)
    return pl.pallas_call(
        functools.partial(_win_kernel, latent=latent, layer=layer),
        grid=grid,
        in_specs=in_specs,
        out_specs=out_spec,
        out_shape=jax.ShapeDtypeStruct((b, t, W_BR), F32),
        compiler_params=_params("win", sem),
        name="win",
    )(*args)


def _diff_kernel(*refs, latent, lam_init, layer):
    if latent:
        lam_ref, g_ref, q_ref, k_ref, v_ref, ck_ref, cv_ref, o_ref = refs
    else:
        lam_ref, g_ref, q_ref, k_ref, v_ref, o_ref = refs
    lp = lam_ref[layer]
    lam = (jnp.exp(jnp.sum(lp[0:1] * lp[1:2], axis=-1, keepdims=True))
           - jnp.exp(jnp.sum(lp[2:3] * lp[3:4], axis=-1, keepdims=True)) + lam_init)
    q = _bf(q_ref[0] * (DQ_D ** -0.5 * LOG2_E))
    key_sets = [(_bf(k_ref[0]), _bf(v_ref[0]), False)]
    if latent:
        key_sets.append((_bf(ck_ref[0, 0]), _bf(cv_ref[0, 0]), True))
    lane = lax.broadcasted_iota(jnp.int32, (1, W_BR), 1)
    out = jnp.zeros(q.shape, F32)
    for h in range(W_BR // HEAD):
        probs = []
        for m in range(2):
            qm = jnp.where(lane // DQ_D == 2 * h + m, q, jnp.zeros((), BF16))
            logits = [_dotf(qm, kk) if fm else _dotf(qm, kk, _NT) for kk, _, fm in key_sets]
            mx = functools.reduce(jnp.maximum, [jnp.max(s, axis=-1, keepdims=True) for s in logits])
            es = [jnp.exp2(s - mx) for s in logits]
            den = functools.reduce(lambda a, b: a + b, [jnp.sum(e, axis=-1, keepdims=True) for e in es])
            inv = 1.0 / den
            probs.append([e * inv for e in es])
        o = None
        for i, (_, vv, fm) in enumerate(key_sets):
            pi = _bf(probs[0][i] - lam * probs[1][i])
            oi = _dotf(pi, vv, _NT) if fm else _dotf(pi, vv)
            o = oi if o is None else o + oi
        out = jnp.where(lane // HEAD == h, o, out)
    ms = _mm2(out * out, _head_ones(W_BR)) * (1.0 / HEAD)
    gain = jnp.concatenate([g_ref[layer:layer + 1]] * (W_BR // HEAD), axis=1)
    o_ref[0] = out * lax.rsqrt(ms + NORM_EPS) * gain * (1.0 - lam_init)


def _diff(pd, lam_p, subln_g, cache_k, cache_v, layer, lam_init):
    b, t, _ = pd.shape
    latent = cache_k is not None
    tq = 256
    in_specs = [pl.BlockSpec(lam_p.shape, lambda i, j: (0, 0, 0)),
                pl.BlockSpec(subln_g.shape, lambda i, j: (0, 0)),
                pl.BlockSpec((1, tq, 256), lambda i, j: (i, j, 0)),
                pl.BlockSpec((1, t, 256), lambda i, j: (i, 0, 1)),
                pl.BlockSpec((1, t, 256), lambda i, j: (i, 0, 2))]
    args = [lam_p, subln_g, pd, pd, pd]
    if latent:
        in_specs += [pl.BlockSpec((1, 1) + cache_k.shape[2:], lambda i, j: (i, layer, 0, 0)),
                     pl.BlockSpec((1, 1) + cache_v.shape[2:], lambda i, j: (i, layer, 0, 0))]
        args += [cache_k, cache_v]
    return pl.pallas_call(
        functools.partial(_diff_kernel, latent=latent, lam_init=lam_init, layer=layer),
        grid=(b, t // tq),
        in_specs=in_specs,
        out_specs=pl.BlockSpec((1, tq, W_BR), lambda i, j: (i, j, 0)),
        out_shape=jax.ShapeDtypeStruct((b, t, W_BR), F32),
        compiler_params=_params("diff", ("parallel", "parallel")),
        name="diff",
    )(*args)


def _out_kernel(x_ref, mod_ref, gpost_ref, w_ref, pa_ref, yf_ref, yb_ref, gng_ref, gnb_ref, rk_ref,
                ywin_ref, bg_ref, ylru_ref, cg_ref, ydiff_ref, dg_ref, o_ref, *, layer):
    lrow = slice(layer, layer + 1)
    ones = _head_ones(W_BR)
    pa = pa_ref[0]
    r, k, v, ag = pa[:, 0:256], pa[:, 256:512], pa[:, 512:768], pa[:, 896:1152]
    y = yf_ref[0] + yb_ref[0]
    mu = _mm2(y, ones) * (1.0 / HEAD)
    yc = y - mu
    var = _mm2(yc * yc, ones) * (1.0 / HEAD)
    ya = yc * lax.rsqrt(var + GN_EPS) * gng_ref[lrow] + gnb_ref[lrow]
    ya = ya + _mm2(r * k * rk_ref[lrow], ones) * v
    mix = jnp.concatenate([ya * _silu(ag), ywin_ref[0] * _silu(bg_ref[0]),
                           ylru_ref[0] * _silu(cg_ref[0]), ydiff_ref[0] * _silu(dg_ref[0])], axis=-1)
    o = jnp.dot(mix.astype(BF16), w_ref[0], preferred_element_type=F32)
    o = o * lax.rsqrt(jnp.mean(o * o, -1, keepdims=True) + NORM_EPS) * gpost_ref[lrow]
    o_ref[0] = x_ref[0] + mod_ref[0][:, 2 * D_MODEL:] * o


def _out(x, mod_l, g_post, w_out_bf, layer, latent, pa, yf, yb, gn_g, gn_b, r_k, ywin, pb, ylru, pc, ydiff, pd):
    b, t, _ = x.shape
    tm = ROW_TILE
    mod_idx = (lambda i, j: (1 + i, 0, 0)) if latent else (lambda i, j: (0, 0, 0))
    rows = lambda w, c=0: pl.BlockSpec((1, tm, w), lambda i, j: (i, j, c))
    vec = pl.BlockSpec(gn_g.shape, lambda i, j: (0, 0))
    in_specs = [rows(D_MODEL),
                pl.BlockSpec((1, 1, 3 * D_MODEL), mod_idx),
                pl.BlockSpec(g_post.shape, lambda i, j: (0, 0)),
                pl.BlockSpec((1, D_MODEL, D_MODEL), lambda i, j: (layer, 0, 0)),
                rows(WA), rows(W_BR), rows(W_BR), vec, vec, vec,
                rows(W_BR), rows(W_BR, 2), rows(W_BR), rows(W_BR, 1), rows(W_BR), rows(W_BR, 3)]
    return pl.pallas_call(
        functools.partial(_out_kernel, layer=layer),
        grid=(b, t // tm),
        in_specs=in_specs,
        out_specs=rows(D_MODEL),
        out_shape=jax.ShapeDtypeStruct((b, t, D_MODEL), F32),
        compiler_params=_params("out", ("parallel", "parallel")),
        name="out",
    )(x, mod_l, g_post, w_out_bf, pa, yf, yb, gn_g, gn_b, r_k, ywin, pb, ylru, pc, ydiff, pd)


def _layer(x, mod_l, layer, lam_init, wts, cache, tables):
    latent = cache is not None
    pa, pb, pc, pd, *ctx_t = _project(x, mod_l, wts['g_pre'], wts['w_in_bf'], layer, tables)
    rw = _rwkv(pa, wts['rwkv_w0'], wts['rwkv_w_up'], wts['rwkv_a0'], wts['rwkv_a_up'], wts['rwkv_k_k'],
               wts['rwkv_k_a'], cache['rwkv'] if latent else None, layer)
    lr = _lru(pc, wts['lru_conv_w'], wts['lru_conv_b'], wts['lru_wa'], wts['lru_ba'], wts['lru_wx'],
              wts['lru_bx'], wts['lru_lambda'], cache['lru'] if latent else None, layer)
    ywin = _win(pb, wts['win_sink'], cache['win_k'] if latent else None,
                cache['win_v'] if latent else None, layer)
    ydiff = _diff(pd, wts['diff_lambda'], wts['diff_subln_g'],
                  cache['diff_k'] if latent else None, cache['diff_v'] if latent else None, layer, lam_init)
    y = _out(x, mod_l, wts['g_post'], wts['w_out_bf'], layer, latent, pa, rw[0], rw[1],
             wts['rwkv_gn_g'], wts['rwkv_gn_b'], wts['rwkv_r_k'], ywin, pb, lr[0], pc, ydiff, pd)
    new_cache = None if latent else (ctx_t, rw[2], lr[1])
    return y, new_cache


def kernel(x_prompt, x_sample, c, cache_win_k, cache_win_v, cache_diff_k, cache_diff_v, state_rwkv, state_lru,
           c_ctx, w_mod, b_mod, g_pre, g_post, w_in, w_out,
           rwkv_w0, rwkv_w_up, rwkv_a0, rwkv_a_up, rwkv_k_k, rwkv_k_a, rwkv_r_k, rwkv_gn_g, rwkv_gn_b,
           win_sink, lru_conv_w, lru_conv_b, lru_wa, lru_ba, lru_wx, lru_bx, lru_lambda,
           diff_lambda, diff_subln_g):
    n_b, seq = x_prompt.shape[:2]
    n_dec, dec_seq = x_sample.shape[:2]
    past = cache_win_k.shape[2]
    wts = dict(g_pre=g_pre, g_post=g_post, w_in_bf=w_in.astype(BF16), w_out_bf=w_out.astype(BF16),
               rwkv_w0=rwkv_w0, rwkv_w_up=rwkv_w_up, rwkv_a0=rwkv_a0, rwkv_a_up=rwkv_a_up,
               rwkv_k_k=rwkv_k_k, rwkv_k_a=rwkv_k_a, rwkv_r_k=rwkv_r_k.reshape(DEPTH, W_BR), rwkv_gn_g=rwkv_gn_g,
               rwkv_gn_b=rwkv_gn_b, win_sink=win_sink, lru_conv_w=lru_conv_w, lru_conv_b=lru_conv_b,
               lru_wa=lru_wa, lru_ba=lru_ba, lru_wx=lru_wx, lru_bx=lru_bx, lru_lambda=lru_lambda,
               diff_lambda=diff_lambda, diff_subln_g=diff_subln_g)
    fm = lambda a: jnp.moveaxis(a, 2, -1).reshape(n_dec, DEPTH, -1, past)
    cache = dict(win_k=fm(cache_win_k), win_v=fm(cache_win_v), diff_k=fm(cache_diff_k), diff_v=fm(cache_diff_v),
                 rwkv=state_rwkv, lru=state_lru)
    cvec = jnp.concatenate([c_ctx[None], c, jnp.zeros((8 - 1 - n_dec, D_MODEL), F32)], axis=0)
    mod = _modulation(cvec, w_mod, b_mod)
    cos_b, sin_b = _rope_tables(dec_seq, HEAD, 384)
    cos_d, sin_d = _rope_tables(dec_seq, DQ_D, 512)
    tables = (cos_b, sin_b, cos_d, sin_d)

    y_p, y_s = x_prompt, x_sample
    ctx = []
    for l in range(DEPTH):
        lam_init = 0.8 - 0.6 * math.exp(-0.3 * l)
        mod_l = mod[l].reshape(8, 1, 3 * D_MODEL)
        y_p, nc = _layer(y_p, mod_l, l, lam_init, wts, None, None)
        ctx.append(nc)
        y_s, _ = _layer(y_s, mod_l, l, lam_init, wts, cache, tables)
    stack = lambda f: jnp.stack([f(ct) for ct in ctx], axis=1)
    tm_ = lambda i, dims: jnp.moveaxis(stack(lambda ct: ct[0][i]).reshape((n_b, DEPTH) + dims + (seq,)), -1, 2)
    new_win_k = tm_(0, (2, HEAD))
    new_win_v = tm_(1, (2, HEAD))
    new_diff_k = tm_(2, (4, 2, DQ_D))
    new_diff_v = tm_(3, (4, HEAD))
    new_state_rwkv = stack(lambda ct: ct[1])
    new_state_lru = stack(lambda ct: ct[2])
    return (y_p, y_s, new_win_k, new_win_v, new_diff_k, new_diff_v, new_state_rwkv, new_state_lru)
```

```python
import functools
import math

import numpy as np
import jax
import jax.numpy as jnp
from jax import lax
from jax.experimental import pallas as pl
from jax.experimental.pallas import tpu as pltpu

F32 = jnp.float32
BF16 = jnp.bfloat16
HI = lax.Precision.HIGHEST

D_MODEL = 1024
DEPTH = 2
GRID_W = 64
ROPE_BASE = 10000.0
NORM_EPS = 1e-6
NEG_INF = -1e30
LOG2_E = math.log2(math.e)
GN_EPS = 64e-5
C_RG = 8.0
W_BR = 256
HEAD = 64
SUBLANES = 8
LORA = 64
DQ_D = 32
WINDOW = 128
Q_BLK = 128
P_TOTAL = 3456
WA, WB, WC, WD = 1152, 768, 512, 1024
CHUNK = 64
RWKV_BATCH = 4
ROW_TILE = 256
MIB = 1024 * 1024
VMEM_MIB = dict(mod=58, proj=36, rwkv=58, lru=58, win=58, diff=58, out=20)


_NN = (((1,), (0,)), ((), ()))
_NT = (((1,), (1,)), ((), ()))
_TN = (((0,), (0,)), ((), ()))


def _dot_hi(a, b):
    return jnp.dot(a, b, precision=HI, preferred_element_type=F32)


def _bf(x):
    return x.astype(BF16)


def _dotf(x, y, dims=_NN):
    return lax.dot_general(x, y, dims, preferred_element_type=F32)


def _split(x):
    hi = x.astype(BF16)
    return hi, (x - hi.astype(F32)).astype(BF16)


def _mm3(a, b, dims=_NN):
    return _dotf(a[0], b[0], dims) + (_dotf(a[0], b[1], dims) + _dotf(a[1], b[0], dims))


def _mm2(x, m01, left=False):
    hi, lo = _split(x)
    if left:
        return _dotf(m01, hi) + _dotf(m01, lo)
    return _dotf(hi, m01) + _dotf(lo, m01)


def _dot_bf(a, b):
    return jnp.dot(a.astype(BF16), b.astype(BF16), preferred_element_type=F32)


def _dot_nt_bf(a, b):
    return lax.dot_general(a.astype(BF16), b.astype(BF16), (((1,), (1,)), ((), ())),
                           preferred_element_type=F32)


def _sigmoid(x):
    return 1.0 / (1.0 + jnp.exp(-x))


def _silu(x):
    return x * _sigmoid(x)


def _softplus(x):
    return jnp.maximum(x, 0.0) + jnp.log1p(jnp.exp(-jnp.abs(x)))


def _head_ones(n):
    r = lax.broadcasted_iota(jnp.int32, (n, n), 0) // HEAD
    c = lax.broadcasted_iota(jnp.int32, (n, n), 1) // HEAD
    return jnp.where(r == c, 1.0, 0.0).astype(BF16)


def _params(name, sem):
    return pltpu.CompilerParams(dimension_semantics=sem, vmem_limit_bytes=VMEM_MIB[name] * MIB)


def _mod_kernel(c_ref, w_ref, b_ref, o_ref):
    o_ref[0] = _mm3(_split(_silu(c_ref[...])), _split(w_ref[0])) + b_ref[0]


def _modulation(cvec, w_mod, b_mod):
    n_l = w_mod.shape[0]
    tn = 512
    return pl.pallas_call(
        _mod_kernel,
        grid=(n_l, 3 * D_MODEL // tn),
        in_specs=[pl.BlockSpec((8, D_MODEL), lambda l, j: (0, 0)),
                  pl.BlockSpec((1, D_MODEL, tn), lambda l, j: (l, 0, j)),
                  pl.BlockSpec((1, 1, tn), lambda l, j: (l, 0, j))],
        out_specs=pl.BlockSpec((1, 8, tn), lambda l, j: (l, 0, j)),
        out_shape=jax.ShapeDtypeStruct((n_l, 8, 3 * D_MODEL), F32),
        compiler_params=_params("mod", ("parallel", "parallel")),
        name="mod",
    )(cvec, w_mod, b_mod.reshape(n_l, 1, 3 * D_MODEL))


def _rope(x, cos, sin_signed, off):
    w = x.shape[-1]
    lane = lax.broadcasted_iota(jnp.int32, x.shape, 1)
    first = (lane % (2 * off)) < off
    partner = jnp.where(first, pltpu.roll(x, w - off, 1), pltpu.roll(x, off, 1))
    return x * cos + partner * sin_signed


def _proj_kernel(*refs, latent, layer):
    if latent:
        x_ref, mod_ref, g_ref, w_ref, cb_ref, sb_ref, cd_ref, sd_ref, oa, ob, oc, od = refs
    else:
        x_ref, mod_ref, g_ref, w_ref, oa, ob, oc, od, o_wk, o_wv, o_dk, o_dv = refs
    x = x_ref[0]
    y = x * lax.rsqrt(jnp.mean(x * x, -1, keepdims=True) + NORM_EPS) * g_ref[layer:layer + 1]
    m = mod_ref[0]
    h = y * (1.0 + m[:, D_MODEL:2 * D_MODEL]) + m[:, :D_MODEL]
    p = jnp.dot(h.astype(BF16), w_ref[0], preferred_element_type=F32)
    oa[0] = p[:, :WA]
    pb = p[:, WA:WA + WB]
    pd = p[:, WA + WB + WC:]
    if latent:
        ob[0, :, :384] = _rope(pb[:, :384], cb_ref[...], sb_ref[...], 16)
        ob[0, :, 384:] = pb[:, 384:]
        od[0, :, :512] = _rope(pd[:, :512], cd_ref[...], sd_ref[...], 8)
        od[0, :, 512:] = pd[:, 512:]
    else:
        ob[0] = pb
        od[0] = pd
        o_wk[0] = pb[:, 256:384].T
        o_wv[0] = pb[:, 384:512].T
        o_dk[0] = pd[:, 256:512].T
        o_dv[0] = pd[:, 512:768].T
    oc[0] = p[:, WA + WB:WA + WB + WC]


def _project(x, mod_l, g_pre, w_in_bf, layer, tables):
    b, t, _ = x.shape
    latent = tables is not None
    tm = ROW_TILE
    mod_idx = (lambda i, j: (1 + i, 0, 0)) if latent else (lambda i, j: (0, 0, 0))
    in_specs = [pl.BlockSpec((1, tm, D_MODEL), lambda i, j: (i, j, 0)),
                pl.BlockSpec((1, 1, 3 * D_MODEL), mod_idx),
                pl.BlockSpec(g_pre.shape, lambda i, j: (0, 0)),
                pl.BlockSpec((1, D_MODEL, P_TOTAL), lambda i, j: (layer, 0, 0))]
    args = [x, mod_l, g_pre, w_in_bf]
    if latent:
        for tab in tables:
            in_specs.append(pl.BlockSpec((tm, tab.shape[1]), lambda i, j: (j, 0)))
            args.append(tab)
    widths = (WA, WB, WC, WD)
    out_specs = [pl.BlockSpec((1, tm, w), lambda i, j: (i, j, 0)) for w in widths]
    out_shape = [jax.ShapeDtypeStruct((b, t, w), F32) for w in widths]
    if not latent:
        for w in (128, 128, W_BR, W_BR):
            out_specs.append(pl.BlockSpec((1, w, tm), lambda i, j: (i, 0, j)))
            out_shape.append(jax.ShapeDtypeStruct((b, w, t), F32))
    return pl.pallas_call(
        functools.partial(_proj_kernel, latent=latent, layer=layer),
        grid=(b, t // tm),
        in_specs=in_specs,
        out_specs=out_specs,
        out_shape=out_shape,
        compiler_params=_params("proj", ("parallel", "parallel")),
        name="proj",
    )(*args)


def _rope_tables(t, head_dim, n_lanes):
    half = head_dim // 2
    quarter = half // 2
    pos = np.arange(t)
    row = (pos // GRID_W).astype(np.float32)
    col = (pos % GRID_W).astype(np.float32)
    inv = np.float32(ROPE_BASE) ** (-np.arange(0, half, 2, dtype=np.float32) / np.float32(half))
    lane = np.arange(n_lanes) % head_dim
    in_half = lane % half
    p = np.where((lane < half)[None, :], row[:, None], col[:, None])
    ang = (p * inv[in_half % quarter][None, :]).astype(np.float64)
    sign = np.where(in_half < quarter, -1.0, 1.0)[None, :]
    return jnp.asarray(np.cos(ang), F32), jnp.asarray(np.sin(ang) * sign, F32)


_PREP_FIELDS = (('ar', 2, 1, BF16), ('nm', 1, 1, BF16), ('mm', 1, 1, BF16), ('pq', 1, 2, BF16),
                ('v', 1, 1, BF16), ('bkt', 1, 2, BF16), ('gct', 1, 1, F32))


def _prep_scratch(slots, n_streams):
    return [pltpu.VMEM((slots, n_streams, rows * CHUNK, lanes * W_BR), dt) for _, rows, lanes, dt in _PREP_FIELDS]


class _SlotView:
    def __init__(self, ref, slot):
        self.ref, self.slot, self.shape = ref, slot, ref.shape[1:]

    def __getitem__(self, idx):
        return self.ref[(self.slot,) + (idx if isinstance(idx, tuple) else (idx,))]

    def __setitem__(self, idx, val):
        self.ref[(self.slot,) + (idx if isinstance(idx, tuple) else (idx,))] = val


def _head_transpose(x):
    xt = x.T
    return jnp.concatenate([xt[h * HEAD:(h + 1) * HEAD] for h in range(x.shape[1] // HEAD)], axis=1)


def _expand(x, same):
    return jnp.concatenate([x] * (same.shape[0] // x.shape[0]), axis=0) * same


def _rwkv_prepare(xs, dirs, w0_ref, wup_ref, a0_ref, aup_ref, k_k, k_a, m_ref, same_ref, out):
    c = xs[0].shape[0]
    ones = _head_ones(W_BR)
    ti = lax.broadcasted_iota(jnp.int32, (c, c), 0)
    si = lax.broadcasted_iota(jnp.int32, (c, c), 1)
    cums = [jnp.where(si <= ti, 1.0, 0.0).astype(BF16), jnp.where(si >= ti, 1.0, 0.0).astype(BF16)]
    same = same_ref[...]

    st = []
    for xc, d in zip(xs, dirs):
        r, k, v = xc[:, 0:256], xc[:, 256:512], xc[:, 512:768]
        wd, ad = xc[:, 768:832], xc[:, 832:896]
        kk = k * k_k
        kk = kk * lax.rsqrt(_mm2(kk * kk, ones) + 1e-12)
        z = w0_ref[d:d + 1] + _mm3(_split(jnp.tanh(wd)), _split(wup_ref[d]))
        e = jnp.exp(-_softplus(-z) - 0.5)
        a = _sigmoid(a0_ref[d:d + 1] + _mm3(_split(ad), _split(aup_ref[d])))
        st.append(dict(r=r, v=v, e=e, kd=k * (1.0 + (a - 1.0) * k_a), alpha=-kk, beta=kk * a))
    yield
    for s, d in zip(st, dirs):
        l_incl = _mm2(s['e'], cums[d], left=True)
        l_tot = jnp.sum(s['e'], axis=0, keepdims=True)
        grow = jnp.exp(l_incl)
        tail = jnp.exp(l_incl - l_tot)
        s.update(ar=jnp.concatenate([_bf(s['alpha'] * jnp.exp(s['e'] - l_incl)),
                                     _bf(s['r'] * jnp.exp(-l_incl))], axis=0),
                 b_t=_bf(s['beta'] * grow), k_t=_bf(s['kd'] * grow),
                 b_h=s['beta'] * tail, k_h=s['kd'] * tail,
                 g_c=jnp.broadcast_to(jnp.exp(-l_tot), (c, W_BR)))
    yield
    for g, (s, d) in enumerate(zip(st, dirs)):
        out['ar'][g] = s['ar']
        rhs_t = jnp.concatenate([_expand(s['b_t'], same), _expand(s['k_t'], same)], axis=0)
        g4 = _dotf(s['ar'], rhs_t, _NT)
        strict = m_ref[_M_STRICT + 2 * d]
        incl = m_ref[_M_INCL + 2 * d]
        out['nm'][g] = _bf(g4[:c, :W_BR]) * strict
        out['mm'][g] = _bf(g4[:c, W_BR:]) * strict
        out['pq'][g, :, :W_BR] = _bf(g4[c:, :W_BR]) * incl
        out['pq'][g, :, W_BR:] = _bf(g4[c:, W_BR:]) * incl
    yield
    for g, s in enumerate(st):
        out['v'][g] = _bf(s['v'])
        out['bkt'][g] = jnp.concatenate([_bf(_head_transpose(s['b_h'])), _bf(_head_transpose(s['k_h']))], axis=1)
        out['gct'][g] = _head_transpose(s['g_c'])


def _rwkv_advance(p, m_ref, same_ref, z_ref, y_stores):
    n_streams, c = p['nm'].shape[0], p['nm'].shape[1]
    gs = range(n_streams)
    same = same_ref[...]
    ex = lambda x: _expand(_bf(x), same)
    invs = [(p['nm'][g] * m_ref[_M_PAIR]).astype(F32) + m_ref[_M_EYE].astype(F32) for g in gs]
    for lvl in range(int(math.log2(c)) - 1):
        inv_x = [ex(t) for t in invs]
        half = [_dotf(p['nm'][g] * m_ref[_M_OFF + lvl], inv_x[g]) for g in gs]
        invs = [invs[g] + _dotf(_bf(invs[g]), ex(half[g])) for g in gs]
        yield
    z0 = [z_ref[g] for g in gs]
    as0 = [_dotf(p['ar'][g], ex(z0[g])) for g in gs]
    vx = [_expand(p['v'][g], same) for g in gs]
    rhs = [as0[g][:c] + _dotf(p['mm'][g], vx[g]) for g in gs]
    yield
    uv = [jnp.concatenate([ex(_dotf(_bf(invs[g]), ex(rhs[g]))), vx[g]], axis=0) for g in gs]
    yield
    for g in gs:
        z_ref[g] = z0[g] * p['gct'][g] + _dotf(p['bkt'][g], uv[g])
    yield
    for g in gs:
        y_stores[g](as0[g][c:] + _dotf(p['pq'][g], uv[g]))


def _interleave(*gens):
    live = list(gens)
    while live:
        for gen in list(live):
            try:
                next(gen)
            except StopIteration:
                live.remove(gen)


_M_STRICT, _M_INCL, _M_EYE, _M_PAIR, _M_OFF = 0, 1, 4, 5, 6


def _rwkv_masks(c, n_h):
    t = np.arange(c)[:, None]
    s = np.arange(c)[None, :]
    masks = [s < t, s <= t, s > t, s >= t, s == t, ((t // 2) == (s // 2)) & (t != s)]
    b = 2
    while b < c:
        masks.append(((t // (2 * b)) == (s // (2 * b))) & ((t // b) != (s // b)))
        b *= 2
    masks = np.tile(np.stack(masks).astype(np.float32), (1, 1, n_h))
    r = np.arange(n_h * c)[:, None] // c
    l = np.arange(n_h * HEAD)[None, :] // HEAD
    return jnp.asarray(masks, BF16), jnp.asarray((r == l).astype(np.float32), BF16)


def _rwkv_kernel(*refs, latent, n_b, layer):
    (xf_ref, xb_ref, xfn_ref, xbn_ref, w0_ref, wup_ref, a0_ref, aup_ref, kk_ref, ka_ref,
     m_ref, same_ref) = refs[:12]
    if latent:
        s0_ref, yf_ref, yb_ref = refs[12:15]
    else:
        yf_ref, yb_ref, st_ref = refs[12:15]
    scr = refs[15:]
    z_scr = scr[0]
    names = [f[0] for f in _PREP_FIELDS]
    ci = pl.program_id(1)
    step = pl.program_id(0) * pl.num_programs(1) + ci
    view = lambda refs_, slot: {k: _SlotView(r, slot) for k, r in zip(names, refs_)}
    cur = view(scr[1:1 + len(names)], step % 2)
    nxt = view(scr[1:1 + len(names)], 1 - step % 2)
    mid = view(scr[1 + len(names):], 0)
    n_h = W_BR // HEAD
    c = CHUNK
    head = lambda h: slice(h * HEAD, (h + 1) * HEAD)
    streams = [(bi, d) for bi in range(n_b) for d in range(2)]
    dirs = [d for _, d in streams]
    params = (w0_ref[layer], wup_ref[0], a0_ref[layer], aup_ref[0], kk_ref[layer:layer + 1],
              ka_ref[layer:layer + 1], m_ref, same_ref)
    first = lambda d: slice(0, c) if d == 0 else slice(c, 2 * c)
    second = lambda d: slice(c, 2 * c) if d == 0 else slice(0, c)
    x_refs, xn_refs, y_refs = (xf_ref, xb_ref), (xfn_ref, xbn_ref), (yf_ref, yb_ref)

    def y_store(bi, d, rows):
        def store(y):
            y_refs[d][bi, rows, :] = y
        return store

    @pl.when(ci == 0)
    def _():
        for g, (bi, d) in enumerate(streams):
            if latent:
                z_scr[g] = jnp.concatenate([s0_ref[bi, 0, d, h].T for h in range(n_h)], axis=1)
            else:
                z_scr[g] = jnp.zeros(z_scr.shape[1:], F32)

    @pl.when(step == 0)
    def _():
        _interleave(_rwkv_prepare([x_refs[d][bi, first(d), :] for bi, d in streams], dirs, *params, cur))

    _interleave(_rwkv_advance(cur, m_ref, same_ref, z_scr, [y_store(bi, d, first(d)) for bi, d in streams]),
                _rwkv_prepare([x_refs[d][bi, second(d), :] for bi, d in streams], dirs, *params, mid))
    _interleave(_rwkv_advance(mid, m_ref, same_ref, z_scr, [y_store(bi, d, second(d)) for bi, d in streams]),
                _rwkv_prepare([xn_refs[d][bi, first(d), :] for bi, d in streams], dirs, *params, nxt))

    if not latent:
        @pl.when(ci == pl.num_programs(1) - 1)
        def _():
            for g, (bi, d) in enumerate(streams):
                z = z_scr[g]
                for h in range(n_h):
                    st_ref[bi, d, h] = z[:, head(h)].T


def _rwkv(pa, w0, wup, a0, aup, k_k, k_a, state, layer):
    b, t, _ = pa.shape
    latent = state is not None
    ns = t // (2 * CHUNK)
    n_h = W_BR // HEAD
    n_b = RWKV_BATCH
    masks, same = _rwkv_masks(CHUNK, n_h)
    full = lambda shape: pl.BlockSpec(shape, lambda i, j: (0,) * len(shape))
    one_layer = lambda shape: pl.BlockSpec((1,) + shape[1:], lambda i, j: (layer,) + (0,) * (len(shape) - 1))
    rows = lambda w, idx: pl.BlockSpec((n_b, 2 * CHUNK, w), idx)
    nxt_group = lambda i, j: jnp.minimum(i + (j + 1) // ns, b // n_b - 1)
    in_specs = [rows(WA, lambda i, j: (i, j, 0)),
                rows(WA, lambda i, j: (i, ns - 1 - j, 0)),
                rows(WA, lambda i, j: (nxt_group(i, j), (j + 1) % ns, 0)),
                rows(WA, lambda i, j: (nxt_group(i, j), (2 * ns - 2 - j) % ns, 0)),
                full(w0.shape), one_layer(wup.shape), full(a0.shape), one_layer(aup.shape),
                full(k_k.shape), full(k_a.shape), full(masks.shape), full(same.shape)]
    args = [pa, pa, pa, pa, w0, wup, a0, aup, k_k, k_a, masks, same]
    out_specs = [rows(W_BR, lambda i, j: (i, j, 0)), rows(W_BR, lambda i, j: (i, ns - 1 - j, 0))]
    out_shape = [jax.ShapeDtypeStruct((b, t, W_BR), F32)] * 2
    if latent:
        in_specs.append(pl.BlockSpec((n_b, 1, 2, n_h, HEAD, HEAD), lambda i, j: (i, layer, 0, 0, 0, 0)))
        args.append(state)
    else:
        out_specs.append(pl.BlockSpec((n_b, 2, n_h, HEAD, HEAD), lambda i, j: (i, 0, 0, 0, 0)))
        out_shape.append(jax.ShapeDtypeStruct((b, 2, n_h, HEAD, HEAD), F32))
    return pl.pallas_call(
        functools.partial(_rwkv_kernel, latent=latent, n_b=n_b, layer=layer),
        grid=(b // n_b, ns),
        in_specs=in_specs,
        out_specs=out_specs,
        out_shape=out_shape,
        scratch_shapes=([pltpu.VMEM((2 * n_b, CHUNK, W_BR), F32)]
                        + _prep_scratch(2, 2 * n_b) + _prep_scratch(1, 2 * n_b)),
        compiler_params=_params("rwkv", ("arbitrary", "arbitrary")),
        name="rwkv",
    )(*args)


def _lru_kernel(*refs, latent, layer):
    if latent:
        (x_ref, cw_ref, cb_ref, wa_ref, ba_ref, wx_ref, bx_ref, lam_ref, h0_ref, y_ref,
         a_scr, h_scr) = refs
    else:
        (x_ref, cw_ref, cb_ref, wa_ref, ba_ref, wx_ref, bx_ref, lam_ref, y_ref, st_ref,
         a_scr, h_scr) = refs
    x = x_ref[0, :, :W_BR]
    t = x.shape[0]
    row = lax.broadcasted_iota(jnp.int32, x.shape, 0)

    def shift_dn(z, k, fill):
        return jnp.where(row >= k, pltpu.roll(z, k, 0), fill)

    def shift_up(z, k, fill):
        return jnp.where(row < t - k, pltpu.roll(z, t - k, 0), fill)

    cw = cw_ref[layer]
    xc = (cb_ref[layer:layer + 1] + shift_dn(x, 2, 0.0) * cw[0:1] + shift_dn(x, 1, 0.0) * cw[1:2]
          + x * cw[2:3] + shift_up(x, 1, 0.0) * cw[3:4])
    xs = _split(xc)
    sub = row % SUBLANES
    n_blk = W_BR // HEAD
    zero = jnp.zeros((HEAD, HEAD), F32)

    def block_diag(w_ref, d):
        return jnp.concatenate([jnp.concatenate([w_ref[0, d, n] if m == n else zero for m in range(n_blk)], axis=1)
                                for n in range(n_blk)], axis=0)

    for d in range(2):
        gate_a = _sigmoid(_mm3(xs, _split(block_diag(wa_ref, d))) + ba_ref[layer, d:d + 1])
        gate_x = _sigmoid(_mm3(xs, _split(block_diag(wx_ref, d))) + bx_ref[layer, d:d + 1])
        log_a = -C_RG * gate_a * _softplus(-lam_ref[layer, d:d + 1])
        a = jnp.exp(log_a)
        u = jnp.sqrt(-jnp.tanh(log_a) * (a * a + 1.0)) * (gate_x * xc)
        k = 1
        while k < SUBLANES:
            if d == 0:
                keep = sub >= k
                sh = lambda z: pltpu.roll(z, k, 0)
            else:
                keep = sub < SUBLANES - k
                sh = lambda z: pltpu.roll(z, t - k, 0)
            u = a * jnp.where(keep, sh(u), 0.0) + u
            a = a * jnp.where(keep, sh(a), 1.0)
            k *= 2
        a_scr[d] = a
        h_scr[d] = u

    n_grp = t // SUBLANES
    if latent:
        carry0 = (h0_ref[0, 0, 0:1], h0_ref[0, 0, 1:2])
    else:
        carry0 = (jnp.zeros((1, W_BR), F32),) * 2

    def chain(i, carry):
        cf, cb = carry
        rf = pl.ds(pl.multiple_of(i * SUBLANES, SUBLANES), SUBLANES)
        rb = pl.ds(pl.multiple_of((n_grp - 1 - i) * SUBLANES, SUBLANES), SUBLANES)
        hf = h_scr[0, rf, :] + a_scr[0, rf, :] * cf
        hb = h_scr[1, rb, :] + a_scr[1, rb, :] * cb
        h_scr[0, rf, :] = hf
        h_scr[1, rb, :] = hb
        return hf[SUBLANES - 1:SUBLANES], hb[0:1]

    cf, cb = lax.fori_loop(0, n_grp, chain, carry0, unroll=4)
    y_ref[0] = h_scr[0] + h_scr[1]
    if not latent:
        st_ref[0] = jnp.concatenate([cf, cb], axis=0)


def _lru(pc, cw, cb, wa, ba, wx, bx, lam, state, layer):
    b, t, _ = pc.shape
    latent = state is not None
    full = lambda shape: pl.BlockSpec(shape, lambda i: (0,) * len(shape))
    one_layer = lambda shape: pl.BlockSpec((1,) + shape[1:], lambda i: (layer,) + (0,) * (len(shape) - 1))
    in_specs = [pl.BlockSpec((1, t, WC), lambda i: (i, 0, 0)),
                full(cw.shape), full(cb.shape), one_layer(wa.shape), full(ba.shape),
                one_layer(wx.shape), full(bx.shape), full(lam.shape)]
    args = [pc, cw, cb, wa, ba, wx, bx, lam]
    out_specs = [pl.BlockSpec((1, t, W_BR), lambda i: (i, 0, 0))]
    out_shape = [jax.ShapeDtypeStruct((b, t, W_BR), F32)]
    if latent:
        in_specs.append(pl.BlockSpec((1, 1, 2, W_BR), lambda i: (i, layer, 0, 0)))
        args.append(state)
    else:
        out_specs.append(pl.BlockSpec((1, 2, W_BR), lambda i: (i, 0, 0)))
        out_shape.append(jax.ShapeDtypeStruct((b, 2, W_BR), F32))
    return pl.pallas_call(
        functools.partial(_lru_kernel, latent=latent, layer=layer),
        grid=(b,),
        in_specs=in_specs,
        out_specs=out_specs,
        out_shape=out_shape,
        scratch_shapes=[pltpu.VMEM((2, t, W_BR), F32), pltpu.VMEM((2, t, W_BR), F32)],
        compiler_params=_params("lru", ("parallel",)),
        name="lru",
    )(*args)


def _win_kernel(*refs, latent, layer):
    if latent:
        sink_ref, q_ref, kp_ref, kc_ref, kn_ref, ck_ref, cv_ref, o_ref = refs
        j = pl.program_id(1)
        nq = pl.num_programs(1)
        qi = lax.broadcasted_iota(jnp.int32, (Q_BLK, Q_BLK), 0)
        ki = lax.broadcasted_iota(jnp.int32, (Q_BLK, Q_BLK), 1)
        q = q_ref[0]
        ctx_k, ctx_v = _bf(ck_ref[0, 0]), _bf(cv_ref[0, 0])
        pieces = [(_bf(r[0, :, :128]), _bf(r[0, :, 128:]), m, False) for r, m in
                  ((kp_ref, (ki >= qi) & (j > 0)), (kc_ref, None), (kn_ref, (ki <= qi) & (j < nq - 1)))]
        pieces += [(ctx_k[:, i:i + Q_BLK], ctx_v[:, i:i + Q_BLK], None, True)
                   for i in range(0, ctx_k.shape[1], Q_BLK)]
    else:
        sink_ref, x_ref, o_ref = refs
        q = x_ref[0, :, :256]
        k, v = _bf(x_ref[0, :, 256:384]), _bf(x_ref[0, :, 384:512])
        pieces = [(k[i:i + Q_BLK], v[i:i + Q_BLK], None, False) for i in range(0, k.shape[0], Q_BLK)]
    q = _bf(q * (HEAD ** -0.5 * LOG2_E))
    outs = []
    for h in range(2):
        hs = slice(h * HEAD, (h + 1) * HEAD)
        for g in range(2):
            qg = q[:, (2 * h + g) * HEAD:(2 * h + g + 1) * HEAD]
            sink = sink_ref[layer, 2 * h + g] * LOG2_E
            logits = []
            for kp, _, mask, fm in pieces:
                s = _dotf(qg, kp[hs]) if fm else _dotf(qg, kp[:, hs], _NT)
                logits.append(s if mask is None else jnp.where(mask, s, NEG_INF))
            mx = jnp.maximum(jnp.max(functools.reduce(jnp.maximum, logits), axis=-1, keepdims=True), sink)
            es = [jnp.exp2(s - mx) for s in logits]
            den = jnp.sum(functools.reduce(lambda a, b: a + b, es), axis=-1, keepdims=True) + jnp.exp2(sink - mx)
            acc = functools.reduce(lambda a, b: a + b,
                                   [_dotf(_bf(e), vp[hs], _NT) if fm else _dotf(_bf(e), vp[:, hs])
                                    for e, (_, vp, _, fm) in zip(es, pieces)])
            outs.append(acc / den)
    o_ref[0] = jnp.concatenate(outs, axis=-1)


def _win(pb, sink, cache_k, cache_v, layer):
    b, t, _ = pb.shape
    latent = cache_k is not None
    smem = pl.BlockSpec(memory_space=pltpu.SMEM)
    if latent:
        nq = t // Q_BLK
        grid = (b, nq)
        in_specs = [smem,
                    pl.BlockSpec((1, Q_BLK, 256), lambda i, j: (i, j, 0)),
                    pl.BlockSpec((1, Q_BLK, 256), lambda i, j: (i, jnp.maximum(j - 1, 0), 1)),
                    pl.BlockSpec((1, Q_BLK, 256), lambda i, j: (i, j, 1)),
                    pl.BlockSpec((1, Q_BLK, 256), lambda i, j: (i, jnp.minimum(j + 1, nq - 1), 1)),
                    pl.BlockSpec((1, 1) + cache_k.shape[2:], lambda i, j: (i, layer, 0, 0)),
                    pl.BlockSpec((1, 1) + cache_v.shape[2:], lambda i, j: (i, layer, 0, 0))]
        args = [sink, pb, pb, pb, pb, cache_k, cache_v]
        out_spec = pl.BlockSpec((1, Q_BLK, W_BR), lambda i, j: (i, j, 0))
        sem = ("parallel", "parallel")
    else:
        grid = (b,)
        in_specs = [smem, pl.BlockSpec((1, t, WB), lambda i: (i, 0, 0))]
        args = [sink, pb]
        out_spec = pl.BlockSpec((1, t, W_BR), lambda i: (i, 0, 0))
        sem = ("parallel",)
    return pl.pallas_call(
        functools.partial(_win_kernel, latent=latent, layer=layer),
        grid=grid,
        in_specs=in_specs,
        out_specs=out_spec,
        out_shape=jax.ShapeDtypeStruct((b, t, W_BR), F32),
        compiler_params=_params("win", sem),
        name="win",
    )(*args)


def _diff_kernel(*refs, latent, lam_init, layer):
    if latent:
        lam_ref, g_ref, q_ref, k_ref, v_ref, ck_ref, cv_ref, o_ref = refs
    else:
        lam_ref, g_ref, q_ref, k_ref, v_ref, o_ref = refs
    lp = lam_ref[layer]
    lam = (jnp.exp(jnp.sum(lp[0:1] * lp[1:2], axis=-1, keepdims=True))
           - jnp.exp(jnp.sum(lp[2:3] * lp[3:4], axis=-1, keepdims=True)) + lam_init)
    q = _bf(q_ref[0] * (DQ_D ** -0.5 * LOG2_E))
    key_sets = [(_bf(k_ref[0]), _bf(v_ref[0]), False)]
    if latent:
        key_sets.append((_bf(ck_ref[0, 0]), _bf(cv_ref[0, 0]), True))
    lane = lax.broadcasted_iota(jnp.int32, (1, W_BR), 1)
    out = jnp.zeros(q.shape, F32)
    for h in range(W_BR // HEAD):
        probs = []
        for m in range(2):
            qm = jnp.where(lane // DQ_D == 2 * h + m, q, jnp.zeros((), BF16))
            logits = [_dotf(qm, kk) if fm else _dotf(qm, kk, _NT) for kk, _, fm in key_sets]
            mx = functools.reduce(jnp.maximum, [jnp.max(s, axis=-1, keepdims=True) for s in logits])
            es = [jnp.exp2(s - mx) for s in logits]
            den = functools.reduce(lambda a, b: a + b, [jnp.sum(e, axis=-1, keepdims=True) for e in es])
            inv = 1.0 / den
            probs.append([e * inv for e in es])
        o = None
        for i, (_, vv, fm) in enumerate(key_sets):
            pi = _bf(probs[0][i] - lam * probs[1][i])
            oi = _dotf(pi, vv, _NT) if fm else _dotf(pi, vv)
            o = oi if o is None else o + oi
        out = jnp.where(lane // HEAD == h, o, out)
    ms = _mm2(out * out, _head_ones(W_BR)) * (1.0 / HEAD)
    gain = jnp.concatenate([g_ref[layer:layer + 1]] * (W_BR // HEAD), axis=1)
    o_ref[0] = out * lax.rsqrt(ms + NORM_EPS) * gain * (1.0 - lam_init)


def _diff(pd, lam_p, subln_g, cache_k, cache_v, layer, lam_init):
    b, t, _ = pd.shape
    latent = cache_k is not None
    tq = 256
    in_specs = [pl.BlockSpec(lam_p.shape, lambda i, j: (0, 0, 0)),
                pl.BlockSpec(subln_g.shape, lambda i, j: (0, 0)),
                pl.BlockSpec((1, tq, 256), lambda i, j: (i, j, 0)),
                pl.BlockSpec((1, t, 256), lambda i, j: (i, 0, 1)),
                pl.BlockSpec((1, t, 256), lambda i, j: (i, 0, 2))]
    args = [lam_p, subln_g, pd, pd, pd]
    if latent:
        in_specs += [pl.BlockSpec((1, 1) + cache_k.shape[2:], lambda i, j: (i, layer, 0, 0)),
                     pl.BlockSpec((1, 1) + cache_v.shape[2:], lambda i, j: (i, layer, 0, 0))]
        args += [cache_k, cache_v]
    return pl.pallas_call(
        functools.partial(_diff_kernel, latent=latent, lam_init=lam_init, layer=layer),
        grid=(b, t // tq),
        in_specs=in_specs,
        out_specs=pl.BlockSpec((1, tq, W_BR), lambda i, j: (i, j, 0)),
        out_shape=jax.ShapeDtypeStruct((b, t, W_BR), F32),
        compiler_params=_params("diff", ("parallel", "parallel")),
        name="diff",
    )(*args)


def _out_kernel(x_ref, mod_ref, gpost_ref, w_ref, pa_ref, yf_ref, yb_ref, gng_ref, gnb_ref, rk_ref,
                ywin_ref, bg_ref, ylru_ref, cg_ref, ydiff_ref, dg_ref, o_ref, *, layer):
    lrow = slice(layer, layer + 1)
    ones = _head_ones(W_BR)
    pa = pa_ref[0]
    r, k, v, ag = pa[:, 0:256], pa[:, 256:512], pa[:, 512:768], pa[:, 896:1152]
    y = yf_ref[0] + yb_ref[0]
    mu = _mm2(y, ones) * (1.0 / HEAD)
    yc = y - mu
    var = _mm2(yc * yc, ones) * (1.0 / HEAD)
    ya = yc * lax.rsqrt(var + GN_EPS) * gng_ref[lrow] + gnb_ref[lrow]
    ya = ya + _mm2(r * k * rk_ref[lrow], ones) * v
    mix = jnp.concatenate([ya * _silu(ag), ywin_ref[0] * _silu(bg_ref[0]),
                           ylru_ref[0] * _silu(cg_ref[0]), ydiff_ref[0] * _silu(dg_ref[0])], axis=-1)
    o = jnp.dot(mix.astype(BF16), w_ref[0], preferred_element_type=F32)
    o = o * lax.rsqrt(jnp.mean(o * o, -1, keepdims=True) + NORM_EPS) * gpost_ref[lrow]
    o_ref[0] = x_ref[0] + mod_ref[0][:, 2 * D_MODEL:] * o


def _out(x, mod_l, g_post, w_out_bf, layer, latent, pa, yf, yb, gn_g, gn_b, r_k, ywin, pb, ylru, pc, ydiff, pd):
    b, t, _ = x.shape
    tm = ROW_TILE
    mod_idx = (lambda i, j: (1 + i, 0, 0)) if latent else (lambda i, j: (0, 0, 0))
    rows = lambda w, c=0: pl.BlockSpec((1, tm, w), lambda i, j: (i, j, c))
    vec = pl.BlockSpec(gn_g.shape, lambda i, j: (0, 0))
    in_specs = [rows(D_MODEL),
                pl.BlockSpec((1, 1, 3 * D_MODEL), mod_idx),
                pl.BlockSpec(g_post.shape, lambda i, j: (0, 0)),
                pl.BlockSpec((1, D_MODEL, D_MODEL), lambda i, j: (layer, 0, 0)),
                rows(WA), rows(W_BR), rows(W_BR), vec, vec, vec,
                rows(W_BR), rows(W_BR, 2), rows(W_BR), rows(W_BR, 1), rows(W_BR), rows(W_BR, 3)]
    return pl.pallas_call(
        functools.partial(_out_kernel, layer=layer),
        grid=(b, t // tm),
        in_specs=in_specs,
        out_specs=rows(D_MODEL),
        out_shape=jax.ShapeDtypeStruct((b, t, D_MODEL), F32),
        compiler_params=_params("out", ("parallel", "parallel")),
        name="out",
    )(x, mod_l, g_post, w_out_bf, pa, yf, yb, gn_g, gn_b, r_k, ywin, pb, ylru, pc, ydiff, pd)


def _layer(x, mod_l, layer, lam_init, wts, cache, tables):
    latent = cache is not None
    pa, pb, pc, pd, *ctx_t = _project(x, mod_l, wts['g_pre'], wts['w_in_bf'], layer, tables)
    rw = _rwkv(pa, wts['rwkv_w0'], wts['rwkv_w_up'], wts['rwkv_a0'], wts['rwkv_a_up'], wts['rwkv_k_k'],
               wts['rwkv_k_a'], cache['rwkv'] if latent else None, layer)
    lr = _lru(pc, wts['lru_conv_w'], wts['lru_conv_b'], wts['lru_wa'], wts['lru_ba'], wts['lru_wx'],
              wts['lru_bx'], wts['lru_lambda'], cache['lru'] if latent else None, layer)
    ywin = _win(pb, wts['win_sink'], cache['win_k'] if latent else None,
                cache['win_v'] if latent else None, layer)
    ydiff = _diff(pd, wts['diff_lambda'], wts['diff_subln_g'],
                  cache['diff_k'] if latent else None, cache['diff_v'] if latent else None, layer, lam_init)
    y = _out(x, mod_l, wts['g_post'], wts['w_out_bf'], layer, latent, pa, rw[0], rw[1],
             wts['rwkv_gn_g'], wts['rwkv_gn_b'], wts['rwkv_r_k'], ywin, pb, lr[0], pc, ydiff, pd)
    new_cache = None if latent else (ctx_t, rw[2], lr[1])
    return y, new_cache


def kernel(x_prompt, x_sample, c, cache_win_k, cache_win_v, cache_diff_k, cache_diff_v, state_rwkv, state_lru,
           c_ctx, w_mod, b_mod, g_pre, g_post, w_in, w_out,
           rwkv_w0, rwkv_w_up, rwkv_a0, rwkv_a_up, rwkv_k_k, rwkv_k_a, rwkv_r_k, rwkv_gn_g, rwkv_gn_b,
           win_sink, lru_conv_w, lru_conv_b, lru_wa, lru_ba, lru_wx, lru_bx, lru_lambda,
           diff_lambda, diff_subln_g):
    n_b, seq = x_prompt.shape[:2]
    n_dec, dec_seq = x_sample.shape[:2]
    past = cache_win_k.shape[2]
    wts = dict(g_pre=g_pre, g_post=g_post, w_in_bf=w_in.astype(BF16), w_out_bf=w_out.astype(BF16),
               rwkv_w0=rwkv_w0, rwkv_w_up=rwkv_w_up, rwkv_a0=rwkv_a0, rwkv_a_up=rwkv_a_up,
               rwkv_k_k=rwkv_k_k, rwkv_k_a=rwkv_k_a, rwkv_r_k=rwkv_r_k.reshape(DEPTH, W_BR), rwkv_gn_g=rwkv_gn_g,
               rwkv_gn_b=rwkv_gn_b, win_sink=win_sink, lru_conv_w=lru_conv_w, lru_conv_b=lru_conv_b,
               lru_wa=lru_wa, lru_ba=lru_ba, lru_wx=lru_wx, lru_bx=lru_bx, lru_lambda=lru_lambda,
               diff_lambda=diff_lambda, diff_subln_g=diff_subln_g)
    fm = lambda a: jnp.moveaxis(a, 2, -1).reshape(n_dec, DEPTH, -1, past)
    cache = dict(win_k=fm(cache_win_k), win_v=fm(cache_win_v), diff_k=fm(cache_diff_k), diff_v=fm(cache_diff_v),
                 rwkv=state_rwkv, lru=state_lru)
    cvec = jnp.concatenate([c_ctx[None], c, jnp.zeros((8 - 1 - n_dec, D_MODEL), F32)], axis=0)
    mod = _modulation(cvec, w_mod, b_mod)
    cos_b, sin_b = _rope_tables(dec_seq, HEAD, 384)
    cos_d, sin_d = _rope_tables(dec_seq, DQ_D, 512)
    tables = (cos_b, sin_b, cos_d, sin_d)

    y_p, y_s = x_prompt, x_sample
    ctx = []
    for l in range(DEPTH):
        lam_init = 0.8 - 0.6 * math.exp(-0.3 * l)
        mod_l = mod[l].reshape(8, 1, 3 * D_MODEL)
        y_p, nc = _layer(y_p, mod_l, l, lam_init, wts, None, None)
        ctx.append(nc)
        y_s, _ = _layer(y_s, mod_l, l, lam_init, wts, cache, tables)
    stack = lambda f: jnp.stack([f(ct) for ct in ctx], axis=1)
    tm_ = lambda i, dims: jnp.moveaxis(stack(lambda ct: ct[0][i]).reshape((n_b, DEPTH) + dims + (seq,)), -1, 2)
    new_win_k = tm_(0, (2, HEAD))
    new_win_v = tm_(1, (2, HEAD))
    new_diff_k = tm_(2, (4, 2, DQ_D))
    new_diff_v = tm_(3, (4, HEAD))
    new_state_rwkv = stack(lambda ct: ct[1])
    new_state_lru = stack(lambda ct: ct[2])
    return (y_p, y_s, new_win_k, new_win_v, new_diff_k, new_diff_v, new_state_rwkv, new_state_lru)
```

```python
import functools
import math

import numpy as np
import jax
import jax.numpy as jnp
from jax import lax
from jax.experimental import pallas as pl
from jax.experimental.pallas import tpu as pltpu

F32 = jnp.float32
BF16 = jnp.bfloat16
HI = lax.Precision.HIGHEST

D_MODEL = 1024
DEPTH = 2
GRID_W = 64
ROPE_BASE = 10000.0
NORM_EPS = 1e-6
NEG_INF = -1e30
LOG2_E = math.log2(math.e)
GN_EPS = 64e-5
C_RG = 8.0
W_BR = 256
HEAD = 64
SUBLANES = 8
LORA = 64
DQ_D = 32
WINDOW = 128
Q_BLK = 128
P_TOTAL = 3456
WA, WB, WC, WD = 1152, 768, 512, 1024
CHUNK = 64
RWKV_BATCH = 4
ROW_TILE = 256
VMEM_LIMIT = 58 * 1024 * 1024


_NN = (((1,), (0,)), ((), ()))
_NT = (((1,), (1,)), ((), ()))
_TN = (((0,), (0,)), ((), ()))


def _dot_hi(a, b):
    return jnp.dot(a, b, precision=HI, preferred_element_type=F32)


def _bf(x):
    return x.astype(BF16)


def _dotf(x, y, dims=_NN):
    return lax.dot_general(x, y, dims, preferred_element_type=F32)


def _split(x):
    hi = x.astype(BF16)
    return hi, (x - hi.astype(F32)).astype(BF16)


def _mm3(a, b, dims=_NN):
    return _dotf(a[0], b[0], dims) + (_dotf(a[0], b[1], dims) + _dotf(a[1], b[0], dims))


def _mm2(x, m01, left=False):
    hi, lo = _split(x)
    if left:
        return _dotf(m01, hi) + _dotf(m01, lo)
    return _dotf(hi, m01) + _dotf(lo, m01)


def _dot_bf(a, b):
    return jnp.dot(a.astype(BF16), b.astype(BF16), preferred_element_type=F32)


def _dot_nt_bf(a, b):
    return lax.dot_general(a.astype(BF16), b.astype(BF16), (((1,), (1,)), ((), ())),
                           preferred_element_type=F32)


def _sigmoid(x):
    return 1.0 / (1.0 + jnp.exp(-x))


def _silu(x):
    return x * _sigmoid(x)


def _softplus(x):
    return jnp.maximum(x, 0.0) + jnp.log1p(jnp.exp(-jnp.abs(x)))


def _head_ones(n):
    r = lax.broadcasted_iota(jnp.int32, (n, n), 0) // HEAD
    c = lax.broadcasted_iota(jnp.int32, (n, n), 1) // HEAD
    return jnp.where(r == c, 1.0, 0.0).astype(BF16)


def _params(sem):
    return pltpu.CompilerParams(dimension_semantics=sem, vmem_limit_bytes=VMEM_LIMIT)


def _mod_kernel(c_ref, w_ref, b_ref, o_ref):
    o_ref[0] = _mm3(_split(_silu(c_ref[...])), _split(w_ref[0])) + b_ref[0]


def _modulation(cvec, w_mod, b_mod):
    n_l = w_mod.shape[0]
    tn = 512
    return pl.pallas_call(
        _mod_kernel,
        grid=(n_l, 3 * D_MODEL // tn),
        in_specs=[pl.BlockSpec((8, D_MODEL), lambda l, j: (0, 0)),
                  pl.BlockSpec((1, D_MODEL, tn), lambda l, j: (l, 0, j)),
                  pl.BlockSpec((1, 1, tn), lambda l, j: (l, 0, j))],
        out_specs=pl.BlockSpec((1, 8, tn), lambda l, j: (l, 0, j)),
        out_shape=jax.ShapeDtypeStruct((n_l, 8, 3 * D_MODEL), F32),
        compiler_params=_params(("parallel", "parallel")),
        name="mod",
    )(cvec, w_mod, b_mod.reshape(n_l, 1, 3 * D_MODEL))


def _rope(x, cos, sin_signed, off):
    w = x.shape[-1]
    lane = lax.broadcasted_iota(jnp.int32, x.shape, 1)
    first = (lane % (2 * off)) < off
    partner = jnp.where(first, pltpu.roll(x, w - off, 1), pltpu.roll(x, off, 1))
    return x * cos + partner * sin_signed


def _proj_kernel(*refs, latent, layer):
    if latent:
        x_ref, mod_ref, g_ref, w_ref, cb_ref, sb_ref, cd_ref, sd_ref, oa, ob, oc, od = refs
    else:
        x_ref, mod_ref, g_ref, w_ref, oa, ob, oc, od, o_wk, o_wv, o_dk, o_dv = refs
    x = x_ref[0]
    y = x * lax.rsqrt(jnp.mean(x * x, -1, keepdims=True) + NORM_EPS) * g_ref[layer:layer + 1]
    m = mod_ref[0]
    h = y * (1.0 + m[:, D_MODEL:2 * D_MODEL]) + m[:, :D_MODEL]
    p = jnp.dot(h.astype(BF16), w_ref[0], preferred_element_type=F32)
    oa[0] = p[:, :WA]
    pb = p[:, WA:WA + WB]
    pd = p[:, WA + WB + WC:]
    if latent:
        ob[0, :, :384] = _rope(pb[:, :384], cb_ref[...], sb_ref[...], 16)
        ob[0, :, 384:] = pb[:, 384:]
        od[0, :, :512] = _rope(pd[:, :512], cd_ref[...], sd_ref[...], 8)
        od[0, :, 512:] = pd[:, 512:]
    else:
        ob[0] = pb
        od[0] = pd
        o_wk[0] = pb[:, 256:384].T
        o_wv[0] = pb[:, 384:512].T
        o_dk[0] = pd[:, 256:512].T
        o_dv[0] = pd[:, 512:768].T
    oc[0] = p[:, WA + WB:WA + WB + WC]


def _project(x, mod_l, g_pre, w_in_bf, layer, tables):
    b, t, _ = x.shape
    latent = tables is not None
    tm = ROW_TILE
    mod_idx = (lambda i, j: (1 + i, 0, 0)) if latent else (lambda i, j: (0, 0, 0))
    in_specs = [pl.BlockSpec((1, tm, D_MODEL), lambda i, j: (i, j, 0)),
                pl.BlockSpec((1, 1, 3 * D_MODEL), mod_idx),
                pl.BlockSpec(g_pre.shape, lambda i, j: (0, 0)),
                pl.BlockSpec((1, D_MODEL, P_TOTAL), lambda i, j: (layer, 0, 0))]
    args = [x, mod_l, g_pre, w_in_bf]
    if latent:
        for tab in tables:
            in_specs.append(pl.BlockSpec((tm, tab.shape[1]), lambda i, j: (j, 0)))
            args.append(tab)
    widths = (WA, WB, WC, WD)
    out_specs = [pl.BlockSpec((1, tm, w), lambda i, j: (i, j, 0)) for w in widths]
    out_shape = [jax.ShapeDtypeStruct((b, t, w), F32) for w in widths]
    if not latent:
        for w in (128, 128, W_BR, W_BR):
            out_specs.append(pl.BlockSpec((1, w, tm), lambda i, j: (i, 0, j)))
            out_shape.append(jax.ShapeDtypeStruct((b, w, t), F32))
    return pl.pallas_call(
        functools.partial(_proj_kernel, latent=latent, layer=layer),
        grid=(b, t // tm),
        in_specs=in_specs,
        out_specs=out_specs,
        out_shape=out_shape,
        compiler_params=_params(("parallel", "parallel")),
        name="proj",
    )(*args)


def _rope_tables(t, head_dim, n_lanes):
    half = head_dim // 2
    quarter = half // 2
    pos = np.arange(t)
    row = (pos // GRID_W).astype(np.float32)
    col = (pos % GRID_W).astype(np.float32)
    inv = np.float32(ROPE_BASE) ** (-np.arange(0, half, 2, dtype=np.float32) / np.float32(half))
    lane = np.arange(n_lanes) % head_dim
    in_half = lane % half
    p = np.where((lane < half)[None, :], row[:, None], col[:, None])
    ang = (p * inv[in_half % quarter][None, :]).astype(np.float64)
    sign = np.where(in_half < quarter, -1.0, 1.0)[None, :]
    return jnp.asarray(np.cos(ang), F32), jnp.asarray(np.sin(ang) * sign, F32)


_PREP_FIELDS = (('ar', 2, 1, BF16), ('nm', 1, 1, BF16), ('mm', 1, 1, BF16), ('pq', 1, 2, BF16),
                ('v', 1, 1, BF16), ('bkt', 1, 2, BF16), ('gct', 1, 1, F32))


def _prep_scratch(slots, n_streams):
    return [pltpu.VMEM((slots, n_streams, rows * CHUNK, lanes * W_BR), dt) for _, rows, lanes, dt in _PREP_FIELDS]


class _SlotView:
    def __init__(self, ref, slot):
        self.ref, self.slot, self.shape = ref, slot, ref.shape[1:]

    def __getitem__(self, idx):
        return self.ref[(self.slot,) + (idx if isinstance(idx, tuple) else (idx,))]

    def __setitem__(self, idx, val):
        self.ref[(self.slot,) + (idx if isinstance(idx, tuple) else (idx,))] = val


def _head_transpose(x):
    xt = x.T
    return jnp.concatenate([xt[h * HEAD:(h + 1) * HEAD] for h in range(x.shape[1] // HEAD)], axis=1)


def _expand(x, same):
    return jnp.concatenate([x] * (same.shape[0] // x.shape[0]), axis=0) * same


def _rwkv_prepare(xs, dirs, w0_ref, wup_ref, a0_ref, aup_ref, k_k, k_a, m_ref, same_ref, out):
    c = xs[0].shape[0]
    ones = _head_ones(W_BR)
    ti = lax.broadcasted_iota(jnp.int32, (c, c), 0)
    si = lax.broadcasted_iota(jnp.int32, (c, c), 1)
    cums = [jnp.where(si <= ti, 1.0, 0.0).astype(BF16), jnp.where(si >= ti, 1.0, 0.0).astype(BF16)]
    same = same_ref[...]

    st = []
    for xc, d in zip(xs, dirs):
        r, k, v = xc[:, 0:256], xc[:, 256:512], xc[:, 512:768]
        wd, ad = xc[:, 768:832], xc[:, 832:896]
        kk = k * k_k
        kk = kk * lax.rsqrt(_mm2(kk * kk, ones) + 1e-12)
        z = w0_ref[d:d + 1] + _mm3(_split(jnp.tanh(wd)), _split(wup_ref[d]))
        e = jnp.exp(-_softplus(-z) - 0.5)
        a = _sigmoid(a0_ref[d:d + 1] + _mm3(_split(ad), _split(aup_ref[d])))
        st.append(dict(r=r, v=v, e=e, kd=k * (1.0 + (a - 1.0) * k_a), alpha=-kk, beta=kk * a))
    yield
    for s, d in zip(st, dirs):
        l_incl = _mm2(s['e'], cums[d], left=True)
        l_tot = jnp.sum(s['e'], axis=0, keepdims=True)
        grow = jnp.exp(l_incl)
        tail = jnp.exp(l_incl - l_tot)
        s.update(ar=jnp.concatenate([_bf(s['alpha'] * jnp.exp(s['e'] - l_incl)),
                                     _bf(s['r'] * jnp.exp(-l_incl))], axis=0),
                 b_t=_bf(s['beta'] * grow), k_t=_bf(s['kd'] * grow),
                 b_h=s['beta'] * tail, k_h=s['kd'] * tail,
                 g_c=jnp.broadcast_to(jnp.exp(-l_tot), (c, W_BR)))
    yield
    for g, (s, d) in enumerate(zip(st, dirs)):
        out['ar'][g] = s['ar']
        rhs_t = jnp.concatenate([_expand(s['b_t'], same), _expand(s['k_t'], same)], axis=0)
        g4 = _dotf(s['ar'], rhs_t, _NT)
        strict = m_ref[_M_STRICT + 2 * d]
        incl = m_ref[_M_INCL + 2 * d]
        out['nm'][g] = _bf(g4[:c, :W_BR]) * strict
        out['mm'][g] = _bf(g4[:c, W_BR:]) * strict
        out['pq'][g, :, :W_BR] = _bf(g4[c:, :W_BR]) * incl
        out['pq'][g, :, W_BR:] = _bf(g4[c:, W_BR:]) * incl
    yield
    for g, s in enumerate(st):
        out['v'][g] = _bf(s['v'])
        out['bkt'][g] = jnp.concatenate([_bf(_head_transpose(s['b_h'])), _bf(_head_transpose(s['k_h']))], axis=1)
        out['gct'][g] = _head_transpose(s['g_c'])


def _rwkv_advance(p, m_ref, same_ref, z_ref, y_stores):
    n_streams, c = p['nm'].shape[0], p['nm'].shape[1]
    gs = range(n_streams)
    same = same_ref[...]
    ex = lambda x: _expand(_bf(x), same)
    invs = [(p['nm'][g] * m_ref[_M_PAIR]).astype(F32) + m_ref[_M_EYE].astype(F32) for g in gs]
    for lvl in range(int(math.log2(c)) - 1):
        inv_x = [ex(t) for t in invs]
        half = [_dotf(p['nm'][g] * m_ref[_M_OFF + lvl], inv_x[g]) for g in gs]
        invs = [invs[g] + _dotf(_bf(invs[g]), ex(half[g])) for g in gs]
        yield
    z0 = [z_ref[g] for g in gs]
    as0 = [_dotf(p['ar'][g], ex(z0[g])) for g in gs]
    vx = [_expand(p['v'][g], same) for g in gs]
    rhs = [as0[g][:c] + _dotf(p['mm'][g], vx[g]) for g in gs]
    yield
    uv = [jnp.concatenate([ex(_dotf(_bf(invs[g]), ex(rhs[g]))), vx[g]], axis=0) for g in gs]
    yield
    for g in gs:
        z_ref[g] = z0[g] * p['gct'][g] + _dotf(p['bkt'][g], uv[g])
    yield
    for g in gs:
        y_stores[g](as0[g][c:] + _dotf(p['pq'][g], uv[g]))


def _interleave(*gens):
    live = list(gens)
    while live:
        for gen in list(live):
            try:
                next(gen)
            except StopIteration:
                live.remove(gen)


_M_STRICT, _M_INCL, _M_EYE, _M_PAIR, _M_OFF = 0, 1, 4, 5, 6


def _rwkv_masks(c, n_h):
    t = np.arange(c)[:, None]
    s = np.arange(c)[None, :]
    masks = [s < t, s <= t, s > t, s >= t, s == t, ((t // 2) == (s // 2)) & (t != s)]
    b = 2
    while b < c:
        masks.append(((t // (2 * b)) == (s // (2 * b))) & ((t // b) != (s // b)))
        b *= 2
    masks = np.tile(np.stack(masks).astype(np.float32), (1, 1, n_h))
    r = np.arange(n_h * c)[:, None] // c
    l = np.arange(n_h * HEAD)[None, :] // HEAD
    return jnp.asarray(masks, BF16), jnp.asarray((r == l).astype(np.float32), BF16)


def _rwkv_kernel(*refs, latent, n_b, layer):
    (xf_ref, xb_ref, xfn_ref, xbn_ref, w0_ref, wup_ref, a0_ref, aup_ref, kk_ref, ka_ref,
     m_ref, same_ref) = refs[:12]
    if latent:
        s0_ref, yf_ref, yb_ref = refs[12:15]
    else:
        yf_ref, yb_ref, st_ref = refs[12:15]
    scr = refs[15:]
    z_scr = scr[0]
    names = [f[0] for f in _PREP_FIELDS]
    ci = pl.program_id(1)
    step = pl.program_id(0) * pl.num_programs(1) + ci
    view = lambda refs_, slot: {k: _SlotView(r, slot) for k, r in zip(names, refs_)}
    cur = view(scr[1:1 + len(names)], step % 2)
    nxt = view(scr[1:1 + len(names)], 1 - step % 2)
    mid = view(scr[1 + len(names):], 0)
    n_h = W_BR // HEAD
    c = CHUNK
    head = lambda h: slice(h * HEAD, (h + 1) * HEAD)
    streams = [(bi, d) for bi in range(n_b) for d in range(2)]
    dirs = [d for _, d in streams]
    params = (w0_ref[layer], wup_ref[0], a0_ref[layer], aup_ref[0], kk_ref[layer:layer + 1],
              ka_ref[layer:layer + 1], m_ref, same_ref)
    first = lambda d: slice(0, c) if d == 0 else slice(c, 2 * c)
    second = lambda d: slice(c, 2 * c) if d == 0 else slice(0, c)
    x_refs, xn_refs, y_refs = (xf_ref, xb_ref), (xfn_ref, xbn_ref), (yf_ref, yb_ref)

    def y_store(bi, d, rows):
        def store(y):
            y_refs[d][bi, rows, :] = y
        return store

    @pl.when(ci == 0)
    def _():
        for g, (bi, d) in enumerate(streams):
            if latent:
                z_scr[g] = jnp.concatenate([s0_ref[bi, 0, d, h].T for h in range(n_h)], axis=1)
            else:
                z_scr[g] = jnp.zeros(z_scr.shape[1:], F32)

    @pl.when(step == 0)
    def _():
        _interleave(_rwkv_prepare([x_refs[d][bi, first(d), :] for bi, d in streams], dirs, *params, cur))

    _interleave(_rwkv_advance(cur, m_ref, same_ref, z_scr, [y_store(bi, d, first(d)) for bi, d in streams]),
                _rwkv_prepare([x_refs[d][bi, second(d), :] for bi, d in streams], dirs, *params, mid))
    _interleave(_rwkv_advance(mid, m_ref, same_ref, z_scr, [y_store(bi, d, second(d)) for bi, d in streams]),
                _rwkv_prepare([xn_refs[d][bi, first(d), :] for bi, d in streams], dirs, *params, nxt))

    if not latent:
        @pl.when(ci == pl.num_programs(1) - 1)
        def _():
            for g, (bi, d) in enumerate(streams):
                z = z_scr[g]
                for h in range(n_h):
                    st_ref[bi, d, h] = z[:, head(h)].T


def _rwkv(pa, w0, wup, a0, aup, k_k, k_a, state, layer):
    b, t, _ = pa.shape
    latent = state is not None
    ns = t // (2 * CHUNK)
    n_h = W_BR // HEAD
    n_b = RWKV_BATCH
    masks, same = _rwkv_masks(CHUNK, n_h)
    full = lambda shape: pl.BlockSpec(shape, lambda i, j: (0,) * len(shape))
    one_layer = lambda shape: pl.BlockSpec((1,) + shape[1:], lambda i, j: (layer,) + (0,) * (len(shape) - 1))
    rows = lambda w, idx: pl.BlockSpec((n_b, 2 * CHUNK, w), idx)
    nxt_group = lambda i, j: jnp.minimum(i + (j + 1) // ns, b // n_b - 1)
    in_specs = [rows(WA, lambda i, j: (i, j, 0)),
                rows(WA, lambda i, j: (i, ns - 1 - j, 0)),
                rows(WA, lambda i, j: (nxt_group(i, j), (j + 1) % ns, 0)),
                rows(WA, lambda i, j: (nxt_group(i, j), (2 * ns - 2 - j) % ns, 0)),
                full(w0.shape), one_layer(wup.shape), full(a0.shape), one_layer(aup.shape),
                full(k_k.shape), full(k_a.shape), full(masks.shape), full(same.shape)]
    args = [pa, pa, pa, pa, w0, wup, a0, aup, k_k, k_a, masks, same]
    out_specs = [rows(W_BR, lambda i, j: (i, j, 0)), rows(W_BR, lambda i, j: (i, ns - 1 - j, 0))]
    out_shape = [jax.ShapeDtypeStruct((b, t, W_BR), F32)] * 2
    if latent:
        in_specs.append(pl.BlockSpec((n_b, 1, 2, n_h, HEAD, HEAD), lambda i, j: (i, layer, 0, 0, 0, 0)))
        args.append(state)
    else:
        out_specs.append(pl.BlockSpec((n_b, 2, n_h, HEAD, HEAD), lambda i, j: (i, 0, 0, 0, 0)))
        out_shape.append(jax.ShapeDtypeStruct((b, 2, n_h, HEAD, HEAD), F32))
    return pl.pallas_call(
        functools.partial(_rwkv_kernel, latent=latent, n_b=n_b, layer=layer),
        grid=(b // n_b, ns),
        in_specs=in_specs,
        out_specs=out_specs,
        out_shape=out_shape,
        scratch_shapes=([pltpu.VMEM((2 * n_b, CHUNK, W_BR), F32)]
                        + _prep_scratch(2, 2 * n_b) + _prep_scratch(1, 2 * n_b)),
        compiler_params=_params(("arbitrary", "arbitrary")),
        name="rwkv",
    )(*args)


def _lru_kernel(*refs, latent, layer):
    if latent:
        (x_ref, cw_ref, cb_ref, wa_ref, ba_ref, wx_ref, bx_ref, lam_ref, h0_ref, y_ref,
         a_scr, h_scr) = refs
    else:
        (x_ref, cw_ref, cb_ref, wa_ref, ba_ref, wx_ref, bx_ref, lam_ref, y_ref, st_ref,
         a_scr, h_scr) = refs
    x = x_ref[0, :, :W_BR]
    t = x.shape[0]
    row = lax.broadcasted_iota(jnp.int32, x.shape, 0)

    def shift_dn(z, k, fill):
        return jnp.where(row >= k, pltpu.roll(z, k, 0), fill)

    def shift_up(z, k, fill):
        return jnp.where(row < t - k, pltpu.roll(z, t - k, 0), fill)

    cw = cw_ref[layer]
    xc = (cb_ref[layer:layer + 1] + shift_dn(x, 2, 0.0) * cw[0:1] + shift_dn(x, 1, 0.0) * cw[1:2]
          + x * cw[2:3] + shift_up(x, 1, 0.0) * cw[3:4])
    xs = _split(xc)
    sub = row % SUBLANES
    n_blk = W_BR // HEAD
    zero = jnp.zeros((HEAD, HEAD), F32)

    def block_diag(w_ref, d):
        return jnp.concatenate([jnp.concatenate([w_ref[0, d, n] if m == n else zero for m in range(n_blk)], axis=1)
                                for n in range(n_blk)], axis=0)

    for d in range(2):
        gate_a = _sigmoid(_mm3(xs, _split(block_diag(wa_ref, d))) + ba_ref[layer, d:d + 1])
        gate_x = _sigmoid(_mm3(xs, _split(block_diag(wx_ref, d))) + bx_ref[layer, d:d + 1])
        log_a = -C_RG * gate_a * _softplus(-lam_ref[layer, d:d + 1])
        a = jnp.exp(log_a)
        u = jnp.sqrt(-jnp.tanh(log_a) * (a * a + 1.0)) * (gate_x * xc)
        k = 1
        while k < SUBLANES:
            if d == 0:
                keep = sub >= k
                sh = lambda z: pltpu.roll(z, k, 0)
            else:
                keep = sub < SUBLANES - k
                sh = lambda z: pltpu.roll(z, t - k, 0)
            u = a * jnp.where(keep, sh(u), 0.0) + u
            a = a * jnp.where(keep, sh(a), 1.0)
            k *= 2
        a_scr[d] = a
        h_scr[d] = u

    n_grp = t // SUBLANES
    if latent:
        carry0 = (h0_ref[0, 0, 0:1], h0_ref[0, 0, 1:2])
    else:
        carry0 = (jnp.zeros((1, W_BR), F32),) * 2

    def chain(i, carry):
        cf, cb = carry
        rf = pl.ds(pl.multiple_of(i * SUBLANES, SUBLANES), SUBLANES)
        rb = pl.ds(pl.multiple_of((n_grp - 1 - i) * SUBLANES, SUBLANES), SUBLANES)
        hf = h_scr[0, rf, :] + a_scr[0, rf, :] * cf
        hb = h_scr[1, rb, :] + a_scr[1, rb, :] * cb
        h_scr[0, rf, :] = hf
        h_scr[1, rb, :] = hb
        return hf[SUBLANES - 1:SUBLANES], hb[0:1]

    cf, cb = lax.fori_loop(0, n_grp, chain, carry0, unroll=4)
    y_ref[0] = h_scr[0] + h_scr[1]
    if not latent:
        st_ref[0] = jnp.concatenate([cf, cb], axis=0)


def _lru(pc, cw, cb, wa, ba, wx, bx, lam, state, layer):
    b, t, _ = pc.shape
    latent = state is not None
    full = lambda shape: pl.BlockSpec(shape, lambda i: (0,) * len(shape))
    one_layer = lambda shape: pl.BlockSpec((1,) + shape[1:], lambda i: (layer,) + (0,) * (len(shape) - 1))
    in_specs = [pl.BlockSpec((1, t, WC), lambda i: (i, 0, 0)),
                full(cw.shape), full(cb.shape), one_layer(wa.shape), full(ba.shape),
                one_layer(wx.shape), full(bx.shape), full(lam.shape)]
    args = [pc, cw, cb, wa, ba, wx, bx, lam]
    out_specs = [pl.BlockSpec((1, t, W_BR), lambda i: (i, 0, 0))]
    out_shape = [jax.ShapeDtypeStruct((b, t, W_BR), F32)]
    if latent:
        in_specs.append(pl.BlockSpec((1, 1, 2, W_BR), lambda i: (i, layer, 0, 0)))
        args.append(state)
    else:
        out_specs.append(pl.BlockSpec((1, 2, W_BR), lambda i: (i, 0, 0)))
        out_shape.append(jax.ShapeDtypeStruct((b, 2, W_BR), F32))
    return pl.pallas_call(
        functools.partial(_lru_kernel, latent=latent, layer=layer),
        grid=(b,),
        in_specs=in_specs,
        out_specs=out_specs,
        out_shape=out_shape,
        scratch_shapes=[pltpu.VMEM((2, t, W_BR), F32), pltpu.VMEM((2, t, W_BR), F32)],
        compiler_params=_params(("parallel",)),
        name="lru",
    )(*args)


def _win_kernel(*refs, latent, layer):
    if latent:
        sink_ref, q_ref, kp_ref, kc_ref, kn_ref, ck_ref, cv_ref, o_ref = refs
        j = pl.program_id(1)
        nq = pl.num_programs(1)
        qi = lax.broadcasted_iota(jnp.int32, (Q_BLK, Q_BLK), 0)
        ki = lax.broadcasted_iota(jnp.int32, (Q_BLK, Q_BLK), 1)
        q = q_ref[0]
        ctx_k, ctx_v = _bf(ck_ref[0, 0]), _bf(cv_ref[0, 0])
        pieces = [(_bf(r[0, :, :128]), _bf(r[0, :, 128:]), m, False) for r, m in
                  ((kp_ref, (ki >= qi) & (j > 0)), (kc_ref, None), (kn_ref, (ki <= qi) & (j < nq - 1)))]
        pieces += [(ctx_k[:, i:i + Q_BLK], ctx_v[:, i:i + Q_BLK], None, True)
                   for i in range(0, ctx_k.shape[1], Q_BLK)]
    else:
        sink_ref, x_ref, o_ref = refs
        q = x_ref[0, :, :256]
        k, v = _bf(x_ref[0, :, 256:384]), _bf(x_ref[0, :, 384:512])
        pieces = [(k[i:i + Q_BLK], v[i:i + Q_BLK], None, False) for i in range(0, k.shape[0], Q_BLK)]
    q = _bf(q * (HEAD ** -0.5 * LOG2_E))
    tq = q.shape[0]
    first_g = lax.broadcasted_iota(jnp.int32, (2 * tq, 1), 0) < tq
    outs = []
    for h in range(2):
        hs = slice(h * HEAD, (h + 1) * HEAD)
        qg = jnp.concatenate([q[:, (2 * h + g) * HEAD:(2 * h + g + 1) * HEAD] for g in range(2)], axis=0)
        sink = jnp.where(first_g, sink_ref[layer, 2 * h], sink_ref[layer, 2 * h + 1]) * LOG2_E
        logits = []
        for kp, _, mask, fm in pieces:
            s = _dotf(qg, kp[hs]) if fm else _dotf(qg, kp[:, hs], _NT)
            logits.append(s if mask is None else jnp.where(jnp.concatenate([mask, mask], axis=0), s, NEG_INF))
        mx = jnp.maximum(jnp.max(functools.reduce(jnp.maximum, logits), axis=-1, keepdims=True), sink)
        es = [jnp.exp2(s - mx) for s in logits]
        den = jnp.sum(functools.reduce(lambda a, b: a + b, es), axis=-1, keepdims=True) + jnp.exp2(sink - mx)
        acc = functools.reduce(lambda a, b: a + b,
                               [_dotf(_bf(e), vp[hs], _NT) if fm else _dotf(_bf(e), vp[:, hs])
                                for e, (_, vp, _, fm) in zip(es, pieces)])
        o = acc / den
        outs += [o[:tq], o[tq:]]
    o_ref[0] = jnp.concatenate(outs, axis=-1)


def _win(pb, sink, cache_k, cache_v, layer):
    b, t, _ = pb.shape
    latent = cache_k is not None
    smem = pl.BlockSpec(memory_space=pltpu.SMEM)
    if latent:
        nq = t // Q_BLK
        grid = (b, nq)
        in_specs = [smem,
                    pl.BlockSpec((1, Q_BLK, 256), lambda i, j: (i, j, 0)),
                    pl.BlockSpec((1, Q_BLK, 256), lambda i, j: (i, jnp.maximum(j - 1, 0), 1)),
                    pl.BlockSpec((1, Q_BLK, 256), lambda i, j: (i, j, 1)),
                    pl.BlockSpec((1, Q_BLK, 256), lambda i, j: (i, jnp.minimum(j + 1, nq - 1), 1)),
                    pl.BlockSpec((1, 1) + cache_k.shape[2:], lambda i, j: (i, layer, 0, 0)),
                    pl.BlockSpec((1, 1) + cache_v.shape[2:], lambda i, j: (i, layer, 0, 0))]
        args = [sink, pb, pb, pb, pb, cache_k, cache_v]
        out_spec = pl.BlockSpec((1, Q_BLK, W_BR), lambda i, j: (i, j, 0))
        sem = ("parallel", "parallel")
    else:
        grid = (b,)
        in_specs = [smem, pl.BlockSpec((1, t, WB), lambda i: (i, 0, 0))]
        args = [sink, pb]
        out_spec = pl.BlockSpec((1, t, W_BR), lambda i: (i, 0, 0))
        sem = ("parallel",)
    return pl.pallas_call(
        functools.partial(_win_kernel, latent=latent, layer=layer),
        grid=grid,
        in_specs=in_specs,
        out_specs=out_spec,
        out_shape=jax.ShapeDtypeStruct((b, t, W_BR), F32),
        compiler_params=_params(sem),
        name="win",
    )(*args)


def _diff_kernel(*refs, latent, lam_init, layer):
    if latent:
        lam_ref, g_ref, q_ref, k_ref, v_ref, ck_ref, cv_ref, o_ref = refs
    else:
        lam_ref, g_ref, q_ref, k_ref, v_ref, o_ref = refs
    lp = lam_ref[layer]
    lam = (jnp.exp(jnp.sum(lp[0:1] * lp[1:2], axis=-1, keepdims=True))
           - jnp.exp(jnp.sum(lp[2:3] * lp[3:4], axis=-1, keepdims=True)) + lam_init)
    q = _bf(q_ref[0] * (DQ_D ** -0.5 * LOG2_E))
    key_sets = [(_bf(k_ref[0]), _bf(v_ref[0]), False)]
    if latent:
        key_sets.append((_bf(ck_ref[0, 0]), _bf(cv_ref[0, 0]), True))
    lane = lax.broadcasted_iota(jnp.int32, (1, W_BR), 1)
    out = jnp.zeros(q.shape, F32)
    for h in range(W_BR // HEAD):
        probs = []
        for m in range(2):
            qm = jnp.where(lane // DQ_D == 2 * h + m, q, jnp.zeros((), BF16))
            logits = [_dotf(qm, kk) if fm else _dotf(qm, kk, _NT) for kk, _, fm in key_sets]
            mx = functools.reduce(jnp.maximum, [jnp.max(s, axis=-1, keepdims=True) for s in logits])
            es = [jnp.exp2(s - mx) for s in logits]
            den = functools.reduce(lambda a, b: a + b, [jnp.sum(e, axis=-1, keepdims=True) for e in es])
            inv = 1.0 / den
            probs.append([e * inv for e in es])
        o = None
        for i, (_, vv, fm) in enumerate(key_sets):
            pi = _bf(probs[0][i] - lam * probs[1][i])
            oi = _dotf(pi, vv, _NT) if fm else _dotf(pi, vv)
            o = oi if o is None else o + oi
        out = jnp.where(lane // HEAD == h, o, out)
    ms = _mm2(out * out, _head_ones(W_BR)) * (1.0 / HEAD)
    gain = jnp.concatenate([g_ref[layer:layer + 1]] * (W_BR // HEAD), axis=1)
    o_ref[0] = out * lax.rsqrt(ms + NORM_EPS) * gain * (1.0 - lam_init)


def _diff(pd, lam_p, subln_g, cache_k, cache_v, layer, lam_init):
    b, t, _ = pd.shape
    latent = cache_k is not None
    tq = min(t, 512)
    in_specs = [pl.BlockSpec(lam_p.shape, lambda i, j: (0, 0, 0)),
                pl.BlockSpec(subln_g.shape, lambda i, j: (0, 0)),
                pl.BlockSpec((1, tq, 256), lambda i, j: (i, j, 0)),
                pl.BlockSpec((1, t, 256), lambda i, j: (i, 0, 1)),
                pl.BlockSpec((1, t, 256), lambda i, j: (i, 0, 2))]
    args = [lam_p, subln_g, pd, pd, pd]
    if latent:
        in_specs += [pl.BlockSpec((1, 1) + cache_k.shape[2:], lambda i, j: (i, layer, 0, 0)),
                     pl.BlockSpec((1, 1) + cache_v.shape[2:], lambda i, j: (i, layer, 0, 0))]
        args += [cache_k, cache_v]
    return pl.pallas_call(
        functools.partial(_diff_kernel, latent=latent, lam_init=lam_init, layer=layer),
        grid=(b, t // tq),
        in_specs=in_specs,
        out_specs=pl.BlockSpec((1, tq, W_BR), lambda i, j: (i, j, 0)),
        out_shape=jax.ShapeDtypeStruct((b, t, W_BR), F32),
        compiler_params=_params(("parallel", "parallel")),
        name="diff",
    )(*args)


def _out_kernel(x_ref, mod_ref, gpost_ref, w_ref, pa_ref, yf_ref, yb_ref, gng_ref, gnb_ref, rk_ref,
                ywin_ref, bg_ref, ylru_ref, cg_ref, ydiff_ref, dg_ref, o_ref, *, layer):
    lrow = slice(layer, layer + 1)
    ones = _head_ones(W_BR)
    pa = pa_ref[0]
    r, k, v, ag = pa[:, 0:256], pa[:, 256:512], pa[:, 512:768], pa[:, 896:1152]
    y = yf_ref[0] + yb_ref[0]
    mu = _mm2(y, ones) * (1.0 / HEAD)
    yc = y - mu
    var = _mm2(yc * yc, ones) * (1.0 / HEAD)
    ya = yc * lax.rsqrt(var + GN_EPS) * gng_ref[lrow] + gnb_ref[lrow]
    ya = ya + _mm2(r * k * rk_ref[lrow], ones) * v
    mix = jnp.concatenate([ya * _silu(ag), ywin_ref[0] * _silu(bg_ref[0]),
                           ylru_ref[0] * _silu(cg_ref[0]), ydiff_ref[0] * _silu(dg_ref[0])], axis=-1)
    o = jnp.dot(mix.astype(BF16), w_ref[0], preferred_element_type=F32)
    o = o * lax.rsqrt(jnp.mean(o * o, -1, keepdims=True) + NORM_EPS) * gpost_ref[lrow]
    o_ref[0] = x_ref[0] + mod_ref[0][:, 2 * D_MODEL:] * o


def _out(x, mod_l, g_post, w_out_bf, layer, latent, pa, yf, yb, gn_g, gn_b, r_k, ywin, pb, ylru, pc, ydiff, pd):
    b, t, _ = x.shape
    tm = ROW_TILE
    mod_idx = (lambda i, j: (1 + i, 0, 0)) if latent else (lambda i, j: (0, 0, 0))
    rows = lambda w, c=0: pl.BlockSpec((1, tm, w), lambda i, j: (i, j, c))
    vec = pl.BlockSpec(gn_g.shape, lambda i, j: (0, 0))
    in_specs = [rows(D_MODEL),
                pl.BlockSpec((1, 1, 3 * D_MODEL), mod_idx),
                pl.BlockSpec(g_post.shape, lambda i, j: (0, 0)),
                pl.BlockSpec((1, D_MODEL, D_MODEL), lambda i, j: (layer, 0, 0)),
                rows(WA), rows(W_BR), rows(W_BR), vec, vec, vec,
                rows(W_BR), rows(W_BR, 2), rows(W_BR), rows(W_BR, 1), rows(W_BR), rows(W_BR, 3)]
    return pl.pallas_call(
        functools.partial(_out_kernel, layer=layer),
        grid=(b, t // tm),
        in_specs=in_specs,
        out_specs=rows(D_MODEL),
        out_shape=jax.ShapeDtypeStruct((b, t, D_MODEL), F32),
        compiler_params=_params(("parallel", "parallel")),
        name="out",
    )(x, mod_l, g_post, w_out_bf, pa, yf, yb, gn_g, gn_b, r_k, ywin, pb, ylru, pc, ydiff, pd)


def _layer(x, mod_l, layer, lam_init, wts, cache, tables):
    latent = cache is not None
    pa, pb, pc, pd, *ctx_t = _project(x, mod_l, wts['g_pre'], wts['w_in_bf'], layer, tables)
    rw = _rwkv(pa, wts['rwkv_w0'], wts['rwkv_w_up'], wts['rwkv_a0'], wts['rwkv_a_up'], wts['rwkv_k_k'],
               wts['rwkv_k_a'], cache['rwkv'] if latent else None, layer)
    lr = _lru(pc, wts['lru_conv_w'], wts['lru_conv_b'], wts['lru_wa'], wts['lru_ba'], wts['lru_wx'],
              wts['lru_bx'], wts['lru_lambda'], cache['lru'] if latent else None, layer)
    ywin = _win(pb, wts['win_sink'], cache['win_k'] if latent else None,
                cache['win_v'] if latent else None, layer)
    ydiff = _diff(pd, wts['diff_lambda'], wts['diff_subln_g'],
                  cache['diff_k'] if latent else None, cache['diff_v'] if latent else None, layer, lam_init)
    y = _out(x, mod_l, wts['g_post'], wts['w_out_bf'], layer, latent, pa, rw[0], rw[1],
             wts['rwkv_gn_g'], wts['rwkv_gn_b'], wts['rwkv_r_k'], ywin, pb, lr[0], pc, ydiff, pd)
    new_cache = None if latent else (ctx_t, rw[2], lr[1])
    return y, new_cache


def kernel(x_prompt, x_sample, c, cache_win_k, cache_win_v, cache_diff_k, cache_diff_v, state_rwkv, state_lru,
           c_ctx, w_mod, b_mod, g_pre, g_post, w_in, w_out,
           rwkv_w0, rwkv_w_up, rwkv_a0, rwkv_a_up, rwkv_k_k, rwkv_k_a, rwkv_r_k, rwkv_gn_g, rwkv_gn_b,
           win_sink, lru_conv_w, lru_conv_b, lru_wa, lru_ba, lru_wx, lru_bx, lru_lambda,
           diff_lambda, diff_subln_g):
    n_b, seq = x_prompt.shape[:2]
    n_dec, dec_seq = x_sample.shape[:2]
    past = cache_win_k.shape[2]
    wts = dict(g_pre=g_pre, g_post=g_post, w_in_bf=w_in.astype(BF16), w_out_bf=w_out.astype(BF16),
               rwkv_w0=rwkv_w0, rwkv_w_up=rwkv_w_up, rwkv_a0=rwkv_a0, rwkv_a_up=rwkv_a_up,
               rwkv_k_k=rwkv_k_k, rwkv_k_a=rwkv_k_a, rwkv_r_k=rwkv_r_k.reshape(DEPTH, W_BR), rwkv_gn_g=rwkv_gn_g,
               rwkv_gn_b=rwkv_gn_b, win_sink=win_sink, lru_conv_w=lru_conv_w, lru_conv_b=lru_conv_b,
               lru_wa=lru_wa, lru_ba=lru_ba, lru_wx=lru_wx, lru_bx=lru_bx, lru_lambda=lru_lambda,
               diff_lambda=diff_lambda, diff_subln_g=diff_subln_g)
    fm = lambda a: jnp.moveaxis(a, 2, -1).reshape(n_dec, DEPTH, -1, past)
    cache = dict(win_k=fm(cache_win_k), win_v=fm(cache_win_v), diff_k=fm(cache_diff_k), diff_v=fm(cache_diff_v),
                 rwkv=state_rwkv, lru=state_lru)
    cvec = jnp.concatenate([c_ctx[None], c, jnp.zeros((8 - 1 - n_dec, D_MODEL), F32)], axis=0)
    mod = _modulation(cvec, w_mod, b_mod)
    cos_b, sin_b = _rope_tables(dec_seq, HEAD, 384)
    cos_d, sin_d = _rope_tables(dec_seq, DQ_D, 512)
    tables = (cos_b, sin_b, cos_d, sin_d)

    y_p, y_s = x_prompt, x_sample
    ctx = []
    for l in range(DEPTH):
        lam_init = 0.8 - 0.6 * math.exp(-0.3 * l)
        mod_l = mod[l].reshape(8, 1, 3 * D_MODEL)
        y_p, nc = _layer(y_p, mod_l, l, lam_init, wts, None, None)
        ctx.append(nc)
        y_s, _ = _layer(y_s, mod_l, l, lam_init, wts, cache, tables)
    stack = lambda f: jnp.stack([f(ct) for ct in ctx], axis=1)
    tm_ = lambda i, dims: jnp.moveaxis(stack(lambda ct: ct[0][i]).reshape((n_b, DEPTH) + dims + (seq,)), -1, 2)
    new_win_k = tm_(0, (2, HEAD))
    new_win_v = tm_(1, (2, HEAD))
    new_diff_k = tm_(2, (4, 2, DQ_D))
    new_diff_v = tm_(3, (4, HEAD))
    new_state_rwkv = stack(lambda ct: ct[1])
    new_state_lru = stack(lambda ct: ct[2])
    return (y_p, y_s, new_win_k, new_win_v, new_diff_k, new_diff_v, new_state_rwkv, new_state_lru)
```

```python
import functools
import math

import numpy as np
import jax
import jax.numpy as jnp
from jax import lax
from jax.experimental import pallas as pl
from jax.experimental.pallas import tpu as pltpu

F32 = jnp.float32
BF16 = jnp.bfloat16
HI = lax.Precision.HIGHEST

D_MODEL = 1024
DEPTH = 2
GRID_W = 64
ROPE_BASE = 10000.0
NORM_EPS = 1e-6
NEG_INF = -1e30
LOG2_E = math.log2(math.e)
GN_EPS = 64e-5
C_RG = 8.0
W_BR = 256
HEAD = 64
SUBLANES = 8
LORA = 64
DQ_D = 32
WINDOW = 128
Q_BLK = 128
P_TOTAL = 3456
WA, WB, WC, WD = 1152, 768, 512, 1024
CHUNK = 64
RWKV_BATCH = 4
ROW_TILE = 512
VMEM_LIMIT = 58 * 1024 * 1024


_NN = (((1,), (0,)), ((), ()))
_NT = (((1,), (1,)), ((), ()))
_TN = (((0,), (0,)), ((), ()))


def _dot_hi(a, b):
    return jnp.dot(a, b, precision=HI, preferred_element_type=F32)


def _bf(x):
    return x.astype(BF16)


def _dotf(x, y, dims=_NN):
    return lax.dot_general(x, y, dims, preferred_element_type=F32)


def _split(x):
    hi = x.astype(BF16)
    return hi, (x - hi.astype(F32)).astype(BF16)


def _mm3(a, b, dims=_NN):
    return _dotf(a[0], b[0], dims) + (_dotf(a[0], b[1], dims) + _dotf(a[1], b[0], dims))


def _mm2(x, m01, left=False):
    hi, lo = _split(x)
    if left:
        return _dotf(m01, hi) + _dotf(m01, lo)
    return _dotf(hi, m01) + _dotf(lo, m01)


def _dot_bf(a, b):
    return jnp.dot(a.astype(BF16), b.astype(BF16), preferred_element_type=F32)


def _dot_nt_bf(a, b):
    return lax.dot_general(a.astype(BF16), b.astype(BF16), (((1,), (1,)), ((), ())),
                           preferred_element_type=F32)


def _sigmoid(x):
    return 1.0 / (1.0 + jnp.exp(-x))


def _silu(x):
    return x * _sigmoid(x)


def _softplus(x):
    return jnp.maximum(x, 0.0) + jnp.log1p(jnp.exp(-jnp.abs(x)))


def _head_ones(n):
    r = lax.broadcasted_iota(jnp.int32, (n, n), 0) // HEAD
    c = lax.broadcasted_iota(jnp.int32, (n, n), 1) // HEAD
    return jnp.where(r == c, 1.0, 0.0).astype(BF16)


def _params(sem):
    return pltpu.CompilerParams(dimension_semantics=sem, vmem_limit_bytes=VMEM_LIMIT)


def _mod_kernel(c_ref, w_ref, b_ref, o_ref):
    o_ref[0] = _mm3(_split(_silu(c_ref[...])), _split(w_ref[0])) + b_ref[0]


def _modulation(cvec, w_mod, b_mod):
    n_l = w_mod.shape[0]
    tn = 512
    return pl.pallas_call(
        _mod_kernel,
        grid=(n_l, 3 * D_MODEL // tn),
        in_specs=[pl.BlockSpec((8, D_MODEL), lambda l, j: (0, 0)),
                  pl.BlockSpec((1, D_MODEL, tn), lambda l, j: (l, 0, j)),
                  pl.BlockSpec((1, 1, tn), lambda l, j: (l, 0, j))],
        out_specs=pl.BlockSpec((1, 8, tn), lambda l, j: (l, 0, j)),
        out_shape=jax.ShapeDtypeStruct((n_l, 8, 3 * D_MODEL), F32),
        compiler_params=_params(("parallel", "parallel")),
        name="mod",
    )(cvec, w_mod, b_mod.reshape(n_l, 1, 3 * D_MODEL))


def _rope(x, cos, sin_signed, off):
    w = x.shape[-1]
    lane = lax.broadcasted_iota(jnp.int32, x.shape, 1)
    first = (lane % (2 * off)) < off
    partner = jnp.where(first, pltpu.roll(x, w - off, 1), pltpu.roll(x, off, 1))
    return x * cos + partner * sin_signed


def _proj_kernel(*refs, latent, layer):
    if latent:
        x_ref, mod_ref, g_ref, w_ref, cb_ref, sb_ref, cd_ref, sd_ref, oa, ob, oc, od = refs
    else:
        x_ref, mod_ref, g_ref, w_ref, oa, ob, oc, od, o_wk, o_wv, o_dk, o_dv = refs
    x = x_ref[0]
    y = x * lax.rsqrt(jnp.mean(x * x, -1, keepdims=True) + NORM_EPS) * g_ref[layer:layer + 1]
    m = mod_ref[0]
    h = y * (1.0 + m[:, D_MODEL:2 * D_MODEL]) + m[:, :D_MODEL]
    p = jnp.dot(h.astype(BF16), w_ref[0], preferred_element_type=F32)
    oa[0] = p[:, :WA]
    pb = p[:, WA:WA + WB]
    pd = p[:, WA + WB + WC:]
    if latent:
        ob[0, :, :384] = _rope(pb[:, :384], cb_ref[...], sb_ref[...], 16)
        ob[0, :, 384:] = pb[:, 384:]
        od[0, :, :512] = _rope(pd[:, :512], cd_ref[...], sd_ref[...], 8)
        od[0, :, 512:] = pd[:, 512:]
    else:
        ob[0] = pb
        od[0] = pd
        o_wk[0] = pb[:, 256:384].T
        o_wv[0] = pb[:, 384:512].T
        o_dk[0] = pd[:, 256:512].T
        o_dv[0] = pd[:, 512:768].T
    oc[0] = p[:, WA + WB:WA + WB + WC]


def _project(x, mod_l, g_pre, w_in_bf, layer, tables):
    b, t, _ = x.shape
    latent = tables is not None
    tm = min(t, ROW_TILE)
    mod_idx = (lambda i, j: (1 + i, 0, 0)) if latent else (lambda i, j: (0, 0, 0))
    in_specs = [pl.BlockSpec((1, tm, D_MODEL), lambda i, j: (i, j, 0)),
                pl.BlockSpec((1, 1, 3 * D_MODEL), mod_idx),
                pl.BlockSpec(g_pre.shape, lambda i, j: (0, 0)),
                pl.BlockSpec((1, D_MODEL, P_TOTAL), lambda i, j: (layer, 0, 0))]
    args = [x, mod_l, g_pre, w_in_bf]
    if latent:
        for tab in tables:
            in_specs.append(pl.BlockSpec((tm, tab.shape[1]), lambda i, j: (j, 0)))
            args.append(tab)
    widths = (WA, WB, WC, WD)
    out_specs = [pl.BlockSpec((1, tm, w), lambda i, j: (i, j, 0)) for w in widths]
    out_shape = [jax.ShapeDtypeStruct((b, t, w), F32) for w in widths]
    if not latent:
        for w in (128, 128, W_BR, W_BR):
            out_specs.append(pl.BlockSpec((1, w, tm), lambda i, j: (i, 0, j)))
            out_shape.append(jax.ShapeDtypeStruct((b, w, t), F32))
    return pl.pallas_call(
        functools.partial(_proj_kernel, latent=latent, layer=layer),
        grid=(b, t // tm),
        in_specs=in_specs,
        out_specs=out_specs,
        out_shape=out_shape,
        compiler_params=_params(("parallel", "parallel")),
        name="proj",
    )(*args)


def _rope_tables(t, head_dim, n_lanes):
    half = head_dim // 2
    quarter = half // 2
    pos = np.arange(t)
    row = (pos // GRID_W).astype(np.float32)
    col = (pos % GRID_W).astype(np.float32)
    inv = np.float32(ROPE_BASE) ** (-np.arange(0, half, 2, dtype=np.float32) / np.float32(half))
    lane = np.arange(n_lanes) % head_dim
    in_half = lane % half
    p = np.where((lane < half)[None, :], row[:, None], col[:, None])
    ang = (p * inv[in_half % quarter][None, :]).astype(np.float64)
    sign = np.where(in_half < quarter, -1.0, 1.0)[None, :]
    return jnp.asarray(np.cos(ang), F32), jnp.asarray(np.sin(ang) * sign, F32)


_PREP_FIELDS = (('ar', 2, 1, BF16), ('nm', 1, 1, BF16), ('mm', 1, 1, BF16), ('pq', 1, 2, BF16),
                ('v', 1, 1, BF16), ('bkt', 1, 2, BF16), ('gct', 1, 1, F32))


def _prep_scratch(slots, n_streams):
    return [pltpu.VMEM((slots, n_streams, rows * CHUNK, lanes * W_BR), dt) for _, rows, lanes, dt in _PREP_FIELDS]


class _SlotView:
    def __init__(self, ref, slot):
        self.ref, self.slot, self.shape = ref, slot, ref.shape[1:]

    def __getitem__(self, idx):
        return self.ref[(self.slot,) + (idx if isinstance(idx, tuple) else (idx,))]

    def __setitem__(self, idx, val):
        self.ref[(self.slot,) + (idx if isinstance(idx, tuple) else (idx,))] = val


def _head_transpose(x):
    xt = x.T
    return jnp.concatenate([xt[h * HEAD:(h + 1) * HEAD] for h in range(x.shape[1] // HEAD)], axis=1)


def _expand(x, same):
    return jnp.concatenate([x] * (same.shape[0] // x.shape[0]), axis=0) * same


def _rwkv_prepare(xs, dirs, w0_ref, wup_ref, a0_ref, aup_ref, k_k, k_a, m_ref, same_ref, out):
    c = xs[0].shape[0]
    ones = _head_ones(W_BR)
    ti = lax.broadcasted_iota(jnp.int32, (c, c), 0)
    si = lax.broadcasted_iota(jnp.int32, (c, c), 1)
    cums = [jnp.where(si <= ti, 1.0, 0.0).astype(BF16), jnp.where(si >= ti, 1.0, 0.0).astype(BF16)]
    same = same_ref[...]

    st = []
    for xc, d in zip(xs, dirs):
        r, k, v = xc[:, 0:256], xc[:, 256:512], xc[:, 512:768]
        wd, ad = xc[:, 768:832], xc[:, 832:896]
        kk = k * k_k
        kk = kk * lax.rsqrt(_mm2(kk * kk, ones) + 1e-12)
        z = w0_ref[d:d + 1] + _mm3(_split(jnp.tanh(wd)), _split(wup_ref[d]))
        e = jnp.exp(-_softplus(-z) - 0.5)
        a = _sigmoid(a0_ref[d:d + 1] + _mm3(_split(ad), _split(aup_ref[d])))
        st.append(dict(r=r, v=v, e=e, kd=k * (1.0 + (a - 1.0) * k_a), alpha=-kk, beta=kk * a))
    yield
    for s, d in zip(st, dirs):
        l_incl = _mm2(s['e'], cums[d], left=True)
        l_tot = jnp.sum(s['e'], axis=0, keepdims=True)
        grow = jnp.exp(l_incl)
        tail = jnp.exp(l_incl - l_tot)
        s.update(ar=jnp.concatenate([_bf(s['alpha'] * jnp.exp(s['e'] - l_incl)),
                                     _bf(s['r'] * jnp.exp(-l_incl))], axis=0),
                 b_t=_bf(s['beta'] * grow), k_t=_bf(s['kd'] * grow),
                 b_h=s['beta'] * tail, k_h=s['kd'] * tail,
                 g_c=jnp.broadcast_to(jnp.exp(-l_tot), (c, W_BR)))
    yield
    for g, (s, d) in enumerate(zip(st, dirs)):
        out['ar'][g] = s['ar']
        rhs_t = jnp.concatenate([_expand(s['b_t'], same), _expand(s['k_t'], same)], axis=0)
        g4 = _dotf(s['ar'], rhs_t, _NT)
        strict = m_ref[_M_STRICT + 2 * d]
        incl = m_ref[_M_INCL + 2 * d]
        out['nm'][g] = _bf(g4[:c, :W_BR]) * strict
        out['mm'][g] = _bf(g4[:c, W_BR:]) * strict
        out['pq'][g, :, :W_BR] = _bf(g4[c:, :W_BR]) * incl
        out['pq'][g, :, W_BR:] = _bf(g4[c:, W_BR:]) * incl
    yield
    for g, s in enumerate(st):
        out['v'][g] = _bf(s['v'])
        out['bkt'][g] = jnp.concatenate([_bf(_head_transpose(s['b_h'])), _bf(_head_transpose(s['k_h']))], axis=1)
        out['gct'][g] = _head_transpose(s['g_c'])


def _rwkv_advance(p, m_ref, same_ref, z_ref, y_stores):
    n_streams, c = p['nm'].shape[0], p['nm'].shape[1]
    gs = range(n_streams)
    same = same_ref[...]
    ex = lambda x: _expand(_bf(x), same)
    invs = [(p['nm'][g] * m_ref[_M_PAIR]).astype(F32) + m_ref[_M_EYE].astype(F32) for g in gs]
    for lvl in range(int(math.log2(c)) - 1):
        inv_x = [ex(t) for t in invs]
        half = [_dotf(p['nm'][g] * m_ref[_M_OFF + lvl], inv_x[g]) for g in gs]
        invs = [invs[g] + _dotf(_bf(invs[g]), ex(half[g])) for g in gs]
        yield
    z0 = [z_ref[g] for g in gs]
    as0 = [_dotf(p['ar'][g], ex(z0[g])) for g in gs]
    vx = [_expand(p['v'][g], same) for g in gs]
    rhs = [as0[g][:c] + _dotf(p['mm'][g], vx[g]) for g in gs]
    yield
    uv = [jnp.concatenate([ex(_dotf(_bf(invs[g]), ex(rhs[g]))), vx[g]], axis=0) for g in gs]
    yield
    for g in gs:
        z_ref[g] = z0[g] * p['gct'][g] + _dotf(p['bkt'][g], uv[g])
    yield
    for g in gs:
        y_stores[g](as0[g][c:] + _dotf(p['pq'][g], uv[g]))


def _interleave(*gens):
    live = list(gens)
    while live:
        for gen in list(live):
            try:
                next(gen)
            except StopIteration:
                live.remove(gen)


_M_STRICT, _M_INCL, _M_EYE, _M_PAIR, _M_OFF = 0, 1, 4, 5, 6


def _rwkv_masks(c, n_h):
    t = np.arange(c)[:, None]
    s = np.arange(c)[None, :]
    masks = [s < t, s <= t, s > t, s >= t, s == t, ((t // 2) == (s // 2)) & (t != s)]
    b = 2
    while b < c:
        masks.append(((t // (2 * b)) == (s // (2 * b))) & ((t // b) != (s // b)))
        b *= 2
    masks = np.tile(np.stack(masks).astype(np.float32), (1, 1, n_h))
    r = np.arange(n_h * c)[:, None] // c
    l = np.arange(n_h * HEAD)[None, :] // HEAD
    return jnp.asarray(masks, BF16), jnp.asarray((r == l).astype(np.float32), BF16)


def _rwkv_kernel(*refs, latent, n_b, layer):
    (xf_ref, xb_ref, xfn_ref, xbn_ref, w0_ref, wup_ref, a0_ref, aup_ref, kk_ref, ka_ref,
     m_ref, same_ref) = refs[:12]
    if latent:
        s0_ref, yf_ref, yb_ref = refs[12:15]
    else:
        yf_ref, yb_ref, st_ref = refs[12:15]
    scr = refs[15:]
    z_scr = scr[0]
    names = [f[0] for f in _PREP_FIELDS]
    ci = pl.program_id(1)
    step = pl.program_id(0) * pl.num_programs(1) + ci
    view = lambda refs_, slot: {k: _SlotView(r, slot) for k, r in zip(names, refs_)}
    cur = view(scr[1:1 + len(names)], step % 2)
    nxt = view(scr[1:1 + len(names)], 1 - step % 2)
    mid = view(scr[1 + len(names):], 0)
    n_h = W_BR // HEAD
    c = CHUNK
    head = lambda h: slice(h * HEAD, (h + 1) * HEAD)
    streams = [(bi, d) for bi in range(n_b) for d in range(2)]
    dirs = [d for _, d in streams]
    params = (w0_ref[layer], wup_ref[0], a0_ref[layer], aup_ref[0], kk_ref[layer:layer + 1],
              ka_ref[layer:layer + 1], m_ref, same_ref)
    first = lambda d: slice(0, c) if d == 0 else slice(c, 2 * c)
    second = lambda d: slice(c, 2 * c) if d == 0 else slice(0, c)
    x_refs, xn_refs, y_refs = (xf_ref, xb_ref), (xfn_ref, xbn_ref), (yf_ref, yb_ref)

    def y_store(bi, d, rows):
        def store(y):
            y_refs[d][bi, rows, :] = y
        return store

    @pl.when(ci == 0)
    def _():
        for g, (bi, d) in enumerate(streams):
            if latent:
                z_scr[g] = jnp.concatenate([s0_ref[bi, 0, d, h].T for h in range(n_h)], axis=1)
            else:
                z_scr[g] = jnp.zeros(z_scr.shape[1:], F32)

    @pl.when(step == 0)
    def _():
        _interleave(_rwkv_prepare([x_refs[d][bi, first(d), :] for bi, d in streams], dirs, *params, cur))

    _interleave(_rwkv_advance(cur, m_ref, same_ref, z_scr, [y_store(bi, d, first(d)) for bi, d in streams]),
                _rwkv_prepare([x_refs[d][bi, second(d), :] for bi, d in streams], dirs, *params, mid))
    _interleave(_rwkv_advance(mid, m_ref, same_ref, z_scr, [y_store(bi, d, second(d)) for bi, d in streams]),
                _rwkv_prepare([xn_refs[d][bi, first(d), :] for bi, d in streams], dirs, *params, nxt))

    if not latent:
        @pl.when(ci == pl.num_programs(1) - 1)
        def _():
            for g, (bi, d) in enumerate(streams):
                z = z_scr[g]
                for h in range(n_h):
                    st_ref[bi, d, h] = z[:, head(h)].T


def _rwkv(pa, w0, wup, a0, aup, k_k, k_a, state, layer):
    b, t, _ = pa.shape
    latent = state is not None
    ns = t // (2 * CHUNK)
    n_h = W_BR // HEAD
    n_b = RWKV_BATCH
    masks, same = _rwkv_masks(CHUNK, n_h)
    full = lambda shape: pl.BlockSpec(shape, lambda i, j: (0,) * len(shape))
    one_layer = lambda shape: pl.BlockSpec((1,) + shape[1:], lambda i, j: (layer,) + (0,) * (len(shape) - 1))
    rows = lambda w, idx: pl.BlockSpec((n_b, 2 * CHUNK, w), idx)
    nxt_group = lambda i, j: jnp.minimum(i + (j + 1) // ns, b // n_b - 1)
    in_specs = [rows(WA, lambda i, j: (i, j, 0)),
                rows(WA, lambda i, j: (i, ns - 1 - j, 0)),
                rows(WA, lambda i, j: (nxt_group(i, j), (j + 1) % ns, 0)),
                rows(WA, lambda i, j: (nxt_group(i, j), (2 * ns - 2 - j) % ns, 0)),
                full(w0.shape), one_layer(wup.shape), full(a0.shape), one_layer(aup.shape),
                full(k_k.shape), full(k_a.shape), full(masks.shape), full(same.shape)]
    args = [pa, pa, pa, pa, w0, wup, a0, aup, k_k, k_a, masks, same]
    out_specs = [rows(W_BR, lambda i, j: (i, j, 0)), rows(W_BR, lambda i, j: (i, ns - 1 - j, 0))]
    out_shape = [jax.ShapeDtypeStruct((b, t, W_BR), F32)] * 2
    if latent:
        in_specs.append(pl.BlockSpec((n_b, 1, 2, n_h, HEAD, HEAD), lambda i, j: (i, layer, 0, 0, 0, 0)))
        args.append(state)
    else:
        out_specs.append(pl.BlockSpec((n_b, 2, n_h, HEAD, HEAD), lambda i, j: (i, 0, 0, 0, 0)))
        out_shape.append(jax.ShapeDtypeStruct((b, 2, n_h, HEAD, HEAD), F32))
    return pl.pallas_call(
        functools.partial(_rwkv_kernel, latent=latent, n_b=n_b, layer=layer),
        grid=(b // n_b, ns),
        in_specs=in_specs,
        out_specs=out_specs,
        out_shape=out_shape,
        scratch_shapes=([pltpu.VMEM((2 * n_b, CHUNK, W_BR), F32)]
                        + _prep_scratch(2, 2 * n_b) + _prep_scratch(1, 2 * n_b)),
        compiler_params=_params(("arbitrary", "arbitrary")),
        name="rwkv",
    )(*args)


def _lru_kernel(*refs, latent, layer):
    if latent:
        (x_ref, cw_ref, cb_ref, wa_ref, ba_ref, wx_ref, bx_ref, lam_ref, h0_ref, y_ref,
         a_scr, h_scr) = refs
    else:
        (x_ref, cw_ref, cb_ref, wa_ref, ba_ref, wx_ref, bx_ref, lam_ref, y_ref, st_ref,
         a_scr, h_scr) = refs
    x = x_ref[0, :, :W_BR]
    t = x.shape[0]
    row = lax.broadcasted_iota(jnp.int32, x.shape, 0)

    def shift_dn(z, k, fill):
        return jnp.where(row >= k, pltpu.roll(z, k, 0), fill)

    def shift_up(z, k, fill):
        return jnp.where(row < t - k, pltpu.roll(z, t - k, 0), fill)

    cw = cw_ref[layer]
    xc = (cb_ref[layer:layer + 1] + shift_dn(x, 2, 0.0) * cw[0:1] + shift_dn(x, 1, 0.0) * cw[1:2]
          + x * cw[2:3] + shift_up(x, 1, 0.0) * cw[3:4])
    xs = _split(xc)
    sub = row % SUBLANES
    n_blk = W_BR // HEAD
    zero = jnp.zeros((HEAD, HEAD), F32)

    def block_diag(w_ref, d):
        return jnp.concatenate([jnp.concatenate([w_ref[0, d, n] if m == n else zero for m in range(n_blk)], axis=1)
                                for n in range(n_blk)], axis=0)

    for d in range(2):
        gate_a = _sigmoid(_mm3(xs, _split(block_diag(wa_ref, d))) + ba_ref[layer, d:d + 1])
        gate_x = _sigmoid(_mm3(xs, _split(block_diag(wx_ref, d))) + bx_ref[layer, d:d + 1])
        log_a = -C_RG * gate_a * _softplus(-lam_ref[layer, d:d + 1])
        a = jnp.exp(log_a)
        u = jnp.sqrt(-jnp.tanh(log_a) * (a * a + 1.0)) * (gate_x * xc)
        k = 1
        while k < SUBLANES:
            if d == 0:
                keep = sub >= k
                sh = lambda z: pltpu.roll(z, k, 0)
            else:
                keep = sub < SUBLANES - k
                sh = lambda z: pltpu.roll(z, t - k, 0)
            u = a * jnp.where(keep, sh(u), 0.0) + u
            a = a * jnp.where(keep, sh(a), 1.0)
            k *= 2
        a_scr[d] = a
        h_scr[d] = u

    n_grp = t // SUBLANES
    if latent:
        carry0 = (h0_ref[0, 0, 0:1], h0_ref[0, 0, 1:2])
    else:
        carry0 = (jnp.zeros((1, W_BR), F32),) * 2

    def chain(i, carry):
        cf, cb = carry
        rf = pl.ds(pl.multiple_of(i * SUBLANES, SUBLANES), SUBLANES)
        rb = pl.ds(pl.multiple_of((n_grp - 1 - i) * SUBLANES, SUBLANES), SUBLANES)
        hf = h_scr[0, rf, :] + a_scr[0, rf, :] * cf
        hb = h_scr[1, rb, :] + a_scr[1, rb, :] * cb
        h_scr[0, rf, :] = hf
        h_scr[1, rb, :] = hb
        return hf[SUBLANES - 1:SUBLANES], hb[0:1]

    cf, cb = lax.fori_loop(0, n_grp, chain, carry0, unroll=4)
    y_ref[0] = h_scr[0] + h_scr[1]
    if not latent:
        st_ref[0] = jnp.concatenate([cf, cb], axis=0)


def _lru(pc, cw, cb, wa, ba, wx, bx, lam, state, layer):
    b, t, _ = pc.shape
    latent = state is not None
    full = lambda shape: pl.BlockSpec(shape, lambda i: (0,) * len(shape))
    one_layer = lambda shape: pl.BlockSpec((1,) + shape[1:], lambda i: (layer,) + (0,) * (len(shape) - 1))
    in_specs = [pl.BlockSpec((1, t, WC), lambda i: (i, 0, 0)),
                full(cw.shape), full(cb.shape), one_layer(wa.shape), full(ba.shape),
                one_layer(wx.shape), full(bx.shape), full(lam.shape)]
    args = [pc, cw, cb, wa, ba, wx, bx, lam]
    out_specs = [pl.BlockSpec((1, t, W_BR), lambda i: (i, 0, 0))]
    out_shape = [jax.ShapeDtypeStruct((b, t, W_BR), F32)]
    if latent:
        in_specs.append(pl.BlockSpec((1, 1, 2, W_BR), lambda i: (i, layer, 0, 0)))
        args.append(state)
    else:
        out_specs.append(pl.BlockSpec((1, 2, W_BR), lambda i: (i, 0, 0)))
        out_shape.append(jax.ShapeDtypeStruct((b, 2, W_BR), F32))
    return pl.pallas_call(
        functools.partial(_lru_kernel, latent=latent, layer=layer),
        grid=(b,),
        in_specs=in_specs,
        out_specs=out_specs,
        out_shape=out_shape,
        scratch_shapes=[pltpu.VMEM((2, t, W_BR), F32), pltpu.VMEM((2, t, W_BR), F32)],
        compiler_params=_params(("parallel",)),
        name="lru",
    )(*args)


def _win_kernel(*refs, latent, layer):
    if latent:
        sink_ref, q_ref, kp_ref, kc_ref, kn_ref, ck_ref, cv_ref, o_ref = refs
        j = pl.program_id(1)
        nq = pl.num_programs(1)
        qi = lax.broadcasted_iota(jnp.int32, (Q_BLK, Q_BLK), 0)
        ki = lax.broadcasted_iota(jnp.int32, (Q_BLK, Q_BLK), 1)
        q = q_ref[0]
        ctx_k, ctx_v = _bf(ck_ref[0, 0]), _bf(cv_ref[0, 0])
        pieces = [(_bf(r[0, :, :128]), _bf(r[0, :, 128:]), m, False) for r, m in
                  ((kp_ref, (ki >= qi) & (j > 0)), (kc_ref, None), (kn_ref, (ki <= qi) & (j < nq - 1)))]
        pieces += [(ctx_k[:, i:i + Q_BLK], ctx_v[:, i:i + Q_BLK], None, True)
                   for i in range(0, ctx_k.shape[1], Q_BLK)]
    else:
        sink_ref, x_ref, o_ref = refs
        q = x_ref[0, :, :256]
        k, v = _bf(x_ref[0, :, 256:384]), _bf(x_ref[0, :, 384:512])
        pieces = [(k[i:i + Q_BLK], v[i:i + Q_BLK], None, False) for i in range(0, k.shape[0], Q_BLK)]
    q = _bf(q * (HEAD ** -0.5 * LOG2_E))
    tq = q.shape[0]
    first_g = lax.broadcasted_iota(jnp.int32, (2 * tq, 1), 0) < tq
    outs = []
    for h in range(2):
        hs = slice(h * HEAD, (h + 1) * HEAD)
        qg = jnp.concatenate([q[:, (2 * h + g) * HEAD:(2 * h + g + 1) * HEAD] for g in range(2)], axis=0)
        sink = jnp.where(first_g, sink_ref[layer, 2 * h], sink_ref[layer, 2 * h + 1]) * LOG2_E
        logits = []
        for kp, _, mask, fm in pieces:
            s = _dotf(qg, kp[hs]) if fm else _dotf(qg, kp[:, hs], _NT)
            logits.append(s if mask is None else jnp.where(jnp.concatenate([mask, mask], axis=0), s, NEG_INF))
        mx = jnp.maximum(jnp.max(functools.reduce(jnp.maximum, logits), axis=-1, keepdims=True), sink)
        es = [jnp.exp2(s - mx) for s in logits]
        den = jnp.sum(functools.reduce(lambda a, b: a + b, es), axis=-1, keepdims=True) + jnp.exp2(sink - mx)
        acc = functools.reduce(lambda a, b: a + b,
                               [_dotf(_bf(e), vp[hs], _NT) if fm else _dotf(_bf(e), vp[:, hs])
                                for e, (_, vp, _, fm) in zip(es, pieces)])
        o = acc / den
        outs += [o[:tq], o[tq:]]
    o_ref[0] = jnp.concatenate(outs, axis=-1)


def _win(pb, sink, cache_k, cache_v, layer):
    b, t, _ = pb.shape
    latent = cache_k is not None
    smem = pl.BlockSpec(memory_space=pltpu.SMEM)
    if latent:
        nq = t // Q_BLK
        grid = (b, nq)
        in_specs = [smem,
                    pl.BlockSpec((1, Q_BLK, 256), lambda i, j: (i, j, 0)),
                    pl.BlockSpec((1, Q_BLK, 256), lambda i, j: (i, jnp.maximum(j - 1, 0), 1)),
                    pl.BlockSpec((1, Q_BLK, 256), lambda i, j: (i, j, 1)),
                    pl.BlockSpec((1, Q_BLK, 256), lambda i, j: (i, jnp.minimum(j + 1, nq - 1), 1)),
                    pl.BlockSpec((1, 1) + cache_k.shape[2:], lambda i, j: (i, layer, 0, 0)),
                    pl.BlockSpec((1, 1) + cache_v.shape[2:], lambda i, j: (i, layer, 0, 0))]
        args = [sink, pb, pb, pb, pb, cache_k, cache_v]
        out_spec = pl.BlockSpec((1, Q_BLK, W_BR), lambda i, j: (i, j, 0))
        sem = ("parallel", "parallel")
    else:
        grid = (b,)
        in_specs = [smem, pl.BlockSpec((1, t, WB), lambda i: (i, 0, 0))]
        args = [sink, pb]
        out_spec = pl.BlockSpec((1, t, W_BR), lambda i: (i, 0, 0))
        sem = ("parallel",)
    return pl.pallas_call(
        functools.partial(_win_kernel, latent=latent, layer=layer),
        grid=grid,
        in_specs=in_specs,
        out_specs=out_spec,
        out_shape=jax.ShapeDtypeStruct((b, t, W_BR), F32),
        compiler_params=_params(sem),
        name="win",
    )(*args)


def _diff_kernel(*refs, latent, lam_init, layer):
    if latent:
        lam_ref, g_ref, q_ref, k_ref, v_ref, ck_ref, cv_ref, o_ref = refs
    else:
        lam_ref, g_ref, q_ref, k_ref, v_ref, o_ref = refs
    lp = lam_ref[layer]
    lam = (jnp.exp(jnp.sum(lp[0:1] * lp[1:2], axis=-1, keepdims=True))
           - jnp.exp(jnp.sum(lp[2:3] * lp[3:4], axis=-1, keepdims=True)) + lam_init)
    q = _bf(q_ref[0] * (DQ_D ** -0.5 * LOG2_E))
    key_sets = [(_bf(k_ref[0]), _bf(v_ref[0]), False)]
    if latent:
        key_sets.append((_bf(ck_ref[0, 0]), _bf(cv_ref[0, 0]), True))
    lane = lax.broadcasted_iota(jnp.int32, (1, W_BR), 1)
    out = jnp.zeros(q.shape, F32)
    for h in range(W_BR // HEAD):
        probs = []
        for m in range(2):
            qm = jnp.where(lane // DQ_D == 2 * h + m, q, jnp.zeros((), BF16))
            logits = [_dotf(qm, kk) if fm else _dotf(qm, kk, _NT) for kk, _, fm in key_sets]
            mx = functools.reduce(jnp.maximum, [jnp.max(s, axis=-1, keepdims=True) for s in logits])
            es = [jnp.exp2(s - mx) for s in logits]
            den = functools.reduce(lambda a, b: a + b, [jnp.sum(e, axis=-1, keepdims=True) for e in es])
            inv = 1.0 / den
            probs.append([e * inv for e in es])
        o = None
        for i, (_, vv, fm) in enumerate(key_sets):
            pi = _bf(probs[0][i] - lam * probs[1][i])
            oi = _dotf(pi, vv, _NT) if fm else _dotf(pi, vv)
            o = oi if o is None else o + oi
        out = jnp.where(lane // HEAD == h, o, out)
    ms = _mm2(out * out, _head_ones(W_BR)) * (1.0 / HEAD)
    gain = jnp.concatenate([g_ref[layer:layer + 1]] * (W_BR // HEAD), axis=1)
    o_ref[0] = out * lax.rsqrt(ms + NORM_EPS) * gain * (1.0 - lam_init)


def _diff(pd, lam_p, subln_g, cache_k, cache_v, layer, lam_init):
    b, t, _ = pd.shape
    latent = cache_k is not None
    tq = min(t, 512)
    in_specs = [pl.BlockSpec(lam_p.shape, lambda i, j: (0, 0, 0)),
                pl.BlockSpec(subln_g.shape, lambda i, j: (0, 0)),
                pl.BlockSpec((1, tq, 256), lambda i, j: (i, j, 0)),
                pl.BlockSpec((1, t, 256), lambda i, j: (i, 0, 1)),
                pl.BlockSpec((1, t, 256), lambda i, j: (i, 0, 2))]
    args = [lam_p, subln_g, pd, pd, pd]
    if latent:
        in_specs += [pl.BlockSpec((1, 1) + cache_k.shape[2:], lambda i, j: (i, layer, 0, 0)),
                     pl.BlockSpec((1, 1) + cache_v.shape[2:], lambda i, j: (i, layer, 0, 0))]
        args += [cache_k, cache_v]
    return pl.pallas_call(
        functools.partial(_diff_kernel, latent=latent, lam_init=lam_init, layer=layer),
        grid=(b, t // tq),
        in_specs=in_specs,
        out_specs=pl.BlockSpec((1, tq, W_BR), lambda i, j: (i, j, 0)),
        out_shape=jax.ShapeDtypeStruct((b, t, W_BR), F32),
        compiler_params=_params(("parallel", "parallel")),
        name="diff",
    )(*args)


def _out_kernel(x_ref, mod_ref, gpost_ref, w_ref, pa_ref, yf_ref, yb_ref, gng_ref, gnb_ref, rk_ref,
                ywin_ref, bg_ref, ylru_ref, cg_ref, ydiff_ref, dg_ref, o_ref, *, layer):
    lrow = slice(layer, layer + 1)
    ones = _head_ones(W_BR)
    pa = pa_ref[0]
    r, k, v, ag = pa[:, 0:256], pa[:, 256:512], pa[:, 512:768], pa[:, 896:1152]
    y = yf_ref[0] + yb_ref[0]
    mu = _mm2(y, ones) * (1.0 / HEAD)
    yc = y - mu
    var = _mm2(yc * yc, ones) * (1.0 / HEAD)
    ya = yc * lax.rsqrt(var + GN_EPS) * gng_ref[lrow] + gnb_ref[lrow]
    ya = ya + _mm2(r * k * rk_ref[lrow], ones) * v
    mix = jnp.concatenate([ya * _silu(ag), ywin_ref[0] * _silu(bg_ref[0]),
                           ylru_ref[0] * _silu(cg_ref[0]), ydiff_ref[0] * _silu(dg_ref[0])], axis=-1)
    o = jnp.dot(mix.astype(BF16), w_ref[0], preferred_element_type=F32)
    o = o * lax.rsqrt(jnp.mean(o * o, -1, keepdims=True) + NORM_EPS) * gpost_ref[lrow]
    o_ref[0] = x_ref[0] + mod_ref[0][:, 2 * D_MODEL:] * o


def _out(x, mod_l, g_post, w_out_bf, layer, latent, pa, yf, yb, gn_g, gn_b, r_k, ywin, pb, ylru, pc, ydiff, pd):
    b, t, _ = x.shape
    tm = min(t, ROW_TILE)
    mod_idx = (lambda i, j: (1 + i, 0, 0)) if latent else (lambda i, j: (0, 0, 0))
    rows = lambda w, c=0: pl.BlockSpec((1, tm, w), lambda i, j: (i, j, c))
    vec = pl.BlockSpec(gn_g.shape, lambda i, j: (0, 0))
    in_specs = [rows(D_MODEL),
                pl.BlockSpec((1, 1, 3 * D_MODEL), mod_idx),
                pl.BlockSpec(g_post.shape, lambda i, j: (0, 0)),
                pl.BlockSpec((1, D_MODEL, D_MODEL), lambda i, j: (layer, 0, 0)),
                rows(WA), rows(W_BR), rows(W_BR), vec, vec, vec,
                rows(W_BR), rows(W_BR, 2), rows(W_BR), rows(W_BR, 1), rows(W_BR), rows(W_BR, 3)]
    return pl.pallas_call(
        functools.partial(_out_kernel, layer=layer),
        grid=(b, t // tm),
        in_specs=in_specs,
        out_specs=rows(D_MODEL),
        out_shape=jax.ShapeDtypeStruct((b, t, D_MODEL), F32),
        compiler_params=_params(("parallel", "parallel")),
        name="out",
    )(x, mod_l, g_post, w_out_bf, pa, yf, yb, gn_g, gn_b, r_k, ywin, pb, ylru, pc, ydiff, pd)


def _layer(x, mod_l, layer, lam_init, wts, cache, tables):
    latent = cache is not None
    pa, pb, pc, pd, *ctx_t = _project(x, mod_l, wts['g_pre'], wts['w_in_bf'], layer, tables)
    rw = _rwkv(pa, wts['rwkv_w0'], wts['rwkv_w_up'], wts['rwkv_a0'], wts['rwkv_a_up'], wts['rwkv_k_k'],
               wts['rwkv_k_a'], cache['rwkv'] if latent else None, layer)
    lr = _lru(pc, wts['lru_conv_w'], wts['lru_conv_b'], wts['lru_wa'], wts['lru_ba'], wts['lru_wx'],
              wts['lru_bx'], wts['lru_lambda'], cache['lru'] if latent else None, layer)
    ywin = _win(pb, wts['win_sink'], cache['win_k'] if latent else None,
                cache['win_v'] if latent else None, layer)
    ydiff = _diff(pd, wts['diff_lambda'], wts['diff_subln_g'],
                  cache['diff_k'] if latent else None, cache['diff_v'] if latent else None, layer, lam_init)
    y = _out(x, mod_l, wts['g_post'], wts['w_out_bf'], layer, latent, pa, rw[0], rw[1],
             wts['rwkv_gn_g'], wts['rwkv_gn_b'], wts['rwkv_r_k'], ywin, pb, lr[0], pc, ydiff, pd)
    new_cache = None if latent else (ctx_t, rw[2], lr[1])
    return y, new_cache


def kernel(x_prompt, x_sample, c, cache_win_k, cache_win_v, cache_diff_k, cache_diff_v, state_rwkv, state_lru,
           c_ctx, w_mod, b_mod, g_pre, g_post, w_in, w_out,
           rwkv_w0, rwkv_w_up, rwkv_a0, rwkv_a_up, rwkv_k_k, rwkv_k_a, rwkv_r_k, rwkv_gn_g, rwkv_gn_b,
           win_sink, lru_conv_w, lru_conv_b, lru_wa, lru_ba, lru_wx, lru_bx, lru_lambda,
           diff_lambda, diff_subln_g):
    n_b, seq = x_prompt.shape[:2]
    n_dec, dec_seq = x_sample.shape[:2]
    past = cache_win_k.shape[2]
    wts = dict(g_pre=g_pre, g_post=g_post, w_in_bf=w_in.astype(BF16), w_out_bf=w_out.astype(BF16),
               rwkv_w0=rwkv_w0, rwkv_w_up=rwkv_w_up, rwkv_a0=rwkv_a0, rwkv_a_up=rwkv_a_up,
               rwkv_k_k=rwkv_k_k, rwkv_k_a=rwkv_k_a, rwkv_r_k=rwkv_r_k.reshape(DEPTH, W_BR), rwkv_gn_g=rwkv_gn_g,
               rwkv_gn_b=rwkv_gn_b, win_sink=win_sink, lru_conv_w=lru_conv_w, lru_conv_b=lru_conv_b,
               lru_wa=lru_wa, lru_ba=lru_ba, lru_wx=lru_wx, lru_bx=lru_bx, lru_lambda=lru_lambda,
               diff_lambda=diff_lambda, diff_subln_g=diff_subln_g)
    fm = lambda a: jnp.moveaxis(a, 2, -1).reshape(n_dec, DEPTH, -1, past)
    cache = dict(win_k=fm(cache_win_k), win_v=fm(cache_win_v), diff_k=fm(cache_diff_k), diff_v=fm(cache_diff_v),
                 rwkv=state_rwkv, lru=state_lru)
    cvec = jnp.concatenate([c_ctx[None], c, jnp.zeros((8 - 1 - n_dec, D_MODEL), F32)], axis=0)
    mod = _modulation(cvec, w_mod, b_mod)
    cos_b, sin_b = _rope_tables(dec_seq, HEAD, 384)
    cos_d, sin_d = _rope_tables(dec_seq, DQ_D, 512)
    tables = (cos_b, sin_b, cos_d, sin_d)

    y_p, y_s = x_prompt, x_sample
    ctx = []
    for l in range(DEPTH):
        lam_init = 0.8 - 0.6 * math.exp(-0.3 * l)
        mod_l = mod[l].reshape(8, 1, 3 * D_MODEL)
        y_p, nc = _layer(y_p, mod_l, l, lam_init, wts, None, None)
        ctx.append(nc)
        y_s, _ = _layer(y_s, mod_l, l, lam_init, wts, cache, tables)
    stack = lambda f: jnp.stack([f(ct) for ct in ctx], axis=1)
    tm_ = lambda i, dims: jnp.moveaxis(stack(lambda ct: ct[0][i]).reshape((n_b, DEPTH) + dims + (seq,)), -1, 2)
    new_win_k = tm_(0, (2, HEAD))
    new_win_v = tm_(1, (2, HEAD))
    new_diff_k = tm_(2, (4, 2, DQ_D))
    new_diff_v = tm_(3, (4, HEAD))
    new_state_rwkv = stack(lambda ct: ct[1])
    new_state_lru = stack(lambda ct: ct[2])
    return (y_p, y_s, new_win_k, new_win_v, new_diff_k, new_diff_v, new_state_rwkv, new_state_lru)
```

```python
import functools
import math

import numpy as np
import jax
import jax.numpy as jnp
from jax import lax
from jax.experimental import pallas as pl
from jax.experimental.pallas import tpu as pltpu

F32 = jnp.float32
BF16 = jnp.bfloat16
HI = lax.Precision.HIGHEST

D_MODEL = 1024
DEPTH = 2
GRID_W = 64
ROPE_BASE = 10000.0
NORM_EPS = 1e-6
NEG_INF = -1e30
LOG2_E = math.log2(math.e)
GN_EPS = 64e-5
C_RG = 8.0
W_BR = 256
HEAD = 64
SUBLANES = 8
LORA = 64
DQ_D = 32
WINDOW = 128
Q_BLK = 128
P_TOTAL = 3456
WA, WB, WC, WD = 1152, 768, 512, 1024
CHUNK = 64
RWKV_BATCH = 4
ROW_TILE = 512
VMEM_LIMIT = 58 * 1024 * 1024


_NN = (((1,), (0,)), ((), ()))
_NT = (((1,), (1,)), ((), ()))
_TN = (((0,), (0,)), ((), ()))


def _dot_hi(a, b):
    return jnp.dot(a, b, precision=HI, preferred_element_type=F32)


def _bf(x):
    return x.astype(BF16)


def _dotf(x, y, dims=_NN):
    return lax.dot_general(x, y, dims, preferred_element_type=F32)


def _split(x):
    hi = x.astype(BF16)
    return hi, (x - hi.astype(F32)).astype(BF16)


def _mm3(a, b, dims=_NN):
    return _dotf(a[0], b[0], dims) + (_dotf(a[0], b[1], dims) + _dotf(a[1], b[0], dims))


def _mm2(x, m01, left=False):
    hi, lo = _split(x)
    if left:
        return _dotf(m01, hi) + _dotf(m01, lo)
    return _dotf(hi, m01) + _dotf(lo, m01)


def _dot_bf(a, b):
    return jnp.dot(a.astype(BF16), b.astype(BF16), preferred_element_type=F32)


def _dot_nt_bf(a, b):
    return lax.dot_general(a.astype(BF16), b.astype(BF16), (((1,), (1,)), ((), ())),
                           preferred_element_type=F32)


def _sigmoid(x):
    return 1.0 / (1.0 + jnp.exp(-x))


def _silu(x):
    return x * _sigmoid(x)


def _softplus(x):
    return jnp.maximum(x, 0.0) + jnp.log1p(jnp.exp(-jnp.abs(x)))


def _head_ones(n):
    r = lax.broadcasted_iota(jnp.int32, (n, n), 0) // HEAD
    c = lax.broadcasted_iota(jnp.int32, (n, n), 1) // HEAD
    return jnp.where(r == c, 1.0, 0.0).astype(BF16)


def _params(sem):
    return pltpu.CompilerParams(dimension_semantics=sem, vmem_limit_bytes=VMEM_LIMIT)


def _mod_kernel(c_ref, w_ref, b_ref, o_ref):
    o_ref[0] = _mm3(_split(_silu(c_ref[...])), _split(w_ref[0])) + b_ref[0]


def _modulation(cvec, w_mod, b_mod):
    n_l = w_mod.shape[0]
    tn = 512
    return pl.pallas_call(
        _mod_kernel,
        grid=(n_l, 3 * D_MODEL // tn),
        in_specs=[pl.BlockSpec((8, D_MODEL), lambda l, j: (0, 0)),
                  pl.BlockSpec((1, D_MODEL, tn), lambda l, j: (l, 0, j)),
                  pl.BlockSpec((1, 1, tn), lambda l, j: (l, 0, j))],
        out_specs=pl.BlockSpec((1, 8, tn), lambda l, j: (l, 0, j)),
        out_shape=jax.ShapeDtypeStruct((n_l, 8, 3 * D_MODEL), F32),
        compiler_params=_params(("parallel", "parallel")),
        name="mod",
    )(cvec, w_mod, b_mod.reshape(n_l, 1, 3 * D_MODEL))


def _rope(x, cos, sin_signed, off):
    w = x.shape[-1]
    lane = lax.broadcasted_iota(jnp.int32, x.shape, 1)
    first = (lane % (2 * off)) < off
    partner = jnp.where(first, pltpu.roll(x, w - off, 1), pltpu.roll(x, off, 1))
    return x * cos + partner * sin_signed


def _proj_kernel(*refs, latent, layer):
    if latent:
        x_ref, mod_ref, g_ref, w_ref, cb_ref, sb_ref, cd_ref, sd_ref, oa, ob, oc, od = refs
    else:
        x_ref, mod_ref, g_ref, w_ref, oa, ob, oc, od, o_wk, o_wv, o_dk, o_dv = refs
    x = x_ref[0]
    y = x * lax.rsqrt(jnp.mean(x * x, -1, keepdims=True) + NORM_EPS) * g_ref[layer:layer + 1]
    m = mod_ref[0]
    h = y * (1.0 + m[:, D_MODEL:2 * D_MODEL]) + m[:, :D_MODEL]
    p = jnp.dot(h.astype(BF16), w_ref[0], preferred_element_type=F32)
    oa[0] = p[:, :WA]
    pb = p[:, WA:WA + WB]
    pd = p[:, WA + WB + WC:]
    if latent:
        ob[0, :, :384] = _rope(pb[:, :384], cb_ref[...], sb_ref[...], 16)
        ob[0, :, 384:] = pb[:, 384:]
        od[0, :, :512] = _rope(pd[:, :512], cd_ref[...], sd_ref[...], 8)
        od[0, :, 512:] = pd[:, 512:]
    else:
        ob[0] = pb
        od[0] = pd
        o_wk[0] = pb[:, 256:384].T
        o_wv[0] = pb[:, 384:512].T
        o_dk[0] = pd[:, 256:512].T
        o_dv[0] = pd[:, 512:768].T
    oc[0] = p[:, WA + WB:WA + WB + WC]


def _project(x, mod_l, g_pre, w_in_bf, layer, tables):
    b, t, _ = x.shape
    latent = tables is not None
    tm = min(t, ROW_TILE)
    mod_idx = (lambda i, j: (1 + i, 0, 0)) if latent else (lambda i, j: (0, 0, 0))
    in_specs = [pl.BlockSpec((1, tm, D_MODEL), lambda i, j: (i, j, 0)),
                pl.BlockSpec((1, 1, 3 * D_MODEL), mod_idx),
                pl.BlockSpec(g_pre.shape, lambda i, j: (0, 0)),
                pl.BlockSpec((1, D_MODEL, P_TOTAL), lambda i, j: (layer, 0, 0))]
    args = [x, mod_l, g_pre, w_in_bf]
    if latent:
        for tab in tables:
            in_specs.append(pl.BlockSpec((tm, tab.shape[1]), lambda i, j: (j, 0)))
            args.append(tab)
    widths = (WA, WB, WC, WD)
    out_specs = [pl.BlockSpec((1, tm, w), lambda i, j: (i, j, 0)) for w in widths]
    out_shape = [jax.ShapeDtypeStruct((b, t, w), F32) for w in widths]
    if not latent:
        for w in (128, 128, W_BR, W_BR):
            out_specs.append(pl.BlockSpec((1, w, tm), lambda i, j: (i, 0, j)))
            out_shape.append(jax.ShapeDtypeStruct((b, w, t), F32))
    return pl.pallas_call(
        functools.partial(_proj_kernel, latent=latent, layer=layer),
        grid=(b, t // tm),
        in_specs=in_specs,
        out_specs=out_specs,
        out_shape=out_shape,
        compiler_params=_params(("parallel", "parallel")),
        name="proj",
    )(*args)


def _rope_tables(t, head_dim, n_lanes):
    half = head_dim // 2
    quarter = half // 2
    pos = np.arange(t)
    row = (pos // GRID_W).astype(np.float32)
    col = (pos % GRID_W).astype(np.float32)
    inv = np.float32(ROPE_BASE) ** (-np.arange(0, half, 2, dtype=np.float32) / np.float32(half))
    lane = np.arange(n_lanes) % head_dim
    in_half = lane % half
    p = np.where((lane < half)[None, :], row[:, None], col[:, None])
    ang = (p * inv[in_half % quarter][None, :]).astype(np.float64)
    sign = np.where(in_half < quarter, -1.0, 1.0)[None, :]
    return jnp.asarray(np.cos(ang), F32), jnp.asarray(np.sin(ang) * sign, F32)


_PREP_FIELDS = (('ar', 2, 1, BF16), ('nm', 1, 1, BF16), ('mm', 1, 1, BF16), ('pq', 1, 2, BF16),
                ('v', 1, 1, BF16), ('bkt', 1, 2, BF16), ('gct', 1, 1, F32))


def _prep_scratch(slots, n_streams):
    return [pltpu.VMEM((slots, n_streams, rows * CHUNK, lanes * W_BR), dt) for _, rows, lanes, dt in _PREP_FIELDS]


class _SlotView:
    def __init__(self, ref, slot):
        self.ref, self.slot, self.shape = ref, slot, ref.shape[1:]

    def __getitem__(self, idx):
        return self.ref[(self.slot,) + (idx if isinstance(idx, tuple) else (idx,))]

    def __setitem__(self, idx, val):
        self.ref[(self.slot,) + (idx if isinstance(idx, tuple) else (idx,))] = val


def _head_transpose(x):
    xt = x.T
    return jnp.concatenate([xt[h * HEAD:(h + 1) * HEAD] for h in range(x.shape[1] // HEAD)], axis=1)


def _expand(x, same):
    return jnp.concatenate([x] * (same.shape[0] // x.shape[0]), axis=0) * same


def _rwkv_prepare(xs, dirs, w0_ref, wup_ref, a0_ref, aup_ref, k_k, k_a, m_ref, same_ref, out):
    c = xs[0].shape[0]
    ones = _head_ones(W_BR)
    ti = lax.broadcasted_iota(jnp.int32, (c, c), 0)
    si = lax.broadcasted_iota(jnp.int32, (c, c), 1)
    cums = [jnp.where(si <= ti, 1.0, 0.0).astype(BF16), jnp.where(si >= ti, 1.0, 0.0).astype(BF16)]
    same = same_ref[...]

    st = []
    for xc, d in zip(xs, dirs):
        r, k, v = xc[:, 0:256], xc[:, 256:512], xc[:, 512:768]
        wd, ad = xc[:, 768:832], xc[:, 832:896]
        kk = k * k_k
        kk = kk * lax.rsqrt(_mm2(kk * kk, ones) + 1e-12)
        z = w0_ref[d:d + 1] + _mm3(_split(jnp.tanh(wd)), _split(wup_ref[d]))
        e = jnp.exp(-_softplus(-z) - 0.5)
        a = _sigmoid(a0_ref[d:d + 1] + _mm3(_split(ad), _split(aup_ref[d])))
        st.append(dict(r=r, v=v, e=e, kd=k * (1.0 + (a - 1.0) * k_a), alpha=-kk, beta=kk * a))
    yield
    for s, d in zip(st, dirs):
        l_incl = _mm2(s['e'], cums[d], left=True)
        l_tot = jnp.sum(s['e'], axis=0, keepdims=True)
        grow = jnp.exp(l_incl)
        tail = jnp.exp(l_incl - l_tot)
        s.update(ar=jnp.concatenate([_bf(s['alpha'] * jnp.exp(s['e'] - l_incl)),
                                     _bf(s['r'] * jnp.exp(-l_incl))], axis=0),
                 b_t=_bf(s['beta'] * grow), k_t=_bf(s['kd'] * grow),
                 b_h=s['beta'] * tail, k_h=s['kd'] * tail,
                 g_c=jnp.broadcast_to(jnp.exp(-l_tot), (c, W_BR)))
    yield
    for g, (s, d) in enumerate(zip(st, dirs)):
        out['ar'][g] = s['ar']
        rhs_t = jnp.concatenate([_expand(s['b_t'], same), _expand(s['k_t'], same)], axis=0)
        g4 = _dotf(s['ar'], rhs_t, _NT)
        strict = m_ref[_M_STRICT + 2 * d]
        incl = m_ref[_M_INCL + 2 * d]
        out['nm'][g] = _bf(g4[:c, :W_BR]) * strict
        out['mm'][g] = _bf(g4[:c, W_BR:]) * strict
        out['pq'][g, :, :W_BR] = _bf(g4[c:, :W_BR]) * incl
        out['pq'][g, :, W_BR:] = _bf(g4[c:, W_BR:]) * incl
    yield
    for g, s in enumerate(st):
        out['v'][g] = _bf(s['v'])
        out['bkt'][g] = jnp.concatenate([_bf(_head_transpose(s['b_h'])), _bf(_head_transpose(s['k_h']))], axis=1)
        out['gct'][g] = _head_transpose(s['g_c'])


def _rwkv_advance(p, m_ref, same_ref, z_ref, y_stores):
    n_streams, c = p['nm'].shape[0], p['nm'].shape[1]
    gs = range(n_streams)
    same = same_ref[...]
    ex = lambda x: _expand(_bf(x), same)
    invs = [(p['nm'][g] * m_ref[_M_PAIR]).astype(F32) + m_ref[_M_EYE].astype(F32) for g in gs]
    for lvl in range(int(math.log2(c)) - 1):
        inv_x = [ex(t) for t in invs]
        half = [_dotf(p['nm'][g] * m_ref[_M_OFF + lvl], inv_x[g]) for g in gs]
        invs = [invs[g] + _dotf(_bf(invs[g]), ex(half[g])) for g in gs]
        yield
    z0 = [z_ref[g] for g in gs]
    as0 = [_dotf(p['ar'][g], ex(z0[g])) for g in gs]
    vx = [_expand(p['v'][g], same) for g in gs]
    rhs = [as0[g][:c] + _dotf(p['mm'][g], vx[g]) for g in gs]
    yield
    uv = [jnp.concatenate([ex(_dotf(_bf(invs[g]), ex(rhs[g]))), vx[g]], axis=0) for g in gs]
    yield
    for g in gs:
        z_ref[g] = z0[g] * p['gct'][g] + _dotf(p['bkt'][g], uv[g])
    yield
    for g in gs:
        y_stores[g](as0[g][c:] + _dotf(p['pq'][g], uv[g]))


def _interleave(*gens):
    live = list(gens)
    while live:
        for gen in list(live):
            try:
                next(gen)
            except StopIteration:
                live.remove(gen)


_M_STRICT, _M_INCL, _M_EYE, _M_PAIR, _M_OFF = 0, 1, 4, 5, 6


def _rwkv_masks(c, n_h):
    t = np.arange(c)[:, None]
    s = np.arange(c)[None, :]
    masks = [s < t, s <= t, s > t, s >= t, s == t, ((t // 2) == (s // 2)) & (t != s)]
    b = 2
    while b < c:
        masks.append(((t // (2 * b)) == (s // (2 * b))) & ((t // b) != (s // b)))
        b *= 2
    masks = np.tile(np.stack(masks).astype(np.float32), (1, 1, n_h))
    r = np.arange(n_h * c)[:, None] // c
    l = np.arange(n_h * HEAD)[None, :] // HEAD
    return jnp.asarray(masks, BF16), jnp.asarray((r == l).astype(np.float32), BF16)


def _rwkv_kernel(*refs, latent, n_b, layer):
    (xf_ref, xb_ref, xfn_ref, xbn_ref, w0_ref, wup_ref, a0_ref, aup_ref, kk_ref, ka_ref,
     m_ref, same_ref) = refs[:12]
    if latent:
        s0_ref, yf_ref, yb_ref = refs[12:15]
    else:
        yf_ref, yb_ref, st_ref = refs[12:15]
    scr = refs[15:]
    z_scr = scr[0]
    names = [f[0] for f in _PREP_FIELDS]
    ci = pl.program_id(1)
    step = pl.program_id(0) * pl.num_programs(1) + ci
    view = lambda refs_, slot: {k: _SlotView(r, slot) for k, r in zip(names, refs_)}
    cur = view(scr[1:1 + len(names)], step % 2)
    nxt = view(scr[1:1 + len(names)], 1 - step % 2)
    mid = view(scr[1 + len(names):], 0)
    n_h = W_BR // HEAD
    c = CHUNK
    head = lambda h: slice(h * HEAD, (h + 1) * HEAD)
    streams = [(bi, d) for bi in range(n_b) for d in range(2)]
    dirs = [d for _, d in streams]
    params = (w0_ref[layer], wup_ref[0], a0_ref[layer], aup_ref[0], kk_ref[layer:layer + 1],
              ka_ref[layer:layer + 1], m_ref, same_ref)
    first = lambda d: slice(0, c) if d == 0 else slice(c, 2 * c)
    second = lambda d: slice(c, 2 * c) if d == 0 else slice(0, c)
    x_refs, xn_refs, y_refs = (xf_ref, xb_ref), (xfn_ref, xbn_ref), (yf_ref, yb_ref)

    def y_store(bi, d, rows):
        def store(y):
            y_refs[d][bi, rows, :] = y
        return store

    @pl.when(ci == 0)
    def _():
        for g, (bi, d) in enumerate(streams):
            if latent:
                z_scr[g] = jnp.concatenate([s0_ref[bi, 0, d, h].T for h in range(n_h)], axis=1)
            else:
                z_scr[g] = jnp.zeros(z_scr.shape[1:], F32)

    @pl.when(step == 0)
    def _():
        _interleave(_rwkv_prepare([x_refs[d][bi, first(d), :] for bi, d in streams], dirs, *params, cur))

    _interleave(_rwkv_advance(cur, m_ref, same_ref, z_scr, [y_store(bi, d, first(d)) for bi, d in streams]),
                _rwkv_prepare([x_refs[d][bi, second(d), :] for bi, d in streams], dirs, *params, mid))
    _interleave(_rwkv_advance(mid, m_ref, same_ref, z_scr, [y_store(bi, d, second(d)) for bi, d in streams]),
                _rwkv_prepare([xn_refs[d][bi, first(d), :] for bi, d in streams], dirs, *params, nxt))

    if not latent:
        @pl.when(ci == pl.num_programs(1) - 1)
        def _():
            for g, (bi, d) in enumerate(streams):
                z = z_scr[g]
                for h in range(n_h):
                    st_ref[bi, d, h] = z[:, head(h)].T


def _rwkv(pa, w0, wup, a0, aup, k_k, k_a, state, layer):
    b, t, _ = pa.shape
    latent = state is not None
    ns = t // (2 * CHUNK)
    n_h = W_BR // HEAD
    n_b = RWKV_BATCH
    masks, same = _rwkv_masks(CHUNK, n_h)
    full = lambda shape: pl.BlockSpec(shape, lambda i, j: (0,) * len(shape))
    one_layer = lambda shape: pl.BlockSpec((1,) + shape[1:], lambda i, j: (layer,) + (0,) * (len(shape) - 1))
    rows = lambda w, idx: pl.BlockSpec((n_b, 2 * CHUNK, w), idx)
    nxt_group = lambda i, j: jnp.minimum(i + (j + 1) // ns, b // n_b - 1)
    in_specs = [rows(WA, lambda i, j: (i, j, 0)),
                rows(WA, lambda i, j: (i, ns - 1 - j, 0)),
                rows(WA, lambda i, j: (nxt_group(i, j), (j + 1) % ns, 0)),
                rows(WA, lambda i, j: (nxt_group(i, j), (2 * ns - 2 - j) % ns, 0)),
                full(w0.shape), one_layer(wup.shape), full(a0.shape), one_layer(aup.shape),
                full(k_k.shape), full(k_a.shape), full(masks.shape), full(same.shape)]
    args = [pa, pa, pa, pa, w0, wup, a0, aup, k_k, k_a, masks, same]
    out_specs = [rows(W_BR, lambda i, j: (i, j, 0)), rows(W_BR, lambda i, j: (i, ns - 1 - j, 0))]
    out_shape = [jax.ShapeDtypeStruct((b, t, W_BR), F32)] * 2
    if latent:
        in_specs.append(pl.BlockSpec((n_b, 1, 2, n_h, HEAD, HEAD), lambda i, j: (i, layer, 0, 0, 0, 0)))
        args.append(state)
    else:
        out_specs.append(pl.BlockSpec((n_b, 2, n_h, HEAD, HEAD), lambda i, j: (i, 0, 0, 0, 0)))
        out_shape.append(jax.ShapeDtypeStruct((b, 2, n_h, HEAD, HEAD), F32))
    return pl.pallas_call(
        functools.partial(_rwkv_kernel, latent=latent, n_b=n_b, layer=layer),
        grid=(b // n_b, ns),
        in_specs=in_specs,
        out_specs=out_specs,
        out_shape=out_shape,
        scratch_shapes=([pltpu.VMEM((2 * n_b, CHUNK, W_BR), F32)]
                        + _prep_scratch(2, 2 * n_b) + _prep_scratch(1, 2 * n_b)),
        compiler_params=_params(("arbitrary", "arbitrary")),
        name="rwkv",
    )(*args)


def _lru_kernel(*refs, latent, layer):
    if latent:
        (x_ref, cw_ref, cb_ref, wa_ref, ba_ref, wx_ref, bx_ref, lam_ref, h0_ref, y_ref,
         a_scr, h_scr) = refs
    else:
        (x_ref, cw_ref, cb_ref, wa_ref, ba_ref, wx_ref, bx_ref, lam_ref, y_ref, st_ref,
         a_scr, h_scr) = refs
    x = x_ref[0, :, :W_BR]
    t = x.shape[0]
    row = lax.broadcasted_iota(jnp.int32, x.shape, 0)

    def shift_dn(z, k, fill):
        return jnp.where(row >= k, pltpu.roll(z, k, 0), fill)

    def shift_up(z, k, fill):
        return jnp.where(row < t - k, pltpu.roll(z, t - k, 0), fill)

    cw = cw_ref[layer]
    xc = (cb_ref[layer:layer + 1] + shift_dn(x, 2, 0.0) * cw[0:1] + shift_dn(x, 1, 0.0) * cw[1:2]
          + x * cw[2:3] + shift_up(x, 1, 0.0) * cw[3:4])
    xs = _split(xc)
    sub = row % SUBLANES
    n_blk = W_BR // HEAD
    zero = jnp.zeros((HEAD, HEAD), F32)

    def block_diag(w_ref, d):
        return jnp.concatenate([jnp.concatenate([w_ref[0, d, n] if m == n else zero for m in range(n_blk)], axis=1)
                                for n in range(n_blk)], axis=0)

    for d in range(2):
        gate_a = _sigmoid(_mm3(xs, _split(block_diag(wa_ref, d))) + ba_ref[layer, d:d + 1])
        gate_x = _sigmoid(_mm3(xs, _split(block_diag(wx_ref, d))) + bx_ref[layer, d:d + 1])
        log_a = -C_RG * gate_a * _softplus(-lam_ref[layer, d:d + 1])
        a = jnp.exp(log_a)
        u = jnp.sqrt(-jnp.tanh(log_a) * (a * a + 1.0)) * (gate_x * xc)
        k = 1
        while k < SUBLANES:
            if d == 0:
                keep = sub >= k
                sh = lambda z: pltpu.roll(z, k, 0)
            else:
                keep = sub < SUBLANES - k
                sh = lambda z: pltpu.roll(z, t - k, 0)
            u = a * jnp.where(keep, sh(u), 0.0) + u
            a = a * jnp.where(keep, sh(a), 1.0)
            k *= 2
        a_scr[d] = a
        h_scr[d] = u

    n_grp = t // SUBLANES
    if latent:
        carry0 = (h0_ref[0, 0, 0:1], h0_ref[0, 0, 1:2])
    else:
        carry0 = (jnp.zeros((1, W_BR), F32),) * 2

    def chain(i, carry):
        cf, cb = carry
        rf = pl.ds(pl.multiple_of(i * SUBLANES, SUBLANES), SUBLANES)
        rb = pl.ds(pl.multiple_of((n_grp - 1 - i) * SUBLANES, SUBLANES), SUBLANES)
        hf = h_scr[0, rf, :] + a_scr[0, rf, :] * cf
        hb = h_scr[1, rb, :] + a_scr[1, rb, :] * cb
        h_scr[0, rf, :] = hf
        h_scr[1, rb, :] = hb
        return hf[SUBLANES - 1:SUBLANES], hb[0:1]

    cf, cb = lax.fori_loop(0, n_grp, chain, carry0, unroll=4)
    y_ref[0] = h_scr[0] + h_scr[1]
    if not latent:
        st_ref[0] = jnp.concatenate([cf, cb], axis=0)


def _lru(pc, cw, cb, wa, ba, wx, bx, lam, state, layer):
    b, t, _ = pc.shape
    latent = state is not None
    full = lambda shape: pl.BlockSpec(shape, lambda i: (0,) * len(shape))
    one_layer = lambda shape: pl.BlockSpec((1,) + shape[1:], lambda i: (layer,) + (0,) * (len(shape) - 1))
    in_specs = [pl.BlockSpec((1, t, WC), lambda i: (i, 0, 0)),
                full(cw.shape), full(cb.shape), one_layer(wa.shape), full(ba.shape),
                one_layer(wx.shape), full(bx.shape), full(lam.shape)]
    args = [pc, cw, cb, wa, ba, wx, bx, lam]
    out_specs = [pl.BlockSpec((1, t, W_BR), lambda i: (i, 0, 0))]
    out_shape = [jax.ShapeDtypeStruct((b, t, W_BR), F32)]
    if latent:
        in_specs.append(pl.BlockSpec((1, 1, 2, W_BR), lambda i: (i, layer, 0, 0)))
        args.append(state)
    else:
        out_specs.append(pl.BlockSpec((1, 2, W_BR), lambda i: (i, 0, 0)))
        out_shape.append(jax.ShapeDtypeStruct((b, 2, W_BR), F32))
    return pl.pallas_call(
        functools.partial(_lru_kernel, latent=latent, layer=layer),
        grid=(b,),
        in_specs=in_specs,
        out_specs=out_specs,
        out_shape=out_shape,
        scratch_shapes=[pltpu.VMEM((2, t, W_BR), F32), pltpu.VMEM((2, t, W_BR), F32)],
        compiler_params=_params(("parallel",)),
        name="lru",
    )(*args)


def _win_kernel(*refs, latent, layer):
    if latent:
        sink_ref, q_ref, kp_ref, kc_ref, kn_ref, ck_ref, cv_ref, o_ref = refs
        j = pl.program_id(1)
        nq = pl.num_programs(1)
        qi = lax.broadcasted_iota(jnp.int32, (Q_BLK, Q_BLK), 0)
        ki = lax.broadcasted_iota(jnp.int32, (Q_BLK, Q_BLK), 1)
        q = q_ref[0]
        ctx_k, ctx_v = _bf(ck_ref[0, 0]), _bf(cv_ref[0, 0])
        pieces = [(_bf(r[0, :, :128]), _bf(r[0, :, 128:]), m, False) for r, m in
                  ((kp_ref, (ki >= qi) & (j > 0)), (kc_ref, None), (kn_ref, (ki <= qi) & (j < nq - 1)))]
        pieces += [(ctx_k[:, i:i + Q_BLK], ctx_v[:, i:i + Q_BLK], None, True)
                   for i in range(0, ctx_k.shape[1], Q_BLK)]
    else:
        sink_ref, x_ref, o_ref = refs
        q = x_ref[0, :, :256]
        k, v = _bf(x_ref[0, :, 256:384]), _bf(x_ref[0, :, 384:512])
        pieces = [(k[i:i + Q_BLK], v[i:i + Q_BLK], None, False) for i in range(0, k.shape[0], Q_BLK)]
    q = _bf(q * (HEAD ** -0.5 * LOG2_E))
    tq = q.shape[0]
    first_g = lax.broadcasted_iota(jnp.int32, (2 * tq, 1), 0) < tq
    outs = []
    for h in range(2):
        hs = slice(h * HEAD, (h + 1) * HEAD)
        qg = jnp.concatenate([q[:, (2 * h + g) * HEAD:(2 * h + g + 1) * HEAD] for g in range(2)], axis=0)
        sink = jnp.where(first_g, sink_ref[layer, 2 * h], sink_ref[layer, 2 * h + 1]) * LOG2_E
        logits = []
        for kp, _, mask, fm in pieces:
            s = _dotf(qg, kp[hs]) if fm else _dotf(qg, kp[:, hs], _NT)
            logits.append(s if mask is None else jnp.where(jnp.concatenate([mask, mask], axis=0), s, NEG_INF))
        mx = jnp.maximum(jnp.max(functools.reduce(jnp.maximum, logits), axis=-1, keepdims=True), sink)
        es = [jnp.exp2(s - mx) for s in logits]
        den = jnp.sum(functools.reduce(lambda a, b: a + b, es), axis=-1, keepdims=True) + jnp.exp2(sink - mx)
        acc = functools.reduce(lambda a, b: a + b,
                               [_dotf(_bf(e), vp[hs], _NT) if fm else _dotf(_bf(e), vp[:, hs])
                                for e, (_, vp, _, fm) in zip(es, pieces)])
        o = acc / den
        outs += [o[:tq], o[tq:]]
    o_ref[0] = jnp.concatenate(outs, axis=-1)


def _win(pb, sink, cache_k, cache_v, layer):
    b, t, _ = pb.shape
    latent = cache_k is not None
    smem = pl.BlockSpec(memory_space=pltpu.SMEM)
    if latent:
        nq = t // Q_BLK
        grid = (b, nq)
        in_specs = [smem,
                    pl.BlockSpec((1, Q_BLK, 256), lambda i, j: (i, j, 0)),
                    pl.BlockSpec((1, Q_BLK, 256), lambda i, j: (i, jnp.maximum(j - 1, 0), 1)),
                    pl.BlockSpec((1, Q_BLK, 256), lambda i, j: (i, j, 1)),
                    pl.BlockSpec((1, Q_BLK, 256), lambda i, j: (i, jnp.minimum(j + 1, nq - 1), 1)),
                    pl.BlockSpec((1, 1) + cache_k.shape[2:], lambda i, j: (i, layer, 0, 0)),
                    pl.BlockSpec((1, 1) + cache_v.shape[2:], lambda i, j: (i, layer, 0, 0))]
        args = [sink, pb, pb, pb, pb, cache_k, cache_v]
        out_spec = pl.BlockSpec((1, Q_BLK, W_BR), lambda i, j: (i, j, 0))
        sem = ("parallel", "parallel")
    else:
        grid = (b,)
        in_specs = [smem, pl.BlockSpec((1, t, WB), lambda i: (i, 0, 0))]
        args = [sink, pb]
        out_spec = pl.BlockSpec((1, t, W_BR), lambda i: (i, 0, 0))
        sem = ("parallel",)
    return pl.pallas_call(
        functools.partial(_win_kernel, latent=latent, layer=layer),
        grid=grid,
        in_specs=in_specs,
        out_specs=out_spec,
        out_shape=jax.ShapeDtypeStruct((b, t, W_BR), F32),
        compiler_params=_params(sem),
        name="win",
    )(*args)


def _diff_kernel(*refs, latent, lam_init, layer):
    if latent:
        lam_ref, g_ref, q_ref, k_ref, v_ref, ck_ref, cv_ref, o_ref = refs
    else:
        lam_ref, g_ref, q_ref, k_ref, v_ref, o_ref = refs
    lp = lam_ref[layer]
    lam = (jnp.exp(jnp.sum(lp[0:1] * lp[1:2], axis=-1, keepdims=True))
           - jnp.exp(jnp.sum(lp[2:3] * lp[3:4], axis=-1, keepdims=True)) + lam_init)
    q = _bf(q_ref[0] * (DQ_D ** -0.5 * LOG2_E))
    key_sets = [(_bf(k_ref[0]), _bf(v_ref[0]), False)]
    if latent:
        key_sets.append((_bf(ck_ref[0, 0]), _bf(cv_ref[0, 0]), True))
    lane = lax.broadcasted_iota(jnp.int32, (1, W_BR), 1)
    out = jnp.zeros(q.shape, F32)
    for h in range(W_BR // HEAD):
        probs = []
        for m in range(2):
            qm = jnp.where(lane // DQ_D == 2 * h + m, q, jnp.zeros((), BF16))
            logits = [_dotf(qm, kk) if fm else _dotf(qm, kk, _NT) for kk, _, fm in key_sets]
            mx = functools.reduce(jnp.maximum, [jnp.max(s, axis=-1, keepdims=True) for s in logits])
            es = [jnp.exp2(s - mx) for s in logits]
            den = functools.reduce(lambda a, b: a + b, [jnp.sum(e, axis=-1, keepdims=True) for e in es])
            inv = 1.0 / den
            probs.append([e * inv for e in es])
        o = None
        for i, (_, vv, fm) in enumerate(key_sets):
            pi = _bf(probs[0][i] - lam * probs[1][i])
            oi = _dotf(pi, vv, _NT) if fm else _dotf(pi, vv)
            o = oi if o is None else o + oi
        out = jnp.where(lane // HEAD == h, o, out)
    ms = _mm2(out * out, _head_ones(W_BR)) * (1.0 / HEAD)
    gain = jnp.concatenate([g_ref[layer:layer + 1]] * (W_BR // HEAD), axis=1)
    o_ref[0] = out * lax.rsqrt(ms + NORM_EPS) * gain * (1.0 - lam_init)


def _diff(pd, lam_p, subln_g, cache_k, cache_v, layer, lam_init):
    b, t, _ = pd.shape
    latent = cache_k is not None
    tq = min(t, 512)
    in_specs = [pl.BlockSpec(lam_p.shape, lambda i, j: (0, 0, 0)),
                pl.BlockSpec(subln_g.shape, lambda i, j: (0, 0)),
                pl.BlockSpec((1, tq, 256), lambda i, j: (i, j, 0)),
                pl.BlockSpec((1, t, 256), lambda i, j: (i, 0, 1)),
                pl.BlockSpec((1, t, 256), lambda i, j: (i, 0, 2))]
    args = [lam_p, subln_g, pd, pd, pd]
    if latent:
        in_specs += [pl.BlockSpec((1, 1) + cache_k.shape[2:], lambda i, j: (i, layer, 0, 0)),
                     pl.BlockSpec((1, 1) + cache_v.shape[2:], lambda i, j: (i, layer, 0, 0))]
        args += [cache_k, cache_v]
    return pl.pallas_call(
        functools.partial(_diff_kernel, latent=latent, lam_init=lam_init, layer=layer),
        grid=(b, t // tq),
        in_specs=in_specs,
        out_specs=pl.BlockSpec((1, tq, W_BR), lambda i, j: (i, j, 0)),
        out_shape=jax.ShapeDtypeStruct((b, t, W_BR), F32),
        compiler_params=_params(("parallel", "parallel")),
        name="diff",
    )(*args)


def _out_kernel(x_ref, mod_ref, gpost_ref, w_ref, pa_ref, yf_ref, yb_ref, gng_ref, gnb_ref, rk_ref,
                ywin_ref, bg_ref, ylru_ref, cg_ref, ydiff_ref, dg_ref, o_ref, *, layer):
    lrow = slice(layer, layer + 1)
    ones = _head_ones(W_BR)
    rows = lambda ref: ref[...].reshape(-1, ref.shape[-1])
    pa = rows(pa_ref)
    r, k, v, ag = pa[:, 0:256], pa[:, 256:512], pa[:, 512:768], pa[:, 896:1152]
    y = rows(yf_ref) + rows(yb_ref)
    mu = _mm2(y, ones) * (1.0 / HEAD)
    yc = y - mu
    var = _mm2(yc * yc, ones) * (1.0 / HEAD)
    ya = yc * lax.rsqrt(var + GN_EPS) * gng_ref[lrow] + gnb_ref[lrow]
    ya = ya + _mm2(r * k * rk_ref[lrow], ones) * v
    mix = jnp.concatenate([ya * _silu(ag), rows(ywin_ref) * _silu(rows(bg_ref)),
                           rows(ylru_ref) * _silu(rows(cg_ref)), rows(ydiff_ref) * _silu(rows(dg_ref))], axis=-1)
    o = jnp.dot(mix.astype(BF16), w_ref[0], preferred_element_type=F32)
    o = o * lax.rsqrt(jnp.mean(o * o, -1, keepdims=True) + NORM_EPS) * gpost_ref[lrow]
    o_ref[...] = (rows(x_ref) + mod_ref[0][:, 2 * D_MODEL:] * o).reshape(o_ref.shape)


def _out(x, mod_l, g_post, w_out_bf, layer, latent, pa, yf, yb, gn_g, gn_b, r_k, ywin, pb, ylru, pc, ydiff, pd):
    b, t, _ = x.shape
    tm = min(t, ROW_TILE)
    n_seq = 1 if latent else ROW_TILE // tm
    mod_idx = (lambda i, j: (1 + i, 0, 0)) if latent else (lambda i, j: (0, 0, 0))
    rows = lambda w, c=0: pl.BlockSpec((n_seq, tm, w), lambda i, j: (i, j, c))
    vec = pl.BlockSpec(gn_g.shape, lambda i, j: (0, 0))
    in_specs = [rows(D_MODEL),
                pl.BlockSpec((1, 1, 3 * D_MODEL), mod_idx),
                pl.BlockSpec(g_post.shape, lambda i, j: (0, 0)),
                pl.BlockSpec((1, D_MODEL, D_MODEL), lambda i, j: (layer, 0, 0)),
                rows(WA), rows(W_BR), rows(W_BR), vec, vec, vec,
                rows(W_BR), rows(W_BR, 2), rows(W_BR), rows(W_BR, 1), rows(W_BR), rows(W_BR, 3)]
    return pl.pallas_call(
        functools.partial(_out_kernel, layer=layer),
        grid=(b // n_seq, t // tm),
        in_specs=in_specs,
        out_specs=rows(D_MODEL),
        out_shape=jax.ShapeDtypeStruct((b, t, D_MODEL), F32),
        compiler_params=_params(("parallel", "parallel")),
        name="out",
    )(x, mod_l, g_post, w_out_bf, pa, yf, yb, gn_g, gn_b, r_k, ywin, pb, ylru, pc, ydiff, pd)


def _layer(x, mod_l, layer, lam_init, wts, cache, tables):
    latent = cache is not None
    pa, pb, pc, pd, *ctx_t = _project(x, mod_l, wts['g_pre'], wts['w_in_bf'], layer, tables)
    rw = _rwkv(pa, wts['rwkv_w0'], wts['rwkv_w_up'], wts['rwkv_a0'], wts['rwkv_a_up'], wts['rwkv_k_k'],
               wts['rwkv_k_a'], cache['rwkv'] if latent else None, layer)
    lr = _lru(pc, wts['lru_conv_w'], wts['lru_conv_b'], wts['lru_wa'], wts['lru_ba'], wts['lru_wx'],
              wts['lru_bx'], wts['lru_lambda'], cache['lru'] if latent else None, layer)
    ywin = _win(pb, wts['win_sink'], cache['win_k'] if latent else None,
                cache['win_v'] if latent else None, layer)
    ydiff = _diff(pd, wts['diff_lambda'], wts['diff_subln_g'],
                  cache['diff_k'] if latent else None, cache['diff_v'] if latent else None, layer, lam_init)
    y = _out(x, mod_l, wts['g_post'], wts['w_out_bf'], layer, latent, pa, rw[0], rw[1],
             wts['rwkv_gn_g'], wts['rwkv_gn_b'], wts['rwkv_r_k'], ywin, pb, lr[0], pc, ydiff, pd)
    new_cache = None if latent else (ctx_t, rw[2], lr[1])
    return y, new_cache


def kernel(x_prompt, x_sample, c, cache_win_k, cache_win_v, cache_diff_k, cache_diff_v, state_rwkv, state_lru,
           c_ctx, w_mod, b_mod, g_pre, g_post, w_in, w_out,
           rwkv_w0, rwkv_w_up, rwkv_a0, rwkv_a_up, rwkv_k_k, rwkv_k_a, rwkv_r_k, rwkv_gn_g, rwkv_gn_b,
           win_sink, lru_conv_w, lru_conv_b, lru_wa, lru_ba, lru_wx, lru_bx, lru_lambda,
           diff_lambda, diff_subln_g):
    n_b, seq = x_prompt.shape[:2]
    n_dec, dec_seq = x_sample.shape[:2]
    past = cache_win_k.shape[2]
    wts = dict(g_pre=g_pre, g_post=g_post, w_in_bf=w_in.astype(BF16), w_out_bf=w_out.astype(BF16),
               rwkv_w0=rwkv_w0, rwkv_w_up=rwkv_w_up, rwkv_a0=rwkv_a0, rwkv_a_up=rwkv_a_up,
               rwkv_k_k=rwkv_k_k, rwkv_k_a=rwkv_k_a, rwkv_r_k=rwkv_r_k.reshape(DEPTH, W_BR), rwkv_gn_g=rwkv_gn_g,
               rwkv_gn_b=rwkv_gn_b, win_sink=win_sink, lru_conv_w=lru_conv_w, lru_conv_b=lru_conv_b,
               lru_wa=lru_wa, lru_ba=lru_ba, lru_wx=lru_wx, lru_bx=lru_bx, lru_lambda=lru_lambda,
               diff_lambda=diff_lambda, diff_subln_g=diff_subln_g)
    fm = lambda a: jnp.moveaxis(a, 2, -1).reshape(n_dec, DEPTH, -1, past)
    cache = dict(win_k=fm(cache_win_k), win_v=fm(cache_win_v), diff_k=fm(cache_diff_k), diff_v=fm(cache_diff_v),
                 rwkv=state_rwkv, lru=state_lru)
    cvec = jnp.concatenate([c_ctx[None], c, jnp.zeros((8 - 1 - n_dec, D_MODEL), F32)], axis=0)
    mod = _modulation(cvec, w_mod, b_mod)
    cos_b, sin_b = _rope_tables(dec_seq, HEAD, 384)
    cos_d, sin_d = _rope_tables(dec_seq, DQ_D, 512)
    tables = (cos_b, sin_b, cos_d, sin_d)

    y_p, y_s = x_prompt, x_sample
    ctx = []
    for l in range(DEPTH):
        lam_init = 0.8 - 0.6 * math.exp(-0.3 * l)
        mod_l = mod[l].reshape(8, 1, 3 * D_MODEL)
        y_p, nc = _layer(y_p, mod_l, l, lam_init, wts, None, None)
        ctx.append(nc)
        y_s, _ = _layer(y_s, mod_l, l, lam_init, wts, cache, tables)
    stack = lambda f: jnp.stack([f(ct) for ct in ctx], axis=1)
    tm_ = lambda i, dims: jnp.moveaxis(stack(lambda ct: ct[0][i]).reshape((n_b, DEPTH) + dims + (seq,)), -1, 2)
    new_win_k = tm_(0, (2, HEAD))
    new_win_v = tm_(1, (2, HEAD))
    new_diff_k = tm_(2, (4, 2, DQ_D))
    new_diff_v = tm_(3, (4, HEAD))
    new_state_rwkv = stack(lambda ct: ct[1])
    new_state_lru = stack(lambda ct: ct[2])
    return (y_p, y_s, new_win_k, new_win_v, new_diff_k, new_diff_v, new_state_rwkv, new_state_lru)
```

```python
import functools
import math

import numpy as np
import jax
import jax.numpy as jnp
from jax import lax
from jax.experimental import pallas as pl
from jax.experimental.pallas import tpu as pltpu

F32 = jnp.float32
BF16 = jnp.bfloat16

D_MODEL = 1024
DEPTH = 2
GRID_W = 64
ROPE_BASE = 10000.0
NORM_EPS = 1e-6
NEG_INF = -1e30
LOG2_E = math.log2(math.e)
GN_EPS = 64e-5
C_RG = 8.0
W_BR = 256
HEAD = 64
SUBLANES = 8
DQ_D = 32
Q_BLK = 128
P_TOTAL = 3456
WA, WB, WC, WD = 1152, 768, 512, 1024
CHUNK = 64
RWKV_BATCH = 4
ROW_TILE = 512
VMEM_LIMIT = 58 * 1024 * 1024


_NN = (((1,), (0,)), ((), ()))
_NT = (((1,), (1,)), ((), ()))


def _bf(x):
    return x.astype(BF16)


def _dotf(x, y, dims=_NN):
    return lax.dot_general(x, y, dims, preferred_element_type=F32)


def _split(x):
    hi = x.astype(BF16)
    return hi, (x - hi.astype(F32)).astype(BF16)


def _mm3(a, b, dims=_NN):
    return _dotf(a[0], b[0], dims) + (_dotf(a[0], b[1], dims) + _dotf(a[1], b[0], dims))


def _mm2(x, m01, left=False):
    hi, lo = _split(x)
    if left:
        return _dotf(m01, hi) + _dotf(m01, lo)
    return _dotf(hi, m01) + _dotf(lo, m01)


def _sigmoid(x):
    return 1.0 / (1.0 + jnp.exp(-x))


def _silu(x):
    return x * _sigmoid(x)


def _softplus(x):
    return jnp.maximum(x, 0.0) + jnp.log1p(jnp.exp(-jnp.abs(x)))


def _head_ones(n):
    r = lax.broadcasted_iota(jnp.int32, (n, n), 0) // HEAD
    c = lax.broadcasted_iota(jnp.int32, (n, n), 1) // HEAD
    return jnp.where(r == c, 1.0, 0.0).astype(BF16)


def _params(sem):
    return pltpu.CompilerParams(dimension_semantics=sem, vmem_limit_bytes=VMEM_LIMIT)


def _mod_kernel(c_ref, w_ref, b_ref, o_ref):
    o_ref[0] = _mm3(_split(_silu(c_ref[...])), _split(w_ref[0])) + b_ref[0]


def _modulation(cvec, w_mod, b_mod):
    n_l = w_mod.shape[0]
    tn = 512
    return pl.pallas_call(
        _mod_kernel,
        grid=(n_l, 3 * D_MODEL // tn),
        in_specs=[pl.BlockSpec((8, D_MODEL), lambda l, j: (0, 0)),
                  pl.BlockSpec((1, D_MODEL, tn), lambda l, j: (l, 0, j)),
                  pl.BlockSpec((1, 1, tn), lambda l, j: (l, 0, j))],
        out_specs=pl.BlockSpec((1, 8, tn), lambda l, j: (l, 0, j)),
        out_shape=jax.ShapeDtypeStruct((n_l, 8, 3 * D_MODEL), F32),
        compiler_params=_params(("parallel", "parallel")),
        name="mod",
    )(cvec, w_mod, b_mod.reshape(n_l, 1, 3 * D_MODEL))


def _rope(x, cos, sin_signed, off):
    w = x.shape[-1]
    lane = lax.broadcasted_iota(jnp.int32, x.shape, 1)
    first = (lane % (2 * off)) < off
    partner = jnp.where(first, pltpu.roll(x, w - off, 1), pltpu.roll(x, off, 1))
    return x * cos + partner * sin_signed


def _proj_kernel(*refs, latent, layer):
    if latent:
        x_ref, mod_ref, g_ref, w_ref, cb_ref, sb_ref, cd_ref, sd_ref, oa, ob, oc, od = refs
    else:
        x_ref, mod_ref, g_ref, w_ref, oa, ob, oc, od, o_wk, o_wv, o_dk, o_dv = refs
    x = x_ref[0]
    y = x * lax.rsqrt(jnp.mean(x * x, -1, keepdims=True) + NORM_EPS) * g_ref[layer:layer + 1]
    m = mod_ref[0]
    h = y * (1.0 + m[:, D_MODEL:2 * D_MODEL]) + m[:, :D_MODEL]
    p = jnp.dot(h.astype(BF16), w_ref[0], preferred_element_type=F32)
    oa[0] = p[:, :WA]
    pb = p[:, WA:WA + WB]
    pd = p[:, WA + WB + WC:]
    if latent:
        ob[0, :, :384] = _rope(pb[:, :384], cb_ref[...], sb_ref[...], 16)
        ob[0, :, 384:] = pb[:, 384:]
        od[0, :, :512] = _rope(pd[:, :512], cd_ref[...], sd_ref[...], 8)
        od[0, :, 512:] = pd[:, 512:]
    else:
        ob[0] = pb
        od[0] = pd
        o_wk[0] = pb[:, 256:384].T
        o_wv[0] = pb[:, 384:512].T
        o_dk[0] = pd[:, 256:512].T
        o_dv[0] = pd[:, 512:768].T
    oc[0] = p[:, WA + WB:WA + WB + WC]


def _project(x, mod_l, g_pre, w_in_bf, layer, tables):
    b, t, _ = x.shape
    latent = tables is not None
    tm = min(t, ROW_TILE)
    mod_idx = (lambda i, j: (1 + i, 0, 0)) if latent else (lambda i, j: (0, 0, 0))
    in_specs = [pl.BlockSpec((1, tm, D_MODEL), lambda i, j: (i, j, 0)),
                pl.BlockSpec((1, 1, 3 * D_MODEL), mod_idx),
                pl.BlockSpec(g_pre.shape, lambda i, j: (0, 0)),
                pl.BlockSpec((1, D_MODEL, P_TOTAL), lambda i, j: (layer, 0, 0))]
    args = [x, mod_l, g_pre, w_in_bf]
    if latent:
        for tab in tables:
            in_specs.append(pl.BlockSpec((tm, tab.shape[1]), lambda i, j: (j, 0)))
            args.append(tab)
    widths = (WA, WB, WC, WD)
    out_specs = [pl.BlockSpec((1, tm, w), lambda i, j: (i, j, 0)) for w in widths]
    out_shape = [jax.ShapeDtypeStruct((b, t, w), F32) for w in widths]
    if not latent:
        for w in (128, 128, W_BR, W_BR):
            out_specs.append(pl.BlockSpec((1, w, tm), lambda i, j: (i, 0, j)))
            out_shape.append(jax.ShapeDtypeStruct((b, w, t), F32))
    return pl.pallas_call(
        functools.partial(_proj_kernel, latent=latent, layer=layer),
        grid=(b, t // tm),
        in_specs=in_specs,
        out_specs=out_specs,
        out_shape=out_shape,
        compiler_params=_params(("parallel", "parallel")),
        name="proj",
    )(*args)


def _rope_tables(t, head_dim, n_lanes):
    half = head_dim // 2
    quarter = half // 2
    pos = np.arange(t)
    row = (pos // GRID_W).astype(np.float32)
    col = (pos % GRID_W).astype(np.float32)
    inv = np.float32(ROPE_BASE) ** (-np.arange(0, half, 2, dtype=np.float32) / np.float32(half))
    lane = np.arange(n_lanes) % head_dim
    in_half = lane % half
    p = np.where((lane < half)[None, :], row[:, None], col[:, None])
    ang = (p * inv[in_half % quarter][None, :]).astype(np.float64)
    sign = np.where(in_half < quarter, -1.0, 1.0)[None, :]
    return jnp.asarray(np.cos(ang), F32), jnp.asarray(np.sin(ang) * sign, F32)


_PREP_FIELDS = (('ar', 2, 1, BF16), ('nm', 1, 1, BF16), ('mm', 1, 1, BF16), ('pq', 1, 2, BF16),
                ('v', 1, 1, BF16), ('bkt', 1, 2, BF16), ('gct', 1, 1, F32))


def _prep_scratch(slots, n_streams):
    return [pltpu.VMEM((slots, n_streams, rows * CHUNK, lanes * W_BR), dt) for _, rows, lanes, dt in _PREP_FIELDS]


class _SlotView:
    def __init__(self, ref, slot):
        self.ref, self.slot, self.shape = ref, slot, ref.shape[1:]

    def __getitem__(self, idx):
        return self.ref[(self.slot,) + (idx if isinstance(idx, tuple) else (idx,))]

    def __setitem__(self, idx, val):
        self.ref[(self.slot,) + (idx if isinstance(idx, tuple) else (idx,))] = val


def _head_transpose(x):
    xt = x.T
    return jnp.concatenate([xt[h * HEAD:(h + 1) * HEAD] for h in range(x.shape[1] // HEAD)], axis=1)


def _expand(x, same):
    return jnp.concatenate([x] * (same.shape[0] // x.shape[0]), axis=0) * same


def _rwkv_prepare(xs, dirs, w0_ref, wup_ref, a0_ref, aup_ref, k_k, k_a, m_ref, same_ref, out):
    c = xs[0].shape[0]
    ones = _head_ones(W_BR)
    ti = lax.broadcasted_iota(jnp.int32, (c, c), 0)
    si = lax.broadcasted_iota(jnp.int32, (c, c), 1)
    cums = [jnp.where(si <= ti, 1.0, 0.0).astype(BF16), jnp.where(si >= ti, 1.0, 0.0).astype(BF16)]
    same = same_ref[...]

    st = []
    for xc, d in zip(xs, dirs):
        r, k, v = xc[:, 0:256], xc[:, 256:512], xc[:, 512:768]
        wd, ad = xc[:, 768:832], xc[:, 832:896]
        kk = k * k_k
        kk = kk * lax.rsqrt(_mm2(kk * kk, ones) + 1e-12)
        z = w0_ref[d:d + 1] + _mm3(_split(jnp.tanh(wd)), _split(wup_ref[d]))
        e = jnp.exp(-_softplus(-z) - 0.5)
        a = _sigmoid(a0_ref[d:d + 1] + _mm3(_split(ad), _split(aup_ref[d])))
        st.append(dict(r=r, v=v, e=e, kd=k * (1.0 + (a - 1.0) * k_a), alpha=-kk, beta=kk * a))
    yield
    for s, d in zip(st, dirs):
        l_incl = _mm2(s['e'], cums[d], left=True)
        l_tot = jnp.sum(s['e'], axis=0, keepdims=True)
        grow = jnp.exp(l_incl)
        tail = jnp.exp(l_incl - l_tot)
        s.update(ar=jnp.concatenate([_bf(s['alpha'] * jnp.exp(s['e'] - l_incl)),
                                     _bf(s['r'] * jnp.exp(-l_incl))], axis=0),
                 b_t=_bf(s['beta'] * grow), k_t=_bf(s['kd'] * grow),
                 b_h=s['beta'] * tail, k_h=s['kd'] * tail,
                 g_c=jnp.broadcast_to(jnp.exp(-l_tot), (c, W_BR)))
    yield
    for g, (s, d) in enumerate(zip(st, dirs)):
        out['ar'][g] = s['ar']
        rhs_t = jnp.concatenate([_expand(s['b_t'], same), _expand(s['k_t'], same)], axis=0)
        g4 = _dotf(s['ar'], rhs_t, _NT)
        strict = m_ref[_M_STRICT + 2 * d]
        incl = m_ref[_M_INCL + 2 * d]
        out['nm'][g] = _bf(g4[:c, :W_BR]) * strict
        out['mm'][g] = _bf(g4[:c, W_BR:]) * strict
        out['pq'][g, :, :W_BR] = _bf(g4[c:, :W_BR]) * incl
        out['pq'][g, :, W_BR:] = _bf(g4[c:, W_BR:]) * incl
    yield
    for g, s in enumerate(st):
        out['v'][g] = _bf(s['v'])
        out['bkt'][g] = jnp.concatenate([_bf(_head_transpose(s['b_h'])), _bf(_head_transpose(s['k_h']))], axis=1)
        out['gct'][g] = _head_transpose(s['g_c'])


def _rwkv_advance(p, m_ref, same_ref, z_ref, y_stores):
    n_streams, c = p['nm'].shape[0], p['nm'].shape[1]
    gs = range(n_streams)
    same = same_ref[...]
    ex = lambda x: _expand(_bf(x), same)
    invs = [(p['nm'][g] * m_ref[_M_PAIR]).astype(F32) + m_ref[_M_EYE].astype(F32) for g in gs]
    for lvl in range(int(math.log2(c)) - 1):
        inv_x = [ex(t) for t in invs]
        half = [_dotf(p['nm'][g] * m_ref[_M_OFF + lvl], inv_x[g]) for g in gs]
        invs = [invs[g] + _dotf(_bf(invs[g]), ex(half[g])) for g in gs]
        yield
    z0 = [z_ref[g] for g in gs]
    as0 = [_dotf(p['ar'][g], ex(z0[g])) for g in gs]
    vx = [_expand(p['v'][g], same) for g in gs]
    rhs = [as0[g][:c] + _dotf(p['mm'][g], vx[g]) for g in gs]
    yield
    uv = [jnp.concatenate([ex(_dotf(_bf(invs[g]), ex(rhs[g]))), vx[g]], axis=0) for g in gs]
    yield
    for g in gs:
        z_ref[g] = z0[g] * p['gct'][g] + _dotf(p['bkt'][g], uv[g])
    yield
    for g in gs:
        y_stores[g](as0[g][c:] + _dotf(p['pq'][g], uv[g]))


def _interleave(*gens):
    live = list(gens)
    while live:
        for gen in list(live):
            try:
                next(gen)
            except StopIteration:
                live.remove(gen)


_M_STRICT, _M_INCL, _M_EYE, _M_PAIR, _M_OFF = 0, 1, 4, 5, 6


def _rwkv_masks(c, n_h):
    t = np.arange(c)[:, None]
    s = np.arange(c)[None, :]
    masks = [s < t, s <= t, s > t, s >= t, s == t, ((t // 2) == (s // 2)) & (t != s)]
    b = 2
    while b < c:
        masks.append(((t // (2 * b)) == (s // (2 * b))) & ((t // b) != (s // b)))
        b *= 2
    masks = np.tile(np.stack(masks).astype(np.float32), (1, 1, n_h))
    r = np.arange(n_h * c)[:, None] // c
    l = np.arange(n_h * HEAD)[None, :] // HEAD
    return jnp.asarray(masks, BF16), jnp.asarray((r == l).astype(np.float32), BF16)


def _rwkv_kernel(*refs, latent, n_b, layer):
    (xf_ref, xb_ref, xfn_ref, xbn_ref, w0_ref, wup_ref, a0_ref, aup_ref, kk_ref, ka_ref,
     m_ref, same_ref) = refs[:12]
    if latent:
        s0_ref, yf_ref, yb_ref = refs[12:15]
    else:
        yf_ref, yb_ref, st_ref = refs[12:15]
    scr = refs[15:]
    z_scr = scr[0]
    names = [f[0] for f in _PREP_FIELDS]
    ci = pl.program_id(1)
    step = pl.program_id(0) * pl.num_programs(1) + ci
    view = lambda refs_, slot: {k: _SlotView(r, slot) for k, r in zip(names, refs_)}
    cur = view(scr[1:1 + len(names)], step % 2)
    nxt = view(scr[1:1 + len(names)], 1 - step % 2)
    mid = view(scr[1 + len(names):], 0)
    n_h = W_BR // HEAD
    c = CHUNK
    head = lambda h: slice(h * HEAD, (h + 1) * HEAD)
    streams = [(bi, d) for bi in range(n_b) for d in range(2)]
    dirs = [d for _, d in streams]
    params = (w0_ref[layer], wup_ref[0], a0_ref[layer], aup_ref[0], kk_ref[layer:layer + 1],
              ka_ref[layer:layer + 1], m_ref, same_ref)
    first = lambda d: slice(0, c) if d == 0 else slice(c, 2 * c)
    second = lambda d: slice(c, 2 * c) if d == 0 else slice(0, c)
    x_refs, xn_refs, y_refs = (xf_ref, xb_ref), (xfn_ref, xbn_ref), (yf_ref, yb_ref)

    def y_store(bi, d, rows):
        def store(y):
            y_refs[d][bi, rows, :] = y
        return store

    @pl.when(ci == 0)
    def _():
        for g, (bi, d) in enumerate(streams):
            if latent:
                z_scr[g] = jnp.concatenate([s0_ref[bi, 0, d, h].T for h in range(n_h)], axis=1)
            else:
                z_scr[g] = jnp.zeros(z_scr.shape[1:], F32)

    @pl.when(step == 0)
    def _():
        _interleave(_rwkv_prepare([x_refs[d][bi, first(d), :] for bi, d in streams], dirs, *params, cur))

    _interleave(_rwkv_advance(cur, m_ref, same_ref, z_scr, [y_store(bi, d, first(d)) for bi, d in streams]),
                _rwkv_prepare([x_refs[d][bi, second(d), :] for bi, d in streams], dirs, *params, mid))
    _interleave(_rwkv_advance(mid, m_ref, same_ref, z_scr, [y_store(bi, d, second(d)) for bi, d in streams]),
                _rwkv_prepare([xn_refs[d][bi, first(d), :] for bi, d in streams], dirs, *params, nxt))

    if not latent:
        @pl.when(ci == pl.num_programs(1) - 1)
        def _():
            for g, (bi, d) in enumerate(streams):
                z = z_scr[g]
                for h in range(n_h):
                    st_ref[bi, d, h] = z[:, head(h)].T


def _rwkv(pa, w0, wup, a0, aup, k_k, k_a, state, layer):
    b, t, _ = pa.shape
    latent = state is not None
    ns = t // (2 * CHUNK)
    n_h = W_BR // HEAD
    n_b = RWKV_BATCH
    masks, same = _rwkv_masks(CHUNK, n_h)
    full = lambda shape: pl.BlockSpec(shape, lambda i, j: (0,) * len(shape))
    one_layer = lambda shape: pl.BlockSpec((1,) + shape[1:], lambda i, j: (layer,) + (0,) * (len(shape) - 1))
    rows = lambda w, idx: pl.BlockSpec((n_b, 2 * CHUNK, w), idx)
    nxt_group = lambda i, j: jnp.minimum(i + (j + 1) // ns, b // n_b - 1)
    in_specs = [rows(WA, lambda i, j: (i, j, 0)),
                rows(WA, lambda i, j: (i, ns - 1 - j, 0)),
                rows(WA, lambda i, j: (nxt_group(i, j), (j + 1) % ns, 0)),
                rows(WA, lambda i, j: (nxt_group(i, j), (2 * ns - 2 - j) % ns, 0)),
                full(w0.shape), one_layer(wup.shape), full(a0.shape), one_layer(aup.shape),
                full(k_k.shape), full(k_a.shape), full(masks.shape), full(same.shape)]
    args = [pa, pa, pa, pa, w0, wup, a0, aup, k_k, k_a, masks, same]
    out_specs = [rows(W_BR, lambda i, j: (i, j, 0)), rows(W_BR, lambda i, j: (i, ns - 1 - j, 0))]
    out_shape = [jax.ShapeDtypeStruct((b, t, W_BR), F32)] * 2
    if latent:
        in_specs.append(pl.BlockSpec((n_b, 1, 2, n_h, HEAD, HEAD), lambda i, j: (i, layer, 0, 0, 0, 0)))
        args.append(state)
    else:
        out_specs.append(pl.BlockSpec((n_b, 2, n_h, HEAD, HEAD), lambda i, j: (i, 0, 0, 0, 0)))
        out_shape.append(jax.ShapeDtypeStruct((b, 2, n_h, HEAD, HEAD), F32))
    return pl.pallas_call(
        functools.partial(_rwkv_kernel, latent=latent, n_b=n_b, layer=layer),
        grid=(b // n_b, ns),
        in_specs=in_specs,
        out_specs=out_specs,
        out_shape=out_shape,
        scratch_shapes=([pltpu.VMEM((2 * n_b, CHUNK, W_BR), F32)]
                        + _prep_scratch(2, 2 * n_b) + _prep_scratch(1, 2 * n_b)),
        compiler_params=_params(("arbitrary", "arbitrary")),
        name="rwkv",
    )(*args)


def _lru_kernel(*refs, latent, layer):
    if latent:
        (x_ref, cw_ref, cb_ref, wa_ref, ba_ref, wx_ref, bx_ref, lam_ref, h0_ref, y_ref,
         a_scr, h_scr) = refs
    else:
        (x_ref, cw_ref, cb_ref, wa_ref, ba_ref, wx_ref, bx_ref, lam_ref, y_ref, st_ref,
         a_scr, h_scr) = refs
    x = x_ref[0, :, :W_BR]
    t = x.shape[0]
    row = lax.broadcasted_iota(jnp.int32, x.shape, 0)

    def shift_dn(z, k, fill):
        return jnp.where(row >= k, pltpu.roll(z, k, 0), fill)

    def shift_up(z, k, fill):
        return jnp.where(row < t - k, pltpu.roll(z, t - k, 0), fill)

    cw = cw_ref[layer]
    xc = (cb_ref[layer:layer + 1] + shift_dn(x, 2, 0.0) * cw[0:1] + shift_dn(x, 1, 0.0) * cw[1:2]
          + x * cw[2:3] + shift_up(x, 1, 0.0) * cw[3:4])
    xs = _split(xc)
    sub = row % SUBLANES
    n_blk = W_BR // HEAD
    zero = jnp.zeros((HEAD, HEAD), F32)

    def block_diag(w_ref, d):
        return jnp.concatenate([jnp.concatenate([w_ref[0, d, n] if m == n else zero for m in range(n_blk)], axis=1)
                                for n in range(n_blk)], axis=0)

    for d in range(2):
        gate_a = _sigmoid(_mm3(xs, _split(block_diag(wa_ref, d))) + ba_ref[layer, d:d + 1])
        gate_x = _sigmoid(_mm3(xs, _split(block_diag(wx_ref, d))) + bx_ref[layer, d:d + 1])
        log_a = -C_RG * gate_a * _softplus(-lam_ref[layer, d:d + 1])
        a = jnp.exp(log_a)
        u = jnp.sqrt(-jnp.tanh(log_a) * (a * a + 1.0)) * (gate_x * xc)
        k = 1
        while k < SUBLANES:
            if d == 0:
                keep = sub >= k
                sh = lambda z: pltpu.roll(z, k, 0)
            else:
                keep = sub < SUBLANES - k
                sh = lambda z: pltpu.roll(z, t - k, 0)
            u = a * jnp.where(keep, sh(u), 0.0) + u
            a = a * jnp.where(keep, sh(a), 1.0)
            k *= 2
        a_scr[d] = a
        h_scr[d] = u

    n_grp = t // SUBLANES
    if latent:
        carry0 = (h0_ref[0, 0, 0:1], h0_ref[0, 0, 1:2])
    else:
        carry0 = (jnp.zeros((1, W_BR), F32),) * 2

    def chain(i, carry):
        cf, cb = carry
        rf = pl.ds(pl.multiple_of(i * SUBLANES, SUBLANES), SUBLANES)
        rb = pl.ds(pl.multiple_of((n_grp - 1 - i) * SUBLANES, SUBLANES), SUBLANES)
        hf = h_scr[0, rf, :] + a_scr[0, rf, :] * cf
        hb = h_scr[1, rb, :] + a_scr[1, rb, :] * cb
        h_scr[0, rf, :] = hf
        h_scr[1, rb, :] = hb
        return hf[SUBLANES - 1:SUBLANES], hb[0:1]

    cf, cb = lax.fori_loop(0, n_grp, chain, carry0, unroll=4)
    y_ref[0] = h_scr[0] + h_scr[1]
    if not latent:
        st_ref[0] = jnp.concatenate([cf, cb], axis=0)


def _lru(pc, cw, cb, wa, ba, wx, bx, lam, state, layer):
    b, t, _ = pc.shape
    latent = state is not None
    full = lambda shape: pl.BlockSpec(shape, lambda i: (0,) * len(shape))
    one_layer = lambda shape: pl.BlockSpec((1,) + shape[1:], lambda i: (layer,) + (0,) * (len(shape) - 1))
    in_specs = [pl.BlockSpec((1, t, WC), lambda i: (i, 0, 0)),
                full(cw.shape), full(cb.shape), one_layer(wa.shape), full(ba.shape),
                one_layer(wx.shape), full(bx.shape), full(lam.shape)]
    args = [pc, cw, cb, wa, ba, wx, bx, lam]
    out_specs = [pl.BlockSpec((1, t, W_BR), lambda i: (i, 0, 0))]
    out_shape = [jax.ShapeDtypeStruct((b, t, W_BR), F32)]
    if latent:
        in_specs.append(pl.BlockSpec((1, 1, 2, W_BR), lambda i: (i, layer, 0, 0)))
        args.append(state)
    else:
        out_specs.append(pl.BlockSpec((1, 2, W_BR), lambda i: (i, 0, 0)))
        out_shape.append(jax.ShapeDtypeStruct((b, 2, W_BR), F32))
    return pl.pallas_call(
        functools.partial(_lru_kernel, latent=latent, layer=layer),
        grid=(b,),
        in_specs=in_specs,
        out_specs=out_specs,
        out_shape=out_shape,
        scratch_shapes=[pltpu.VMEM((2, t, W_BR), F32), pltpu.VMEM((2, t, W_BR), F32)],
        compiler_params=_params(("parallel",)),
        name="lru",
    )(*args)


def _win_kernel(*refs, latent, layer):
    if latent:
        sink_ref, q_ref, kp_ref, kc_ref, kn_ref, ck_ref, cv_ref, o_ref = refs
        j = pl.program_id(1)
        nq = pl.num_programs(1)
        qi = lax.broadcasted_iota(jnp.int32, (Q_BLK, Q_BLK), 0)
        ki = lax.broadcasted_iota(jnp.int32, (Q_BLK, Q_BLK), 1)
        q = q_ref[0]
        ctx_k, ctx_v = _bf(ck_ref[0, 0]), _bf(cv_ref[0, 0])
        pieces = [(_bf(r[0, :, :128]), _bf(r[0, :, 128:]), m, False) for r, m in
                  ((kp_ref, (ki >= qi) & (j > 0)), (kc_ref, None), (kn_ref, (ki <= qi) & (j < nq - 1)))]
        pieces += [(ctx_k[:, i:i + Q_BLK], ctx_v[:, i:i + Q_BLK], None, True)
                   for i in range(0, ctx_k.shape[1], Q_BLK)]
    else:
        sink_ref, x_ref, o_ref = refs
        q = x_ref[0, :, :256]
        k, v = _bf(x_ref[0, :, 256:384]), _bf(x_ref[0, :, 384:512])
        pieces = [(k[i:i + Q_BLK], v[i:i + Q_BLK], None, False) for i in range(0, k.shape[0], Q_BLK)]
    q = _bf(q * (HEAD ** -0.5 * LOG2_E))
    tq = q.shape[0]
    first_g = lax.broadcasted_iota(jnp.int32, (2 * tq, 1), 0) < tq
    outs = []
    for h in range(2):
        hs = slice(h * HEAD, (h + 1) * HEAD)
        qg = jnp.concatenate([q[:, (2 * h + g) * HEAD:(2 * h + g + 1) * HEAD] for g in range(2)], axis=0)
        sink = jnp.where(first_g, sink_ref[layer, 2 * h], sink_ref[layer, 2 * h + 1]) * LOG2_E
        logits = []
        for kp, _, mask, fm in pieces:
            s = _dotf(qg, kp[hs]) if fm else _dotf(qg, kp[:, hs], _NT)
            logits.append(s if mask is None else jnp.where(jnp.concatenate([mask, mask], axis=0), s, NEG_INF))
        mx = jnp.maximum(jnp.max(functools.reduce(jnp.maximum, logits), axis=-1, keepdims=True), sink)
        es = [jnp.exp2(s - mx) for s in logits]
        den = jnp.sum(functools.reduce(lambda a, b: a + b, es), axis=-1, keepdims=True) + jnp.exp2(sink - mx)
        acc = functools.reduce(lambda a, b: a + b,
                               [_dotf(_bf(e), vp[hs], _NT) if fm else _dotf(_bf(e), vp[:, hs])
                                for e, (_, vp, _, fm) in zip(es, pieces)])
        o = acc / den
        outs += [o[:tq], o[tq:]]
    o_ref[0] = jnp.concatenate(outs, axis=-1)


def _win(pb, sink, cache_k, cache_v, layer):
    b, t, _ = pb.shape
    latent = cache_k is not None
    smem = pl.BlockSpec(memory_space=pltpu.SMEM)
    if latent:
        nq = t // Q_BLK
        grid = (b, nq)
        in_specs = [smem,
                    pl.BlockSpec((1, Q_BLK, 256), lambda i, j: (i, j, 0)),
                    pl.BlockSpec((1, Q_BLK, 256), lambda i, j: (i, jnp.maximum(j - 1, 0), 1)),
                    pl.BlockSpec((1, Q_BLK, 256), lambda i, j: (i, j, 1)),
                    pl.BlockSpec((1, Q_BLK, 256), lambda i, j: (i, jnp.minimum(j + 1, nq - 1), 1)),
                    pl.BlockSpec((1, 1) + cache_k.shape[2:], lambda i, j: (i, layer, 0, 0)),
                    pl.BlockSpec((1, 1) + cache_v.shape[2:], lambda i, j: (i, layer, 0, 0))]
        args = [sink, pb, pb, pb, pb, cache_k, cache_v]
        out_spec = pl.BlockSpec((1, Q_BLK, W_BR), lambda i, j: (i, j, 0))
        sem = ("parallel", "parallel")
    else:
        grid = (b,)
        in_specs = [smem, pl.BlockSpec((1, t, WB), lambda i: (i, 0, 0))]
        args = [sink, pb]
        out_spec = pl.BlockSpec((1, t, W_BR), lambda i: (i, 0, 0))
        sem = ("parallel",)
    return pl.pallas_call(
        functools.partial(_win_kernel, latent=latent, layer=layer),
        grid=grid,
        in_specs=in_specs,
        out_specs=out_spec,
        out_shape=jax.ShapeDtypeStruct((b, t, W_BR), F32),
        compiler_params=_params(sem),
        name="win",
    )(*args)


def _diff_kernel(*refs, latent, lam_init, layer):
    if latent:
        lam_ref, g_ref, q_ref, k_ref, v_ref, ck_ref, cv_ref, o_ref = refs
    else:
        lam_ref, g_ref, q_ref, k_ref, v_ref, o_ref = refs
    lp = lam_ref[layer]
    lam = (jnp.exp(jnp.sum(lp[0:1] * lp[1:2], axis=-1, keepdims=True))
           - jnp.exp(jnp.sum(lp[2:3] * lp[3:4], axis=-1, keepdims=True)) + lam_init)
    q = _bf(q_ref[0] * (DQ_D ** -0.5 * LOG2_E))
    key_sets = [(_bf(k_ref[0]), _bf(v_ref[0]), False)]
    if latent:
        key_sets.append((_bf(ck_ref[0, 0]), _bf(cv_ref[0, 0]), True))
    lane = lax.broadcasted_iota(jnp.int32, (1, W_BR), 1)
    out = jnp.zeros(q.shape, F32)
    for h in range(W_BR // HEAD):
        probs = []
        for m in range(2):
            qm = jnp.where(lane // DQ_D == 2 * h + m, q, jnp.zeros((), BF16))
            logits = [_dotf(qm, kk) if fm else _dotf(qm, kk, _NT) for kk, _, fm in key_sets]
            mx = functools.reduce(jnp.maximum, [jnp.max(s, axis=-1, keepdims=True) for s in logits])
            es = [jnp.exp2(s - mx) for s in logits]
            den = functools.reduce(lambda a, b: a + b, [jnp.sum(e, axis=-1, keepdims=True) for e in es])
            inv = 1.0 / den
            probs.append([e * inv for e in es])
        o = None
        for i, (_, vv, fm) in enumerate(key_sets):
            pi = _bf(probs[0][i] - lam * probs[1][i])
            oi = _dotf(pi, vv, _NT) if fm else _dotf(pi, vv)
            o = oi if o is None else o + oi
        out = jnp.where(lane // HEAD == h, o, out)
    ms = _mm2(out * out, _head_ones(W_BR)) * (1.0 / HEAD)
    gain = jnp.concatenate([g_ref[layer:layer + 1]] * (W_BR // HEAD), axis=1)
    o_ref[0] = out * lax.rsqrt(ms + NORM_EPS) * gain * (1.0 - lam_init)


def _diff(pd, lam_p, subln_g, cache_k, cache_v, layer, lam_init):
    b, t, _ = pd.shape
    latent = cache_k is not None
    tq = min(t, 512)
    in_specs = [pl.BlockSpec(lam_p.shape, lambda i, j: (0, 0, 0)),
                pl.BlockSpec(subln_g.shape, lambda i, j: (0, 0)),
                pl.BlockSpec((1, tq, 256), lambda i, j: (i, j, 0)),
                pl.BlockSpec((1, t, 256), lambda i, j: (i, 0, 1)),
                pl.BlockSpec((1, t, 256), lambda i, j: (i, 0, 2))]
    args = [lam_p, subln_g, pd, pd, pd]
    if latent:
        in_specs += [pl.BlockSpec((1, 1) + cache_k.shape[2:], lambda i, j: (i, layer, 0, 0)),
                     pl.BlockSpec((1, 1) + cache_v.shape[2:], lambda i, j: (i, layer, 0, 0))]
        args += [cache_k, cache_v]
    return pl.pallas_call(
        functools.partial(_diff_kernel, latent=latent, lam_init=lam_init, layer=layer),
        grid=(b, t // tq),
        in_specs=in_specs,
        out_specs=pl.BlockSpec((1, tq, W_BR), lambda i, j: (i, j, 0)),
        out_shape=jax.ShapeDtypeStruct((b, t, W_BR), F32),
        compiler_params=_params(("parallel", "parallel")),
        name="diff",
    )(*args)


def _out_kernel(x_ref, mod_ref, gpost_ref, w_ref, pa_ref, yf_ref, yb_ref, gng_ref, gnb_ref, rk_ref,
                ywin_ref, bg_ref, ylru_ref, cg_ref, ydiff_ref, dg_ref, o_ref, *, layer):
    lrow = slice(layer, layer + 1)
    ones = _head_ones(W_BR)
    rows = lambda ref: ref[...].reshape(-1, ref.shape[-1])
    pa = rows(pa_ref)
    r, k, v, ag = pa[:, 0:256], pa[:, 256:512], pa[:, 512:768], pa[:, 896:1152]
    y = rows(yf_ref) + rows(yb_ref)
    mu = _mm2(y, ones) * (1.0 / HEAD)
    yc = y - mu
    var = _mm2(yc * yc, ones) * (1.0 / HEAD)
    ya = yc * lax.rsqrt(var + GN_EPS) * gng_ref[lrow] + gnb_ref[lrow]
    ya = ya + _mm2(r * k * rk_ref[lrow], ones) * v
    mix = jnp.concatenate([ya * _silu(ag), rows(ywin_ref) * _silu(rows(bg_ref)),
                           rows(ylru_ref) * _silu(rows(cg_ref)), rows(ydiff_ref) * _silu(rows(dg_ref))], axis=-1)
    o = jnp.dot(mix.astype(BF16), w_ref[0], preferred_element_type=F32)
    o = o * lax.rsqrt(jnp.mean(o * o, -1, keepdims=True) + NORM_EPS) * gpost_ref[lrow]
    o_ref[...] = (rows(x_ref) + mod_ref[0][:, 2 * D_MODEL:] * o).reshape(o_ref.shape)


def _out(x, mod_l, g_post, w_out_bf, layer, latent, pa, yf, yb, gn_g, gn_b, r_k, ywin, pb, ylru, pc, ydiff, pd):
    b, t, _ = x.shape
    tm = min(t, ROW_TILE)
    n_seq = 1 if latent else ROW_TILE // tm
    mod_idx = (lambda i, j: (1 + i, 0, 0)) if latent else (lambda i, j: (0, 0, 0))
    rows = lambda w, c=0: pl.BlockSpec((n_seq, tm, w), lambda i, j: (i, j, c))
    vec = pl.BlockSpec(gn_g.shape, lambda i, j: (0, 0))
    in_specs = [rows(D_MODEL),
                pl.BlockSpec((1, 1, 3 * D_MODEL), mod_idx),
                pl.BlockSpec(g_post.shape, lambda i, j: (0, 0)),
                pl.BlockSpec((1, D_MODEL, D_MODEL), lambda i, j: (layer, 0, 0)),
                rows(WA), rows(W_BR), rows(W_BR), vec, vec, vec,
                rows(W_BR), rows(W_BR, 2), rows(W_BR), rows(W_BR, 1), rows(W_BR), rows(W_BR, 3)]
    return pl.pallas_call(
        functools.partial(_out_kernel, layer=layer),
        grid=(b // n_seq, t // tm),
        in_specs=in_specs,
        out_specs=rows(D_MODEL),
        out_shape=jax.ShapeDtypeStruct((b, t, D_MODEL), F32),
        compiler_params=_params(("parallel", "parallel")),
        name="out",
    )(x, mod_l, g_post, w_out_bf, pa, yf, yb, gn_g, gn_b, r_k, ywin, pb, ylru, pc, ydiff, pd)


def _layer(x, mod_l, layer, lam_init, wts, cache, tables):
    latent = cache is not None
    pa, pb, pc, pd, *ctx_t = _project(x, mod_l, wts['g_pre'], wts['w_in_bf'], layer, tables)
    rw = _rwkv(pa, wts['rwkv_w0'], wts['rwkv_w_up'], wts['rwkv_a0'], wts['rwkv_a_up'], wts['rwkv_k_k'],
               wts['rwkv_k_a'], cache['rwkv'] if latent else None, layer)
    lr = _lru(pc, wts['lru_conv_w'], wts['lru_conv_b'], wts['lru_wa'], wts['lru_ba'], wts['lru_wx'],
              wts['lru_bx'], wts['lru_lambda'], cache['lru'] if latent else None, layer)
    ywin = _win(pb, wts['win_sink'], cache['win_k'] if latent else None,
                cache['win_v'] if latent else None, layer)
    ydiff = _diff(pd, wts['diff_lambda'], wts['diff_subln_g'],
                  cache['diff_k'] if latent else None, cache['diff_v'] if latent else None, layer, lam_init)
    y = _out(x, mod_l, wts['g_post'], wts['w_out_bf'], layer, latent, pa, rw[0], rw[1],
             wts['rwkv_gn_g'], wts['rwkv_gn_b'], wts['rwkv_r_k'], ywin, pb, lr[0], pc, ydiff, pd)
    new_cache = None if latent else (ctx_t, rw[2], lr[1])
    return y, new_cache


def kernel(x_prompt, x_sample, c, cache_win_k, cache_win_v, cache_diff_k, cache_diff_v, state_rwkv, state_lru,
           c_ctx, w_mod, b_mod, g_pre, g_post, w_in, w_out,
           rwkv_w0, rwkv_w_up, rwkv_a0, rwkv_a_up, rwkv_k_k, rwkv_k_a, rwkv_r_k, rwkv_gn_g, rwkv_gn_b,
           win_sink, lru_conv_w, lru_conv_b, lru_wa, lru_ba, lru_wx, lru_bx, lru_lambda,
           diff_lambda, diff_subln_g):
    n_b, seq = x_prompt.shape[:2]
    n_dec, dec_seq = x_sample.shape[:2]
    past = cache_win_k.shape[2]
    wts = dict(g_pre=g_pre, g_post=g_post, w_in_bf=w_in.astype(BF16), w_out_bf=w_out.astype(BF16),
               rwkv_w0=rwkv_w0, rwkv_w_up=rwkv_w_up, rwkv_a0=rwkv_a0, rwkv_a_up=rwkv_a_up,
               rwkv_k_k=rwkv_k_k, rwkv_k_a=rwkv_k_a, rwkv_r_k=rwkv_r_k.reshape(DEPTH, W_BR), rwkv_gn_g=rwkv_gn_g,
               rwkv_gn_b=rwkv_gn_b, win_sink=win_sink, lru_conv_w=lru_conv_w, lru_conv_b=lru_conv_b,
               lru_wa=lru_wa, lru_ba=lru_ba, lru_wx=lru_wx, lru_bx=lru_bx, lru_lambda=lru_lambda,
               diff_lambda=diff_lambda, diff_subln_g=diff_subln_g)
    fm = lambda a: jnp.moveaxis(a, 2, -1).reshape(n_dec, DEPTH, -1, past)
    cache = dict(win_k=fm(cache_win_k), win_v=fm(cache_win_v), diff_k=fm(cache_diff_k), diff_v=fm(cache_diff_v),
                 rwkv=state_rwkv, lru=state_lru)
    cvec = jnp.concatenate([c_ctx[None], c, jnp.zeros((8 - 1 - n_dec, D_MODEL), F32)], axis=0)
    mod = _modulation(cvec, w_mod, b_mod)
    cos_b, sin_b = _rope_tables(dec_seq, HEAD, 384)
    cos_d, sin_d = _rope_tables(dec_seq, DQ_D, 512)
    tables = (cos_b, sin_b, cos_d, sin_d)

    y_p, y_s = x_prompt, x_sample
    ctx = []
    for l in range(DEPTH):
        lam_init = 0.8 - 0.6 * math.exp(-0.3 * l)
        mod_l = mod[l].reshape(8, 1, 3 * D_MODEL)
        y_p, nc = _layer(y_p, mod_l, l, lam_init, wts, None, None)
        ctx.append(nc)
        y_s, _ = _layer(y_s, mod_l, l, lam_init, wts, cache, tables)
    stack = lambda f: jnp.stack([f(ct) for ct in ctx], axis=1)
    tm_ = lambda i, dims: jnp.moveaxis(stack(lambda ct: ct[0][i]).reshape((n_b, DEPTH) + dims + (seq,)), -1, 2)
    new_win_k = tm_(0, (2, HEAD))
    new_win_v = tm_(1, (2, HEAD))
    new_diff_k = tm_(2, (4, 2, DQ_D))
    new_diff_v = tm_(3, (4, HEAD))
    new_state_rwkv = stack(lambda ct: ct[1])
    new_state_lru = stack(lambda ct: ct[2])
    return (y_p, y_s, new_win_k, new_win_v, new_diff_k, new_diff_v, new_state_rwkv, new_state_lru)
```

```python
import functools
import math

import numpy as np
import jax
import jax.numpy as jnp
from jax import lax
from jax.experimental import pallas as pl
from jax.experimental.pallas import tpu as pltpu

F32 = jnp.float32
BF16 = jnp.bfloat16

D_MODEL = 1024
DEPTH = 2
GRID_W = 64
ROPE_BASE = 10000.0
NORM_EPS = 1e-6
NEG_INF = -1e30
LOG2_E = math.log2(math.e)
GN_EPS = 64e-5
C_RG = 8.0
W_BR = 256
HEAD = 64
SUBLANES = 8
DQ_D = 32
Q_BLK = 128
P_TOTAL = 3456
WA, WB, WC, WD = 1152, 768, 512, 1024
CHUNK = 64
RWKV_BATCH = 4
ROW_TILE = 512
VMEM_LIMIT = 58 * 1024 * 1024


_NN = (((1,), (0,)), ((), ()))
_NT = (((1,), (1,)), ((), ()))


def _bf(x):
    return x.astype(BF16)


def _dotf(x, y, dims=_NN):
    return lax.dot_general(x, y, dims, preferred_element_type=F32)


def _split(x):
    hi = x.astype(BF16)
    return hi, (x - hi.astype(F32)).astype(BF16)


def _mm3(a, b, dims=_NN):
    return _dotf(a[0], b[0], dims) + (_dotf(a[0], b[1], dims) + _dotf(a[1], b[0], dims))


def _mm2(x, m01, left=False):
    hi, lo = _split(x)
    if left:
        return _dotf(m01, hi) + _dotf(m01, lo)
    return _dotf(hi, m01) + _dotf(lo, m01)


def _sigmoid(x):
    return 1.0 / (1.0 + jnp.exp(-x))


def _silu(x):
    return x * _sigmoid(x)


def _softplus(x):
    return jnp.maximum(x, 0.0) + jnp.log1p(jnp.exp(-jnp.abs(x)))


def _head_ones(n):
    r = lax.broadcasted_iota(jnp.int32, (n, n), 0) // HEAD
    c = lax.broadcasted_iota(jnp.int32, (n, n), 1) // HEAD
    return jnp.where(r == c, 1.0, 0.0).astype(BF16)


def _params(sem):
    return pltpu.CompilerParams(dimension_semantics=sem, vmem_limit_bytes=VMEM_LIMIT)


def _mod_kernel(c_ref, w_ref, b_ref, o_ref):
    o_ref[0] = _mm3(_split(_silu(c_ref[...])), _split(w_ref[0])) + b_ref[0]


def _modulation(cvec, w_mod, b_mod):
    n_l = w_mod.shape[0]
    tn = 512
    return pl.pallas_call(
        _mod_kernel,
        grid=(n_l, 3 * D_MODEL // tn),
        in_specs=[pl.BlockSpec((8, D_MODEL), lambda l, j: (0, 0)),
                  pl.BlockSpec((1, D_MODEL, tn), lambda l, j: (l, 0, j)),
                  pl.BlockSpec((1, 1, tn), lambda l, j: (l, 0, j))],
        out_specs=pl.BlockSpec((1, 8, tn), lambda l, j: (l, 0, j)),
        out_shape=jax.ShapeDtypeStruct((n_l, 8, 3 * D_MODEL), F32),
        compiler_params=_params(("parallel", "parallel")),
        name="mod",
    )(cvec, w_mod, b_mod.reshape(n_l, 1, 3 * D_MODEL))


def _rope(x, cos, sin_signed, off):
    w = x.shape[-1]
    lane = lax.broadcasted_iota(jnp.int32, x.shape, 1)
    first = (lane % (2 * off)) < off
    partner = jnp.where(first, pltpu.roll(x, w - off, 1), pltpu.roll(x, off, 1))
    return x * cos + partner * sin_signed


def _proj_kernel(*refs, latent, layer):
    if latent:
        x_ref, mod_ref, g_ref, w_ref, cb_ref, sb_ref, cd_ref, sd_ref, oa, ob, oc, od = refs
    else:
        x_ref, mod_ref, g_ref, w_ref, oa, ob, oc, od, o_wk, o_wv, o_dk, o_dv = refs
    x = x_ref[0]
    y = x * lax.rsqrt(jnp.mean(x * x, -1, keepdims=True) + NORM_EPS) * g_ref[layer:layer + 1]
    m = mod_ref[0]
    h = y * (1.0 + m[:, D_MODEL:2 * D_MODEL]) + m[:, :D_MODEL]
    p = jnp.dot(h.astype(BF16), w_ref[0], preferred_element_type=F32)
    oa[0] = p[:, :WA]
    pb = p[:, WA:WA + WB]
    pd = p[:, WA + WB + WC:]
    if latent:
        ob[0, :, :384] = _rope(pb[:, :384], cb_ref[...], sb_ref[...], 16)
        ob[0, :, 384:] = pb[:, 384:]
        od[0, :, :512] = _rope(pd[:, :512], cd_ref[...], sd_ref[...], 8)
        od[0, :, 512:] = pd[:, 512:]
    else:
        ob[0] = pb
        od[0] = pd
        o_wk[0] = pb[:, 256:384].T
        o_wv[0] = pb[:, 384:512].T
        o_dk[0] = pd[:, 256:512].T
        o_dv[0] = pd[:, 512:768].T
    oc[0] = p[:, WA + WB:WA + WB + WC]


def _project(x, mod_l, g_pre, w_in_bf, layer, tables):
    b, t, _ = x.shape
    latent = tables is not None
    tm = min(t, ROW_TILE)
    mod_idx = (lambda i, j: (1 + i, 0, 0)) if latent else (lambda i, j: (0, 0, 0))
    in_specs = [pl.BlockSpec((1, tm, D_MODEL), lambda i, j: (i, j, 0)),
                pl.BlockSpec((1, 1, 3 * D_MODEL), mod_idx),
                pl.BlockSpec(g_pre.shape, lambda i, j: (0, 0)),
                pl.BlockSpec((1, D_MODEL, P_TOTAL), lambda i, j: (layer, 0, 0))]
    args = [x, mod_l, g_pre, w_in_bf]
    if latent:
        for tab in tables:
            in_specs.append(pl.BlockSpec((tm, tab.shape[1]), lambda i, j: (j, 0)))
            args.append(tab)
    widths = (WA, WB, WC, WD)
    out_specs = [pl.BlockSpec((1, tm, w), lambda i, j: (i, j, 0)) for w in widths]
    out_shape = [jax.ShapeDtypeStruct((b, t, w), F32) for w in widths]
    if not latent:
        for w in (128, 128, W_BR, W_BR):
            out_specs.append(pl.BlockSpec((1, w, tm), lambda i, j: (i, 0, j)))
            out_shape.append(jax.ShapeDtypeStruct((b, w, t), F32))
    return pl.pallas_call(
        functools.partial(_proj_kernel, latent=latent, layer=layer),
        grid=(b, t // tm),
        in_specs=in_specs,
        out_specs=out_specs,
        out_shape=out_shape,
        compiler_params=_params(("parallel", "parallel")),
        name="proj",
    )(*args)


def _rope_tables(t, head_dim, n_lanes):
    half = head_dim // 2
    quarter = half // 2
    pos = np.arange(t)
    row = (pos // GRID_W).astype(np.float32)
    col = (pos % GRID_W).astype(np.float32)
    inv = np.float32(ROPE_BASE) ** (-np.arange(0, half, 2, dtype=np.float32) / np.float32(half))
    lane = np.arange(n_lanes) % head_dim
    in_half = lane % half
    p = np.where((lane < half)[None, :], row[:, None], col[:, None])
    ang = (p * inv[in_half % quarter][None, :]).astype(np.float64)
    sign = np.where(in_half < quarter, -1.0, 1.0)[None, :]
    return jnp.asarray(np.cos(ang), F32), jnp.asarray(np.sin(ang) * sign, F32)


_PREP_FIELDS = (('ar', 2, 1, BF16), ('nm', 1, 1, BF16), ('mm', 1, 1, BF16), ('pq', 1, 2, BF16),
                ('v', 1, 1, BF16), ('bkt', 1, 2, BF16), ('gct', 1, 1, F32))


def _prep_scratch(slots, n_streams):
    return [pltpu.VMEM((slots, n_streams, rows * CHUNK, lanes * W_BR), dt) for _, rows, lanes, dt in _PREP_FIELDS]


class _SlotView:
    def __init__(self, ref, slot):
        self.ref, self.slot, self.shape = ref, slot, ref.shape[1:]

    def __getitem__(self, idx):
        return self.ref[(self.slot,) + (idx if isinstance(idx, tuple) else (idx,))]

    def __setitem__(self, idx, val):
        self.ref[(self.slot,) + (idx if isinstance(idx, tuple) else (idx,))] = val


def _head_transpose(x):
    xt = x.T
    return jnp.concatenate([xt[h * HEAD:(h + 1) * HEAD] for h in range(x.shape[1] // HEAD)], axis=1)


def _expand(x, same):
    return jnp.concatenate([x] * (same.shape[0] // x.shape[0]), axis=0) * same


def _rwkv_prepare(xs, dirs, w0_ref, wup_ref, a0_ref, aup_ref, k_k, k_a, m_ref, same_ref, out):
    c = xs[0].shape[0]
    ones = _head_ones(W_BR)
    ti = lax.broadcasted_iota(jnp.int32, (c, c), 0)
    si = lax.broadcasted_iota(jnp.int32, (c, c), 1)
    cums = [jnp.where(si <= ti, 1.0, 0.0).astype(BF16), jnp.where(si >= ti, 1.0, 0.0).astype(BF16)]
    same = same_ref[...]

    st = []
    for xc, d in zip(xs, dirs):
        r, k, v = xc[:, 0:256], xc[:, 256:512], xc[:, 512:768]
        wd, ad = xc[:, 768:832], xc[:, 832:896]
        kk = k * k_k
        kk = kk * lax.rsqrt(_mm2(kk * kk, ones) + 1e-12)
        z = w0_ref[d:d + 1] + _mm3(_split(jnp.tanh(wd)), _split(wup_ref[d]))
        e = jnp.exp(-_softplus(-z) - 0.5)
        a = _sigmoid(a0_ref[d:d + 1] + _mm3(_split(ad), _split(aup_ref[d])))
        st.append(dict(r=r, v=v, e=e, kd=k * (1.0 + (a - 1.0) * k_a), alpha=-kk, beta=kk * a))
    yield
    for s, d in zip(st, dirs):
        l_incl = _mm2(s['e'], cums[d], left=True)
        l_tot = jnp.sum(s['e'], axis=0, keepdims=True)
        grow = jnp.exp(l_incl)
        tail = jnp.exp(l_incl - l_tot)
        s.update(ar=jnp.concatenate([_bf(s['alpha'] * jnp.exp(s['e'] - l_incl)),
                                     _bf(s['r'] * jnp.exp(-l_incl))], axis=0),
                 b_t=_bf(s['beta'] * grow), k_t=_bf(s['kd'] * grow),
                 b_h=s['beta'] * tail, k_h=s['kd'] * tail,
                 g_c=jnp.broadcast_to(jnp.exp(-l_tot), (c, W_BR)))
    yield
    for g, (s, d) in enumerate(zip(st, dirs)):
        out['ar'][g] = s['ar']
        rhs_t = jnp.concatenate([_expand(s['b_t'], same), _expand(s['k_t'], same)], axis=0)
        g4 = _dotf(s['ar'], rhs_t, _NT)
        strict = m_ref[_M_STRICT + 2 * d]
        incl = m_ref[_M_INCL + 2 * d]
        out['nm'][g] = _bf(g4[:c, :W_BR]) * strict
        out['mm'][g] = _bf(g4[:c, W_BR:]) * strict
        out['pq'][g, :, :W_BR] = _bf(g4[c:, :W_BR]) * incl
        out['pq'][g, :, W_BR:] = _bf(g4[c:, W_BR:]) * incl
    yield
    for g, s in enumerate(st):
        out['v'][g] = _bf(s['v'])
        out['bkt'][g] = jnp.concatenate([_bf(_head_transpose(s['b_h'])), _bf(_head_transpose(s['k_h']))], axis=1)
        out['gct'][g] = _head_transpose(s['g_c'])


def _rwkv_advance(p, m_ref, same_ref, z_ref, y_stores):
    n_streams, c = p['nm'].shape[0], p['nm'].shape[1]
    gs = range(n_streams)
    same = same_ref[...]
    ex = lambda x: _expand(_bf(x), same)
    invs = [(p['nm'][g] * m_ref[_M_PAIR]).astype(F32) + m_ref[_M_EYE].astype(F32) for g in gs]
    for lvl in range(int(math.log2(c)) - 1):
        inv_x = [ex(t) for t in invs]
        half = [_dotf(p['nm'][g] * m_ref[_M_OFF + lvl], inv_x[g]) for g in gs]
        invs = [invs[g] + _dotf(_bf(invs[g]), ex(half[g])) for g in gs]
        yield
    z0 = [z_ref[g] for g in gs]
    as0 = [_dotf(p['ar'][g], ex(z0[g])) for g in gs]
    vx = [_expand(p['v'][g], same) for g in gs]
    rhs = [as0[g][:c] + _dotf(p['mm'][g], vx[g]) for g in gs]
    yield
    uv = [jnp.concatenate([ex(_dotf(_bf(invs[g]), ex(rhs[g]))), vx[g]], axis=0) for g in gs]
    yield
    for g in gs:
        z_ref[g] = z0[g] * p['gct'][g] + _dotf(p['bkt'][g], uv[g])
    yield
    for g in gs:
        y_stores[g](as0[g][c:] + _dotf(p['pq'][g], uv[g]))


def _interleave(*gens):
    live = list(gens)
    while live:
        for gen in list(live):
            try:
                next(gen)
            except StopIteration:
                live.remove(gen)


_M_STRICT, _M_INCL, _M_EYE, _M_PAIR, _M_OFF = 0, 1, 4, 5, 6


def _rwkv_masks(c, n_h):
    t = np.arange(c)[:, None]
    s = np.arange(c)[None, :]
    masks = [s < t, s <= t, s > t, s >= t, s == t, ((t // 2) == (s // 2)) & (t != s)]
    b = 2
    while b < c:
        masks.append(((t // (2 * b)) == (s // (2 * b))) & ((t // b) != (s // b)))
        b *= 2
    masks = np.tile(np.stack(masks).astype(np.float32), (1, 1, n_h))
    r = np.arange(n_h * c)[:, None] // c
    l = np.arange(n_h * HEAD)[None, :] // HEAD
    return jnp.asarray(masks, BF16), jnp.asarray((r == l).astype(np.float32), BF16)


def _rwkv_kernel(*refs, latent, n_b, layer):
    (xf_ref, xb_ref, xfn_ref, xbn_ref, w0_ref, wup_ref, a0_ref, aup_ref, kk_ref, ka_ref,
     m_ref, same_ref) = refs[:12]
    if latent:
        s0_ref, yf_ref, yb_ref = refs[12:15]
    else:
        yf_ref, yb_ref, st_ref = refs[12:15]
    scr = refs[15:]
    z_scr = scr[0]
    names = [f[0] for f in _PREP_FIELDS]
    ci = pl.program_id(1)
    step = pl.program_id(0) * pl.num_programs(1) + ci
    view = lambda refs_, slot: {k: _SlotView(r, slot) for k, r in zip(names, refs_)}
    cur = view(scr[1:1 + len(names)], step % 2)
    nxt = view(scr[1:1 + len(names)], 1 - step % 2)
    mid = view(scr[1 + len(names):], 0)
    n_h = W_BR // HEAD
    c = CHUNK
    head = lambda h: slice(h * HEAD, (h + 1) * HEAD)
    streams = [(bi, d) for bi in range(n_b) for d in range(2)]
    dirs = [d for _, d in streams]
    params = (w0_ref[layer], wup_ref[0], a0_ref[layer], aup_ref[0], kk_ref[layer:layer + 1],
              ka_ref[layer:layer + 1], m_ref, same_ref)
    first = lambda d: slice(0, c) if d == 0 else slice(c, 2 * c)
    second = lambda d: slice(c, 2 * c) if d == 0 else slice(0, c)
    x_refs, xn_refs, y_refs = (xf_ref, xb_ref), (xfn_ref, xbn_ref), (yf_ref, yb_ref)

    def y_store(bi, d, rows):
        def store(y):
            y_refs[d][bi, rows, :] = y
        return store

    @pl.when(ci == 0)
    def _():
        for g, (bi, d) in enumerate(streams):
            if latent:
                z_scr[g] = jnp.concatenate([s0_ref[bi, 0, d, h].T for h in range(n_h)], axis=1)
            else:
                z_scr[g] = jnp.zeros(z_scr.shape[1:], F32)

    @pl.when(step == 0)
    def _():
        _interleave(_rwkv_prepare([x_refs[d][bi, first(d), :] for bi, d in streams], dirs, *params, cur))

    _interleave(_rwkv_advance(cur, m_ref, same_ref, z_scr, [y_store(bi, d, first(d)) for bi, d in streams]),
                _rwkv_prepare([x_refs[d][bi, second(d), :] for bi, d in streams], dirs, *params, mid))
    _interleave(_rwkv_advance(mid, m_ref, same_ref, z_scr, [y_store(bi, d, second(d)) for bi, d in streams]),
                _rwkv_prepare([xn_refs[d][bi, first(d), :] for bi, d in streams], dirs, *params, nxt))

    if not latent:
        @pl.when(ci == pl.num_programs(1) - 1)
        def _():
            for g, (bi, d) in enumerate(streams):
                z = z_scr[g]
                for h in range(n_h):
                    st_ref[bi, d, h] = z[:, head(h)].T


def _rwkv(pa, w0, wup, a0, aup, k_k, k_a, state, layer):
    b, t, _ = pa.shape
    latent = state is not None
    ns = t // (2 * CHUNK)
    n_h = W_BR // HEAD
    n_b = RWKV_BATCH
    masks, same = _rwkv_masks(CHUNK, n_h)
    full = lambda shape: pl.BlockSpec(shape, lambda i, j: (0,) * len(shape))
    one_layer = lambda shape: pl.BlockSpec((1,) + shape[1:], lambda i, j: (layer,) + (0,) * (len(shape) - 1))
    rows = lambda w, idx: pl.BlockSpec((n_b, 2 * CHUNK, w), idx)
    nxt_group = lambda i, j: jnp.minimum(i + (j + 1) // ns, b // n_b - 1)
    in_specs = [rows(WA, lambda i, j: (i, j, 0)),
                rows(WA, lambda i, j: (i, ns - 1 - j, 0)),
                rows(WA, lambda i, j: (nxt_group(i, j), (j + 1) % ns, 0)),
                rows(WA, lambda i, j: (nxt_group(i, j), (2 * ns - 2 - j) % ns, 0)),
                full(w0.shape), one_layer(wup.shape), full(a0.shape), one_layer(aup.shape),
                full(k_k.shape), full(k_a.shape), full(masks.shape), full(same.shape)]
    args = [pa, pa, pa, pa, w0, wup, a0, aup, k_k, k_a, masks, same]
    out_specs = [rows(W_BR, lambda i, j: (i, j, 0)), rows(W_BR, lambda i, j: (i, ns - 1 - j, 0))]
    out_shape = [jax.ShapeDtypeStruct((b, t, W_BR), F32)] * 2
    if latent:
        in_specs.append(pl.BlockSpec((n_b, 1, 2, n_h, HEAD, HEAD), lambda i, j: (i, layer, 0, 0, 0, 0)))
        args.append(state)
    else:
        out_specs.append(pl.BlockSpec((n_b, 2, n_h, HEAD, HEAD), lambda i, j: (i, 0, 0, 0, 0)))
        out_shape.append(jax.ShapeDtypeStruct((b, 2, n_h, HEAD, HEAD), F32))
    return pl.pallas_call(
        functools.partial(_rwkv_kernel, latent=latent, n_b=n_b, layer=layer),
        grid=(b // n_b, ns),
        in_specs=in_specs,
        out_specs=out_specs,
        out_shape=out_shape,
        scratch_shapes=([pltpu.VMEM((2 * n_b, CHUNK, W_BR), F32)]
                        + _prep_scratch(2, 2 * n_b) + _prep_scratch(1, 2 * n_b)),
        compiler_params=_params(("arbitrary", "arbitrary")),
        name="rwkv",
    )(*args)


def _lru_kernel(*refs, latent, layer):
    if latent:
        (x_ref, cw_ref, cb_ref, wa_ref, ba_ref, wx_ref, bx_ref, lam_ref, h0_ref, y_ref,
         a_scr, h_scr) = refs
    else:
        (x_ref, cw_ref, cb_ref, wa_ref, ba_ref, wx_ref, bx_ref, lam_ref, y_ref, st_ref,
         a_scr, h_scr) = refs
    x = x_ref[0, :, :W_BR]
    t = x.shape[0]
    row = lax.broadcasted_iota(jnp.int32, x.shape, 0)

    def shift_dn(z, k, fill):
        return jnp.where(row >= k, pltpu.roll(z, k, 0), fill)

    def shift_up(z, k, fill):
        return jnp.where(row < t - k, pltpu.roll(z, t - k, 0), fill)

    cw = cw_ref[layer]
    xc = (cb_ref[layer:layer + 1] + shift_dn(x, 2, 0.0) * cw[0:1] + shift_dn(x, 1, 0.0) * cw[1:2]
          + x * cw[2:3] + shift_up(x, 1, 0.0) * cw[3:4])
    xs = _split(xc)
    sub = row % SUBLANES
    n_blk = W_BR // HEAD
    zero = jnp.zeros((HEAD, HEAD), F32)

    def block_diag(w_ref, d):
        return jnp.concatenate([jnp.concatenate([w_ref[0, d, n] if m == n else zero for m in range(n_blk)], axis=1)
                                for n in range(n_blk)], axis=0)

    for d in range(2):
        gate_a = _sigmoid(_mm3(xs, _split(block_diag(wa_ref, d))) + ba_ref[layer, d:d + 1])
        gate_x = _sigmoid(_mm3(xs, _split(block_diag(wx_ref, d))) + bx_ref[layer, d:d + 1])
        log_a = -C_RG * gate_a * _softplus(-lam_ref[layer, d:d + 1])
        a = jnp.exp(log_a)
        u = jnp.sqrt(-jnp.tanh(log_a) * (a * a + 1.0)) * (gate_x * xc)
        k = 1
        while k < SUBLANES:
            if d == 0:
                keep = sub >= k
                sh = lambda z: pltpu.roll(z, k, 0)
            else:
                keep = sub < SUBLANES - k
                sh = lambda z: pltpu.roll(z, t - k, 0)
            u = a * jnp.where(keep, sh(u), 0.0) + u
            a = a * jnp.where(keep, sh(a), 1.0)
            k *= 2
        a_scr[d] = a
        h_scr[d] = u

    n_grp = t // SUBLANES
    if latent:
        carry0 = (h0_ref[0, 0, 0:1], h0_ref[0, 0, 1:2])
    else:
        carry0 = (jnp.zeros((1, W_BR), F32),) * 2

    def chain(i, carry):
        cf, cb = carry
        rf = pl.ds(pl.multiple_of(i * SUBLANES, SUBLANES), SUBLANES)
        rb = pl.ds(pl.multiple_of((n_grp - 1 - i) * SUBLANES, SUBLANES), SUBLANES)
        hf = h_scr[0, rf, :] + a_scr[0, rf, :] * cf
        hb = h_scr[1, rb, :] + a_scr[1, rb, :] * cb
        h_scr[0, rf, :] = hf
        h_scr[1, rb, :] = hb
        return hf[SUBLANES - 1:SUBLANES], hb[0:1]

    cf, cb = lax.fori_loop(0, n_grp, chain, carry0, unroll=4)
    y_ref[0] = h_scr[0] + h_scr[1]
    if not latent:
        st_ref[0] = jnp.concatenate([cf, cb], axis=0)


def _lru(pc, cw, cb, wa, ba, wx, bx, lam, state, layer):
    b, t, _ = pc.shape
    latent = state is not None
    full = lambda shape: pl.BlockSpec(shape, lambda i: (0,) * len(shape))
    one_layer = lambda shape: pl.BlockSpec((1,) + shape[1:], lambda i: (layer,) + (0,) * (len(shape) - 1))
    in_specs = [pl.BlockSpec((1, t, WC), lambda i: (i, 0, 0)),
                full(cw.shape), full(cb.shape), one_layer(wa.shape), full(ba.shape),
                one_layer(wx.shape), full(bx.shape), full(lam.shape)]
    args = [pc, cw, cb, wa, ba, wx, bx, lam]
    out_specs = [pl.BlockSpec((1, t, W_BR), lambda i: (i, 0, 0))]
    out_shape = [jax.ShapeDtypeStruct((b, t, W_BR), F32)]
    if latent:
        in_specs.append(pl.BlockSpec((1, 1, 2, W_BR), lambda i: (i, layer, 0, 0)))
        args.append(state)
    else:
        out_specs.append(pl.BlockSpec((1, 2, W_BR), lambda i: (i, 0, 0)))
        out_shape.append(jax.ShapeDtypeStruct((b, 2, W_BR), F32))
    return pl.pallas_call(
        functools.partial(_lru_kernel, latent=latent, layer=layer),
        grid=(b,),
        in_specs=in_specs,
        out_specs=out_specs,
        out_shape=out_shape,
        scratch_shapes=[pltpu.VMEM((2, t, W_BR), F32), pltpu.VMEM((2, t, W_BR), F32)],
        compiler_params=_params(("parallel",)),
        name="lru",
    )(*args)


def _win_kernel(*refs, latent, layer):
    if latent:
        sink_ref, q_ref, kp_ref, kc_ref, kn_ref, ck_ref, cv_ref, o_ref = refs
        j = pl.program_id(1)
        nq = pl.num_programs(1)
        qi = lax.broadcasted_iota(jnp.int32, (Q_BLK, Q_BLK), 0)
        ki = lax.broadcasted_iota(jnp.int32, (Q_BLK, Q_BLK), 1)
        q = q_ref[0]
        ctx_k, ctx_v = _bf(ck_ref[0, 0]), _bf(cv_ref[0, 0])
        pieces = [(_bf(r[0, :, :128]), _bf(r[0, :, 128:]), m, False) for r, m in
                  ((kp_ref, (ki >= qi) & (j > 0)), (kc_ref, None), (kn_ref, (ki <= qi) & (j < nq - 1)))]
        pieces += [(ctx_k[:, i:i + Q_BLK], ctx_v[:, i:i + Q_BLK], None, True)
                   for i in range(0, ctx_k.shape[1], Q_BLK)]
    else:
        sink_ref, x_ref, o_ref = refs
        q = x_ref[0, :, :256]
        k, v = _bf(x_ref[0, :, 256:384]), _bf(x_ref[0, :, 384:512])
        pieces = [(k[i:i + Q_BLK], v[i:i + Q_BLK], None, False) for i in range(0, k.shape[0], Q_BLK)]
    q = _bf(q * (HEAD ** -0.5 * LOG2_E))
    tq = q.shape[0]
    first_g = lax.broadcasted_iota(jnp.int32, (2 * tq, 1), 0) < tq
    outs = []
    for h in range(2):
        hs = slice(h * HEAD, (h + 1) * HEAD)
        qg = jnp.concatenate([q[:, (2 * h + g) * HEAD:(2 * h + g + 1) * HEAD] for g in range(2)], axis=0)
        sink = jnp.where(first_g, sink_ref[layer, 2 * h], sink_ref[layer, 2 * h + 1]) * LOG2_E
        logits = []
        for kp, _, mask, fm in pieces:
            s = _dotf(qg, kp[hs]) if fm else _dotf(qg, kp[:, hs], _NT)
            logits.append(s if mask is None else jnp.where(jnp.concatenate([mask, mask], axis=0), s, NEG_INF))
        mx = jnp.maximum(jnp.max(functools.reduce(jnp.maximum, logits), axis=-1, keepdims=True), sink)
        es = [jnp.exp2(s - mx) for s in logits]
        den = jnp.sum(functools.reduce(lambda a, b: a + b, es), axis=-1, keepdims=True) + jnp.exp2(sink - mx)
        acc = functools.reduce(lambda a, b: a + b,
                               [_dotf(_bf(e), vp[hs], _NT) if fm else _dotf(_bf(e), vp[:, hs])
                                for e, (_, vp, _, fm) in zip(es, pieces)])
        o = acc / den
        outs += [o[:tq], o[tq:]]
    o_ref[0] = jnp.concatenate(outs, axis=-1)


def _win(pb, sink, cache_k, cache_v, layer):
    b, t, _ = pb.shape
    latent = cache_k is not None
    smem = pl.BlockSpec(memory_space=pltpu.SMEM)
    if latent:
        nq = t // Q_BLK
        grid = (b, nq)
        in_specs = [smem,
                    pl.BlockSpec((1, Q_BLK, 256), lambda i, j: (i, j, 0)),
                    pl.BlockSpec((1, Q_BLK, 256), lambda i, j: (i, jnp.maximum(j - 1, 0), 1)),
                    pl.BlockSpec((1, Q_BLK, 256), lambda i, j: (i, j, 1)),
                    pl.BlockSpec((1, Q_BLK, 256), lambda i, j: (i, jnp.minimum(j + 1, nq - 1), 1)),
                    pl.BlockSpec((1, 1) + cache_k.shape[2:], lambda i, j: (i, layer, 0, 0)),
                    pl.BlockSpec((1, 1) + cache_v.shape[2:], lambda i, j: (i, layer, 0, 0))]
        args = [sink, pb, pb, pb, pb, cache_k, cache_v]
        out_spec = pl.BlockSpec((1, Q_BLK, W_BR), lambda i, j: (i, j, 0))
        sem = ("parallel", "parallel")
    else:
        grid = (b,)
        in_specs = [smem, pl.BlockSpec((1, t, WB), lambda i: (i, 0, 0))]
        args = [sink, pb]
        out_spec = pl.BlockSpec((1, t, W_BR), lambda i: (i, 0, 0))
        sem = ("parallel",)
    return pl.pallas_call(
        functools.partial(_win_kernel, latent=latent, layer=layer),
        grid=grid,
        in_specs=in_specs,
        out_specs=out_spec,
        out_shape=jax.ShapeDtypeStruct((b, t, W_BR), F32),
        compiler_params=_params(sem),
        name="win",
    )(*args)


def _diff_kernel(*refs, latent, lam_init, layer):
    if latent:
        lam_ref, g_ref, q_ref, k_ref, v_ref, ck_ref, cv_ref, o_ref = refs
    else:
        lam_ref, g_ref, q_ref, k_ref, v_ref, o_ref = refs
    lp = lam_ref[layer]
    lam = (jnp.exp(jnp.sum(lp[0:1] * lp[1:2], axis=-1, keepdims=True))
           - jnp.exp(jnp.sum(lp[2:3] * lp[3:4], axis=-1, keepdims=True)) + lam_init)
    q = _bf(q_ref[0] * (DQ_D ** -0.5 * LOG2_E))
    key_sets = [(_bf(k_ref[0]), _bf(v_ref[0]), False)]
    if latent:
        key_sets.append((_bf(ck_ref[0, 0]), _bf(cv_ref[0, 0]), True))
    lane = lax.broadcasted_iota(jnp.int32, (1, W_BR), 1)
    out = jnp.zeros(q.shape, F32)
    for h in range(W_BR // HEAD):
        probs = []
        for m in range(2):
            qm = jnp.where(lane // DQ_D == 2 * h + m, q, jnp.zeros((), BF16))
            logits = [_dotf(qm, kk) if fm else _dotf(qm, kk, _NT) for kk, _, fm in key_sets]
            mx = functools.reduce(jnp.maximum, [jnp.max(s, axis=-1, keepdims=True) for s in logits])
            es = [jnp.exp2(s - mx) for s in logits]
            den = functools.reduce(lambda a, b: a + b, [jnp.sum(e, axis=-1, keepdims=True) for e in es])
            inv = 1.0 / den
            probs.append([e * inv for e in es])
        o = None
        for i, (_, vv, fm) in enumerate(key_sets):
            pi = _bf(probs[0][i] - lam * probs[1][i])
            oi = _dotf(pi, vv, _NT) if fm else _dotf(pi, vv)
            o = oi if o is None else o + oi
        out = jnp.where(lane // HEAD == h, o, out)
    ms = _mm2(out * out, _head_ones(W_BR)) * (1.0 / HEAD)
    gain = jnp.concatenate([g_ref[layer:layer + 1]] * (W_BR // HEAD), axis=1)
    o_ref[0] = out * lax.rsqrt(ms + NORM_EPS) * gain * (1.0 - lam_init)


def _diff(pd, lam_p, subln_g, cache_k, cache_v, layer, lam_init):
    b, t, _ = pd.shape
    latent = cache_k is not None
    tq = min(t, 1024)
    in_specs = [pl.BlockSpec(lam_p.shape, lambda i, j: (0, 0, 0)),
                pl.BlockSpec(subln_g.shape, lambda i, j: (0, 0)),
                pl.BlockSpec((1, tq, 256), lambda i, j: (i, j, 0)),
                pl.BlockSpec((1, t, 256), lambda i, j: (i, 0, 1)),
                pl.BlockSpec((1, t, 256), lambda i, j: (i, 0, 2))]
    args = [lam_p, subln_g, pd, pd, pd]
    if latent:
        in_specs += [pl.BlockSpec((1, 1) + cache_k.shape[2:], lambda i, j: (i, layer, 0, 0)),
                     pl.BlockSpec((1, 1) + cache_v.shape[2:], lambda i, j: (i, layer, 0, 0))]
        args += [cache_k, cache_v]
    return pl.pallas_call(
        functools.partial(_diff_kernel, latent=latent, lam_init=lam_init, layer=layer),
        grid=(b, t // tq),
        in_specs=in_specs,
        out_specs=pl.BlockSpec((1, tq, W_BR), lambda i, j: (i, j, 0)),
        out_shape=jax.ShapeDtypeStruct((b, t, W_BR), F32),
        compiler_params=_params(("parallel", "parallel")),
        name="diff",
    )(*args)


def _out_kernel(x_ref, mod_ref, gpost_ref, w_ref, pa_ref, yf_ref, yb_ref, gng_ref, gnb_ref, rk_ref,
                ywin_ref, bg_ref, ylru_ref, cg_ref, ydiff_ref, dg_ref, o_ref, *, layer):
    lrow = slice(layer, layer + 1)
    ones = _head_ones(W_BR)
    rows = lambda ref: ref[...].reshape(-1, ref.shape[-1])
    pa = rows(pa_ref)
    r, k, v, ag = pa[:, 0:256], pa[:, 256:512], pa[:, 512:768], pa[:, 896:1152]
    y = rows(yf_ref) + rows(yb_ref)
    mu = _mm2(y, ones) * (1.0 / HEAD)
    yc = y - mu
    var = _mm2(yc * yc, ones) * (1.0 / HEAD)
    ya = yc * lax.rsqrt(var + GN_EPS) * gng_ref[lrow] + gnb_ref[lrow]
    ya = ya + _mm2(r * k * rk_ref[lrow], ones) * v
    mix = jnp.concatenate([ya * _silu(ag), rows(ywin_ref) * _silu(rows(bg_ref)),
                           rows(ylru_ref) * _silu(rows(cg_ref)), rows(ydiff_ref) * _silu(rows(dg_ref))], axis=-1)
    o = jnp.dot(mix.astype(BF16), w_ref[0], preferred_element_type=F32)
    o = o * lax.rsqrt(jnp.mean(o * o, -1, keepdims=True) + NORM_EPS) * gpost_ref[lrow]
    o_ref[...] = (rows(x_ref) + mod_ref[0][:, 2 * D_MODEL:] * o).reshape(o_ref.shape)


def _out(x, mod_l, g_post, w_out_bf, layer, latent, pa, yf, yb, gn_g, gn_b, r_k, ywin, pb, ylru, pc, ydiff, pd):
    b, t, _ = x.shape
    tm = min(t, ROW_TILE)
    n_seq = 1 if latent else ROW_TILE // tm
    mod_idx = (lambda i, j: (1 + i, 0, 0)) if latent else (lambda i, j: (0, 0, 0))
    rows = lambda w, c=0: pl.BlockSpec((n_seq, tm, w), lambda i, j: (i, j, c))
    vec = pl.BlockSpec(gn_g.shape, lambda i, j: (0, 0))
    in_specs = [rows(D_MODEL),
                pl.BlockSpec((1, 1, 3 * D_MODEL), mod_idx),
                pl.BlockSpec(g_post.shape, lambda i, j: (0, 0)),
                pl.BlockSpec((1, D_MODEL, D_MODEL), lambda i, j: (layer, 0, 0)),
                rows(WA), rows(W_BR), rows(W_BR), vec, vec, vec,
                rows(W_BR), rows(W_BR, 2), rows(W_BR), rows(W_BR, 1), rows(W_BR), rows(W_BR, 3)]
    return pl.pallas_call(
        functools.partial(_out_kernel, layer=layer),
        grid=(b // n_seq, t // tm),
        in_specs=in_specs,
        out_specs=rows(D_MODEL),
        out_shape=jax.ShapeDtypeStruct((b, t, D_MODEL), F32),
        compiler_params=_params(("parallel", "parallel")),
        name="out",
    )(x, mod_l, g_post, w_out_bf, pa, yf, yb, gn_g, gn_b, r_k, ywin, pb, ylru, pc, ydiff, pd)


def _layer(x, mod_l, layer, lam_init, wts, cache, tables):
    latent = cache is not None
    pa, pb, pc, pd, *ctx_t = _project(x, mod_l, wts['g_pre'], wts['w_in_bf'], layer, tables)
    rw = _rwkv(pa, wts['rwkv_w0'], wts['rwkv_w_up'], wts['rwkv_a0'], wts['rwkv_a_up'], wts['rwkv_k_k'],
               wts['rwkv_k_a'], cache['rwkv'] if latent else None, layer)
    lr = _lru(pc, wts['lru_conv_w'], wts['lru_conv_b'], wts['lru_wa'], wts['lru_ba'], wts['lru_wx'],
              wts['lru_bx'], wts['lru_lambda'], cache['lru'] if latent else None, layer)
    ywin = _win(pb, wts['win_sink'], cache['win_k'] if latent else None,
                cache['win_v'] if latent else None, layer)
    ydiff = _diff(pd, wts['diff_lambda'], wts['diff_subln_g'],
                  cache['diff_k'] if latent else None, cache['diff_v'] if latent else None, layer, lam_init)
    y = _out(x, mod_l, wts['g_post'], wts['w_out_bf'], layer, latent, pa, rw[0], rw[1],
             wts['rwkv_gn_g'], wts['rwkv_gn_b'], wts['rwkv_r_k'], ywin, pb, lr[0], pc, ydiff, pd)
    new_cache = None if latent else (ctx_t, rw[2], lr[1])
    return y, new_cache


def kernel(x_prompt, x_sample, c, cache_win_k, cache_win_v, cache_diff_k, cache_diff_v, state_rwkv, state_lru,
           c_ctx, w_mod, b_mod, g_pre, g_post, w_in, w_out,
           rwkv_w0, rwkv_w_up, rwkv_a0, rwkv_a_up, rwkv_k_k, rwkv_k_a, rwkv_r_k, rwkv_gn_g, rwkv_gn_b,
           win_sink, lru_conv_w, lru_conv_b, lru_wa, lru_ba, lru_wx, lru_bx, lru_lambda,
           diff_lambda, diff_subln_g):
    n_b, seq = x_prompt.shape[:2]
    n_dec, dec_seq = x_sample.shape[:2]
    past = cache_win_k.shape[2]
    wts = dict(g_pre=g_pre, g_post=g_post, w_in_bf=w_in.astype(BF16), w_out_bf=w_out.astype(BF16),
               rwkv_w0=rwkv_w0, rwkv_w_up=rwkv_w_up, rwkv_a0=rwkv_a0, rwkv_a_up=rwkv_a_up,
               rwkv_k_k=rwkv_k_k, rwkv_k_a=rwkv_k_a, rwkv_r_k=rwkv_r_k.reshape(DEPTH, W_BR), rwkv_gn_g=rwkv_gn_g,
               rwkv_gn_b=rwkv_gn_b, win_sink=win_sink, lru_conv_w=lru_conv_w, lru_conv_b=lru_conv_b,
               lru_wa=lru_wa, lru_ba=lru_ba, lru_wx=lru_wx, lru_bx=lru_bx, lru_lambda=lru_lambda,
               diff_lambda=diff_lambda, diff_subln_g=diff_subln_g)
    fm = lambda a: jnp.moveaxis(a, 2, -1).reshape(n_dec, DEPTH, -1, past)
    cache = dict(win_k=fm(cache_win_k), win_v=fm(cache_win_v), diff_k=fm(cache_diff_k), diff_v=fm(cache_diff_v),
                 rwkv=state_rwkv, lru=state_lru)
    cvec = jnp.concatenate([c_ctx[None], c, jnp.zeros((8 - 1 - n_dec, D_MODEL), F32)], axis=0)
    mod = _modulation(cvec, w_mod, b_mod)
    cos_b, sin_b = _rope_tables(dec_seq, HEAD, 384)
    cos_d, sin_d = _rope_tables(dec_seq, DQ_D, 512)
    tables = (cos_b, sin_b, cos_d, sin_d)

    y_p, y_s = x_prompt, x_sample
    ctx = []
    for l in range(DEPTH):
        lam_init = 0.8 - 0.6 * math.exp(-0.3 * l)
        mod_l = mod[l].reshape(8, 1, 3 * D_MODEL)
        y_p, nc = _layer(y_p, mod_l, l, lam_init, wts, None, None)
        ctx.append(nc)
        y_s, _ = _layer(y_s, mod_l, l, lam_init, wts, cache, tables)
    stack = lambda f: jnp.stack([f(ct) for ct in ctx], axis=1)
    tm_ = lambda i, dims: jnp.moveaxis(stack(lambda ct: ct[0][i]).reshape((n_b, DEPTH) + dims + (seq,)), -1, 2)
    new_win_k = tm_(0, (2, HEAD))
    new_win_v = tm_(1, (2, HEAD))
    new_diff_k = tm_(2, (4, 2, DQ_D))
    new_diff_v = tm_(3, (4, HEAD))
    new_state_rwkv = stack(lambda ct: ct[1])
    new_state_lru = stack(lambda ct: ct[2])
    return (y_p, y_s, new_win_k, new_win_v, new_diff_k, new_diff_v, new_state_rwkv, new_state_lru)
```
